```python
import math
import jax
import jax.numpy as jnp
from jax import lax
import numpy as np

D_MODEL = 1024
BATCH = 8
SEQ = 4096
DEPTH = 2

GRID_W = 64
CTX_LEN = 256
HEAD_DIM = 64
ROPE_THETA = 10000.0
NORM_EPS = 1e-6
Q_BLOCK = 128
NEG_INF = -1e30
ADA_CHUNKS = 6

NA_HEADS = 8
NA_WIN_H = 8
NA_WIN_W = 16
DIFF_HEADS = 4
DIFF_V_DIM = 2 * HEAD_DIM
GQA_Q_HEADS = 8
GQA_KV_HEADS = 2
GQA_GROUP = GQA_Q_HEADS // GQA_KV_HEADS
MLA_HEADS = 8
MLA_NOPE = 64
MLA_ROPE = 32
MLA_QK = MLA_NOPE + MLA_ROPE
MLA_V = 64
MLA_Q_RANK = 256
MLA_KV_RANK = 128
D_FF = 2816
N_EXPERTS = 8
TOP_K = 2
D_FF_EXPERT = 3584
MOE_BLOCK = 128

N_EVEN = (DEPTH + 1) // 2
N_ODD = DEPTH // 2

EVEN_COLS = (NA_HEADS * HEAD_DIM, NA_HEADS * HEAD_DIM, NA_HEADS * HEAD_DIM,
             DIFF_HEADS * 2 * HEAD_DIM, DIFF_HEADS * 2 * HEAD_DIM, DIFF_HEADS * DIFF_V_DIM)
ODD_COLS = (GQA_Q_HEADS * HEAD_DIM, GQA_KV_HEADS * HEAD_DIM, GQA_KV_HEADS * HEAD_DIM,
            MLA_Q_RANK, MLA_KV_RANK, MLA_ROPE)
EVEN_IN = sum(EVEN_COLS)
ODD_IN = sum(ODD_COLS)
EVEN_SPLITS = tuple(int(s) for s in np.cumsum(EVEN_COLS)[:-1])
ODD_SPLITS = tuple(int(s) for s in np.cumsum(ODD_COLS)[:-1])
EVEN_OUT = NA_HEADS * HEAD_DIM + DIFF_HEADS * DIFF_V_DIM
ODD_OUT = GQA_Q_HEADS * HEAD_DIM + MLA_HEADS * MLA_V

kernel_name = 'hybrid_na_diff_gqa_mla_moe_dit'


def rms_norm(x, g):
    xf = x.astype(jnp.float32)
    y = xf * lax.rsqrt(jnp.mean(xf * xf, axis=-1, keepdims=True) + NORM_EPS)
    return (y * g.astype(jnp.float32)).astype(x.dtype)


def modulate(h, shift, scale):
    return h * (1.0 + scale) + shift


def rope_1d(x, pos):
    half = x.shape[-1] // 2
    freqs = ROPE_THETA ** (-jnp.arange(half, dtype=jnp.float32) / half)
    ang = pos.astype(jnp.float32)[:, None] * freqs[None, :]
    cos = jnp.cos(ang).astype(x.dtype)
    sin = jnp.sin(ang).astype(x.dtype)
    x1, x2 = x[..., :half], x[..., half:]
    return jnp.concatenate([x1 * cos - x2 * sin, x1 * sin + x2 * cos], axis=-1)


def rope_2d(x, rows, cols):
    half = x.shape[-1] // 2
    return jnp.concatenate([rope_1d(x[..., :half], rows), rope_1d(x[..., half:], cols)], axis=-1)


def split_heads(t, n_heads, d):
    B, S, _ = t.shape
    return t.reshape(B, S, n_heads, d).transpose(0, 2, 1, 3)


def merge_heads(o):
    B, S = o.shape[0], o.shape[-2]
    return jnp.moveaxis(o, -2, 1).reshape(B, S, -1)


def over_query_blocks(fn, q):
    *lead, S, d = q.shape
    nb = S // Q_BLOCK
    qb = jnp.moveaxis(q.reshape(*lead, nb, Q_BLOCK, d), -3, 0)
    ob = jnp.moveaxis(lax.map(fn, qb), 0, -3)
    return ob.reshape(*ob.shape[:-3], S, ob.shape[-1])


def softmax_attend(q, k, v, scale):
    s = jnp.einsum('bhgqd,bhkd->bhgqk', q, k).astype(jnp.float32) * scale
    p = jax.nn.softmax(s, axis=-1).astype(v.dtype)
    return jnp.einsum('bhgqk,bhkd->bhgqd', p, v)


def diff_attend(q, k, v, lam, scale):
    s = jnp.einsum('bhmqd,bhmkd->bhmqk', q, k).astype(jnp.float32) * scale
    p = jax.nn.softmax(s, axis=-1)
    pd = (p[:, :, 0] - lam * p[:, :, 1]).astype(v.dtype)
    return jnp.einsum('bhqk,bhkd->bhqd', pd, v)


def neighbourhood_attention(q, k, v, k_ctx, v_ctx, rpb, scale):
    B, H, S, d = q.shape
    rows_n = S // GRID_W
    kh = min(NA_WIN_H, rows_n)
    qg = q.reshape(B, H, rows_n, GRID_W, d)
    kg = k.reshape(B, H, rows_n, GRID_W, d)
    vg = v.reshape(B, H, rows_n, GRID_W, d)
    col = jnp.arange(GRID_W)
    c0 = jnp.clip(col - NA_WIN_W // 2, 0, GRID_W - NA_WIN_W)
    col_in = (col[None, :] >= c0[:, None]) & (col[None, :] < c0[:, None] + NA_WIN_W)
    mask = jnp.broadcast_to(col_in[:, None, :], (GRID_W, kh, GRID_W)).reshape(GRID_W, kh * GRID_W)
    dc = jnp.clip(col[None, :] - col[:, None], -(NA_WIN_W - 1), NA_WIN_W - 1) + NA_WIN_W - 1
    n_lat = kh * GRID_W

    def one_row(r):
        r0 = jnp.clip(r - kh // 2, 0, rows_n - kh)
        kb = lax.dynamic_slice_in_dim(kg, r0, kh, axis=2).reshape(B, H, n_lat, d)
        vb = lax.dynamic_slice_in_dim(vg, r0, kh, axis=2).reshape(B, H, n_lat, d)
        qr = lax.dynamic_index_in_dim(qg, r, axis=2, keepdims=False)
        dr = r0 + jnp.arange(kh) - r + NA_WIN_H - 1
        bias = rpb[:, dr[None, :, None], dc[:, None, :]].reshape(H, GRID_W, n_lat)
        s_lat = jnp.einsum('bhqd,bhkd->bhqk', qr, kb).astype(jnp.float32) * scale + bias.astype(jnp.float32)
        s_lat = jnp.where(mask, s_lat, NEG_INF)
        s_ctx = jnp.einsum('bhqd,bhkd->bhqk', qr, k_ctx).astype(jnp.float32) * scale
        p = jax.nn.softmax(jnp.concatenate([s_lat, s_ctx], axis=-1), axis=-1).astype(v.dtype)
        return (jnp.einsum('bhqk,bhkd->bhqd', p[..., :n_lat], vb)
                + jnp.einsum('bhqk,bhkd->bhqd', p[..., n_lat:], v_ctx))

    o = lax.map(one_row, jnp.arange(rows_n))
    return jnp.moveaxis(o, 0, 2).reshape(B, H, S, d)


def even_project(t, w_in, na_q_g, na_k_g, diff_q_g, diff_k_g):
    B, S, _ = t.shape
    qa, ka, va, qb, kb, vb = jnp.split(t @ w_in, EVEN_SPLITS, axis=-1)
    qa = rms_norm(split_heads(qa, NA_HEADS, HEAD_DIM), na_q_g)
    ka = rms_norm(split_heads(ka, NA_HEADS, HEAD_DIM), na_k_g)
    va = split_heads(va, NA_HEADS, HEAD_DIM)
    qb = rms_norm(qb.reshape(B, S, DIFF_HEADS, 2, HEAD_DIM).transpose(0, 2, 3, 1, 4), diff_q_g)
    kb = rms_norm(kb.reshape(B, S, DIFF_HEADS, 2, HEAD_DIM).transpose(0, 2, 3, 1, 4), diff_k_g)
    vb = split_heads(vb, DIFF_HEADS, DIFF_V_DIM)
    return qa, ka, va, qb, kb, vb


def even_mixer(h, hc, w_in, w_out, na_q_g, na_k_g, na_rpb, diff_q_g, diff_k_g,
               lq1, lk1, lq2, lk2, subln_g, layer_idx, need_ctx):
    S = h.shape[1]
    pos = jnp.arange(S)
    rows, cols = pos // GRID_W, pos % GRID_W
    scale = HEAD_DIM ** -0.5
    lam_init = 0.8 - 0.6 * math.exp(-0.3 * layer_idx)
    lam = (jnp.exp(jnp.sum(lq1.astype(jnp.float32) * lk1.astype(jnp.float32)))
           - jnp.exp(jnp.sum(lq2.astype(jnp.float32) * lk2.astype(jnp.float32))) + lam_init)
    qa, ka, va, qb, kb, vb = even_project(h, w_in, na_q_g, na_k_g, diff_q_g, diff_k_g)
    qca, kca, vca, qcb, kcb, vcb = even_project(hc, w_in, na_q_g, na_k_g, diff_q_g, diff_k_g)
    out_a = neighbourhood_attention(qa, ka, va, kca, vca, na_rpb, scale)
    qb = rope_2d(qb, rows, cols)
    kb = rope_2d(kb, rows, cols)
    kb_all = jnp.concatenate([kb, kcb], axis=3)
    vb_all = jnp.concatenate([vb, vcb], axis=2)
    out_b = over_query_blocks(lambda qi: diff_attend(qi, kb_all, vb_all, lam, scale), qb)
    out_b = rms_norm(out_b, subln_g) * (1.0 - lam_init)
    y = jnp.concatenate([merge_heads(out_a), merge_heads(out_b)], axis=-1) @ w_out
    if not need_ctx:
        return y, None
    oca = softmax_attend(qca[:, :, None], kca, vca, scale)[:, :, 0]
    ocb = rms_norm(diff_attend(qcb, kcb, vcb, lam, scale), subln_g) * (1.0 - lam_init)
    yc = jnp.concatenate([merge_heads(oca), merge_heads(ocb)], axis=-1) @ w_out
    return y, yc


def odd_project(t, w_in, gqa_q_g, gqa_k_g, cq_g, w_uq, ckv_g, w_ukv, mla_q_g, mla_k_g):
    B, S, _ = t.shape
    qc, kc, vc, cq, ckv, kr = jnp.split(t @ w_in, ODD_SPLITS, axis=-1)
    qc = rms_norm(qc.reshape(B, S, GQA_KV_HEADS, GQA_GROUP, HEAD_DIM).transpose(0, 2, 3, 1, 4), gqa_q_g)
    kc = rms_norm(split_heads(kc, GQA_KV_HEADS, HEAD_DIM), gqa_k_g)
    vc = split_heads(vc, GQA_KV_HEADS, HEAD_DIM)
    qd = split_heads(rms_norm(cq, cq_g) @ w_uq, MLA_HEADS, MLA_QK)
    kv = split_heads(rms_norm(ckv, ckv_g) @ w_ukv, MLA_HEADS, MLA_NOPE + MLA_V)
    kr = jnp.broadcast_to(split_heads(kr, 1, MLA_ROPE), (B, MLA_HEADS, S, MLA_ROPE))
    kd = jnp.concatenate([kv[..., :MLA_NOPE], kr], axis=-1)
    vd = kv[..., MLA_NOPE:]
    return qc, kc, vc, rms_norm(qd, mla_q_g), rms_norm(kd, mla_k_g), vd


def odd_mixer(h, hc, w_in, w_out, gqa_q_g, gqa_k_g, cq_g, w_uq, ckv_g, w_ukv, mla_q_g, mla_k_g, need_ctx):
    S = h.shape[1]
    pos = jnp.arange(S)
    rows, cols = pos // GRID_W, pos % GRID_W
    qc, kc, vc, qd, kd, vd = odd_project(h, w_in, gqa_q_g, gqa_k_g, cq_g, w_uq, ckv_g, w_ukv, mla_q_g, mla_k_g)
    qcc, kcc, vcc, qcd, kcd, vcd = odd_project(hc, w_in, gqa_q_g, gqa_k_g, cq_g, w_uq, ckv_g, w_ukv, mla_q_g, mla_k_g)
    scale_c = HEAD_DIM ** -0.5
    qc = rope_2d(qc, rows, cols)
    kc_all = jnp.concatenate([rope_2d(kc, rows, cols), kcc], axis=2)
    vc_all = jnp.concatenate([vc, vcc], axis=2)
    out_c = over_query_blocks(lambda qi: softmax_attend(qi, kc_all, vc_all, scale_c), qc)
    scale_d = MLA_QK ** -0.5

    def rope_mla(t):
        return jnp.concatenate([t[..., :MLA_NOPE], rope_2d(t[..., MLA_NOPE:], rows, cols)], axis=-1)

    qd = rope_mla(qd)[:, :, None]
    kd_all = jnp.concatenate([rope_mla(kd), kcd], axis=2)
    vd_all = jnp.concatenate([vd, vcd], axis=2)
    out_d = over_query_blocks(lambda qi: softmax_attend(qi, kd_all, vd_all, scale_d), qd)
    y = jnp.concatenate([merge_heads(out_c), merge_heads(out_d)], axis=-1) @ w_out
    if not need_ctx:
        return y, None
    occ = softmax_attend(qcc, kcc, vcc, scale_c)
    ocd = softmax_attend(qcd[:, :, None], kcd, vcd, scale_d)
    yc = jnp.concatenate([merge_heads(occ), merge_heads(ocd)], axis=-1) @ w_out
    return y, yc


def swiglu(h, w_gate, w_up, w_down):
    return (jax.nn.silu(h @ w_gate) * (h @ w_up)) @ w_down


def moe_swiglu(h, router, w1, w3, w2):
    B, S, D = h.shape
    t = h.reshape(-1, D)
    N = t.shape[0]
    logits = (t @ router).astype(jnp.float32)
    top_val, top_idx = lax.top_k(logits, TOP_K)
    gates = jax.nn.softmax(top_val, axis=-1)
    flat_e = top_idx.reshape(-1)
    flat_tok = jnp.repeat(jnp.arange(N), TOP_K)
    flat_g = gates.reshape(-1)
    order = jnp.argsort(flat_e)
    e_sorted, tok_sorted, g_sorted = flat_e[order], flat_tok[order], flat_g[order]
    counts = jnp.bincount(flat_e, length=N_EXPERTS)
    padded = ((counts + MOE_BLOCK - 1) // MOE_BLOCK) * MOE_BLOCK
    bounds = jnp.cumsum(padded)
    pad_start = bounds - padded
    start = jnp.cumsum(counts) - counts
    dest = pad_start[e_sorted] + jnp.arange(N * TOP_K) - start[e_sorted]
    cap = (-(-(N * TOP_K) // MOE_BLOCK)) * MOE_BLOCK + N_EXPERTS * MOE_BLOCK
    nb = cap // MOE_BLOCK
    slot_tok = jnp.zeros((cap,), jnp.int32).at[dest].set(tok_sorted.astype(jnp.int32))
    slot_g = jnp.zeros((cap,), jnp.float32).at[dest].set(g_sorted)
    block_e = jnp.minimum(jnp.sum(jnp.arange(nb)[:, None] * MOE_BLOCK >= bounds[None, :], axis=1), N_EXPERTS - 1)
    xb = t[slot_tok].reshape(nb, MOE_BLOCK, D)

    def one_block(args):
        xi, e = args
        return (jax.nn.silu(xi @ w1[e]) * (xi @ w3[e])) @ w2[e]

    yb = lax.map(one_block, (xb, block_e)).reshape(cap, D)
    out = jnp.zeros_like(t).at[slot_tok].add(yb * slot_g[:, None].astype(yb.dtype))
    return out.reshape(B, S, D)


def setup_inputs(seed: int = 0) -> dict:
    key = jax.random.key(seed)
    keys = iter(jax.random.split(key, 48))

    def normal(shape, scale):
        return jax.random.normal(next(keys), shape, dtype=jnp.float32) * scale

    def gain(shape):
        return 1.0 + normal(shape, 0.05)

    D = D_MODEL
    return {
        'x': normal((BATCH, SEQ, D), 1.0),
        'c': normal((BATCH, D), 1.0),
        'ctx': normal((BATCH, CTX_LEN, D), 1.0),
        'c_ctx': normal((D,), 1.0),
        'ada_w': normal((DEPTH, D, ADA_CHUNKS * D), 0.5 * D ** -0.5),
        'ada_b': normal((DEPTH, ADA_CHUNKS * D), 0.02),
        'norm1_g': gain((DEPTH, D)),
        'norm2_g': gain((DEPTH, D)),
        'ev_w_in': normal((N_EVEN, D, EVEN_IN), D ** -0.5),
        'ev_w_out': normal((N_EVEN, EVEN_OUT, D), EVEN_OUT ** -0.5),
        'na_q_g': gain((N_EVEN, HEAD_DIM)),
        'na_k_g': gain((N_EVEN, HEAD_DIM)),
        'na_rpb': normal((N_EVEN, NA_HEADS, 2 * NA_WIN_H - 1, 2 * NA_WIN_W - 1), 0.1),
        'diff_q_g': gain((N_EVEN, HEAD_DIM)),
        'diff_k_g': gain((N_EVEN, HEAD_DIM)),
        'diff_lq1': normal((N_EVEN, HEAD_DIM), 0.1),
        'diff_lk1': normal((N_EVEN, HEAD_DIM), 0.1),
        'diff_lq2': normal((N_EVEN, HEAD_DIM), 0.1),
        'diff_lk2': normal((N_EVEN, HEAD_DIM), 0.1),
        'diff_subln_g': gain((N_EVEN, DIFF_V_DIM)),
        'ffn_w_gate': normal((N_EVEN, D, D_FF), D ** -0.5),
        'ffn_w_up': normal((N_EVEN, D, D_FF), D ** -0.5),
        'ffn_w_down': normal((N_EVEN, D_FF, D), D_FF ** -0.5),
        'od_w_in': normal((N_ODD, D, ODD_IN), D ** -0.5),
        'od_w_out': normal((N_ODD, ODD_OUT, D), ODD_OUT ** -0.5),
        'gqa_q_g': gain((N_ODD, HEAD_DIM)),
        'gqa_k_g': gain((N_ODD, HEAD_DIM)),
        'mla_cq_g': gain((N_ODD, MLA_Q_RANK)),
        'mla_w_uq': normal((N_ODD, MLA_Q_RANK, MLA_HEADS * MLA_QK), MLA_Q_RANK ** -0.5),
        'mla_ckv_g': gain((N_ODD, MLA_KV_RANK)),
        'mla_w_ukv': normal((N_ODD, MLA_KV_RANK, MLA_HEADS * (MLA_NOPE + MLA_V)), MLA_KV_RANK ** -0.5),
        'mla_q_g': gain((N_ODD, MLA_QK)),
        'mla_k_g': gain((N_ODD, MLA_QK)),
        'moe_router': normal((N_ODD, D, N_EXPERTS), D ** -0.5),
        'moe_w1': normal((N_ODD, N_EXPERTS, D, D_FF_EXPERT), D ** -0.5),
        'moe_w3': normal((N_ODD, N_EXPERTS, D, D_FF_EXPERT), D ** -0.5),
        'moe_w2': normal((N_ODD, N_EXPERTS, D_FF_EXPERT, D), D_FF_EXPERT ** -0.5),
    }


def reference(x, c, ctx, c_ctx, ada_w, ada_b, norm1_g, norm2_g,
              ev_w_in, ev_w_out, na_q_g, na_k_g, na_rpb, diff_q_g, diff_k_g,
              diff_lq1, diff_lk1, diff_lq2, diff_lk2, diff_subln_g,
              ffn_w_gate, ffn_w_up, ffn_w_down,
              od_w_in, od_w_out, gqa_q_g, gqa_k_g, mla_cq_g, mla_w_uq, mla_ckv_g, mla_w_ukv,
              mla_q_g, mla_k_g, moe_router, moe_w1, moe_w3, moe_w2):
    xc = ctx
    silu_c = jax.nn.silu(c)
    silu_cc = jax.nn.silu(c_ctx)
    for l in range(DEPTH):
        need_ctx = l < DEPTH - 1
        i = l // 2
        even = l % 2 == 0
        sh1, sc1, g1, sh2, sc2, g2 = jnp.split((silu_c @ ada_w[l] + ada_b[l])[:, None, :], ADA_CHUNKS, axis=-1)
        csh1, csc1, cg1, csh2, csc2, cg2 = jnp.split(silu_cc @ ada_w[l] + ada_b[l], ADA_CHUNKS, axis=-1)
        h = modulate(rms_norm(x, norm1_g[l]), sh1, sc1)
        hc = modulate(rms_norm(xc, norm1_g[l]), csh1, csc1)
        if even:
            y, yc = even_mixer(h, hc, ev_w_in[i], ev_w_out[i], na_q_g[i], na_k_g[i], na_rpb[i],
                               diff_q_g[i], diff_k_g[i], diff_lq1[i], diff_lk1[i], diff_lq2[i], diff_lk2[i],
                               diff_subln_g[i], l, need_ctx)
        else:
            y, yc = odd_mixer(h, hc, od_w_in[i], od_w_out[i], gqa_q_g[i], gqa_k_g[i], mla_cq_g[i], mla_w_uq[i],
                              mla_ckv_g[i], mla_w_ukv[i], mla_q_g[i], mla_k_g[i], need_ctx)

        def channel_mix(t):
            if even:
                return swiglu(t, ffn_w_gate[i], ffn_w_up[i], ffn_w_down[i])
            return moe_swiglu(t, moe_router[i], moe_w1[i], moe_w3[i], moe_w2[i])

        x = x + g1 * y
        x = x + g2 * channel_mix(modulate(rms_norm(x, norm2_g[l]), sh2, sc2))
        if need_ctx:
            xc = xc + cg1 * yc
            xc = xc + cg2 * channel_mix(modulate(rms_norm(xc, norm2_g[l]), csh2, csc2))
    return x
```

```python
import functools
import math

import jax
import jax.numpy as jnp
from jax import lax
from jax.experimental import pallas as pl
from jax.experimental.pallas import tpu as pltpu

F32 = jnp.float32
BF16 = jnp.bfloat16
HIGHEST = lax.Precision.HIGHEST

D_MODEL = 1024
GRID_W = 64
HEAD_DIM = 64
ROPE_THETA = 10000.0
NORM_EPS = 1e-6
NEG_INF = -1e30
ADA_CHUNKS = 6
LOG2E = 1.4426950408889634

NA_HEADS = 8
NA_WIN_H = 8
NA_WIN_W = 16
DIFF_HEADS = 4
DIFF_V_DIM = 2 * HEAD_DIM
GQA_Q_HEADS = 8
GQA_KV_HEADS = 2
MLA_HEADS = 8
MLA_NOPE = 64
MLA_ROPE = 32
MLA_QK = MLA_NOPE + MLA_ROPE
MLA_V = 64
MLA_Q_RANK = 256
MLA_KV_RANK = 128
D_FF = 2816
N_EXPERTS = 8
D_FF_EXPERT = 3584

LANES = 128
MXU_DIM = 256
VMEM_LIMIT = 56 * 1024 * 1024
MOD_ROWS = 16
MOE_TM = 256
MOE_TF = 512


def _cparams(sem):
    return pltpu.CompilerParams(dimension_semantics=sem, vmem_limit_bytes=VMEM_LIMIT)


def _dot(a, b):
    return jnp.dot(a, b, preferred_element_type=F32)


def _dot_nt(a, b):
    return lax.dot_general(a, b, (((1,), (1,)), ((), ())), preferred_element_type=F32)


def _silu(x):
    return x * (1.0 / (1.0 + jnp.exp(-x)))


def _modvec_kernel(c_ref, w_ref, b_ref, o_ref):
    s = _silu(c_ref[...])
    o_ref[...] = jnp.dot(s, w_ref[...], preferred_element_type=F32, precision=HIGHEST) + b_ref[...]


def _modvec(cc, ada_w, ada_b):
    n_layers, d, n = ada_w.shape
    tn = 768
    return pl.pallas_call(
        _modvec_kernel,
        out_shape=jax.ShapeDtypeStruct((n_layers, MOD_ROWS, n), F32),
        grid=(n_layers, n // tn),
        in_specs=[
            pl.BlockSpec((MOD_ROWS, d), lambda l, j: (0, 0)),
            pl.BlockSpec((None, d, tn), lambda l, j: (l, 0, j)),
            pl.BlockSpec((None, 1, tn), lambda l, j: (l, 0, j)),
        ],
        out_specs=pl.BlockSpec((None, MOD_ROWS, tn), lambda l, j: (l, 0, j)),
        compiler_params=_cparams(("arbitrary", "arbitrary")),
        name="modvec",
    )(cc, ada_w, ada_b)


def _norm_mod(x, g, shift, scale):
    ms = jnp.mean(x * x, axis=-1, keepdims=True)
    return (x * lax.rsqrt(ms + NORM_EPS) * g) * (1.0 + scale) + shift


def _group_norm(t, e, g, inv_n):
    ss = _dot((t * t).astype(BF16), e)
    return t * lax.rsqrt(ss * inv_n + NORM_EPS) * g


def _rope_chunks(t, cos, sin, lane_hi, shift):
    outs = []
    for c in range(t.shape[1] // LANES):
        tc = t[:, c * LANES:(c + 1) * LANES]
        up = pltpu.roll(tc, LANES - shift, 1)
        dn = pltpu.roll(tc, shift, 1)
        outs.append(tc * cos + jnp.where(lane_hi, dn, up) * sin)
    return outs[0] if len(outs) == 1 else jnp.concatenate(outs, axis=1)


def _rope_tables(seq, unit, pair_shift):
    pos = jnp.arange(seq)
    rows, cols = pos // GRID_W, pos % GRID_W
    quarter = unit // 4
    assert quarter == pair_shift
    freqs = ROPE_THETA ** (-jnp.arange(quarter, dtype=F32) / quarter)
    lane = jnp.arange(LANES)
    u = lane % unit
    use_col = (u // (unit // 2)) == 1
    fi = u % quarter
    p = jnp.where(use_col[None, :], cols[:, None], rows[:, None]).astype(F32)
    ang = p * freqs[fi][None, :]
    second = ((u % (unit // 2)) // quarter) == 1
    cos = jnp.cos(ang)
    sin = jnp.where(second[None, :], jnp.sin(ang), -jnp.sin(ang))
    return cos, sin


def _even_proj_kernel(x_ref, sh_ref, sc_ref, ng_ref, w_ref, e_ref, hg_ref, cos_ref, sin_ref,
                      qa_ref, ka_ref, va_ref, qb_ref, kb_ref, vb_ref, *, use_rope):
    h = _norm_mod(x_ref[0], ng_ref[...], sh_ref[0], sc_ref[0])
    p = _dot(h.astype(BF16), w_ref[...])
    hg = hg_ref[...]
    e = e_ref[...]
    inv_n = 1.0 / HEAD_DIM
    qa = _group_norm(p[:, 0:512], e, hg[0:1], inv_n)
    ka = _group_norm(p[:, 512:1024], e, hg[1:2], inv_n)
    qb = _group_norm(p[:, 1536:2048], e, hg[2:3], inv_n)
    kb = _group_norm(p[:, 2048:2560], e, hg[3:4], inv_n)
    if use_rope:
        lane = lax.broadcasted_iota(jnp.int32, (1, LANES), 1)
        lane_hi = (lane & 16) != 0
        cos, sin = cos_ref[...], sin_ref[...]
        qb = _rope_chunks(qb, cos, sin, lane_hi, 16)
        kb = _rope_chunks(kb, cos, sin, lane_hi, 16)
    qa_ref[0] = qa.astype(BF16)
    ka_ref[0] = ka.astype(BF16)
    va_ref[0] = p[:, 1024:1536].astype(BF16)
    qb_ref[0] = qb.astype(BF16)
    kb_ref[0] = kb.astype(BF16)
    vb_ref[0] = p[:, 2560:3072].astype(BF16)


def _even_proj(x, shift, scale, ng, w, e512, hg, cos, sin, *, use_rope, tm):
    b, s, d = x.shape
    n = w.shape[1]
    row = lambda bi, i: (bi, i, 0)
    vec = lambda bi, i: (bi, 0, 0)
    const = lambda bi, i: (0, 0)
    out = jax.ShapeDtypeStruct((b, s, 512), BF16)
    return pl.pallas_call(
        functools.partial(_even_proj_kernel, use_rope=use_rope),
        out_shape=(out,) * 6,
        grid=(b, s // tm),
        in_specs=[
            pl.BlockSpec((1, tm, d), row),
            pl.BlockSpec((1, 1, d), vec),
            pl.BlockSpec((1, 1, d), vec),
            pl.BlockSpec((1, d), const),
            pl.BlockSpec((d, n), const),
            pl.BlockSpec((512, 512), const),
            pl.BlockSpec((4, 512), const),
            pl.BlockSpec((tm, LANES), lambda bi, i: (i, 0)),
            pl.BlockSpec((tm, LANES), lambda bi, i: (i, 0)),
        ],
        out_specs=(pl.BlockSpec((1, tm, 512), row),) * 6,
        compiler_params=_cparams(("parallel", "parallel")),
        name="even_proj",
    )(x, shift, scale, ng, w, e512, hg, cos, sin)


def _pair_stack(q):
    lane = lax.broadcasted_iota(jnp.int32, q.shape, 1)
    zero = jnp.zeros_like(q)
    return jnp.concatenate([jnp.where(lane < HEAD_DIM, q, zero), jnp.where(lane >= HEAD_DIM, q, zero)], axis=0)


def _pair_merge(o, tq):
    lane = lax.broadcasted_iota(jnp.int32, (tq, LANES), 1)
    return jnp.where(lane < HEAD_DIM, o[:tq], o[tq:])


def _na_kernel(q_ref, k_ref, v_ref, kc_ref, vc_ref, bias_ref, o_ref, *, rows_per_step, n_rows):
    rb = pl.program_id(2)
    kc = kc_ref[0]
    vc = vc_ref[0]
    n_lat = NA_WIN_H * GRID_W
    for i in range(rows_per_step):
        r = rb * rows_per_step + i
        r0 = jnp.clip(r - NA_WIN_H // 2, 0, n_rows - NA_WIN_H)
        start = pl.multiple_of(r0 * GRID_W, GRID_W)
        kw = k_ref[0, pl.ds(start, n_lat), :]
        vw = v_ref[0, pl.ds(start, n_lat), :]
        qs = _pair_stack(q_ref[0, i * GRID_W:(i + 1) * GRID_W, :])
        s_lat = _dot_nt(qs, kw) + bias_ref[r - r0]
        s_ctx = _dot_nt(qs, kc)
        m = jnp.maximum(jnp.max(s_lat, axis=-1, keepdims=True), jnp.max(s_ctx, axis=-1, keepdims=True))
        p_lat = jnp.exp2(s_lat - m)
        p_ctx = jnp.exp2(s_ctx - m)
        l = jnp.sum(p_lat, axis=-1, keepdims=True) + jnp.sum(p_ctx, axis=-1, keepdims=True)
        o = _dot(p_lat.astype(BF16), vw) + _dot(p_ctx.astype(BF16), vc)
        o = o / l
        o_ref[0, i * GRID_W:(i + 1) * GRID_W, :] = _pair_merge(o, GRID_W).astype(BF16)


def _na_attention(q, k, v, kc, vc, bias, *, rows_per_step=8):
    b, s, _ = q.shape
    c = kc.shape[1]
    n_rows = s // GRID_W
    tq = rows_per_step * GRID_W
    return pl.pallas_call(
        functools.partial(_na_kernel, rows_per_step=rows_per_step, n_rows=n_rows),
        out_shape=jax.ShapeDtypeStruct((b, s, 512), BF16),
        grid=(b, 4, n_rows // rows_per_step),
        in_specs=[
            pl.BlockSpec((1, tq, LANES), lambda bi, j, i: (bi, i, j)),
            pl.BlockSpec((1, s, LANES), lambda bi, j, i: (bi, 0, j)),
            pl.BlockSpec((1, s, LANES), lambda bi, j, i: (bi, 0, j)),
            pl.BlockSpec((1, c, LANES), lambda bi, j, i: (bi, 0, j)),
            pl.BlockSpec((1, c, LANES), lambda bi, j, i: (bi, 0, j)),
            pl.BlockSpec((NA_WIN_H, None, LANES, NA_WIN_H * GRID_W), lambda bi, j, i: (0, j, 0, 0)),
        ],
        out_specs=pl.BlockSpec((1, tq, LANES), lambda bi, j, i: (bi, i, j)),
        compiler_params=_cparams(("parallel", "parallel", "parallel")),
        name="na_attention",
    )(q, k, v, kc, vc, bias)


def _na_bias_table(rpb):
    col = jnp.arange(GRID_W)
    c0 = jnp.clip(col - NA_WIN_W // 2, 0, GRID_W - NA_WIN_W)
    col_in = (col[None, :] >= c0[:, None]) & (col[None, :] < c0[:, None] + NA_WIN_W)
    dc = jnp.clip(col[None, :] - col[:, None], -(NA_WIN_W - 1), NA_WIN_W - 1) + NA_WIN_W - 1
    variants = []
    for v in range(NA_WIN_H):
        dr = jnp.arange(NA_WIN_H) - v + NA_WIN_H - 1
        bias = rpb[:, dr[None, :, None], dc[:, None, :]] * LOG2E
        bias = jnp.where(col_in[None, :, None, :], bias, NEG_INF)
        variants.append(bias.reshape(NA_HEADS // 2, 2 * GRID_W, NA_WIN_H * GRID_W))
    return jnp.stack(variants, axis=0).astype(F32)


def _flash_kernel(*refs, mode, tq, tk, s_len, c_len, lam_init):
    if mode == "diff":
        q_ref, k_ref, v_ref, kc_ref, vc_ref, lam_ref, sg_ref, o_ref = refs
    else:
        q_ref, k_ref, v_ref, kc_ref, vc_ref, o_ref = refs

    q = q_ref[0]
    if mode == "mla":
        q_parts = (q[:, :LANES], q[:, LANES:])
    else:
        qs = _pair_stack(q)

    def scores(kt):
        if mode == "mla":
            return jnp.concatenate([_dot_nt(q_parts[0], kt[:, :LANES]), _dot_nt(q_parts[1], kt[:, LANES:])], axis=0)
        return _dot_nt(qs, kt)

    def step(kt, vt, carry):
        m, l, acc = carry
        s = scores(kt)
        m_new = jnp.maximum(m, jnp.max(s, axis=-1, keepdims=True))
        alpha = jnp.exp2(m - m_new)
        p = jnp.exp2(s - m_new)
        l = alpha * l + jnp.sum(p, axis=-1, keepdims=True)
        acc = alpha * acc + _dot(p.astype(BF16), vt)
        return m_new, l, acc

    def body(t, carry):
        start = pl.multiple_of(t * tk, tk)
        return step(k_ref[0, pl.ds(start, tk), :], v_ref[0, pl.ds(start, tk), :], carry)

    carry = (jnp.full((2 * tq, 1), -jnp.inf, F32), jnp.zeros((2 * tq, 1), F32), jnp.zeros((2 * tq, LANES), F32))
    if s_len:
        carry = lax.fori_loop(0, s_len // tk, body, carry)
    if c_len:
        carry = step(kc_ref[0], vc_ref[0], carry)
    _, l, acc = carry
    o = acc / l
    if mode == "diff":
        lp = lam_ref[...]
        lam = (jnp.exp(jnp.sum(lp[0:1] * lp[1:2], axis=-1, keepdims=True))
               - jnp.exp(jnp.sum(lp[2:3] * lp[3:4], axis=-1, keepdims=True)) + lam_init)
        d = o[:tq] - lam * o[tq:]
        ms = jnp.mean(d * d, axis=-1, keepdims=True)
        o_ref[0] = (d * lax.rsqrt(ms + NORM_EPS) * sg_ref[...] * (1.0 - lam_init)).astype(BF16)
    else:
        o_ref[0] = _pair_merge(o, tq).astype(BF16)


def _flash(q, k, v, kc, vc, *, mode, kv_map, n_chunks, tq, tk, extra=(), lam_init=0.0):
    b, s, _ = q.shape
    qw = 2 * LANES if mode == "mla" else LANES
    c_len = kc.shape[1]
    if k is None:
        k, v, s_len = kc, vc, 0
    else:
        s_len = k.shape[1]
    kk = k.shape[1]
    in_specs = [
        pl.BlockSpec((1, tq, qw), lambda bi, j, i: (bi, i, j)),
        pl.BlockSpec((1, kk, qw), lambda bi, j, i: (bi, 0, kv_map(j) if mode != "mla" else j)),
        pl.BlockSpec((1, kk, LANES), lambda bi, j, i: (bi, 0, kv_map(j))),
        pl.BlockSpec((1, c_len, qw), lambda bi, j, i: (bi, 0, kv_map(j) if mode != "mla" else j)),
        pl.BlockSpec((1, c_len, LANES), lambda bi, j, i: (bi, 0, kv_map(j))),
    ]
    for e in extra:
        in_specs.append(pl.BlockSpec(e.shape, lambda bi, j, i: (0, 0)))
    return pl.pallas_call(
        functools.partial(_flash_kernel, mode=mode, tq=tq, tk=tk, s_len=s_len, c_len=c_len, lam_init=lam_init),
        out_shape=jax.ShapeDtypeStruct((b, s, n_chunks * LANES), BF16),
        grid=(b, n_chunks, s // tq),
        in_specs=in_specs,
        out_specs=pl.BlockSpec((1, tq, LANES), lambda bi, j, i: (bi, i, j)),
        compiler_params=_cparams(("parallel", "parallel", "parallel")),
        name="flash_" + mode,
    )(q, k, v, kc, vc, *extra)


def _post_attn_kernel(oa_ref, ob_ref, wa_ref, wb_ref, x_ref, g_ref, ng_ref, sh_ref, sc_ref, x1_ref, h2_ref):
    y = _dot(oa_ref[0], wa_ref[...]) + _dot(ob_ref[0], wb_ref[...])
    x1 = x_ref[0] + g_ref[0] * y
    x1_ref[0] = x1
    h2_ref[0] = _norm_mod(x1, ng_ref[...], sh_ref[0], sc_ref[0]).astype(BF16)


def _post_attn(oa, ob, wa, wb, x, gate, ng, shift, scale, *, tm):
    b, s, d = x.shape
    row = lambda bi, i: (bi, i, 0)
    vec = lambda bi, i: (bi, 0, 0)
    const = lambda bi, i: (0, 0)
    return pl.pallas_call(
        _post_attn_kernel,
        out_shape=(jax.ShapeDtypeStruct((b, s, d), F32), jax.ShapeDtypeStruct((b, s, d), BF16)),
        grid=(b, s // tm),
        in_specs=[
            pl.BlockSpec((1, tm, 512), row),
            pl.BlockSpec((1, tm, 512), row),
            pl.BlockSpec((512, d), const),
            pl.BlockSpec((512, d), const),
            pl.BlockSpec((1, tm, d), row),
            pl.BlockSpec((1, 1, d), vec),
            pl.BlockSpec((1, d), const),
            pl.BlockSpec((1, 1, d), vec),
            pl.BlockSpec((1, 1, d), vec),
        ],
        out_specs=(pl.BlockSpec((1, tm, d), row), pl.BlockSpec((1, tm, d), row)),
        compiler_params=_cparams(("parallel", "parallel")),
        name="post_attn",
    )(oa, ob, wa, wb, x, gate, ng, shift, scale)


def _ffn_kernel(h_ref, x1_ref, g_ref, wg_ref, wu_ref, wd_ref, o_ref, *, tf):
    h = h_ref[0]
    acc = jnp.zeros(o_ref.shape[1:], F32)
    for f in range(wg_ref.shape[1] // tf):
        sl = slice(f * tf, (f + 1) * tf)
        a = _dot(h, wg_ref[:, sl])
        u = _dot(h, wu_ref[:, sl])
        acc = acc + _dot((_silu(a) * u).astype(BF16), wd_ref[sl, :])
    o_ref[0] = x1_ref[0] + g_ref[0] * acc


def _ffn(h2, x1, gate, wg, wu, wd, *, tm, tf=256):
    b, s, d = x1.shape
    f = wg.shape[1]
    row = lambda bi, i: (bi, i, 0)
    vec = lambda bi, i: (bi, 0, 0)
    const = lambda bi, i: (0, 0)
    return pl.pallas_call(
        functools.partial(_ffn_kernel, tf=tf),
        out_shape=jax.ShapeDtypeStruct((b, s, d), F32),
        grid=(b, s // tm),
        in_specs=[
            pl.BlockSpec((1, tm, d), row),
            pl.BlockSpec((1, tm, d), row),
            pl.BlockSpec((1, 1, d), vec),
            pl.BlockSpec((d, f), const),
            pl.BlockSpec((d, f), const),
            pl.BlockSpec((f, d), const),
        ],
        out_specs=pl.BlockSpec((1, tm, d), row),
        compiler_params=_cparams(("parallel", "parallel")),
        name="ffn",
    )(h2, x1, gate, wg, wu, wd)


def _rms(t, g):
    return t * lax.rsqrt(jnp.mean(t * t, axis=-1, keepdims=True) + NORM_EPS) * g


def _head_norm_128(t, a_ref, at_ref, g, inv_n):
    ss = _dot((t * t).astype(BF16), a_ref[...])
    r = lax.rsqrt(ss * inv_n + NORM_EPS)
    r_hi = r.astype(BF16)
    r_lo = (r - r_hi.astype(F32)).astype(BF16)
    rb = _dot(r_hi, at_ref[...]) + _dot(r_lo, at_ref[...])
    return t * rb * g


def _odd_proj_kernel(x_ref, sh_ref, sc_ref, ng_ref, w_ref, wuq_ref, wukv_ref, e_ref, a_ref, at_ref, g_ref,
                     cos_ref, sin_ref, cosm_ref, sinm_ref,
                     qc_ref, kc_ref, vc_ref, qd_ref, kd_ref, vd_ref, *, use_rope):
    h = _norm_mod(x_ref[0], ng_ref[...], sh_ref[0], sc_ref[0])
    p = _dot(h.astype(BF16), w_ref[...])
    g = g_ref[...]
    e = e_ref[...]
    inv_n = 1.0 / HEAD_DIM
    qc = _group_norm(p[:, 0:512], e, g[0:1, 0:512], inv_n)
    kc = _group_norm(p[:, 512:768], e[0:256, 0:256], g[1:2, 0:256], inv_n)
    cq = _rms(p[:, 1024:1280], g[2:3, 0:256])
    ckv = _rms(p[:, 1280:1408], g[3:4, 0:128])
    krc = p[:, 1408:1536]
    qd = _dot(cq.astype(BF16), wuq_ref[...])
    kvv = _dot(ckv.astype(BF16), wukv_ref[...])
    kd = kvv[:, 0:1024] + jnp.concatenate([krc] * MLA_HEADS, axis=1)
    qd = _head_norm_128(qd, a_ref, at_ref, g[4:5], 1.0 / MLA_QK)
    kd = _head_norm_128(kd, a_ref, at_ref, g[5:6], 1.0 / MLA_QK)
    if use_rope:
        lane = lax.broadcasted_iota(jnp.int32, (1, LANES), 1)
        cos, sin = cos_ref[...], sin_ref[...]
        qc = _rope_chunks(qc, cos, sin, (lane & 16) != 0, 16)
        kc = _rope_chunks(kc, cos, sin, (lane & 16) != 0, 16)
        cosm, sinm = cosm_ref[...], sinm_ref[...]
        qd = _rope_chunks(qd, cosm, sinm, (lane & 8) != 0, 8)
        kd = _rope_chunks(kd, cosm, sinm, (lane & 8) != 0, 8)
    qc_ref[0] = qc.astype(BF16)
    kc_ref[0] = kc.astype(BF16)
    vc_ref[0] = p[:, 768:1024].astype(BF16)
    qd_ref[0] = qd.astype(BF16)
    kd_ref[0] = kd.astype(BF16)
    vd_ref[0] = kvv[:, 1024:1536].astype(BF16)


def _odd_proj(x, shift, scale, ng, w, wuq, wukv, e512, a, at, gains, cos, sin, cosm, sinm, *, use_rope, tm):
    b, s, d = x.shape
    row = lambda bi, i: (bi, i, 0)
    vec = lambda bi, i: (bi, 0, 0)
    const = lambda bi, i: (0, 0)
    tab = lambda bi, i: (i, 0)
    widths = (512, 256, 256, 1024, 1024, 512)
    return pl.pallas_call(
        functools.partial(_odd_proj_kernel, use_rope=use_rope),
        out_shape=tuple(jax.ShapeDtypeStruct((b, s, n), BF16) for n in widths),
        grid=(b, s // tm),
        in_specs=[
            pl.BlockSpec((1, tm, d), row),
            pl.BlockSpec((1, 1, d), vec),
            pl.BlockSpec((1, 1, d), vec),
            pl.BlockSpec((1, d), const),
            pl.BlockSpec(w.shape, const),
            pl.BlockSpec(wuq.shape, const),
            pl.BlockSpec(wukv.shape, const),
            pl.BlockSpec(e512.shape, const),
            pl.BlockSpec(a.shape, const),
            pl.BlockSpec(at.shape, const),
            pl.BlockSpec(gains.shape, const),
            pl.BlockSpec((tm, LANES), tab),
            pl.BlockSpec((tm, LANES), tab),
            pl.BlockSpec((tm, LANES), tab),
            pl.BlockSpec((tm, LANES), tab),
        ],
        out_specs=tuple(pl.BlockSpec((1, tm, n), row) for n in widths),
        compiler_params=_cparams(("parallel", "parallel")),
        name="odd_proj",
    )(x, shift, scale, ng, w, wuq, wukv, e512, a, at, gains, cos, sin, cosm, sinm)


def _mla_rope_tables(seq):
    cos, sin = _rope_tables(seq, MLA_ROPE, MLA_ROPE // 4)
    lane = jnp.arange(LANES)
    on = (lane >= MLA_NOPE) & (lane < MLA_QK)
    return jnp.where(on[None, :], cos, 1.0), jnp.where(on[None, :], sin, 0.0)


def _router_kernel(x1_ref, ng_ref, sh_ref, sc_ref, rt_ref, u_ref, ei_ref, gt_ref, rk_ref, cnt_ref, carry_ref):
    first = (pl.program_id(0) == 0) & (pl.program_id(1) == 0)

    @pl.when(first)
    def _():
        carry_ref[...] = jnp.zeros_like(carry_ref)

    h = _norm_mod(x1_ref[0], ng_ref[...], sh_ref[0], sc_ref[0])
    logits = lax.dot_general(rt_ref[...], h, (((1,), (1,)), ((), ())),
                             preferred_element_type=F32, precision=HIGHEST)
    eidx = lax.broadcasted_iota(jnp.int32, logits.shape, 0)
    m1 = jnp.max(logits, axis=0, keepdims=True)
    i1 = jnp.min(jnp.where(logits == m1, eidx, N_EXPERTS), axis=0, keepdims=True)
    rest = jnp.where(eidx == i1, -jnp.inf, logits)
    m2 = jnp.max(rest, axis=0, keepdims=True)
    i2 = jnp.min(jnp.where(rest == m2, eidx, N_EXPERTS), axis=0, keepdims=True)
    e2 = jnp.exp(m2 - m1)
    g1 = 1.0 / (1.0 + e2)
    sel1 = eidx == i1
    sel2 = eidx == i2
    onehot = jnp.where(sel1 | sel2, 1.0, 0.0)
    before = _dot(onehot.astype(BF16), u_ref[...]) + carry_ref[:, 0:1]
    r1 = jnp.sum(jnp.where(sel1, before, 0.0), axis=0, keepdims=True)
    r2 = jnp.sum(jnp.where(sel2, before, 0.0), axis=0, keepdims=True)
    ei_ref[...] = jnp.concatenate([i1, i2], axis=0)
    gt_ref[...] = jnp.concatenate([g1, e2 * g1], axis=0)
    rk_ref[...] = jnp.concatenate([r1, r2], axis=0).astype(jnp.int32)
    total = carry_ref[...] + jnp.sum(onehot, axis=1, keepdims=True)
    carry_ref[...] = total
    cnt_ref[...] = total.astype(jnp.int32)


def _router(x1, ng, shift, scale, router_t, *, tm):
    b, s, d = x1.shape
    n = b * s
    nt = s // tm
    tri = (jnp.arange(tm)[:, None] < jnp.arange(tm)[None, :]).astype(BF16)
    flat = lambda bi, i: (0, bi * nt + i)
    return pl.pallas_call(
        _router_kernel,
        out_shape=(jax.ShapeDtypeStruct((2, n), jnp.int32), jax.ShapeDtypeStruct((2, n), F32),
                   jax.ShapeDtypeStruct((2, n), jnp.int32), jax.ShapeDtypeStruct((N_EXPERTS, LANES), jnp.int32)),
        grid=(b, nt),
        in_specs=[
            pl.BlockSpec((1, tm, d), lambda bi, i: (bi, i, 0)),
            pl.BlockSpec((1, d), lambda bi, i: (0, 0)),
            pl.BlockSpec((1, 1, d), lambda bi, i: (bi, 0, 0)),
            pl.BlockSpec((1, 1, d), lambda bi, i: (bi, 0, 0)),
            pl.BlockSpec((N_EXPERTS, d), lambda bi, i: (0, 0)),
            pl.BlockSpec((tm, tm), lambda bi, i: (0, 0)),
        ],
        out_specs=(pl.BlockSpec((2, tm), flat), pl.BlockSpec((2, tm), flat), pl.BlockSpec((2, tm), flat),
                   pl.BlockSpec((N_EXPERTS, LANES), lambda bi, i: (0, 0))),
        scratch_shapes=[pltpu.VMEM((N_EXPERTS, LANES), F32)],
        compiler_params=_cparams(("arbitrary", "arbitrary")),
        name="router",
    )(x1, ng, shift, scale, router_t, tri)


MOE_SUB = 256
MOE_BLK = 1024
MOE_TC = 256


def _moe_plan(ei, rk, counts, n_tok):
    n_chunks = n_tok // MOE_TC
    cap = 2 * n_tok + N_EXPERTS * MOE_BLK
    nb_sub = cap // MOE_SUB
    nb_blk = cap // MOE_BLK
    padded = ((counts + MOE_BLK - 1) // MOE_BLK) * MOE_BLK
    pad_end = jnp.cumsum(padded)
    pad_start = pad_end - padded
    dest = pad_start[ei] + rk

    onehot = ei.reshape(2, n_chunks, MOE_TC)[..., None] == jnp.arange(N_EXPERTS)
    cnt = jnp.sum(onehot, axis=(0, 2)).astype(jnp.int32)
    cum = jnp.concatenate([jnp.zeros((1, N_EXPERTS), jnp.int32), jnp.cumsum(cnt, axis=0)], axis=0)

    lo = pad_start[None, :] + cum[:-1]
    has = cnt > 0
    blk0 = lo // MOE_SUB
    two = has & ((lo + cnt - 1) // MOE_SUB > blk0)
    blk1 = jnp.minimum(blk0 + 1, nb_sub - 1)
    blk0 = jnp.where(has, blk0, 0)
    c_blk0 = blk0.reshape(-1).astype(jnp.int32)
    c_blk1 = jnp.where(two, blk1, blk0).reshape(-1).astype(jnp.int32)
    c_flag = (has.astype(jnp.int32) + 2 * two.astype(jnp.int32)).reshape(-1)

    sb = jnp.arange(nb_sub)
    e_sb = jnp.minimum(jnp.sum(sb[:, None] * MOE_SUB >= pad_end[None, :], axis=1), N_EXPERTS - 1)
    r0 = sb * MOE_SUB - pad_start[e_sb]
    valid_sb = (sb * MOE_SUB < pad_end[-1]) & (r0 < counts[e_sb])
    r1 = jnp.minimum(r0 + MOE_SUB, counts[e_sb]) - 1
    cum_sb = cum[1:, :].T[e_sb]
    cmin = jnp.sum(cum_sb <= r0[:, None], axis=1)
    cmax = jnp.sum(cum_sb <= r1[:, None], axis=1)
    items = jnp.where(valid_sb, cmax - cmin + 1, 0)
    off_end = jnp.cumsum(items)
    off = off_end - items
    total = off_end[-1]
    w_max = (2 * n_tok) // MOE_SUB + N_EXPERTS + N_EXPERTS * (n_chunks - 1)
    w = jnp.arange(w_max)
    wv = w < total
    wq = jnp.minimum(w, total - 1)
    w_sb = jnp.minimum(jnp.sum(off_end[None, :] <= wq[:, None], axis=1), nb_sub - 1)
    w_chunk = cmin[w_sb] + (wq - off[w_sb])
    w_flag = wv.astype(jnp.int32) + 2 * (wq == off[w_sb]).astype(jnp.int32)

    bi = jnp.arange(nb_blk)
    e_blk = jnp.minimum(jnp.sum(bi[:, None] * MOE_BLK >= pad_end[None, :], axis=1), N_EXPERTS - 1)
    rows = jnp.where(bi * MOE_BLK < pad_end[-1], counts[e_blk] - (bi * MOE_BLK - pad_start[e_blk]), 0)
    n_sub = jnp.clip((rows + MOE_SUB - 1) // MOE_SUB, 0, MOE_BLK // MOE_SUB)
    return dict(dest=dest.astype(jnp.int32), cap=cap,
                w_sb=w_sb.astype(jnp.int32), w_chunk=w_chunk.astype(jnp.int32), w_flag=w_flag,
                e_blk=e_blk.astype(jnp.int32), n_sub=n_sub.astype(jnp.int32),
                c_blk0=c_blk0, c_blk1=c_blk1, c_flag=c_flag)


def _moe_gather_kernel(wsb_ref, wch_ref, wfl_ref, t_ref, d_ref, g_ref, xg_ref, gs_ref):
    w = pl.program_id(0)
    flag = wfl_ref[w]
    base = wsb_ref[w] * MOE_SUB

    def contrib():
        d = d_ref[...]
        g = g_ref[...]
        srow = lax.broadcasted_iota(jnp.int32, (MOE_SUB, MOE_TC), 0) + base
        hit0 = srow == d[0:1]
        hit1 = srow == d[1:2]
        p = jnp.where(hit0 | hit1, 1.0, 0.0).astype(BF16)
        rows = _dot(p, t_ref[...]).astype(BF16)
        gate = jnp.sum(jnp.where(hit0, g[0:1], 0.0) + jnp.where(hit1, g[1:2], 0.0), axis=-1, keepdims=True)
        return rows, gate

    @pl.when(flag == 3)
    def _():
        rows, gate = contrib()
        xg_ref[...] = rows
        gs_ref[...] = gate

    @pl.when(flag == 1)
    def _():
        rows, gate = contrib()
        xg_ref[...] += rows
        gs_ref[...] += gate


def _moe_gather(t, dest, gates, plan):
    n_tok, d = t.shape
    cap = plan["cap"]
    w_max = plan["w_sb"].shape[0]
    grid_spec = pltpu.PrefetchScalarGridSpec(
        num_scalar_prefetch=3,
        grid=(w_max,),
        in_specs=[
            pl.BlockSpec((MOE_TC, d), lambda w, sb, ch, fl: (ch[w], 0)),
            pl.BlockSpec((2, MOE_TC), lambda w, sb, ch, fl: (0, ch[w])),
            pl.BlockSpec((2, MOE_TC), lambda w, sb, ch, fl: (0, ch[w])),
        ],
        out_specs=(pl.BlockSpec((MOE_SUB, d), lambda w, sb, ch, fl: (sb[w], 0)),
                   pl.BlockSpec((MOE_SUB, 1), lambda w, sb, ch, fl: (sb[w], 0))),
    )
    return pl.pallas_call(
        _moe_gather_kernel,
        out_shape=(jax.ShapeDtypeStruct((cap, d), BF16), jax.ShapeDtypeStruct((cap, 1), F32)),
        grid_spec=grid_spec,
        compiler_params=_cparams(("arbitrary",)),
        name="moe_gather",
    )(plan["w_sb"], plan["w_chunk"], plan["w_flag"], t, dest, gates)


def _moe_expert_kernel(eb_ref, ns_ref, x_ref, gs_ref, w1_ref, w3_ref, w2_ref, y_ref, acc_ref):
    i = pl.program_id(0)
    j = pl.program_id(1)
    n_sub = ns_ref[i]
    for sub in range(MOE_BLK // MOE_SUB):
        rows = slice(sub * MOE_SUB, (sub + 1) * MOE_SUB)

        @pl.when(sub < n_sub)
        def _():
            xs = x_ref[rows, :]
            a = _dot(xs, w1_ref[...])
            u = _dot(xs, w3_ref[...])
            part = _dot((_silu(a) * u).astype(BF16), w2_ref[...])

            @pl.when(j == 0)
            def _():
                acc_ref[rows, :] = part

            @pl.when(j > 0)
            def _():
                acc_ref[rows, :] += part

            @pl.when(j == pl.num_programs(1) - 1)
            def _():
                y_ref[rows, :] = (acc_ref[rows, :] * gs_ref[rows, :]).astype(BF16)

        @pl.when((sub >= n_sub) & (j == pl.num_programs(1) - 1))
        def _():
            y_ref[rows, :] = jnp.zeros((MOE_SUB, y_ref.shape[1]), BF16)


def _moe_experts(xg, gs, w1, w3, w2, plan):
    cap, d = xg.shape
    f = w1.shape[2]
    grid_spec = pltpu.PrefetchScalarGridSpec(
        num_scalar_prefetch=2,
        grid=(cap // MOE_BLK, f // MOE_TF),
        in_specs=[
            pl.BlockSpec((MOE_BLK, d), lambda i, j, eb, ns: (i, 0)),
            pl.BlockSpec((MOE_BLK, 1), lambda i, j, eb, ns: (i, 0)),
            pl.BlockSpec((None, d, MOE_TF), lambda i, j, eb, ns: (eb[i], 0, j)),
            pl.BlockSpec((None, d, MOE_TF), lambda i, j, eb, ns: (eb[i], 0, j)),
            pl.BlockSpec((None, MOE_TF, d), lambda i, j, eb, ns: (eb[i], j, 0)),
        ],
        out_specs=pl.BlockSpec((MOE_BLK, d), lambda i, j, eb, ns: (i, 0)),
        scratch_shapes=[pltpu.VMEM((MOE_BLK, d), F32)],
    )
    return pl.pallas_call(
        _moe_expert_kernel,
        out_shape=jax.ShapeDtypeStruct((cap, d), BF16),
        grid_spec=grid_spec,
        compiler_params=_cparams(("arbitrary", "arbitrary")),
        name="moe_experts",
    )(plan["e_blk"], plan["n_sub"], xg, gs, w1, w3, w2)


def _moe_combine_kernel(b0_ref, b1_ref, fl_ref, dt_ref, y0_ref, y1_ref, x1_ref, g_ref, o_ref, acc_ref):
    c = pl.program_id(0)
    e = pl.program_id(1)
    idx = c * N_EXPERTS + e
    flag = fl_ref[idx]

    @pl.when(e == 0)
    def _():
        acc_ref[...] = jnp.zeros_like(acc_ref)

    def window(y_ref, blk):
        d = dt_ref[...]
        scol = lax.broadcasted_iota(jnp.int32, (MOE_TC, MOE_SUB), 1) + blk * MOE_SUB
        p = jnp.where((scol == d[:, 0:1]) | (scol == d[:, 1:2]), 1.0, 0.0).astype(BF16)
        acc_ref[...] += _dot(p, y_ref[...])

    @pl.when((flag & 1) != 0)
    def _():
        window(y0_ref, b0_ref[idx])

    @pl.when((flag & 2) != 0)
    def _():
        window(y1_ref, b1_ref[idx])

    @pl.when(e == N_EXPERTS - 1)
    def _():
        o_ref[...] = x1_ref[...] + g_ref[0] * acc_ref[...]


def _moe_combine(y, dest_t, x1, gate, plan, seq):
    n_tok, d = x1.shape
    n_chunks = n_tok // MOE_TC
    per_b = seq // MOE_TC
    grid_spec = pltpu.PrefetchScalarGridSpec(
        num_scalar_prefetch=3,
        grid=(n_chunks, N_EXPERTS),
        in_specs=[
            pl.BlockSpec((MOE_TC, 2), lambda c, e, b0, b1, fl: (c, 0)),
            pl.BlockSpec((MOE_SUB, d), lambda c, e, b0, b1, fl: (b0[c * N_EXPERTS + e], 0)),
            pl.BlockSpec((MOE_SUB, d), lambda c, e, b0, b1, fl: (b1[c * N_EXPERTS + e], 0)),
            pl.BlockSpec((MOE_TC, d), lambda c, e, b0, b1, fl: (c, 0)),
            pl.BlockSpec((1, 1, d), lambda c, e, b0, b1, fl: (c // per_b, 0, 0)),
        ],
        out_specs=pl.BlockSpec((MOE_TC, d), lambda c, e, b0, b1, fl: (c, 0)),
        scratch_shapes=[pltpu.VMEM((MOE_TC, d), F32)],
    )
    return pl.pallas_call(
        _moe_combine_kernel,
        out_shape=jax.ShapeDtypeStruct((n_tok, d), F32),
        grid_spec=grid_spec,
        compiler_params=_cparams(("arbitrary", "arbitrary")),
        name="moe_combine",
    )(plan["c_blk0"], plan["c_blk1"], plan["c_flag"], dest_t, y, y, x1, gate)


def _split_mod(mod_l, batch):
    d = D_MODEL
    lat = tuple(mod_l[:batch, k * d:(k + 1) * d][:, None, :] for k in range(ADA_CHUNKS))
    ctx = tuple(jnp.broadcast_to(mod_l[batch:batch + 1, k * d:(k + 1) * d][:, None, :], (batch, 1, d))
                for k in range(ADA_CHUNKS))
    return lat, ctx


def _block_ones(n, block):
    idx = jnp.arange(n) // block
    return (idx[:, None] == idx[None, :]).astype(BF16)


def _even_layer(x, xc, mod_l, norm1_g, norm2_g, w_in, w_out, na_q_g, na_k_g, na_rpb, diff_q_g, diff_k_g,
                lq1, lk1, lq2, lk2, subln_g, wg, wu, wd, layer_idx, need_ctx):
    b, s, d = x.shape
    (sh1, sc1, g1, sh2, sc2, g2), (csh1, csc1, cg1, csh2, csc2, cg2) = _split_mod(mod_l, b)
    qscale = HEAD_DIM ** -0.5 * LOG2E
    lam_init = 0.8 - 0.6 * math.exp(-0.3 * layer_idx)
    hg = jnp.stack([jnp.tile(na_q_g, 8) * qscale, jnp.tile(na_k_g, 8),
                    jnp.tile(diff_q_g, 8) * qscale, jnp.tile(diff_k_g, 8)]).astype(F32)
    e512 = _block_ones(512, HEAD_DIM)
    cos, sin = _rope_tables(s, HEAD_DIM, 16)
    ng1 = norm1_g[None, :]
    ng2 = norm2_g[None, :]
    w_in_b = w_in.astype(BF16)
    wo_a = w_out[:512].astype(BF16)
    wo_b = w_out[512:].astype(BF16)
    wg_b, wu_b, wd_b = wg.astype(BF16), wu.astype(BF16), wd.astype(BF16)
    lam_p = jnp.stack([lq1, lk1, lq2, lk2]).astype(F32)
    sg = subln_g[None, :].astype(F32)
    ident = lambda j: j

    tm = min(512, s)
    tc = xc.shape[1]
    qa, ka, va, qb, kb, vb = _even_proj(x, sh1, sc1, ng1, w_in_b, e512, hg, cos, sin, use_rope=True, tm=tm)
    qca, kca, vca, qcb, kcb, vcb = _even_proj(xc, csh1, csc1, ng1, w_in_b, e512, hg, cos, sin, use_rope=False, tm=tc)
    out_a = _na_attention(qa, ka, va, kca, vca, _na_bias_table(na_rpb))
    out_b = _flash(qb, kb, vb, kcb, vcb, mode="diff", kv_map=ident, n_chunks=4, tq=256, tk=512,
                   extra=(lam_p, sg), lam_init=lam_init)
    x1, h2 = _post_attn(out_a, out_b, wo_a, wo_b, x, g1, ng2, sh2, sc2, tm=tm)
    x2 = _ffn(h2, x1, g2, wg_b, wu_b, wd_b, tm=tm)
    if not need_ctx:
        return x2, None
    oca = _flash(qca, None, None, kca, vca, mode="pair", kv_map=ident, n_chunks=4, tq=tc, tk=tc)
    ocb = _flash(qcb, None, None, kcb, vcb, mode="diff", kv_map=ident, n_chunks=4, tq=tc, tk=tc,
                 extra=(lam_p, sg), lam_init=lam_init)
    xc1, hc2 = _post_attn(oca, ocb, wo_a, wo_b, xc, cg1, ng2, csh2, csc2, tm=tc)
    xc2 = _ffn(hc2, xc1, cg2, wg_b, wu_b, wd_b, tm=tc)
    return x2, xc2


def _odd_layer(x, xc, mod_l, norm1_g, norm2_g, w_in, w_out, gqa_q_g, gqa_k_g, cq_g, w_uq, ckv_g, w_ukv,
               mla_q_g, mla_k_g, router, w1, w3, w2):
    b, s, d = x.shape
    n_tok = b * s
    (sh1, sc1, g1, sh2, sc2, g2), (csh1, csc1, _, _, _, _) = _split_mod(mod_l, b)
    ng1 = norm1_g[None, :]
    ng2 = norm2_g[None, :]

    z = lambda n: jnp.zeros((d, n), w_in.dtype)
    k0, k1, v0, v1 = (w_in[:, 512 + 64 * i:576 + 64 * i] for i in range(4))
    w_p = jnp.concatenate([w_in[:, 0:512], k0, k0, k1, k1, v0, v0, v1, v1, w_in[:, 768:1024], w_in[:, 1024:1152],
                           z(MLA_NOPE), w_in[:, 1152:1184], z(LANES - MLA_QK)], axis=1).astype(BF16)
    wuq_p = jnp.pad(w_uq.reshape(MLA_Q_RANK, MLA_HEADS, MLA_QK), ((0, 0), (0, 0), (0, LANES - MLA_QK)))
    wuq_p = wuq_p.reshape(MLA_Q_RANK, MLA_HEADS * LANES).astype(BF16)
    ukv = w_ukv.reshape(MLA_KV_RANK, MLA_HEADS, MLA_NOPE + MLA_V)
    uk = jnp.pad(ukv[:, :, :MLA_NOPE], ((0, 0), (0, 0), (0, LANES - MLA_NOPE))).reshape(MLA_KV_RANK, -1)
    uv = ukv[:, :, MLA_NOPE:].reshape(MLA_KV_RANK, -1)
    wukv_p = jnp.concatenate([uk, uv], axis=1).astype(BF16)
    e512 = _block_ones(512, HEAD_DIM)
    a = (jnp.arange(MLA_HEADS * LANES)[:, None] // LANES == jnp.arange(LANES)[None, :]).astype(BF16)
    at = a.T
    qscale = HEAD_DIM ** -0.5 * LOG2E
    mscale = MLA_QK ** -0.5 * LOG2E
    pad_row = lambda v: jnp.pad(v, (0, MLA_HEADS * LANES - v.shape[0]))
    pad_head = lambda v: jnp.tile(jnp.pad(v, (0, LANES - MLA_QK)), MLA_HEADS)
    gains = jnp.stack([pad_row(jnp.tile(gqa_q_g, 8) * qscale), pad_row(jnp.tile(gqa_k_g, 4)),
                       pad_row(cq_g), pad_row(ckv_g), pad_head(mla_q_g) * mscale, pad_head(mla_k_g),
                       jnp.zeros((MLA_HEADS * LANES,), F32), jnp.zeros((MLA_HEADS * LANES,), F32)]).astype(F32)
    cos, sin = _rope_tables(s, HEAD_DIM, 16)
    cosm, sinm = _mla_rope_tables(s)
    wo_a = w_out[:512].astype(BF16)
    wo_b = w_out[512:].astype(BF16)

    tm = min(512, s)
    tc = xc.shape[1]
    proj = functools.partial(_odd_proj, ng=ng1, w=w_p, wuq=wuq_p, wukv=wukv_p, e512=e512, a=a, at=at, gains=gains,
                             cos=cos, sin=sin, cosm=cosm, sinm=sinm)
    qc, kc, vc, qd, kd, vd = proj(x, sh1, sc1, use_rope=True, tm=tm)
    _, kcc, vcc, _, kcd, vcd = proj(xc, csh1, csc1, use_rope=False, tm=tc)
    out_c = _flash(qc, kc, vc, kcc, vcc, mode="pair", kv_map=lambda j: j // 2, n_chunks=4, tq=256, tk=512)
    out_d = _flash(qd, kd, vd, kcd, vcd, mode="mla", kv_map=lambda j: j, n_chunks=4, tq=256, tk=512)
    x1, h2 = _post_attn(out_c, out_d, wo_a, wo_b, x, g1, ng2, sh2, sc2, tm=tm)

    ei, gt, rk, cnt = _router(x1, ng2, sh2, sc2, router.T.astype(F32), tm=tm)
    plan = _moe_plan(ei, rk, cnt[:, 0], n_tok)
    xg, gs = _moe_gather(h2.reshape(n_tok, d), plan["dest"], gt, plan)
    y = _moe_experts(xg, gs, w1.astype(BF16), w3.astype(BF16), w2.astype(BF16), plan)
    out = _moe_combine(y, plan["dest"].T, x1.reshape(n_tok, d), g2, plan, s)
    return out.reshape(b, s, d)


def _mod_vectors(c, c_ctx, ada_w, ada_b):
    b = c.shape[0]
    cc = jnp.zeros((MOD_ROWS, D_MODEL), F32).at[:b].set(c).at[b].set(c_ctx)
    return _modvec(cc, ada_w, ada_b[:, None, :])


def kernel(x, c, ctx, c_ctx, ada_w, ada_b, norm1_g, norm2_g, ev_w_in, ev_w_out, na_q_g, na_k_g, na_rpb,
           diff_q_g, diff_k_g, diff_lq1, diff_lk1, diff_lq2, diff_lk2, diff_subln_g,
           ffn_w_gate, ffn_w_up, ffn_w_down, od_w_in, od_w_out, gqa_q_g, gqa_k_g, mla_cq_g, mla_w_uq,
           mla_ckv_g, mla_w_ukv, mla_q_g, mla_k_g, moe_router, moe_w1, moe_w3, moe_w2):
    mod = _mod_vectors(c, c_ctx, ada_w, ada_b)
    x, xc = _even_layer(x, ctx, mod[0], norm1_g[0], norm2_g[0], ev_w_in[0], ev_w_out[0], na_q_g[0], na_k_g[0],
                        na_rpb[0], diff_q_g[0], diff_k_g[0], diff_lq1[0], diff_lk1[0], diff_lq2[0], diff_lk2[0],
                        diff_subln_g[0], ffn_w_gate[0], ffn_w_up[0], ffn_w_down[0], 0, True)
    return _odd_layer(x, xc, mod[1], norm1_g[1], norm2_g[1], od_w_in[0], od_w_out[0], gqa_q_g[0], gqa_k_g[0],
                      mla_cq_g[0], mla_w_uq[0], mla_ckv_g[0], mla_w_ukv[0], mla_q_g[0], mla_k_g[0],
                      moe_router[0], moe_w1[0], moe_w3[0], moe_w2[0])
```

```python
import functools
import math

import jax
import jax.numpy as jnp
from jax import lax
from jax.experimental import pallas as pl
from jax.experimental.pallas import tpu as pltpu

F32 = jnp.float32
BF16 = jnp.bfloat16
HIGHEST = lax.Precision.HIGHEST

D_MODEL = 1024
GRID_W = 64
HEAD_DIM = 64
ROPE_THETA = 10000.0
NORM_EPS = 1e-6
NEG_INF = -1e30
ADA_CHUNKS = 6
LOG2E = 1.4426950408889634

NA_HEADS = 8
NA_WIN_H = 8
NA_WIN_W = 16
DIFF_HEADS = 4
DIFF_V_DIM = 2 * HEAD_DIM
GQA_Q_HEADS = 8
GQA_KV_HEADS = 2
MLA_HEADS = 8
MLA_NOPE = 64
MLA_ROPE = 32
MLA_QK = MLA_NOPE + MLA_ROPE
MLA_V = 64
MLA_Q_RANK = 256
MLA_KV_RANK = 128
D_FF = 2816
N_EXPERTS = 8
D_FF_EXPERT = 3584

LANES = 128
MXU_DIM = 256
VMEM_LIMIT = 56 * 1024 * 1024
MOD_ROWS = 16
MOE_TM = 256
MOE_TF = 512


def _cparams(sem):
    return pltpu.CompilerParams(dimension_semantics=sem, vmem_limit_bytes=VMEM_LIMIT)


def _dot(a, b):
    return jnp.dot(a, b, preferred_element_type=F32)


def _dot_nt(a, b):
    return lax.dot_general(a, b, (((1,), (1,)), ((), ())), preferred_element_type=F32)


def _silu(x):
    return x * (1.0 / (1.0 + jnp.exp(-x)))


def _modvec_kernel(c_ref, w_ref, b_ref, o_ref):
    s = _silu(c_ref[...])
    o_ref[...] = jnp.dot(s, w_ref[...], preferred_element_type=F32, precision=HIGHEST) + b_ref[...]


def _modvec(cc, ada_w, ada_b):
    n_layers, d, n = ada_w.shape
    tn = 768
    return pl.pallas_call(
        _modvec_kernel,
        out_shape=jax.ShapeDtypeStruct((n_layers, MOD_ROWS, n), F32),
        grid=(n_layers, n // tn),
        in_specs=[
            pl.BlockSpec((MOD_ROWS, d), lambda l, j: (0, 0)),
            pl.BlockSpec((None, d, tn), lambda l, j: (l, 0, j)),
            pl.BlockSpec((None, 1, tn), lambda l, j: (l, 0, j)),
        ],
        out_specs=pl.BlockSpec((None, MOD_ROWS, tn), lambda l, j: (l, 0, j)),
        compiler_params=_cparams(("arbitrary", "arbitrary")),
        name="modvec",
    )(cc, ada_w, ada_b)


def _norm_mod(x, g, shift, scale):
    ms = jnp.mean(x * x, axis=-1, keepdims=True)
    return (x * lax.rsqrt(ms + NORM_EPS) * g) * (1.0 + scale) + shift


def _group_norm(t, e, g, inv_n):
    ss = _dot((t * t).astype(BF16), e)
    return t * lax.rsqrt(ss * inv_n + NORM_EPS) * g


def _rope_chunks(t, cos, sin, lane_hi, shift):
    outs = []
    for c in range(t.shape[1] // LANES):
        tc = t[:, c * LANES:(c + 1) * LANES]
        up = pltpu.roll(tc, LANES - shift, 1)
        dn = pltpu.roll(tc, shift, 1)
        outs.append(tc * cos + jnp.where(lane_hi, dn, up) * sin)
    return outs[0] if len(outs) == 1 else jnp.concatenate(outs, axis=1)


def _rope_tables(seq, unit, pair_shift):
    pos = jnp.arange(seq)
    rows, cols = pos // GRID_W, pos % GRID_W
    quarter = unit // 4
    assert quarter == pair_shift
    freqs = ROPE_THETA ** (-jnp.arange(quarter, dtype=F32) / quarter)
    lane = jnp.arange(LANES)
    u = lane % unit
    use_col = (u // (unit // 2)) == 1
    fi = u % quarter
    p = jnp.where(use_col[None, :], cols[:, None], rows[:, None]).astype(F32)
    ang = p * freqs[fi][None, :]
    second = ((u % (unit // 2)) // quarter) == 1
    cos = jnp.cos(ang)
    sin = jnp.where(second[None, :], jnp.sin(ang), -jnp.sin(ang))
    return cos, sin


def _even_proj_kernel(x_ref, sh_ref, sc_ref, ng_ref, w_ref, e_ref, hg_ref, cos_ref, sin_ref,
                      qa_ref, ka_ref, va_ref, qb_ref, kb_ref, vb_ref, *, use_rope):
    h = _norm_mod(x_ref[0], ng_ref[...], sh_ref[0], sc_ref[0])
    p = _dot(h.astype(BF16), w_ref[...])
    hg = hg_ref[...]
    e = e_ref[...]
    inv_n = 1.0 / HEAD_DIM
    qa = _group_norm(p[:, 0:512], e, hg[0:1], inv_n)
    ka = _group_norm(p[:, 512:1024], e, hg[1:2], inv_n)
    qb = _group_norm(p[:, 1536:2048], e, hg[2:3], inv_n)
    kb = _group_norm(p[:, 2048:2560], e, hg[3:4], inv_n)
    if use_rope:
        lane = lax.broadcasted_iota(jnp.int32, (1, LANES), 1)
        lane_hi = (lane & 16) != 0
        cos, sin = cos_ref[...], sin_ref[...]
        qb = _rope_chunks(qb, cos, sin, lane_hi, 16)
        kb = _rope_chunks(kb, cos, sin, lane_hi, 16)
    qa_ref[0] = qa.astype(BF16)
    ka_ref[0] = ka.astype(BF16)
    va_ref[0] = p[:, 1024:1536].astype(BF16)
    qb_ref[0] = qb.astype(BF16)
    kb_ref[0] = kb.astype(BF16)
    vb_ref[0] = p[:, 2560:3072].astype(BF16)


def _even_proj(x, shift, scale, ng, w, e512, hg, cos, sin, *, use_rope, tm):
    b, s, d = x.shape
    n = w.shape[1]
    row = lambda bi, i: (bi, i, 0)
    vec = lambda bi, i: (bi, 0, 0)
    const = lambda bi, i: (0, 0)
    out = jax.ShapeDtypeStruct((b, s, 512), BF16)
    return pl.pallas_call(
        functools.partial(_even_proj_kernel, use_rope=use_rope),
        out_shape=(out,) * 6,
        grid=(b, s // tm),
        in_specs=[
            pl.BlockSpec((1, tm, d), row),
            pl.BlockSpec((1, 1, d), vec),
            pl.BlockSpec((1, 1, d), vec),
            pl.BlockSpec((1, d), const),
            pl.BlockSpec((d, n), const),
            pl.BlockSpec((512, 512), const),
            pl.BlockSpec((4, 512), const),
            pl.BlockSpec((tm, LANES), lambda bi, i: (i, 0)),
            pl.BlockSpec((tm, LANES), lambda bi, i: (i, 0)),
        ],
        out_specs=(pl.BlockSpec((1, tm, 512), row),) * 6,
        compiler_params=_cparams(("parallel", "parallel")),
        name="even_proj",
    )(x, shift, scale, ng, w, e512, hg, cos, sin)


def _pair_stack(q):
    lane = lax.broadcasted_iota(jnp.int32, q.shape, 1)
    zero = jnp.zeros_like(q)
    return jnp.concatenate([jnp.where(lane < HEAD_DIM, q, zero), jnp.where(lane >= HEAD_DIM, q, zero)], axis=0)


def _pair_merge(o, tq):
    lane = lax.broadcasted_iota(jnp.int32, (tq, LANES), 1)
    return jnp.where(lane < HEAD_DIM, o[:tq], o[tq:])


def _na_kernel(q_ref, k_ref, v_ref, kc_ref, vc_ref, bias_ref, o_ref, *, rows_per_step, n_rows):
    rb = pl.program_id(2)
    kc = kc_ref[0]
    vc = vc_ref[0]
    n_lat = NA_WIN_H * GRID_W
    for i in range(rows_per_step):
        r = rb * rows_per_step + i
        r0 = jnp.clip(r - NA_WIN_H // 2, 0, n_rows - NA_WIN_H)
        start = pl.multiple_of(r0 * GRID_W, GRID_W)
        kw = k_ref[0, pl.ds(start, n_lat), :]
        vw = v_ref[0, pl.ds(start, n_lat), :]
        qs = _pair_stack(q_ref[0, i * GRID_W:(i + 1) * GRID_W, :])
        s_lat = _dot_nt(qs, kw) + bias_ref[r - r0]
        s_ctx = _dot_nt(qs, kc)
        m = jnp.maximum(jnp.max(s_lat, axis=-1, keepdims=True), jnp.max(s_ctx, axis=-1, keepdims=True))
        p_lat = jnp.exp2(s_lat - m)
        p_ctx = jnp.exp2(s_ctx - m)
        l = jnp.sum(p_lat, axis=-1, keepdims=True) + jnp.sum(p_ctx, axis=-1, keepdims=True)
        o = _dot(p_lat.astype(BF16), vw) + _dot(p_ctx.astype(BF16), vc)
        o = o / l
        o_ref[0, i * GRID_W:(i + 1) * GRID_W, :] = _pair_merge(o, GRID_W).astype(BF16)


def _na_attention(q, k, v, kc, vc, bias, *, rows_per_step=8):
    b, s, _ = q.shape
    c = kc.shape[1]
    n_rows = s // GRID_W
    tq = rows_per_step * GRID_W
    return pl.pallas_call(
        functools.partial(_na_kernel, rows_per_step=rows_per_step, n_rows=n_rows),
        out_shape=jax.ShapeDtypeStruct((b, s, 512), BF16),
        grid=(b, 4, n_rows // rows_per_step),
        in_specs=[
            pl.BlockSpec((1, tq, LANES), lambda bi, j, i: (bi, i, j)),
            pl.BlockSpec((1, s, LANES), lambda bi, j, i: (bi, 0, j)),
            pl.BlockSpec((1, s, LANES), lambda bi, j, i: (bi, 0, j)),
            pl.BlockSpec((1, c, LANES), lambda bi, j, i: (bi, 0, j)),
            pl.BlockSpec((1, c, LANES), lambda bi, j, i: (bi, 0, j)),
            pl.BlockSpec((NA_WIN_H, None, LANES, NA_WIN_H * GRID_W), lambda bi, j, i: (0, j, 0, 0)),
        ],
        out_specs=pl.BlockSpec((1, tq, LANES), lambda bi, j, i: (bi, i, j)),
        compiler_params=_cparams(("parallel", "parallel", "parallel")),
        name="na_attention",
    )(q, k, v, kc, vc, bias)


def _na_bias_table(rpb):
    w = GRID_W
    col = jnp.arange(w)
    c0 = jnp.clip(col - NA_WIN_W // 2, 0, w - NA_WIN_W)
    col_in = (col[None, :] >= c0[:, None]) & (col[None, :] < c0[:, None] + NA_WIN_W)
    left = (w - 1) - (NA_WIN_W - 1)
    ext = jnp.pad(rpb, ((0, 0), (0, 0), (left, 2 * w - left - (2 * NA_WIN_W - 1))), mode="edge")
    h, nr, _ = rpb.shape
    flat = jnp.broadcast_to(ext[:, :, None, :], (h, nr, w, 2 * w)).reshape(h, nr, w * 2 * w)
    toep = flat[:, :, :w * (2 * w - 1)].reshape(h, nr, w, 2 * w - 1)[:, :, :, w - 1:]
    toep = jnp.where(col_in[None, None], toep * LOG2E, NEG_INF)
    variants = []
    for v in range(NA_WIN_H):
        tv = toep[:, NA_WIN_H - 1 - v:2 * NA_WIN_H - 1 - v]
        variants.append(tv.transpose(0, 2, 1, 3).reshape(NA_HEADS // 2, 2 * w, NA_WIN_H * w))
    return jnp.stack(variants, axis=0).astype(F32)


def _flash_kernel(*refs, mode, online, tq, tk, s_len, c_len, lam_init):
    if mode == "diff":
        q_ref, k_ref, v_ref, kc_ref, vc_ref, lam_ref, sg_ref, o_ref = refs
    else:
        q_ref, k_ref, v_ref, kc_ref, vc_ref, o_ref = refs

    q = q_ref[0]
    if mode == "mla":
        q_parts = (q[:, :LANES], q[:, LANES:])
    else:
        qs = _pair_stack(q)

    def scores(kt):
        if mode == "mla":
            return jnp.concatenate([_dot_nt(q_parts[0], kt[:, :LANES]), _dot_nt(q_parts[1], kt[:, LANES:])], axis=0)
        return _dot_nt(qs, kt)

    def step(kt, vt, carry):
        m, l, acc = carry
        s = scores(kt)
        m_new = jnp.maximum(m, jnp.max(s, axis=-1, keepdims=True))
        alpha = jnp.exp2(m - m_new)
        p = jnp.exp2(s - m_new)
        l = alpha * l + jnp.sum(p, axis=-1, keepdims=True)
        acc = alpha * acc + _dot(p.astype(BF16), vt)
        return m_new, l, acc

    def body(t, carry):
        start = pl.multiple_of(t * tk, tk)
        return step(k_ref[0, pl.ds(start, tk), :], v_ref[0, pl.ds(start, tk), :], carry)

    def direct(kt, vt, acc):
        p = jnp.exp2(scores(kt)).astype(BF16)
        v1 = jnp.concatenate([vt, jnp.ones(vt.shape, BF16)], axis=1)
        return acc + _dot(p, v1)

    if online:
        carry = (jnp.full((2 * tq, 1), -jnp.inf, F32), jnp.zeros((2 * tq, 1), F32),
                 jnp.zeros((2 * tq, LANES), F32))
        if s_len:
            carry = lax.fori_loop(0, s_len // tk, body, carry)
        if c_len:
            carry = step(kc_ref[0], vc_ref[0], carry)
        _, l, acc = carry
        o = acc / l
    else:
        acc = jnp.zeros((2 * tq, 2 * LANES), F32)
        for t in range(s_len // tk):
            acc = direct(k_ref[0, t * tk:(t + 1) * tk, :], v_ref[0, t * tk:(t + 1) * tk, :], acc)
        if c_len:
            acc = direct(kc_ref[0], vc_ref[0], acc)
        o = acc[:, :LANES] / acc[:, LANES:]
    if mode == "diff":
        lp = lam_ref[...]
        lam = (jnp.exp(jnp.sum(lp[0:1] * lp[1:2], axis=-1, keepdims=True))
               - jnp.exp(jnp.sum(lp[2:3] * lp[3:4], axis=-1, keepdims=True)) + lam_init)
        d = o[:tq] - lam * o[tq:]
        ms = jnp.mean(d * d, axis=-1, keepdims=True)
        o_ref[0] = (d * lax.rsqrt(ms + NORM_EPS) * sg_ref[...] * (1.0 - lam_init)).astype(BF16)
    else:
        o_ref[0] = _pair_merge(o, tq).astype(BF16)


EXP2_SAFE_RANGE = 64.0


def _flash(q, k, v, kc, vc, *, score_bound=None, **kw):
    if score_bound is None:
        return _flash_call(q, k, v, kc, vc, online=True, **kw)
    return lax.cond(score_bound <= EXP2_SAFE_RANGE,
                    lambda: _flash_call(q, k, v, kc, vc, online=False, **kw),
                    lambda: _flash_call(q, k, v, kc, vc, online=True, **kw))


def _flash_call(q, k, v, kc, vc, *, mode, online, kv_map, n_chunks, tq, tk, extra=(), lam_init=0.0):
    b, s, _ = q.shape
    qw = 2 * LANES if mode == "mla" else LANES
    c_len = kc.shape[1]
    if k is None:
        k, v, s_len = kc, vc, 0
    else:
        s_len = k.shape[1]
    kk = k.shape[1]
    in_specs = [
        pl.BlockSpec((1, tq, qw), lambda bi, j, i: (bi, i, j)),
        pl.BlockSpec((1, kk, qw), lambda bi, j, i: (bi, 0, kv_map(j) if mode != "mla" else j)),
        pl.BlockSpec((1, kk, LANES), lambda bi, j, i: (bi, 0, kv_map(j))),
        pl.BlockSpec((1, c_len, qw), lambda bi, j, i: (bi, 0, kv_map(j) if mode != "mla" else j)),
        pl.BlockSpec((1, c_len, LANES), lambda bi, j, i: (bi, 0, kv_map(j))),
    ]
    for e in extra:
        in_specs.append(pl.BlockSpec(e.shape, lambda bi, j, i: (0, 0)))
    return pl.pallas_call(
        functools.partial(_flash_kernel, mode=mode, online=online, tq=tq, tk=tk, s_len=s_len, c_len=c_len,
                          lam_init=lam_init),
        out_shape=jax.ShapeDtypeStruct((b, s, n_chunks * LANES), BF16),
        grid=(b, n_chunks, s // tq),
        in_specs=in_specs,
        out_specs=pl.BlockSpec((1, tq, LANES), lambda bi, j, i: (bi, i, j)),
        compiler_params=_cparams(("parallel", "parallel", "parallel")),
        name="flash_" + mode + ("_online" if online else ""),
    )(q, k, v, kc, vc, *extra)


def _post_attn_kernel(oa_ref, ob_ref, wa_ref, wb_ref, x_ref, g_ref, ng_ref, sh_ref, sc_ref, x1_ref, h2_ref):
    y = _dot(oa_ref[0], wa_ref[...]) + _dot(ob_ref[0], wb_ref[...])
    x1 = x_ref[0] + g_ref[0] * y
    x1_ref[0] = x1
    h2_ref[0] = _norm_mod(x1, ng_ref[...], sh_ref[0], sc_ref[0]).astype(BF16)


def _post_attn(oa, ob, wa, wb, x, gate, ng, shift, scale, *, tm):
    b, s, d = x.shape
    row = lambda bi, i: (bi, i, 0)
    vec = lambda bi, i: (bi, 0, 0)
    const = lambda bi, i: (0, 0)
    return pl.pallas_call(
        _post_attn_kernel,
        out_shape=(jax.ShapeDtypeStruct((b, s, d), F32), jax.ShapeDtypeStruct((b, s, d), BF16)),
        grid=(b, s // tm),
        in_specs=[
            pl.BlockSpec((1, tm, 512), row),
            pl.BlockSpec((1, tm, 512), row),
            pl.BlockSpec((512, d), const),
            pl.BlockSpec((512, d), const),
            pl.BlockSpec((1, tm, d), row),
            pl.BlockSpec((1, 1, d), vec),
            pl.BlockSpec((1, d), const),
            pl.BlockSpec((1, 1, d), vec),
            pl.BlockSpec((1, 1, d), vec),
        ],
        out_specs=(pl.BlockSpec((1, tm, d), row), pl.BlockSpec((1, tm, d), row)),
        compiler_params=_cparams(("parallel", "parallel")),
        name="post_attn",
    )(oa, ob, wa, wb, x, gate, ng, shift, scale)


def _ffn_kernel(h_ref, x1_ref, g_ref, wg_ref, wu_ref, wd_ref, o_ref, *, tf):
    h = h_ref[0]
    acc = jnp.zeros(o_ref.shape[1:], F32)
    for f in range(wg_ref.shape[1] // tf):
        sl = slice(f * tf, (f + 1) * tf)
        a = _dot(h, wg_ref[:, sl])
        u = _dot(h, wu_ref[:, sl])
        acc = acc + _dot((_silu(a) * u).astype(BF16), wd_ref[sl, :])
    o_ref[0] = x1_ref[0] + g_ref[0] * acc


def _ffn(h2, x1, gate, wg, wu, wd, *, tm, tf=256):
    b, s, d = x1.shape
    f = wg.shape[1]
    row = lambda bi, i: (bi, i, 0)
    vec = lambda bi, i: (bi, 0, 0)
    const = lambda bi, i: (0, 0)
    return pl.pallas_call(
        functools.partial(_ffn_kernel, tf=tf),
        out_shape=jax.ShapeDtypeStruct((b, s, d), F32),
        grid=(b, s // tm),
        in_specs=[
            pl.BlockSpec((1, tm, d), row),
            pl.BlockSpec((1, tm, d), row),
            pl.BlockSpec((1, 1, d), vec),
            pl.BlockSpec((d, f), const),
            pl.BlockSpec((d, f), const),
            pl.BlockSpec((f, d), const),
        ],
        out_specs=pl.BlockSpec((1, tm, d), row),
        compiler_params=_cparams(("parallel", "parallel")),
        name="ffn",
    )(h2, x1, gate, wg, wu, wd)


def _rms(t, g):
    return t * lax.rsqrt(jnp.mean(t * t, axis=-1, keepdims=True) + NORM_EPS) * g


def _head_norm_128(t, a_ref, at_ref, g, inv_n):
    ss = _dot((t * t).astype(BF16), a_ref[...])
    r = lax.rsqrt(ss * inv_n + NORM_EPS)
    r_hi = r.astype(BF16)
    r_lo = (r - r_hi.astype(F32)).astype(BF16)
    rb = _dot(r_hi, at_ref[...]) + _dot(r_lo, at_ref[...])
    return t * rb * g


def _odd_proj_kernel(x_ref, sh_ref, sc_ref, ng_ref, w_ref, wuq_ref, wukv_ref, e_ref, a_ref, at_ref, g_ref,
                     cos_ref, sin_ref, cosm_ref, sinm_ref,
                     qc_ref, kc_ref, vc_ref, qd_ref, kd_ref, vd_ref, *, use_rope):
    h = _norm_mod(x_ref[0], ng_ref[...], sh_ref[0], sc_ref[0])
    p = _dot(h.astype(BF16), w_ref[...])
    g = g_ref[...]
    e = e_ref[...]
    inv_n = 1.0 / HEAD_DIM
    qc = _group_norm(p[:, 0:512], e, g[0:1, 0:512], inv_n)
    kc = _group_norm(p[:, 512:768], e[0:256, 0:256], g[1:2, 0:256], inv_n)
    cq = _rms(p[:, 1024:1280], g[2:3, 0:256])
    ckv = _rms(p[:, 1280:1408], g[3:4, 0:128])
    krc = p[:, 1408:1536]
    qd = _dot(cq.astype(BF16), wuq_ref[...])
    kvv = _dot(ckv.astype(BF16), wukv_ref[...])
    kd = kvv[:, 0:1024] + jnp.concatenate([krc] * MLA_HEADS, axis=1)
    qd = _head_norm_128(qd, a_ref, at_ref, g[4:5], 1.0 / MLA_QK)
    kd = _head_norm_128(kd, a_ref, at_ref, g[5:6], 1.0 / MLA_QK)
    if use_rope:
        lane = lax.broadcasted_iota(jnp.int32, (1, LANES), 1)
        cos, sin = cos_ref[...], sin_ref[...]
        qc = _rope_chunks(qc, cos, sin, (lane & 16) != 0, 16)
        kc = _rope_chunks(kc, cos, sin, (lane & 16) != 0, 16)
        cosm, sinm = cosm_ref[...], sinm_ref[...]
        qd = _rope_chunks(qd, cosm, sinm, (lane & 8) != 0, 8)
        kd = _rope_chunks(kd, cosm, sinm, (lane & 8) != 0, 8)
    qc_ref[0] = qc.astype(BF16)
    kc_ref[0] = kc.astype(BF16)
    vc_ref[0] = p[:, 768:1024].astype(BF16)
    qd_ref[0] = qd.astype(BF16)
    kd_ref[0] = kd.astype(BF16)
    vd_ref[0] = kvv[:, 1024:1536].astype(BF16)


def _odd_proj(x, shift, scale, ng, w, wuq, wukv, e512, a, at, gains, cos, sin, cosm, sinm, *, use_rope, tm):
    b, s, d = x.shape
    row = lambda bi, i: (bi, i, 0)
    vec = lambda bi, i: (bi, 0, 0)
    const = lambda bi, i: (0, 0)
    tab = lambda bi, i: (i, 0)
    widths = (512, 256, 256, 1024, 1024, 512)
    return pl.pallas_call(
        functools.partial(_odd_proj_kernel, use_rope=use_rope),
        out_shape=tuple(jax.ShapeDtypeStruct((b, s, n), BF16) for n in widths),
        grid=(b, s // tm),
        in_specs=[
            pl.BlockSpec((1, tm, d), row),
            pl.BlockSpec((1, 1, d), vec),
            pl.BlockSpec((1, 1, d), vec),
            pl.BlockSpec((1, d), const),
            pl.BlockSpec(w.shape, const),
            pl.BlockSpec(wuq.shape, const),
            pl.BlockSpec(wukv.shape, const),
            pl.BlockSpec(e512.shape, const),
            pl.BlockSpec(a.shape, const),
            pl.BlockSpec(at.shape, const),
            pl.BlockSpec(gains.shape, const),
            pl.BlockSpec((tm, LANES), tab),
            pl.BlockSpec((tm, LANES), tab),
            pl.BlockSpec((tm, LANES), tab),
            pl.BlockSpec((tm, LANES), tab),
        ],
        out_specs=tuple(pl.BlockSpec((1, tm, n), row) for n in widths),
        compiler_params=_cparams(("parallel", "parallel")),
        name="odd_proj",
    )(x, shift, scale, ng, w, wuq, wukv, e512, a, at, gains, cos, sin, cosm, sinm)


def _mla_rope_tables(seq):
    cos, sin = _rope_tables(seq, MLA_ROPE, MLA_ROPE // 4)
    lane = jnp.arange(LANES)
    on = (lane >= MLA_NOPE) & (lane < MLA_QK)
    return jnp.where(on[None, :], cos, 1.0), jnp.where(on[None, :], sin, 0.0)


def _router_kernel(x1_ref, ng_ref, sh_ref, sc_ref, rt_ref, u_ref, ei_ref, gt_ref, rk_ref, cnt_ref, carry_ref):
    first = (pl.program_id(0) == 0) & (pl.program_id(1) == 0)

    @pl.when(first)
    def _():
        carry_ref[...] = jnp.zeros_like(carry_ref)

    h = _norm_mod(x1_ref[0], ng_ref[...], sh_ref[0], sc_ref[0])
    logits = lax.dot_general(rt_ref[...], h, (((1,), (1,)), ((), ())),
                             preferred_element_type=F32, precision=HIGHEST)
    eidx = lax.broadcasted_iota(jnp.int32, logits.shape, 0)
    m1 = jnp.max(logits, axis=0, keepdims=True)
    i1 = jnp.min(jnp.where(logits == m1, eidx, N_EXPERTS), axis=0, keepdims=True)
    rest = jnp.where(eidx == i1, -jnp.inf, logits)
    m2 = jnp.max(rest, axis=0, keepdims=True)
    i2 = jnp.min(jnp.where(rest == m2, eidx, N_EXPERTS), axis=0, keepdims=True)
    e2 = jnp.exp(m2 - m1)
    g1 = 1.0 / (1.0 + e2)
    sel1 = eidx == i1
    sel2 = eidx == i2
    onehot = jnp.where(sel1 | sel2, 1.0, 0.0)
    before = _dot(onehot.astype(BF16), u_ref[...]) + carry_ref[:, 0:1]
    r1 = jnp.sum(jnp.where(sel1, before, 0.0), axis=0, keepdims=True)
    r2 = jnp.sum(jnp.where(sel2, before, 0.0), axis=0, keepdims=True)
    ei_ref[...] = jnp.concatenate([i1, i2], axis=0)
    gt_ref[...] = jnp.concatenate([g1, e2 * g1], axis=0)
    rk_ref[...] = jnp.concatenate([r1, r2], axis=0).astype(jnp.int32)
    total = carry_ref[...] + jnp.sum(onehot, axis=1, keepdims=True)
    carry_ref[...] = total
    cnt_ref[...] = total.astype(jnp.int32)


def _router(x1, ng, shift, scale, router_t, *, tm):
    b, s, d = x1.shape
    n = b * s
    nt = s // tm
    tri = (jnp.arange(tm)[:, None] < jnp.arange(tm)[None, :]).astype(BF16)
    flat = lambda bi, i: (0, bi * nt + i)
    return pl.pallas_call(
        _router_kernel,
        out_shape=(jax.ShapeDtypeStruct((2, n), jnp.int32), jax.ShapeDtypeStruct((2, n), F32),
                   jax.ShapeDtypeStruct((2, n), jnp.int32), jax.ShapeDtypeStruct((N_EXPERTS, LANES), jnp.int32)),
        grid=(b, nt),
        in_specs=[
            pl.BlockSpec((1, tm, d), lambda bi, i: (bi, i, 0)),
            pl.BlockSpec((1, d), lambda bi, i: (0, 0)),
            pl.BlockSpec((1, 1, d), lambda bi, i: (bi, 0, 0)),
            pl.BlockSpec((1, 1, d), lambda bi, i: (bi, 0, 0)),
            pl.BlockSpec((N_EXPERTS, d), lambda bi, i: (0, 0)),
            pl.BlockSpec((tm, tm), lambda bi, i: (0, 0)),
        ],
        out_specs=(pl.BlockSpec((2, tm), flat), pl.BlockSpec((2, tm), flat), pl.BlockSpec((2, tm), flat),
                   pl.BlockSpec((N_EXPERTS, LANES), lambda bi, i: (0, 0))),
        scratch_shapes=[pltpu.VMEM((N_EXPERTS, LANES), F32)],
        compiler_params=_cparams(("arbitrary", "arbitrary")),
        name="router",
    )(x1, ng, shift, scale, router_t, tri)


MOE_SUB = 256
MOE_BLK = 1024
MOE_TC = 256


def _lookup(table, idx):
    hit = idx[..., None] == jnp.arange(table.shape[0])
    return jnp.sum(jnp.where(hit, table, 0), axis=-1)


def _moe_plan(ei, rk, counts, n_tok):
    n_chunks = n_tok // MOE_TC
    cap = 2 * n_tok + N_EXPERTS * MOE_BLK
    nb_sub = cap // MOE_SUB
    nb_blk = cap // MOE_BLK
    padded = ((counts + MOE_BLK - 1) // MOE_BLK) * MOE_BLK
    pad_end = jnp.cumsum(padded)
    pad_start = pad_end - padded
    dest = _lookup(pad_start, ei) + rk

    onehot = ei.reshape(2, n_chunks, MOE_TC)[..., None] == jnp.arange(N_EXPERTS)
    cnt = jnp.sum(onehot, axis=(0, 2)).astype(jnp.int32)
    cum = jnp.concatenate([jnp.zeros((1, N_EXPERTS), jnp.int32), jnp.cumsum(cnt, axis=0)], axis=0)

    lo = pad_start[None, :] + cum[:-1]
    has = cnt > 0
    blk0 = lo // MOE_SUB
    two = has & ((lo + cnt - 1) // MOE_SUB > blk0)
    blk1 = jnp.minimum(blk0 + 1, nb_sub - 1)
    blk0 = jnp.where(has, blk0, 0)
    c_blk0 = blk0.reshape(-1).astype(jnp.int32)
    c_blk1 = jnp.where(two, blk1, blk0).reshape(-1).astype(jnp.int32)
    c_flag = (has.astype(jnp.int32) + 2 * two.astype(jnp.int32)).reshape(-1)

    sb = jnp.arange(nb_sub)
    e_sb = jnp.minimum(jnp.sum(sb[:, None] * MOE_SUB >= pad_end[None, :], axis=1), N_EXPERTS - 1)
    counts_sb = _lookup(counts, e_sb)
    r0 = sb * MOE_SUB - _lookup(pad_start, e_sb)
    valid_sb = (sb * MOE_SUB < pad_end[-1]) & (r0 < counts_sb)
    r1 = jnp.minimum(r0 + MOE_SUB, counts_sb) - 1
    hit_sb = e_sb[:, None] == jnp.arange(N_EXPERTS)[None, :]
    cum_sb = jnp.sum(jnp.where(hit_sb[:, None, :], cum[None, 1:, :], 0), axis=-1)
    cmin = jnp.sum(cum_sb <= r0[:, None], axis=1)
    cmax = jnp.sum(cum_sb <= r1[:, None], axis=1)
    items = jnp.where(valid_sb, cmax - cmin + 1, 0)
    off_end = jnp.cumsum(items)
    off = off_end - items
    total = off_end[-1]
    w_max = (2 * n_tok) // MOE_SUB + N_EXPERTS + N_EXPERTS * (n_chunks - 1)
    w = jnp.arange(w_max)
    wv = w < total
    wq = jnp.minimum(w, total - 1)
    w_sb = jnp.minimum(jnp.sum(off_end[None, :] <= wq[:, None], axis=1), nb_sub - 1)
    off_w = _lookup(off, w_sb)
    w_chunk = _lookup(cmin, w_sb) + (wq - off_w)
    w_flag = wv.astype(jnp.int32) + 2 * (wq == off_w).astype(jnp.int32)

    bi = jnp.arange(nb_blk)
    e_blk = jnp.minimum(jnp.sum(bi[:, None] * MOE_BLK >= pad_end[None, :], axis=1), N_EXPERTS - 1)
    rows = jnp.where(bi * MOE_BLK < pad_end[-1],
                     _lookup(counts, e_blk) - (bi * MOE_BLK - _lookup(pad_start, e_blk)), 0)
    n_sub = jnp.clip((rows + MOE_SUB - 1) // MOE_SUB, 0, MOE_BLK // MOE_SUB)
    return dict(dest=dest.astype(jnp.int32), cap=cap,
                w_sb=w_sb.astype(jnp.int32), w_chunk=w_chunk.astype(jnp.int32), w_flag=w_flag,
                e_blk=e_blk.astype(jnp.int32), n_sub=n_sub.astype(jnp.int32),
                c_blk0=c_blk0, c_blk1=c_blk1, c_flag=c_flag)


def _moe_gather_kernel(wsb_ref, wch_ref, wfl_ref, t_ref, d_ref, g_ref, xg_ref, gs_ref):
    w = pl.program_id(0)
    flag = wfl_ref[w]
    base = wsb_ref[w] * MOE_SUB

    def contrib():
        d = d_ref[...]
        g = g_ref[...]
        srow = lax.broadcasted_iota(jnp.int32, (MOE_SUB, MOE_TC), 0) + base
        hit0 = srow == d[0:1]
        hit1 = srow == d[1:2]
        p = jnp.where(hit0 | hit1, 1.0, 0.0).astype(BF16)
        rows = _dot(p, t_ref[...]).astype(BF16)
        gate = jnp.sum(jnp.where(hit0, g[0:1], 0.0) + jnp.where(hit1, g[1:2], 0.0), axis=-1, keepdims=True)
        return rows, gate

    @pl.when(flag == 3)
    def _():
        rows, gate = contrib()
        xg_ref[...] = rows
        gs_ref[...] = gate

    @pl.when(flag == 1)
    def _():
        rows, gate = contrib()
        xg_ref[...] += rows
        gs_ref[...] += gate


def _moe_gather(t, dest, gates, plan):
    n_tok, d = t.shape
    cap = plan["cap"]
    w_max = plan["w_sb"].shape[0]
    grid_spec = pltpu.PrefetchScalarGridSpec(
        num_scalar_prefetch=3,
        grid=(w_max,),
        in_specs=[
            pl.BlockSpec((MOE_TC, d), lambda w, sb, ch, fl: (ch[w], 0)),
            pl.BlockSpec((2, MOE_TC), lambda w, sb, ch, fl: (0, ch[w])),
            pl.BlockSpec((2, MOE_TC), lambda w, sb, ch, fl: (0, ch[w])),
        ],
        out_specs=(pl.BlockSpec((MOE_SUB, d), lambda w, sb, ch, fl: (sb[w], 0)),
                   pl.BlockSpec((MOE_SUB, 1), lambda w, sb, ch, fl: (sb[w], 0))),
    )
    return pl.pallas_call(
        _moe_gather_kernel,
        out_shape=(jax.ShapeDtypeStruct((cap, d), BF16), jax.ShapeDtypeStruct((cap, 1), F32)),
        grid_spec=grid_spec,
        compiler_params=_cparams(("arbitrary",)),
        name="moe_gather",
    )(plan["w_sb"], plan["w_chunk"], plan["w_flag"], t, dest, gates)


def _moe_expert_kernel(eb_ref, ns_ref, x_ref, gs_ref, w1_ref, w3_ref, w2_ref, y_ref, acc_ref):
    i = pl.program_id(0)
    j = pl.program_id(1)
    n_sub = ns_ref[i]
    for sub in range(MOE_BLK // MOE_SUB):
        rows = slice(sub * MOE_SUB, (sub + 1) * MOE_SUB)

        @pl.when(sub < n_sub)
        def _():
            xs = x_ref[rows, :]
            a = _dot(xs, w1_ref[...])
            u = _dot(xs, w3_ref[...])
            part = _dot((_silu(a) * u).astype(BF16), w2_ref[...])

            @pl.when(j == 0)
            def _():
                acc_ref[rows, :] = part

            @pl.when(j > 0)
            def _():
                acc_ref[rows, :] += part

            @pl.when(j == pl.num_programs(1) - 1)
            def _():
                y_ref[rows, :] = (acc_ref[rows, :] * gs_ref[rows, :]).astype(BF16)

        @pl.when((sub >= n_sub) & (j == pl.num_programs(1) - 1))
        def _():
            y_ref[rows, :] = jnp.zeros((MOE_SUB, y_ref.shape[1]), BF16)


def _moe_experts(xg, gs, w1, w3, w2, plan):
    cap, d = xg.shape
    f = w1.shape[2]
    grid_spec = pltpu.PrefetchScalarGridSpec(
        num_scalar_prefetch=2,
        grid=(cap // MOE_BLK, f // MOE_TF),
        in_specs=[
            pl.BlockSpec((MOE_BLK, d), lambda i, j, eb, ns: (i, 0)),
            pl.BlockSpec((MOE_BLK, 1), lambda i, j, eb, ns: (i, 0)),
            pl.BlockSpec((None, d, MOE_TF), lambda i, j, eb, ns: (eb[i], 0, j)),
            pl.BlockSpec((None, d, MOE_TF), lambda i, j, eb, ns: (eb[i], 0, j)),
            pl.BlockSpec((None, MOE_TF, d), lambda i, j, eb, ns: (eb[i], j, 0)),
        ],
        out_specs=pl.BlockSpec((MOE_BLK, d), lambda i, j, eb, ns: (i, 0)),
        scratch_shapes=[pltpu.VMEM((MOE_BLK, d), F32)],
    )
    return pl.pallas_call(
        _moe_expert_kernel,
        out_shape=jax.ShapeDtypeStruct((cap, d), BF16),
        grid_spec=grid_spec,
        compiler_params=_cparams(("arbitrary", "arbitrary")),
        name="moe_experts",
    )(plan["e_blk"], plan["n_sub"], xg, gs, w1, w3, w2)


def _moe_combine_kernel(b0_ref, b1_ref, fl_ref, dt_ref, y0_ref, y1_ref, x1_ref, g_ref, o_ref, acc_ref):
    c = pl.program_id(0)
    e = pl.program_id(1)
    idx = c * N_EXPERTS + e
    flag = fl_ref[idx]

    @pl.when(e == 0)
    def _():
        acc_ref[...] = jnp.zeros_like(acc_ref)

    def window(y_ref, blk):
        d = dt_ref[...]
        scol = lax.broadcasted_iota(jnp.int32, (MOE_TC, MOE_SUB), 1) + blk * MOE_SUB
        p = jnp.where((scol == d[:, 0:1]) | (scol == d[:, 1:2]), 1.0, 0.0).astype(BF16)
        acc_ref[...] += _dot(p, y_ref[...])

    @pl.when((flag & 1) != 0)
    def _():
        window(y0_ref, b0_ref[idx])

    @pl.when((flag & 2) != 0)
    def _():
        window(y1_ref, b1_ref[idx])

    @pl.when(e == N_EXPERTS - 1)
    def _():
        o_ref[...] = x1_ref[...] + g_ref[0] * acc_ref[...]


def _moe_combine(y, dest_t, x1, gate, plan, seq):
    n_tok, d = x1.shape
    n_chunks = n_tok // MOE_TC
    per_b = seq // MOE_TC
    grid_spec = pltpu.PrefetchScalarGridSpec(
        num_scalar_prefetch=3,
        grid=(n_chunks, N_EXPERTS),
        in_specs=[
            pl.BlockSpec((MOE_TC, 2), lambda c, e, b0, b1, fl: (c, 0)),
            pl.BlockSpec((MOE_SUB, d), lambda c, e, b0, b1, fl: (b0[c * N_EXPERTS + e], 0)),
            pl.BlockSpec((MOE_SUB, d), lambda c, e, b0, b1, fl: (b1[c * N_EXPERTS + e], 0)),
            pl.BlockSpec((MOE_TC, d), lambda c, e, b0, b1, fl: (c, 0)),
            pl.BlockSpec((1, 1, d), lambda c, e, b0, b1, fl: (c // per_b, 0, 0)),
        ],
        out_specs=pl.BlockSpec((MOE_TC, d), lambda c, e, b0, b1, fl: (c, 0)),
        scratch_shapes=[pltpu.VMEM((MOE_TC, d), F32)],
    )
    return pl.pallas_call(
        _moe_combine_kernel,
        out_shape=jax.ShapeDtypeStruct((n_tok, d), F32),
        grid_spec=grid_spec,
        compiler_params=_cparams(("arbitrary", "arbitrary")),
        name="moe_combine",
    )(plan["c_blk0"], plan["c_blk1"], plan["c_flag"], dest_t, y, y, x1, gate)


def _split_mod(mod_l, batch):
    d = D_MODEL
    lat = tuple(mod_l[:batch, k * d:(k + 1) * d][:, None, :] for k in range(ADA_CHUNKS))
    ctx = tuple(jnp.broadcast_to(mod_l[batch:batch + 1, k * d:(k + 1) * d][:, None, :], (batch, 1, d))
                for k in range(ADA_CHUNKS))
    return lat, ctx


def _score_bound(q_gain, k_gain, n):
    return 1.02 * n * jnp.max(jnp.abs(q_gain)) * jnp.max(jnp.abs(k_gain))


def _block_ones(n, block):
    idx = jnp.arange(n) // block
    return (idx[:, None] == idx[None, :]).astype(BF16)


def _even_layer(x, xc, mod_l, norm1_g, norm2_g, w_in, w_out, na_q_g, na_k_g, na_rpb, diff_q_g, diff_k_g,
                lq1, lk1, lq2, lk2, subln_g, wg, wu, wd, layer_idx, need_ctx):
    b, s, d = x.shape
    (sh1, sc1, g1, sh2, sc2, g2), (csh1, csc1, cg1, csh2, csc2, cg2) = _split_mod(mod_l, b)
    qscale = HEAD_DIM ** -0.5 * LOG2E
    lam_init = 0.8 - 0.6 * math.exp(-0.3 * layer_idx)
    hg = jnp.stack([jnp.tile(na_q_g, 8) * qscale, jnp.tile(na_k_g, 8),
                    jnp.tile(diff_q_g, 8) * qscale, jnp.tile(diff_k_g, 8)]).astype(F32)
    e512 = _block_ones(512, HEAD_DIM)
    cos, sin = _rope_tables(s, HEAD_DIM, 16)
    ng1 = norm1_g[None, :]
    ng2 = norm2_g[None, :]
    w_in_b = w_in.astype(BF16)
    wo_a = w_out[:512].astype(BF16)
    wo_b = w_out[512:].astype(BF16)
    wg_b, wu_b, wd_b = wg.astype(BF16), wu.astype(BF16), wd.astype(BF16)
    lam_p = jnp.stack([lq1, lk1, lq2, lk2]).astype(F32)
    sg = subln_g[None, :].astype(F32)
    ident = lambda j: j

    tm = min(512, s)
    tc = xc.shape[1]
    qa, ka, va, qb, kb, vb = _even_proj(x, sh1, sc1, ng1, w_in_b, e512, hg, cos, sin, use_rope=True, tm=tm)
    qca, kca, vca, qcb, kcb, vcb = _even_proj(xc, csh1, csc1, ng1, w_in_b, e512, hg, cos, sin, use_rope=False, tm=tc)
    out_a = _na_attention(qa, ka, va, kca, vca, _na_bias_table(na_rpb))
    out_b = _flash(qb, kb, vb, kcb, vcb, mode="diff", kv_map=ident, n_chunks=4, tq=256, tk=512,
                   extra=(lam_p, sg), lam_init=lam_init, score_bound=_score_bound(hg[2], hg[3], HEAD_DIM))
    x1, h2 = _post_attn(out_a, out_b, wo_a, wo_b, x, g1, ng2, sh2, sc2, tm=tm)
    x2 = _ffn(h2, x1, g2, wg_b, wu_b, wd_b, tm=tm)
    if not need_ctx:
        return x2, None
    oca = _flash(qca, None, None, kca, vca, mode="pair", kv_map=ident, n_chunks=4, tq=tc, tk=tc)
    ocb = _flash(qcb, None, None, kcb, vcb, mode="diff", kv_map=ident, n_chunks=4, tq=tc, tk=tc,
                 extra=(lam_p, sg), lam_init=lam_init)
    xc1, hc2 = _post_attn(oca, ocb, wo_a, wo_b, xc, cg1, ng2, csh2, csc2, tm=tc)
    xc2 = _ffn(hc2, xc1, cg2, wg_b, wu_b, wd_b, tm=tc)
    return x2, xc2


def _odd_layer(x, xc, mod_l, norm1_g, norm2_g, w_in, w_out, gqa_q_g, gqa_k_g, cq_g, w_uq, ckv_g, w_ukv,
               mla_q_g, mla_k_g, router, w1, w3, w2):
    b, s, d = x.shape
    n_tok = b * s
    (sh1, sc1, g1, sh2, sc2, g2), (csh1, csc1, _, _, _, _) = _split_mod(mod_l, b)
    ng1 = norm1_g[None, :]
    ng2 = norm2_g[None, :]

    z = lambda n: jnp.zeros((d, n), w_in.dtype)
    k0, k1, v0, v1 = (w_in[:, 512 + 64 * i:576 + 64 * i] for i in range(4))
    w_p = jnp.concatenate([w_in[:, 0:512], k0, k0, k1, k1, v0, v0, v1, v1, w_in[:, 768:1024], w_in[:, 1024:1152],
                           z(MLA_NOPE), w_in[:, 1152:1184], z(LANES - MLA_QK)], axis=1).astype(BF16)
    wuq_p = jnp.pad(w_uq.reshape(MLA_Q_RANK, MLA_HEADS, MLA_QK), ((0, 0), (0, 0), (0, LANES - MLA_QK)))
    wuq_p = wuq_p.reshape(MLA_Q_RANK, MLA_HEADS * LANES).astype(BF16)
    ukv = w_ukv.reshape(MLA_KV_RANK, MLA_HEADS, MLA_NOPE + MLA_V)
    uk = jnp.pad(ukv[:, :, :MLA_NOPE], ((0, 0), (0, 0), (0, LANES - MLA_NOPE))).reshape(MLA_KV_RANK, -1)
    uv = ukv[:, :, MLA_NOPE:].reshape(MLA_KV_RANK, -1)
    wukv_p = jnp.concatenate([uk, uv], axis=1).astype(BF16)
    e512 = _block_ones(512, HEAD_DIM)
    a = (jnp.arange(MLA_HEADS * LANES)[:, None] // LANES == jnp.arange(LANES)[None, :]).astype(BF16)
    at = a.T
    qscale = HEAD_DIM ** -0.5 * LOG2E
    mscale = MLA_QK ** -0.5 * LOG2E
    pad_row = lambda v: jnp.pad(v, (0, MLA_HEADS * LANES - v.shape[0]))
    pad_head = lambda v: jnp.tile(jnp.pad(v, (0, LANES - MLA_QK)), MLA_HEADS)
    gains = jnp.stack([pad_row(jnp.tile(gqa_q_g, 8) * qscale), pad_row(jnp.tile(gqa_k_g, 4)),
                       pad_row(cq_g), pad_row(ckv_g), pad_head(mla_q_g) * mscale, pad_head(mla_k_g),
                       jnp.zeros((MLA_HEADS * LANES,), F32), jnp.zeros((MLA_HEADS * LANES,), F32)]).astype(F32)
    cos, sin = _rope_tables(s, HEAD_DIM, 16)
    cosm, sinm = _mla_rope_tables(s)
    wo_a = w_out[:512].astype(BF16)
    wo_b = w_out[512:].astype(BF16)

    tm = min(512, s)
    tc = xc.shape[1]
    proj = functools.partial(_odd_proj, ng=ng1, w=w_p, wuq=wuq_p, wukv=wukv_p, e512=e512, a=a, at=at, gains=gains,
                             cos=cos, sin=sin, cosm=cosm, sinm=sinm)
    qc, kc, vc, qd, kd, vd = proj(x, sh1, sc1, use_rope=True, tm=tm)
    _, kcc, vcc, _, kcd, vcd = proj(xc, csh1, csc1, use_rope=False, tm=tc)
    out_c = _flash(qc, kc, vc, kcc, vcc, mode="pair", kv_map=lambda j: j // 2, n_chunks=4, tq=256, tk=512,
                   score_bound=_score_bound(gains[0], gains[1], HEAD_DIM))
    out_d = _flash(qd, kd, vd, kcd, vcd, mode="mla", kv_map=lambda j: j, n_chunks=4, tq=256, tk=512,
                   score_bound=_score_bound(gains[4], gains[5], MLA_QK))
    x1, h2 = _post_attn(out_c, out_d, wo_a, wo_b, x, g1, ng2, sh2, sc2, tm=tm)

    ei, gt, rk, cnt = _router(x1, ng2, sh2, sc2, router.T.astype(F32), tm=tm)
    plan = _moe_plan(ei, rk, cnt[:, 0], n_tok)
    xg, gs = _moe_gather(h2.reshape(n_tok, d), plan["dest"], gt, plan)
    y = _moe_experts(xg, gs, w1.astype(BF16), w3.astype(BF16), w2.astype(BF16), plan)
    out = _moe_combine(y, plan["dest"].T, x1.reshape(n_tok, d), g2, plan, s)
    return out.reshape(b, s, d)


def _mod_vectors(c, c_ctx, ada_w, ada_b):
    b = c.shape[0]
    cc = jnp.zeros((MOD_ROWS, D_MODEL), F32).at[:b].set(c).at[b].set(c_ctx)
    return _modvec(cc, ada_w, ada_b[:, None, :])


def kernel(x, c, ctx, c_ctx, ada_w, ada_b, norm1_g, norm2_g, ev_w_in, ev_w_out, na_q_g, na_k_g, na_rpb,
           diff_q_g, diff_k_g, diff_lq1, diff_lk1, diff_lq2, diff_lk2, diff_subln_g,
           ffn_w_gate, ffn_w_up, ffn_w_down, od_w_in, od_w_out, gqa_q_g, gqa_k_g, mla_cq_g, mla_w_uq,
           mla_ckv_g, mla_w_ukv, mla_q_g, mla_k_g, moe_router, moe_w1, moe_w3, moe_w2):
    mod = _mod_vectors(c, c_ctx, ada_w, ada_b)
    x, xc = _even_layer(x, ctx, mod[0], norm1_g[0], norm2_g[0], ev_w_in[0], ev_w_out[0], na_q_g[0], na_k_g[0],
                        na_rpb[0], diff_q_g[0], diff_k_g[0], diff_lq1[0], diff_lk1[0], diff_lq2[0], diff_lk2[0],
                        diff_subln_g[0], ffn_w_gate[0], ffn_w_up[0], ffn_w_down[0], 0, True)
    return _odd_layer(x, xc, mod[1], norm1_g[1], norm2_g[1], od_w_in[0], od_w_out[0], gqa_q_g[0], gqa_k_g[0],
                      mla_cq_g[0], mla_w_uq[0], mla_ckv_g[0], mla_w_ukv[0], mla_q_g[0], mla_k_g[0],
                      moe_router[0], moe_w1[0], moe_w3[0], moe_w2[0])
```

```python
import functools
import math

import jax
import jax.numpy as jnp
from jax import lax
from jax.experimental import pallas as pl
from jax.experimental.pallas import tpu as pltpu

F32 = jnp.float32
BF16 = jnp.bfloat16
HIGHEST = lax.Precision.HIGHEST

D_MODEL = 1024
GRID_W = 64
HEAD_DIM = 64
ROPE_THETA = 10000.0
NORM_EPS = 1e-6
NEG_INF = -1e30
ADA_CHUNKS = 6
LOG2E = 1.4426950408889634

NA_HEADS = 8
NA_WIN_H = 8
NA_WIN_W = 16
DIFF_HEADS = 4
DIFF_V_DIM = 2 * HEAD_DIM
GQA_Q_HEADS = 8
GQA_KV_HEADS = 2
MLA_HEADS = 8
MLA_NOPE = 64
MLA_ROPE = 32
MLA_QK = MLA_NOPE + MLA_ROPE
MLA_V = 64
MLA_Q_RANK = 256
MLA_KV_RANK = 128
D_FF = 2816
N_EXPERTS = 8
D_FF_EXPERT = 3584

LANES = 128
MXU_DIM = 256
VMEM_LIMIT = 56 * 1024 * 1024
MOD_ROWS = 16
MOE_TF = 512
FLASH_TQ = 512
FLASH_TK = 512
EXP2_SAFE_RANGE = 64.0


def _cparams(sem):
    return pltpu.CompilerParams(dimension_semantics=sem, vmem_limit_bytes=VMEM_LIMIT)


def _dot(a, b):
    return jnp.dot(a, b, preferred_element_type=F32)


def _dot_nt(a, b):
    return lax.dot_general(a, b, (((1,), (1,)), ((), ())), preferred_element_type=F32)


def _silu(x):
    return x * (1.0 / (1.0 + jnp.exp(-x)))


def _modvec_kernel(c_ref, w_ref, b_ref, o_ref):
    s = _silu(c_ref[...])
    o_ref[...] = jnp.dot(s, w_ref[...], preferred_element_type=F32, precision=HIGHEST) + b_ref[...]


def _modvec(cc, ada_w, ada_b):
    n_layers, d, n = ada_w.shape
    tn = 768
    return pl.pallas_call(
        _modvec_kernel,
        out_shape=jax.ShapeDtypeStruct((n_layers, MOD_ROWS, n), F32),
        grid=(n_layers, n // tn),
        in_specs=[
            pl.BlockSpec((MOD_ROWS, d), lambda l, j: (0, 0)),
            pl.BlockSpec((None, d, tn), lambda l, j: (l, 0, j)),
            pl.BlockSpec((None, 1, tn), lambda l, j: (l, 0, j)),
        ],
        out_specs=pl.BlockSpec((None, MOD_ROWS, tn), lambda l, j: (l, 0, j)),
        compiler_params=_cparams(("arbitrary", "arbitrary")),
        name="modvec",
    )(cc, ada_w, ada_b)


def _norm_mod(x, g, shift, scale):
    ms = jnp.mean(x * x, axis=-1, keepdims=True)
    return (x * lax.rsqrt(ms + NORM_EPS) * g) * (1.0 + scale) + shift


def _group_norm(t, e, g, inv_n):
    ss = _dot((t * t).astype(BF16), e)
    return t * lax.rsqrt(ss * inv_n + NORM_EPS) * g


def _rope_chunks(t, cos, sin, lane_hi, shift):
    outs = []
    for c in range(t.shape[1] // LANES):
        tc = t[:, c * LANES:(c + 1) * LANES]
        up = pltpu.roll(tc, LANES - shift, 1)
        dn = pltpu.roll(tc, shift, 1)
        outs.append(tc * cos + jnp.where(lane_hi, dn, up) * sin)
    return outs[0] if len(outs) == 1 else jnp.concatenate(outs, axis=1)


def _rope_tables(seq, unit, pair_shift):
    pos = jnp.arange(seq)
    rows, cols = pos // GRID_W, pos % GRID_W
    quarter = unit // 4
    assert quarter == pair_shift
    freqs = ROPE_THETA ** (-jnp.arange(quarter, dtype=F32) / quarter)
    lane = jnp.arange(LANES)
    u = lane % unit
    use_col = (u // (unit // 2)) == 1
    fi = u % quarter
    p = jnp.where(use_col[None, :], cols[:, None], rows[:, None]).astype(F32)
    ang = p * freqs[fi][None, :]
    second = ((u % (unit // 2)) // quarter) == 1
    cos = jnp.cos(ang)
    sin = jnp.where(second[None, :], jnp.sin(ang), -jnp.sin(ang))
    return cos, sin


def _even_proj_kernel(x_ref, sh_ref, sc_ref, ng_ref, w_ref, e_ref, hg_ref, cos_ref, sin_ref,
                      qa_ref, ka_ref, va_ref, qb_ref, kb_ref, vb_ref, *, use_rope):
    h = _norm_mod(x_ref[0], ng_ref[...], sh_ref[0], sc_ref[0])
    p = _dot(h.astype(BF16), w_ref[...])
    hg = hg_ref[...]
    e = e_ref[...]
    inv_n = 1.0 / HEAD_DIM
    qa = _group_norm(p[:, 0:512], e, hg[0:1], inv_n)
    ka = _group_norm(p[:, 512:1024], e, hg[1:2], inv_n)
    qb = _group_norm(p[:, 1536:2048], e, hg[2:3], inv_n)
    kb = _group_norm(p[:, 2048:2560], e, hg[3:4], inv_n)
    if use_rope:
        lane = lax.broadcasted_iota(jnp.int32, (1, LANES), 1)
        lane_hi = (lane & 16) != 0
        cos, sin = cos_ref[...], sin_ref[...]
        qb = _rope_chunks(qb, cos, sin, lane_hi, 16)
        kb = _rope_chunks(kb, cos, sin, lane_hi, 16)
    qa_ref[0] = qa.astype(BF16)
    ka_ref[0] = ka.astype(BF16)
    va_ref[0] = p[:, 1024:1536].astype(BF16)
    qb_ref[0] = qb.astype(BF16)
    kb_ref[0] = kb.astype(BF16)
    vb_ref[0] = p[:, 2560:3072].astype(BF16)


def _even_proj(x, shift, scale, ng, w, e512, hg, cos, sin, *, use_rope, tm):
    b, s, d = x.shape
    n = w.shape[1]
    row = lambda bi, i: (bi, i, 0)
    vec = lambda bi, i: (bi, 0, 0)
    const = lambda bi, i: (0, 0)
    out = jax.ShapeDtypeStruct((b, s, 512), BF16)
    return pl.pallas_call(
        functools.partial(_even_proj_kernel, use_rope=use_rope),
        out_shape=(out,) * 6,
        grid=(b, s // tm),
        in_specs=[
            pl.BlockSpec((1, tm, d), row),
            pl.BlockSpec((1, 1, d), vec),
            pl.BlockSpec((1, 1, d), vec),
            pl.BlockSpec((1, d), const),
            pl.BlockSpec((d, n), const),
            pl.BlockSpec((512, 512), const),
            pl.BlockSpec((4, 512), const),
            pl.BlockSpec((tm, LANES), lambda bi, i: (i, 0)),
            pl.BlockSpec((tm, LANES), lambda bi, i: (i, 0)),
        ],
        out_specs=(pl.BlockSpec((1, tm, 512), row),) * 6,
        compiler_params=_cparams(("parallel", "parallel")),
        name="even_proj",
    )(x, shift, scale, ng, w, e512, hg, cos, sin)


def _pair_stack(q):
    lane = lax.broadcasted_iota(jnp.int32, q.shape, 1)
    zero = jnp.zeros_like(q)
    return jnp.concatenate([jnp.where(lane < HEAD_DIM, q, zero), jnp.where(lane >= HEAD_DIM, q, zero)], axis=0)


def _pair_merge(o, tq):
    lane = lax.broadcasted_iota(jnp.int32, (tq, LANES), 1)
    return jnp.where(lane < HEAD_DIM, o[:tq], o[tq:])


def _with_ones(v):
    return jnp.concatenate([v, jnp.ones(v.shape, BF16)], axis=1)


def _na_kernel(q_ref, k_ref, v_ref, kc_ref, vc_ref, bias_ref, o_ref, *, rows_per_step, n_rows, direct):
    rb = pl.program_id(2)
    kc = kc_ref[0]
    vc = vc_ref[0]
    n_lat = NA_WIN_H * GRID_W
    if direct:
        nq = rows_per_step * GRID_W
        qs_all = _pair_stack(q_ref[0])
        o_ctx = _dot(jnp.exp2(_dot_nt(qs_all, kc)).astype(BF16), _with_ones(vc))
        for i in range(rows_per_step):
            r = rb * rows_per_step + i
            r0 = jnp.clip(r - NA_WIN_H // 2, 0, n_rows - NA_WIN_H)
            start = pl.multiple_of(r0 * GRID_W, GRID_W)
            kw = k_ref[0, pl.ds(start, n_lat), :]
            vw = v_ref[0, pl.ds(start, n_lat), :]
            lo = slice(i * GRID_W, (i + 1) * GRID_W)
            hi = slice(nq + i * GRID_W, nq + (i + 1) * GRID_W)
            qs = jnp.concatenate([qs_all[lo], qs_all[hi]], axis=0)
            p = jnp.exp2(_dot_nt(qs, kw) + bias_ref[r - r0]).astype(BF16)
            tot = _dot(p, _with_ones(vw)) + jnp.concatenate([o_ctx[lo], o_ctx[hi]], axis=0)
            o = tot[:, :LANES] / tot[:, LANES:]
            o_ref[0, lo, :] = _pair_merge(o, GRID_W).astype(BF16)
        return
    for i in range(rows_per_step):
        r = rb * rows_per_step + i
        r0 = jnp.clip(r - NA_WIN_H // 2, 0, n_rows - NA_WIN_H)
        start = pl.multiple_of(r0 * GRID_W, GRID_W)
        kw = k_ref[0, pl.ds(start, n_lat), :]
        vw = v_ref[0, pl.ds(start, n_lat), :]
        qs = _pair_stack(q_ref[0, i * GRID_W:(i + 1) * GRID_W, :])
        s_lat = _dot_nt(qs, kw) + bias_ref[r - r0]
        s_ctx = _dot_nt(qs, kc)
        m = jnp.maximum(jnp.max(s_lat, axis=-1, keepdims=True), jnp.max(s_ctx, axis=-1, keepdims=True))
        p_lat = jnp.exp2(s_lat - m)
        p_ctx = jnp.exp2(s_ctx - m)
        l = jnp.sum(p_lat, axis=-1, keepdims=True) + jnp.sum(p_ctx, axis=-1, keepdims=True)
        o = _dot(p_lat.astype(BF16), vw) + _dot(p_ctx.astype(BF16), vc)
        o = o / l
        o_ref[0, i * GRID_W:(i + 1) * GRID_W, :] = _pair_merge(o, GRID_W).astype(BF16)


def _na_attention(q, k, v, kc, vc, bias, score_bound, *, rows_per_step=8):
    bound = score_bound + jnp.max(jnp.where(bias > 0.5 * NEG_INF, jnp.abs(bias), 0.0))
    return lax.cond(bound <= EXP2_SAFE_RANGE,
                    lambda: _na_call(q, k, v, kc, vc, bias, rows_per_step=rows_per_step, direct=True),
                    lambda: _na_call(q, k, v, kc, vc, bias, rows_per_step=rows_per_step, direct=False))


def _na_call(q, k, v, kc, vc, bias, *, rows_per_step, direct):
    b, s, _ = q.shape
    c = kc.shape[1]
    n_rows = s // GRID_W
    tq = rows_per_step * GRID_W
    return pl.pallas_call(
        functools.partial(_na_kernel, rows_per_step=rows_per_step, n_rows=n_rows, direct=direct),
        out_shape=jax.ShapeDtypeStruct((b, s, 512), BF16),
        grid=(b, 4, n_rows // rows_per_step),
        in_specs=[
            pl.BlockSpec((1, tq, LANES), lambda bi, j, i: (bi, i, j)),
            pl.BlockSpec((1, s, LANES), lambda bi, j, i: (bi, 0, j)),
            pl.BlockSpec((1, s, LANES), lambda bi, j, i: (bi, 0, j)),
            pl.BlockSpec((1, c, LANES), lambda bi, j, i: (bi, 0, j)),
            pl.BlockSpec((1, c, LANES), lambda bi, j, i: (bi, 0, j)),
            pl.BlockSpec((NA_WIN_H, None, LANES, NA_WIN_H * GRID_W), lambda bi, j, i: (0, j, 0, 0)),
        ],
        out_specs=pl.BlockSpec((1, tq, LANES), lambda bi, j, i: (bi, i, j)),
        compiler_params=_cparams(("parallel", "parallel", "parallel")),
        name="na_attention" + ("" if direct else "_online"),
    )(q, k, v, kc, vc, bias)


def _na_bias_table(rpb):
    w = GRID_W
    col = jnp.arange(w)
    c0 = jnp.clip(col - NA_WIN_W // 2, 0, w - NA_WIN_W)
    col_in = (col[None, :] >= c0[:, None]) & (col[None, :] < c0[:, None] + NA_WIN_W)
    left = (w - 1) - (NA_WIN_W - 1)
    ext = jnp.pad(rpb, ((0, 0), (0, 0), (left, 2 * w - left - (2 * NA_WIN_W - 1))), mode="edge")
    h, nr, _ = rpb.shape
    flat = jnp.broadcast_to(ext[:, :, None, :], (h, nr, w, 2 * w)).reshape(h, nr, w * 2 * w)
    toep = flat[:, :, :w * (2 * w - 1)].reshape(h, nr, w, 2 * w - 1)[:, :, :, w - 1:]
    toep = jnp.where(col_in[None, None], toep * LOG2E, NEG_INF)
    variants = []
    for v in range(NA_WIN_H):
        tv = toep[:, NA_WIN_H - 1 - v:2 * NA_WIN_H - 1 - v]
        variants.append(tv.transpose(0, 2, 1, 3).reshape(NA_HEADS // 2, 2 * w, NA_WIN_H * w))
    return jnp.stack(variants, axis=0).astype(F32)


def _flash_kernel(*refs, mode, online, tq, tk, s_len, c_len, lam_init):
    if mode == "diff":
        q_ref, k_ref, v_ref, kc_ref, vc_ref, lam_ref, sg_ref, o_ref = refs
    else:
        q_ref, k_ref, v_ref, kc_ref, vc_ref, o_ref = refs

    q = q_ref[0]
    if mode == "mla":
        q_parts = (q[:, :LANES], q[:, LANES:])
    else:
        qs = _pair_stack(q)

    def scores(kt):
        if mode == "mla":
            return jnp.concatenate([_dot_nt(q_parts[0], kt[:, :LANES]), _dot_nt(q_parts[1], kt[:, LANES:])], axis=0)
        return _dot_nt(qs, kt)

    def step(kt, vt, carry):
        m, l, acc = carry
        s = scores(kt)
        m_new = jnp.maximum(m, jnp.max(s, axis=-1, keepdims=True))
        alpha = jnp.exp2(m - m_new)
        p = jnp.exp2(s - m_new)
        l = alpha * l + jnp.sum(p, axis=-1, keepdims=True)
        acc = alpha * acc + _dot(p.astype(BF16), vt)
        return m_new, l, acc

    def body(t, carry):
        start = pl.multiple_of(t * tk, tk)
        return step(k_ref[0, pl.ds(start, tk), :], v_ref[0, pl.ds(start, tk), :], carry)

    def direct(kt, vt, acc):
        return acc + _dot(jnp.exp2(scores(kt)).astype(BF16), _with_ones(vt))

    if online:
        carry = (jnp.full((2 * tq, 1), -jnp.inf, F32), jnp.zeros((2 * tq, 1), F32),
                 jnp.zeros((2 * tq, LANES), F32))
        if s_len:
            carry = lax.fori_loop(0, s_len // tk, body, carry)
        if c_len:
            carry = step(kc_ref[0], vc_ref[0], carry)
        _, l, acc = carry
        o = acc / l
    else:
        acc = jnp.zeros((2 * tq, 2 * LANES), F32)
        for t in range(s_len // tk):
            acc = direct(k_ref[0, t * tk:(t + 1) * tk, :], v_ref[0, t * tk:(t + 1) * tk, :], acc)
        if c_len:
            acc = direct(kc_ref[0], vc_ref[0], acc)
        o = acc[:, :LANES] / acc[:, LANES:]
    if mode == "diff":
        lp = lam_ref[...]
        lam = (jnp.exp(jnp.sum(lp[0:1] * lp[1:2], axis=-1, keepdims=True))
               - jnp.exp(jnp.sum(lp[2:3] * lp[3:4], axis=-1, keepdims=True)) + lam_init)
        d = o[:tq] - lam * o[tq:]
        ms = jnp.mean(d * d, axis=-1, keepdims=True)
        o_ref[0] = (d * lax.rsqrt(ms + NORM_EPS) * sg_ref[...] * (1.0 - lam_init)).astype(BF16)
    else:
        o_ref[0] = _pair_merge(o, tq).astype(BF16)


def _flash(q, k, v, kc, vc, *, score_bound=None, **kw):
    if score_bound is None:
        return _flash_call(q, k, v, kc, vc, online=True, **kw)
    return lax.cond(score_bound <= EXP2_SAFE_RANGE,
                    lambda: _flash_call(q, k, v, kc, vc, online=False, **kw),
                    lambda: _flash_call(q, k, v, kc, vc, online=True, **kw))


def _flash_call(q, k, v, kc, vc, *, mode, online, kv_map, n_chunks, tq, tk, extra=(), lam_init=0.0):
    b, s, _ = q.shape
    qw = 2 * LANES if mode == "mla" else LANES
    c_len = kc.shape[1]
    if k is None:
        k, v, s_len = kc, vc, 0
    else:
        s_len = k.shape[1]
    kk = k.shape[1]
    in_specs = [
        pl.BlockSpec((1, tq, qw), lambda bi, j, i: (bi, i, j)),
        pl.BlockSpec((1, kk, qw), lambda bi, j, i: (bi, 0, kv_map(j) if mode != "mla" else j)),
        pl.BlockSpec((1, kk, LANES), lambda bi, j, i: (bi, 0, kv_map(j))),
        pl.BlockSpec((1, c_len, qw), lambda bi, j, i: (bi, 0, kv_map(j) if mode != "mla" else j)),
        pl.BlockSpec((1, c_len, LANES), lambda bi, j, i: (bi, 0, kv_map(j))),
    ]
    for e in extra:
        in_specs.append(pl.BlockSpec(e.shape, lambda bi, j, i: (0, 0)))
    return pl.pallas_call(
        functools.partial(_flash_kernel, mode=mode, online=online, tq=tq, tk=tk, s_len=s_len, c_len=c_len,
                          lam_init=lam_init),
        out_shape=jax.ShapeDtypeStruct((b, s, n_chunks * LANES), BF16),
        grid=(b, n_chunks, s // tq),
        in_specs=in_specs,
        out_specs=pl.BlockSpec((1, tq, LANES), lambda bi, j, i: (bi, i, j)),
        compiler_params=_cparams(("parallel", "parallel", "parallel")),
        name="flash_" + mode + ("_online" if online else ""),
    )(q, k, v, kc, vc, *extra)


def _post_attn_kernel(oa_ref, ob_ref, wa_ref, wb_ref, x_ref, g_ref, ng_ref, sh_ref, sc_ref, x1_ref, h2_ref):
    y = _dot(oa_ref[0], wa_ref[...]) + _dot(ob_ref[0], wb_ref[...])
    x1 = x_ref[0] + g_ref[0] * y
    x1_ref[0] = x1
    h2_ref[0] = _norm_mod(x1, ng_ref[...], sh_ref[0], sc_ref[0]).astype(BF16)


def _post_attn(oa, ob, wa, wb, x, gate, ng, shift, scale, *, tm):
    b, s, d = x.shape
    row = lambda bi, i: (bi, i, 0)
    vec = lambda bi, i: (bi, 0, 0)
    const = lambda bi, i: (0, 0)
    return pl.pallas_call(
        _post_attn_kernel,
        out_shape=(jax.ShapeDtypeStruct((b, s, d), F32), jax.ShapeDtypeStruct((b, s, d), BF16)),
        grid=(b, s // tm),
        in_specs=[
            pl.BlockSpec((1, tm, 512), row),
            pl.BlockSpec((1, tm, 512), row),
            pl.BlockSpec((512, d), const),
            pl.BlockSpec((512, d), const),
            pl.BlockSpec((1, tm, d), row),
            pl.BlockSpec((1, 1, d), vec),
            pl.BlockSpec((1, d), const),
            pl.BlockSpec((1, 1, d), vec),
            pl.BlockSpec((1, 1, d), vec),
        ],
        out_specs=(pl.BlockSpec((1, tm, d), row), pl.BlockSpec((1, tm, d), row)),
        compiler_params=_cparams(("parallel", "parallel")),
        name="post_attn",
    )(oa, ob, wa, wb, x, gate, ng, shift, scale)


def _ffn_kernel(h_ref, x1_ref, g_ref, wg_ref, wu_ref, wd_ref, o_ref, *, tf):
    h = h_ref[0]
    acc = jnp.zeros(o_ref.shape[1:], F32)
    for f in range(wg_ref.shape[1] // tf):
        sl = slice(f * tf, (f + 1) * tf)
        a = _dot(h, wg_ref[:, sl])
        u = _dot(h, wu_ref[:, sl])
        acc = acc + _dot((_silu(a) * u).astype(BF16), wd_ref[sl, :])
    o_ref[0] = x1_ref[0] + g_ref[0] * acc


def _ffn(h2, x1, gate, wg, wu, wd, *, tm, tf=256):
    b, s, d = x1.shape
    f = wg.shape[1]
    row = lambda bi, i: (bi, i, 0)
    vec = lambda bi, i: (bi, 0, 0)
    const = lambda bi, i: (0, 0)
    return pl.pallas_call(
        functools.partial(_ffn_kernel, tf=tf),
        out_shape=jax.ShapeDtypeStruct((b, s, d), F32),
        grid=(b, s // tm),
        in_specs=[
            pl.BlockSpec((1, tm, d), row),
            pl.BlockSpec((1, tm, d), row),
            pl.BlockSpec((1, 1, d), vec),
            pl.BlockSpec((d, f), const),
            pl.BlockSpec((d, f), const),
            pl.BlockSpec((f, d), const),
        ],
        out_specs=pl.BlockSpec((1, tm, d), row),
        compiler_params=_cparams(("parallel", "parallel")),
        name="ffn",
    )(h2, x1, gate, wg, wu, wd)


def _rms(t, g):
    return t * lax.rsqrt(jnp.mean(t * t, axis=-1, keepdims=True) + NORM_EPS) * g


def _head_norm_128(t, a_ref, at_ref, g, inv_n):
    ss = _dot((t * t).astype(BF16), a_ref[...])
    r = lax.rsqrt(ss * inv_n + NORM_EPS)
    r_hi = r.astype(BF16)
    r_lo = (r - r_hi.astype(F32)).astype(BF16)
    rb = _dot(r_hi, at_ref[...]) + _dot(r_lo, at_ref[...])
    return t * rb * g


def _odd_proj_kernel(x_ref, sh_ref, sc_ref, ng_ref, w_ref, wuq_ref, wukv_ref, e_ref, a_ref, at_ref, g_ref,
                     cos_ref, sin_ref, cosm_ref, sinm_ref,
                     qc_ref, kc_ref, vc_ref, qd_ref, kd_ref, vd_ref, *, use_rope):
    h = _norm_mod(x_ref[0], ng_ref[...], sh_ref[0], sc_ref[0])
    p = _dot(h.astype(BF16), w_ref[...])
    g = g_ref[...]
    e = e_ref[...]
    inv_n = 1.0 / HEAD_DIM
    qc = _group_norm(p[:, 0:512], e, g[0:1, 0:512], inv_n)
    kc = _group_norm(p[:, 512:768], e[0:256, 0:256], g[1:2, 0:256], inv_n)
    cq = _rms(p[:, 1024:1280], g[2:3, 0:256])
    ckv = _rms(p[:, 1280:1408], g[3:4, 0:128])
    krc = p[:, 1408:1536]
    qd = _dot(cq.astype(BF16), wuq_ref[...])
    kvv = _dot(ckv.astype(BF16), wukv_ref[...])
    kd = kvv[:, 0:1024] + jnp.concatenate([krc] * MLA_HEADS, axis=1)
    qd = _head_norm_128(qd, a_ref, at_ref, g[4:5], 1.0 / MLA_QK)
    kd = _head_norm_128(kd, a_ref, at_ref, g[5:6], 1.0 / MLA_QK)
    if use_rope:
        lane = lax.broadcasted_iota(jnp.int32, (1, LANES), 1)
        cos, sin = cos_ref[...], sin_ref[...]
        qc = _rope_chunks(qc, cos, sin, (lane & 16) != 0, 16)
        kc = _rope_chunks(kc, cos, sin, (lane & 16) != 0, 16)
        cosm, sinm = cosm_ref[...], sinm_ref[...]
        qd = _rope_chunks(qd, cosm, sinm, (lane & 8) != 0, 8)
        kd = _rope_chunks(kd, cosm, sinm, (lane & 8) != 0, 8)
    qc_ref[0] = qc.astype(BF16)
    kc_ref[0] = kc.astype(BF16)
    vc_ref[0] = p[:, 768:1024].astype(BF16)
    qd_ref[0] = qd.astype(BF16)
    kd_ref[0] = kd.astype(BF16)
    vd_ref[0] = kvv[:, 1024:1536].astype(BF16)


def _odd_proj(x, shift, scale, ng, w, wuq, wukv, e512, a, at, gains, cos, sin, cosm, sinm, *, use_rope, tm):
    b, s, d = x.shape
    row = lambda bi, i: (bi, i, 0)
    vec = lambda bi, i: (bi, 0, 0)
    const = lambda bi, i: (0, 0)
    tab = lambda bi, i: (i, 0)
    widths = (512, 256, 256, 1024, 1024, 512)
    return pl.pallas_call(
        functools.partial(_odd_proj_kernel, use_rope=use_rope),
        out_shape=tuple(jax.ShapeDtypeStruct((b, s, n), BF16) for n in widths),
        grid=(b, s // tm),
        in_specs=[
            pl.BlockSpec((1, tm, d), row),
            pl.BlockSpec((1, 1, d), vec),
            pl.BlockSpec((1, 1, d), vec),
            pl.BlockSpec((1, d), const),
            pl.BlockSpec(w.shape, const),
            pl.BlockSpec(wuq.shape, const),
            pl.BlockSpec(wukv.shape, const),
            pl.BlockSpec(e512.shape, const),
            pl.BlockSpec(a.shape, const),
            pl.BlockSpec(at.shape, const),
            pl.BlockSpec(gains.shape, const),
            pl.BlockSpec((tm, LANES), tab),
            pl.BlockSpec((tm, LANES), tab),
            pl.BlockSpec((tm, LANES), tab),
            pl.BlockSpec((tm, LANES), tab),
        ],
        out_specs=tuple(pl.BlockSpec((1, tm, n), row) for n in widths),
        compiler_params=_cparams(("parallel", "parallel")),
        name="odd_proj",
    )(x, shift, scale, ng, w, wuq, wukv, e512, a, at, gains, cos, sin, cosm, sinm)


def _mla_rope_tables(seq):
    cos, sin = _rope_tables(seq, MLA_ROPE, MLA_ROPE // 4)
    lane = jnp.arange(LANES)
    on = (lane >= MLA_NOPE) & (lane < MLA_QK)
    return jnp.where(on[None, :], cos, 1.0), jnp.where(on[None, :], sin, 0.0)


def _router_kernel(x1_ref, ng_ref, sh_ref, sc_ref, rt_ref, u_ref, ei_ref, gt_ref, rk_ref, cnt_ref, carry_ref):
    first = (pl.program_id(0) == 0) & (pl.program_id(1) == 0)

    @pl.when(first)
    def _():
        carry_ref[...] = jnp.zeros_like(carry_ref)

    h = _norm_mod(x1_ref[0], ng_ref[...], sh_ref[0], sc_ref[0])
    logits = lax.dot_general(rt_ref[...], h, (((1,), (1,)), ((), ())),
                             preferred_element_type=F32, precision=HIGHEST)
    eidx = lax.broadcasted_iota(jnp.int32, logits.shape, 0)
    m1 = jnp.max(logits, axis=0, keepdims=True)
    i1 = jnp.min(jnp.where(logits == m1, eidx, N_EXPERTS), axis=0, keepdims=True)
    rest = jnp.where(eidx == i1, -jnp.inf, logits)
    m2 = jnp.max(rest, axis=0, keepdims=True)
    i2 = jnp.min(jnp.where(rest == m2, eidx, N_EXPERTS), axis=0, keepdims=True)
    e2 = jnp.exp(m2 - m1)
    g1 = 1.0 / (1.0 + e2)
    sel1 = eidx == i1
    sel2 = eidx == i2
    onehot = jnp.where(sel1 | sel2, 1.0, 0.0)
    before = _dot(onehot.astype(BF16), u_ref[...]) + carry_ref[:, 0:1]
    r1 = jnp.sum(jnp.where(sel1, before, 0.0), axis=0, keepdims=True)
    r2 = jnp.sum(jnp.where(sel2, before, 0.0), axis=0, keepdims=True)
    ei_ref[...] = jnp.concatenate([i1, i2], axis=0)
    gt_ref[...] = jnp.concatenate([g1, e2 * g1], axis=0)
    rk_ref[...] = jnp.concatenate([r1, r2], axis=0).astype(jnp.int32)
    total = carry_ref[...] + jnp.sum(onehot, axis=1, keepdims=True)
    carry_ref[...] = total
    cnt_ref[...] = total.astype(jnp.int32)


def _router(x1, ng, shift, scale, router_t, *, tm):
    b, s, d = x1.shape
    n = b * s
    nt = s // tm
    tri = (jnp.arange(tm)[:, None] < jnp.arange(tm)[None, :]).astype(BF16)
    flat = lambda bi, i: (0, bi * nt + i)
    return pl.pallas_call(
        _router_kernel,
        out_shape=(jax.ShapeDtypeStruct((2, n), jnp.int32), jax.ShapeDtypeStruct((2, n), F32),
                   jax.ShapeDtypeStruct((2, n), jnp.int32), jax.ShapeDtypeStruct((N_EXPERTS, LANES), jnp.int32)),
        grid=(b, nt),
        in_specs=[
            pl.BlockSpec((1, tm, d), lambda bi, i: (bi, i, 0)),
            pl.BlockSpec((1, d), lambda bi, i: (0, 0)),
            pl.BlockSpec((1, 1, d), lambda bi, i: (bi, 0, 0)),
            pl.BlockSpec((1, 1, d), lambda bi, i: (bi, 0, 0)),
            pl.BlockSpec((N_EXPERTS, d), lambda bi, i: (0, 0)),
            pl.BlockSpec((tm, tm), lambda bi, i: (0, 0)),
        ],
        out_specs=(pl.BlockSpec((2, tm), flat), pl.BlockSpec((2, tm), flat), pl.BlockSpec((2, tm), flat),
                   pl.BlockSpec((N_EXPERTS, LANES), lambda bi, i: (0, 0))),
        scratch_shapes=[pltpu.VMEM((N_EXPERTS, LANES), F32)],
        compiler_params=_cparams(("arbitrary", "arbitrary")),
        name="router",
    )(x1, ng, shift, scale, router_t, tri)


MOE_SUB = 256
MOE_BLK = 1024
MOE_TC = 256


def _lookup(table, idx):
    hit = idx[..., None] == jnp.arange(table.shape[0])
    return jnp.sum(jnp.where(hit, table, 0), axis=-1)


def _moe_plan(ei, rk, counts, n_tok):
    n_chunks = n_tok // MOE_TC
    cap = 2 * n_tok + N_EXPERTS * MOE_BLK
    nb_sub = cap // MOE_SUB
    nb_blk = cap // MOE_BLK
    padded = ((counts + MOE_BLK - 1) // MOE_BLK) * MOE_BLK
    pad_end = jnp.cumsum(padded)
    pad_start = pad_end - padded
    dest = _lookup(pad_start, ei) + rk

    onehot = ei.reshape(2, n_chunks, MOE_TC)[..., None] == jnp.arange(N_EXPERTS)
    cnt = jnp.sum(onehot, axis=(0, 2)).astype(jnp.int32)
    cum = jnp.concatenate([jnp.zeros((1, N_EXPERTS), jnp.int32), jnp.cumsum(cnt, axis=0)], axis=0)

    lo = pad_start[None, :] + cum[:-1]
    has = cnt > 0
    blk0 = lo // MOE_SUB
    two = has & ((lo + cnt - 1) // MOE_SUB > blk0)
    blk1 = jnp.minimum(blk0 + 1, nb_sub - 1)
    blk0 = jnp.where(has, blk0, 0)
    c_blk0 = blk0.reshape(-1).astype(jnp.int32)
    c_blk1 = jnp.where(two, blk1, blk0).reshape(-1).astype(jnp.int32)
    c_flag = (has.astype(jnp.int32) + 2 * two.astype(jnp.int32)).reshape(-1)

    sb = jnp.arange(nb_sub)
    e_sb = jnp.minimum(jnp.sum(sb[:, None] * MOE_SUB >= pad_end[None, :], axis=1), N_EXPERTS - 1)
    counts_sb = _lookup(counts, e_sb)
    r0 = sb * MOE_SUB - _lookup(pad_start, e_sb)
    valid_sb = (sb * MOE_SUB < pad_end[-1]) & (r0 < counts_sb)
    r1 = jnp.minimum(r0 + MOE_SUB, counts_sb) - 1
    hit_sb = e_sb[:, None] == jnp.arange(N_EXPERTS)[None, :]
    cum_sb = jnp.sum(jnp.where(hit_sb[:, None, :], cum[None, 1:, :], 0), axis=-1)
    cmin = jnp.sum(cum_sb <= r0[:, None], axis=1)
    cmax = jnp.sum(cum_sb <= r1[:, None], axis=1)
    items = jnp.where(valid_sb, cmax - cmin + 1, 0)
    off_end = jnp.cumsum(items)
    off = off_end - items
    total = off_end[-1]
    w_max = (2 * n_tok) // MOE_SUB + N_EXPERTS + N_EXPERTS * (n_chunks - 1)
    w = jnp.arange(w_max)
    wv = w < total
    wq = jnp.minimum(w, total - 1)
    w_sb = jnp.minimum(jnp.sum(off_end[None, :] <= wq[:, None], axis=1), nb_sub - 1)
    off_w = _lookup(off, w_sb)
    w_chunk = _lookup(cmin, w_sb) + (wq - off_w)
    w_flag = wv.astype(jnp.int32) + 2 * (wq == off_w).astype(jnp.int32)

    bi = jnp.arange(nb_blk)
    e_blk = jnp.minimum(jnp.sum(bi[:, None] * MOE_BLK >= pad_end[None, :], axis=1), N_EXPERTS - 1)
    rows = jnp.where(bi * MOE_BLK < pad_end[-1],
                     _lookup(counts, e_blk) - (bi * MOE_BLK - _lookup(pad_start, e_blk)), 0)
    n_sub = jnp.clip((rows + MOE_SUB - 1) // MOE_SUB, 0, MOE_BLK // MOE_SUB)
    return dict(dest=dest.astype(jnp.int32), cap=cap,
                w_sb=w_sb.astype(jnp.int32), w_chunk=w_chunk.astype(jnp.int32), w_flag=w_flag,
                e_blk=e_blk.astype(jnp.int32), n_sub=n_sub.astype(jnp.int32),
                c_blk0=c_blk0, c_blk1=c_blk1, c_flag=c_flag)


def _moe_gather_kernel(wsb_ref, wch_ref, wfl_ref, t_ref, d_ref, g_ref, xg_ref, gs_ref):
    w = pl.program_id(0)
    flag = wfl_ref[w]
    base = wsb_ref[w] * MOE_SUB

    def contrib():
        d = d_ref[...]
        g = g_ref[...]
        srow = lax.broadcasted_iota(jnp.int32, (MOE_SUB, MOE_TC), 0) + base
        hit0 = srow == d[0:1]
        hit1 = srow == d[1:2]
        p = jnp.where(hit0 | hit1, 1.0, 0.0).astype(BF16)
        rows = _dot(p, t_ref[...]).astype(BF16)
        gate = jnp.sum(jnp.where(hit0, g[0:1], 0.0) + jnp.where(hit1, g[1:2], 0.0), axis=-1, keepdims=True)
        return rows, gate

    @pl.when(flag == 3)
    def _():
        rows, gate = contrib()
        xg_ref[...] = rows
        gs_ref[...] = gate

    @pl.when(flag == 1)
    def _():
        rows, gate = contrib()
        xg_ref[...] += rows
        gs_ref[...] += gate


def _moe_gather(t, dest, gates, plan):
    n_tok, d = t.shape
    cap = plan["cap"]
    w_max = plan["w_sb"].shape[0]
    grid_spec = pltpu.PrefetchScalarGridSpec(
        num_scalar_prefetch=3,
        grid=(w_max,),
        in_specs=[
            pl.BlockSpec((MOE_TC, d), lambda w, sb, ch, fl: (ch[w], 0)),
            pl.BlockSpec((2, MOE_TC), lambda w, sb, ch, fl: (0, ch[w])),
            pl.BlockSpec((2, MOE_TC), lambda w, sb, ch, fl: (0, ch[w])),
        ],
        out_specs=(pl.BlockSpec((MOE_SUB, d), lambda w, sb, ch, fl: (sb[w], 0)),
                   pl.BlockSpec((MOE_SUB, 1), lambda w, sb, ch, fl: (sb[w], 0))),
    )
    return pl.pallas_call(
        _moe_gather_kernel,
        out_shape=(jax.ShapeDtypeStruct((cap, d), BF16), jax.ShapeDtypeStruct((cap, 1), F32)),
        grid_spec=grid_spec,
        compiler_params=_cparams(("arbitrary",)),
        name="moe_gather",
    )(plan["w_sb"], plan["w_chunk"], plan["w_flag"], t, dest, gates)


def _moe_expert_kernel(eb_ref, ns_ref, x_ref, gs_ref, w1_ref, w3_ref, w2_ref, y_ref):
    n_sub = ns_ref[pl.program_id(0)]
    subs = MOE_BLK // MOE_SUB

    def mlp(rows):
        xs = x_ref[rows, :]
        acc = jnp.zeros((rows.stop - rows.start, y_ref.shape[1]), F32)
        for f in range(w1_ref.shape[1] // MOE_TF):
            cols = slice(f * MOE_TF, (f + 1) * MOE_TF)
            a = _dot(xs, w1_ref[:, cols])
            u = _dot(xs, w3_ref[:, cols])
            acc = acc + _dot((_silu(a) * u).astype(BF16), w2_ref[cols, :])
        y_ref[rows, :] = (acc * gs_ref[rows, :]).astype(BF16)

    @pl.when(n_sub == subs)
    def _():
        mlp(slice(0, MOE_BLK))

    for sub in range(subs):
        rows = slice(sub * MOE_SUB, (sub + 1) * MOE_SUB)

        @pl.when((n_sub < subs) & (sub < n_sub))
        def _():
            mlp(rows)

        @pl.when(sub >= n_sub)
        def _():
            y_ref[rows, :] = jnp.zeros((MOE_SUB, y_ref.shape[1]), BF16)


def _moe_experts(xg, gs, w1, w3, w2, plan):
    cap, d = xg.shape
    f = w1.shape[2]
    once = pl.Buffered(1)
    grid_spec = pltpu.PrefetchScalarGridSpec(
        num_scalar_prefetch=2,
        grid=(cap // MOE_BLK,),
        in_specs=[
            pl.BlockSpec((MOE_BLK, d), lambda i, eb, ns: (i, 0)),
            pl.BlockSpec((MOE_BLK, 1), lambda i, eb, ns: (i, 0)),
            pl.BlockSpec((None, d, f), lambda i, eb, ns: (eb[i], 0, 0), pipeline_mode=once),
            pl.BlockSpec((None, d, f), lambda i, eb, ns: (eb[i], 0, 0), pipeline_mode=once),
            pl.BlockSpec((None, f, d), lambda i, eb, ns: (eb[i], 0, 0), pipeline_mode=once),
        ],
        out_specs=pl.BlockSpec((MOE_BLK, d), lambda i, eb, ns: (i, 0)),
    )
    return pl.pallas_call(
        _moe_expert_kernel,
        out_shape=jax.ShapeDtypeStruct((cap, d), BF16),
        grid_spec=grid_spec,
        compiler_params=_cparams(("arbitrary",)),
        name="moe_experts",
    )(plan["e_blk"], plan["n_sub"], xg, gs, w1, w3, w2)


def _moe_combine_kernel(b0_ref, b1_ref, fl_ref, dt_ref, y0_ref, y1_ref, x1_ref, g_ref, o_ref, acc_ref):
    c = pl.program_id(0)
    e = pl.program_id(1)
    idx = c * N_EXPERTS + e
    flag = fl_ref[idx]

    @pl.when(e == 0)
    def _():
        acc_ref[...] = jnp.zeros_like(acc_ref)

    def window(y_ref, blk):
        d = dt_ref[...]
        scol = lax.broadcasted_iota(jnp.int32, (MOE_TC, MOE_SUB), 1) + blk * MOE_SUB
        p = jnp.where((scol == d[:, 0:1]) | (scol == d[:, 1:2]), 1.0, 0.0).astype(BF16)
        acc_ref[...] += _dot(p, y_ref[...])

    @pl.when((flag & 1) != 0)
    def _():
        window(y0_ref, b0_ref[idx])

    @pl.when((flag & 2) != 0)
    def _():
        window(y1_ref, b1_ref[idx])

    @pl.when(e == N_EXPERTS - 1)
    def _():
        o_ref[...] = x1_ref[...] + g_ref[0] * acc_ref[...]


def _moe_combine(y, dest_t, x1, gate, plan, seq):
    n_tok, d = x1.shape
    n_chunks = n_tok // MOE_TC
    per_b = seq // MOE_TC
    grid_spec = pltpu.PrefetchScalarGridSpec(
        num_scalar_prefetch=3,
        grid=(n_chunks, N_EXPERTS),
        in_specs=[
            pl.BlockSpec((MOE_TC, 2), lambda c, e, b0, b1, fl: (c, 0)),
            pl.BlockSpec((MOE_SUB, d), lambda c, e, b0, b1, fl: (b0[c * N_EXPERTS + e], 0)),
            pl.BlockSpec((MOE_SUB, d), lambda c, e, b0, b1, fl: (b1[c * N_EXPERTS + e], 0)),
            pl.BlockSpec((MOE_TC, d), lambda c, e, b0, b1, fl: (c, 0)),
            pl.BlockSpec((1, 1, d), lambda c, e, b0, b1, fl: (c // per_b, 0, 0)),
        ],
        out_specs=pl.BlockSpec((MOE_TC, d), lambda c, e, b0, b1, fl: (c, 0)),
        scratch_shapes=[pltpu.VMEM((MOE_TC, d), F32)],
    )
    return pl.pallas_call(
        _moe_combine_kernel,
        out_shape=jax.ShapeDtypeStruct((n_tok, d), F32),
        grid_spec=grid_spec,
        compiler_params=_cparams(("arbitrary", "arbitrary")),
        name="moe_combine",
    )(plan["c_blk0"], plan["c_blk1"], plan["c_flag"], dest_t, y, y, x1, gate)


def _split_mod(mod_l, batch):
    d = D_MODEL
    lat = tuple(mod_l[:batch, k * d:(k + 1) * d][:, None, :] for k in range(ADA_CHUNKS))
    ctx = tuple(jnp.broadcast_to(mod_l[batch:batch + 1, k * d:(k + 1) * d][:, None, :], (batch, 1, d))
                for k in range(ADA_CHUNKS))
    return lat, ctx


def _score_bound(q_gain, k_gain, n):
    return 1.02 * n * jnp.max(jnp.abs(q_gain)) * jnp.max(jnp.abs(k_gain))


def _block_ones(n, block):
    idx = jnp.arange(n) // block
    return (idx[:, None] == idx[None, :]).astype(BF16)


def _even_layer(x, xc, mod_l, norm1_g, norm2_g, w_in, w_out, na_q_g, na_k_g, na_rpb, diff_q_g, diff_k_g,
                lq1, lk1, lq2, lk2, subln_g, wg, wu, wd, layer_idx, need_ctx):
    b, s, d = x.shape
    (sh1, sc1, g1, sh2, sc2, g2), (csh1, csc1, cg1, csh2, csc2, cg2) = _split_mod(mod_l, b)
    qscale = HEAD_DIM ** -0.5 * LOG2E
    lam_init = 0.8 - 0.6 * math.exp(-0.3 * layer_idx)
    hg = jnp.stack([jnp.tile(na_q_g, 8) * qscale, jnp.tile(na_k_g, 8),
                    jnp.tile(diff_q_g, 8) * qscale, jnp.tile(diff_k_g, 8)]).astype(F32)
    e512 = _block_ones(512, HEAD_DIM)
    cos, sin = _rope_tables(s, HEAD_DIM, 16)
    ng1 = norm1_g[None, :]
    ng2 = norm2_g[None, :]
    w_in_b = w_in.astype(BF16)
    wo_a = w_out[:512].astype(BF16)
    wo_b = w_out[512:].astype(BF16)
    wg_b, wu_b, wd_b = wg.astype(BF16), wu.astype(BF16), wd.astype(BF16)
    lam_p = jnp.stack([lq1, lk1, lq2, lk2]).astype(F32)
    sg = subln_g[None, :].astype(F32)
    ident = lambda j: j

    tm = min(512, s)
    tc = xc.shape[1]
    qa, ka, va, qb, kb, vb = _even_proj(x, sh1, sc1, ng1, w_in_b, e512, hg, cos, sin, use_rope=True, tm=tm)
    qca, kca, vca, qcb, kcb, vcb = _even_proj(xc, csh1, csc1, ng1, w_in_b, e512, hg, cos, sin, use_rope=False, tm=tc)
    out_a = _na_attention(qa, ka, va, kca, vca, _na_bias_table(na_rpb), _score_bound(hg[0], hg[1], HEAD_DIM))
    out_b = _flash(qb, kb, vb, kcb, vcb, mode="diff", kv_map=ident, n_chunks=4, tq=FLASH_TQ, tk=FLASH_TK,
                   extra=(lam_p, sg), lam_init=lam_init, score_bound=_score_bound(hg[2], hg[3], HEAD_DIM))
    x1, h2 = _post_attn(out_a, out_b, wo_a, wo_b, x, g1, ng2, sh2, sc2, tm=tm)
    x2 = _ffn(h2, x1, g2, wg_b, wu_b, wd_b, tm=tm)
    if not need_ctx:
        return x2, None
    oca = _flash(qca, None, None, kca, vca, mode="pair", kv_map=ident, n_chunks=4, tq=tc, tk=tc)
    ocb = _flash(qcb, None, None, kcb, vcb, mode="diff", kv_map=ident, n_chunks=4, tq=tc, tk=tc,
                 extra=(lam_p, sg), lam_init=lam_init)
    xc1, hc2 = _post_attn(oca, ocb, wo_a, wo_b, xc, cg1, ng2, csh2, csc2, tm=tc)
    xc2 = _ffn(hc2, xc1, cg2, wg_b, wu_b, wd_b, tm=tc)
    return x2, xc2


def _odd_layer(x, xc, mod_l, norm1_g, norm2_g, w_in, w_out, gqa_q_g, gqa_k_g, cq_g, w_uq, ckv_g, w_ukv,
               mla_q_g, mla_k_g, router, w1, w3, w2):
    b, s, d = x.shape
    n_tok = b * s
    (sh1, sc1, g1, sh2, sc2, g2), (csh1, csc1, _, _, _, _) = _split_mod(mod_l, b)
    ng1 = norm1_g[None, :]
    ng2 = norm2_g[None, :]

    z = lambda n: jnp.zeros((d, n), w_in.dtype)
    k0, k1, v0, v1 = (w_in[:, 512 + 64 * i:576 + 64 * i] for i in range(4))
    w_p = jnp.concatenate([w_in[:, 0:512], k0, k0, k1, k1, v0, v0, v1, v1, w_in[:, 768:1024], w_in[:, 1024:1152],
                           z(MLA_NOPE), w_in[:, 1152:1184], z(LANES - MLA_QK)], axis=1).astype(BF16)
    wuq_p = jnp.pad(w_uq.reshape(MLA_Q_RANK, MLA_HEADS, MLA_QK), ((0, 0), (0, 0), (0, LANES - MLA_QK)))
    wuq_p = wuq_p.reshape(MLA_Q_RANK, MLA_HEADS * LANES).astype(BF16)
    ukv = w_ukv.reshape(MLA_KV_RANK, MLA_HEADS, MLA_NOPE + MLA_V)
    uk = jnp.pad(ukv[:, :, :MLA_NOPE], ((0, 0), (0, 0), (0, LANES - MLA_NOPE))).reshape(MLA_KV_RANK, -1)
    uv = ukv[:, :, MLA_NOPE:].reshape(MLA_KV_RANK, -1)
    wukv_p = jnp.concatenate([uk, uv], axis=1).astype(BF16)
    e512 = _block_ones(512, HEAD_DIM)
    a = (jnp.arange(MLA_HEADS * LANES)[:, None] // LANES == jnp.arange(LANES)[None, :]).astype(BF16)
    at = a.T
    qscale = HEAD_DIM ** -0.5 * LOG2E
    mscale = MLA_QK ** -0.5 * LOG2E
    pad_row = lambda v: jnp.pad(v, (0, MLA_HEADS * LANES - v.shape[0]))
    pad_head = lambda v: jnp.tile(jnp.pad(v, (0, LANES - MLA_QK)), MLA_HEADS)
    gains = jnp.stack([pad_row(jnp.tile(gqa_q_g, 8) * qscale), pad_row(jnp.tile(gqa_k_g, 4)),
                       pad_row(cq_g), pad_row(ckv_g), pad_head(mla_q_g) * mscale, pad_head(mla_k_g),
                       jnp.zeros((MLA_HEADS * LANES,), F32), jnp.zeros((MLA_HEADS * LANES,), F32)]).astype(F32)
    cos, sin = _rope_tables(s, HEAD_DIM, 16)
    cosm, sinm = _mla_rope_tables(s)
    wo_a = w_out[:512].astype(BF16)
    wo_b = w_out[512:].astype(BF16)

    tm = min(512, s)
    tc = xc.shape[1]
    proj = functools.partial(_odd_proj, ng=ng1, w=w_p, wuq=wuq_p, wukv=wukv_p, e512=e512, a=a, at=at, gains=gains,
                             cos=cos, sin=sin, cosm=cosm, sinm=sinm)
    qc, kc, vc, qd, kd, vd = proj(x, sh1, sc1, use_rope=True, tm=tm)
    _, kcc, vcc, _, kcd, vcd = proj(xc, csh1, csc1, use_rope=False, tm=tc)
    out_c = _flash(qc, kc, vc, kcc, vcc, mode="pair", kv_map=lambda j: j // 2, n_chunks=4, tq=FLASH_TQ, tk=FLASH_TK,
                   score_bound=_score_bound(gains[0], gains[1], HEAD_DIM))
    out_d = _flash(qd, kd, vd, kcd, vcd, mode="mla", kv_map=lambda j: j, n_chunks=4, tq=FLASH_TQ, tk=FLASH_TK,
                   score_bound=_score_bound(gains[4], gains[5], MLA_QK))
    x1, h2 = _post_attn(out_c, out_d, wo_a, wo_b, x, g1, ng2, sh2, sc2, tm=tm)

    ei, gt, rk, cnt = _router(x1, ng2, sh2, sc2, router.T.astype(F32), tm=tm)
    plan = _moe_plan(ei, rk, cnt[:, 0], n_tok)
    xg, gs = _moe_gather(h2.reshape(n_tok, d), plan["dest"], gt, plan)
    y = _moe_experts(xg, gs, w1.astype(BF16), w3.astype(BF16), w2.astype(BF16), plan)
    out = _moe_combine(y, plan["dest"].T, x1.reshape(n_tok, d), g2, plan, s)
    return out.reshape(b, s, d)


def _mod_vectors(c, c_ctx, ada_w, ada_b):
    b = c.shape[0]
    cc = jnp.zeros((MOD_ROWS, D_MODEL), F32).at[:b].set(c).at[b].set(c_ctx)
    return _modvec(cc, ada_w, ada_b[:, None, :])


def kernel(x, c, ctx, c_ctx, ada_w, ada_b, norm1_g, norm2_g, ev_w_in, ev_w_out, na_q_g, na_k_g, na_rpb,
           diff_q_g, diff_k_g, diff_lq1, diff_lk1, diff_lq2, diff_lk2, diff_subln_g,
           ffn_w_gate, ffn_w_up, ffn_w_down, od_w_in, od_w_out, gqa_q_g, gqa_k_g, mla_cq_g, mla_w_uq,
           mla_ckv_g, mla_w_ukv, mla_q_g, mla_k_g, moe_router, moe_w1, moe_w3, moe_w2):
    mod = _mod_vectors(c, c_ctx, ada_w, ada_b)
    x, xc = _even_layer(x, ctx, mod[0], norm1_g[0], norm2_g[0], ev_w_in[0], ev_w_out[0], na_q_g[0], na_k_g[0],
                        na_rpb[0], diff_q_g[0], diff_k_g[0], diff_lq1[0], diff_lk1[0], diff_lq2[0], diff_lk2[0],
                        diff_subln_g[0], ffn_w_gate[0], ffn_w_up[0], ffn_w_down[0], 0, True)
    return _odd_layer(x, xc, mod[1], norm1_g[1], norm2_g[1], od_w_in[0], od_w_out[0], gqa_q_g[0], gqa_k_g[0],
                      mla_cq_g[0], mla_w_uq[0], mla_ckv_g[0], mla_w_ukv[0], mla_q_g[0], mla_k_g[0],
                      moe_router[0], moe_w1[0], moe_w3[0], moe_w2[0])
```

```python
import functools
import math

import jax
import jax.numpy as jnp
from jax import lax
from jax.experimental import pallas as pl
from jax.experimental.pallas import tpu as pltpu

F32 = jnp.float32
BF16 = jnp.bfloat16
HIGHEST = lax.Precision.HIGHEST

D_MODEL = 1024
GRID_W = 64
HEAD_DIM = 64
ROPE_THETA = 10000.0
NORM_EPS = 1e-6
NEG_INF = -1e30
ADA_CHUNKS = 6
LOG2E = 1.4426950408889634

NA_HEADS = 8
NA_WIN_H = 8
NA_WIN_W = 16
DIFF_HEADS = 4
DIFF_V_DIM = 2 * HEAD_DIM
GQA_Q_HEADS = 8
GQA_KV_HEADS = 2
MLA_HEADS = 8
MLA_NOPE = 64
MLA_ROPE = 32
MLA_QK = MLA_NOPE + MLA_ROPE
MLA_V = 64
MLA_Q_RANK = 256
MLA_KV_RANK = 128
D_FF = 2816
N_EXPERTS = 8
D_FF_EXPERT = 3584

LANES = 128
MXU_DIM = 256
VMEM_LIMIT = 56 * 1024 * 1024
MOD_ROWS = 16
MOE_TF = 512
FLASH_TQ = 512
FLASH_TK = 512
EXP2_SAFE_RANGE = 64.0


def _cparams(sem):
    return pltpu.CompilerParams(dimension_semantics=sem, vmem_limit_bytes=VMEM_LIMIT)


def _dot(a, b):
    return jnp.dot(a, b, preferred_element_type=F32)


def _dot_nt(a, b):
    return lax.dot_general(a, b, (((1,), (1,)), ((), ())), preferred_element_type=F32)


def _silu(x):
    return x * (1.0 / (1.0 + jnp.exp(-x)))


def _modvec_kernel(c_ref, w_ref, b_ref, o_ref):
    s = _silu(c_ref[...])
    o_ref[...] = jnp.dot(s, w_ref[...], preferred_element_type=F32, precision=HIGHEST) + b_ref[...]


def _modvec(cc, ada_w, ada_b):
    n_layers, d, n = ada_w.shape
    tn = 768
    return pl.pallas_call(
        _modvec_kernel,
        out_shape=jax.ShapeDtypeStruct((n_layers, MOD_ROWS, n), F32),
        grid=(n_layers, n // tn),
        in_specs=[
            pl.BlockSpec((MOD_ROWS, d), lambda l, j: (0, 0)),
            pl.BlockSpec((None, d, tn), lambda l, j: (l, 0, j)),
            pl.BlockSpec((None, 1, tn), lambda l, j: (l, 0, j)),
        ],
        out_specs=pl.BlockSpec((None, MOD_ROWS, tn), lambda l, j: (l, 0, j)),
        compiler_params=_cparams(("arbitrary", "arbitrary")),
        name="modvec",
    )(cc, ada_w, ada_b)


def _norm_mod(x, g, shift, scale):
    ms = jnp.mean(x * x, axis=-1, keepdims=True)
    return (x * lax.rsqrt(ms + NORM_EPS) * g) * (1.0 + scale) + shift


def _group_norm(t, e, g, inv_n):
    ss = _dot((t * t).astype(BF16), e)
    return t * lax.rsqrt(ss * inv_n + NORM_EPS) * g


def _rope_chunks(t, cos, sin, lane_hi, shift):
    outs = []
    for c in range(t.shape[1] // LANES):
        tc = t[:, c * LANES:(c + 1) * LANES]
        up = pltpu.roll(tc, LANES - shift, 1)
        dn = pltpu.roll(tc, shift, 1)
        outs.append(tc * cos + jnp.where(lane_hi, dn, up) * sin)
    return outs[0] if len(outs) == 1 else jnp.concatenate(outs, axis=1)


def _rope_tables(seq, unit, pair_shift):
    pos = jnp.arange(seq)
    rows, cols = pos // GRID_W, pos % GRID_W
    quarter = unit // 4
    assert quarter == pair_shift
    freqs = ROPE_THETA ** (-jnp.arange(quarter, dtype=F32) / quarter)
    lane = jnp.arange(LANES)
    u = lane % unit
    use_col = (u // (unit // 2)) == 1
    fi = u % quarter
    p = jnp.where(use_col[None, :], cols[:, None], rows[:, None]).astype(F32)
    ang = p * freqs[fi][None, :]
    second = ((u % (unit // 2)) // quarter) == 1
    cos = jnp.cos(ang)
    sin = jnp.where(second[None, :], jnp.sin(ang), -jnp.sin(ang))
    return cos, sin


def _even_proj_kernel(x_ref, sh_ref, sc_ref, ng_ref, w_ref, e_ref, hg_ref, cos_ref, sin_ref,
                      qa_ref, ka_ref, va_ref, qb_ref, kb_ref, vb_ref, *, use_rope):
    h = _norm_mod(x_ref[0], ng_ref[...], sh_ref[0], sc_ref[0])
    p = _dot(h.astype(BF16), w_ref[...])
    hg = hg_ref[...]
    e = e_ref[...]
    inv_n = 1.0 / HEAD_DIM
    qa = _group_norm(p[:, 0:512], e, hg[0:1], inv_n)
    ka = _group_norm(p[:, 512:1024], e, hg[1:2], inv_n)
    qb = _group_norm(p[:, 1536:2048], e, hg[2:3], inv_n)
    kb = _group_norm(p[:, 2048:2560], e, hg[3:4], inv_n)
    if use_rope:
        lane = lax.broadcasted_iota(jnp.int32, (1, LANES), 1)
        lane_hi = (lane & 16) != 0
        cos, sin = cos_ref[...], sin_ref[...]
        qb = _rope_chunks(qb, cos, sin, lane_hi, 16)
        kb = _rope_chunks(kb, cos, sin, lane_hi, 16)
    qa_ref[0] = qa.astype(BF16)
    ka_ref[0] = ka.astype(BF16)
    va_ref[0] = p[:, 1024:1536].astype(BF16)
    qb_ref[0] = qb.astype(BF16)
    kb_ref[0] = kb.astype(BF16)
    vb_ref[0] = p[:, 2560:3072].astype(BF16)


def _even_proj(x, shift, scale, ng, w, e512, hg, cos, sin, *, use_rope, tm):
    b, s, d = x.shape
    n = w.shape[1]
    row = lambda bi, i: (bi, i, 0)
    vec = lambda bi, i: (bi, 0, 0)
    const = lambda bi, i: (0, 0)
    out = jax.ShapeDtypeStruct((b, s, 512), BF16)
    return pl.pallas_call(
        functools.partial(_even_proj_kernel, use_rope=use_rope),
        out_shape=(out,) * 6,
        grid=(b, s // tm),
        in_specs=[
            pl.BlockSpec((1, tm, d), row),
            pl.BlockSpec((1, 1, d), vec),
            pl.BlockSpec((1, 1, d), vec),
            pl.BlockSpec((1, d), const),
            pl.BlockSpec((d, n), const),
            pl.BlockSpec((512, 512), const),
            pl.BlockSpec((4, 512), const),
            pl.BlockSpec((tm, LANES), lambda bi, i: (i, 0)),
            pl.BlockSpec((tm, LANES), lambda bi, i: (i, 0)),
        ],
        out_specs=(pl.BlockSpec((1, tm, 512), row),) * 6,
        compiler_params=_cparams(("parallel", "parallel")),
        name="even_proj",
    )(x, shift, scale, ng, w, e512, hg, cos, sin)


def _pair_stack(q):
    lane = lax.broadcasted_iota(jnp.int32, q.shape, 1)
    zero = jnp.zeros_like(q)
    return jnp.concatenate([jnp.where(lane < HEAD_DIM, q, zero), jnp.where(lane >= HEAD_DIM, q, zero)], axis=0)


def _pair_merge(o, tq):
    lane = lax.broadcasted_iota(jnp.int32, (tq, LANES), 1)
    return jnp.where(lane < HEAD_DIM, o[:tq], o[tq:])


def _with_ones(v):
    return jnp.concatenate([v, jnp.ones(v.shape, BF16)], axis=1)


def _na_kernel(q_ref, k_ref, v_ref, kc_ref, vc_ref, bias_ref, o_ref, *, rows_per_step, n_rows, direct):
    rb = pl.program_id(2)
    kc = kc_ref[0]
    vc = vc_ref[0]
    n_lat = NA_WIN_H * GRID_W
    if direct:
        nq = rows_per_step * GRID_W
        qs_all = _pair_stack(q_ref[0])
        o_ctx = _dot(jnp.exp2(_dot_nt(qs_all, kc)).astype(BF16), _with_ones(vc))
        for i in range(rows_per_step):
            r = rb * rows_per_step + i
            r0 = jnp.clip(r - NA_WIN_H // 2, 0, n_rows - NA_WIN_H)
            start = pl.multiple_of(r0 * GRID_W, GRID_W)
            kw = k_ref[0, pl.ds(start, n_lat), :]
            vw = v_ref[0, pl.ds(start, n_lat), :]
            lo = slice(i * GRID_W, (i + 1) * GRID_W)
            hi = slice(nq + i * GRID_W, nq + (i + 1) * GRID_W)
            qs = jnp.concatenate([qs_all[lo], qs_all[hi]], axis=0)
            p = jnp.exp2(_dot_nt(qs, kw) + bias_ref[r - r0]).astype(BF16)
            tot = _dot(p, _with_ones(vw)) + jnp.concatenate([o_ctx[lo], o_ctx[hi]], axis=0)
            o = tot[:, :LANES] / tot[:, LANES:]
            o_ref[0, lo, :] = _pair_merge(o, GRID_W).astype(BF16)
        return
    for i in range(rows_per_step):
        r = rb * rows_per_step + i
        r0 = jnp.clip(r - NA_WIN_H // 2, 0, n_rows - NA_WIN_H)
        start = pl.multiple_of(r0 * GRID_W, GRID_W)
        kw = k_ref[0, pl.ds(start, n_lat), :]
        vw = v_ref[0, pl.ds(start, n_lat), :]
        qs = _pair_stack(q_ref[0, i * GRID_W:(i + 1) * GRID_W, :])
        s_lat = _dot_nt(qs, kw) + bias_ref[r - r0]
        s_ctx = _dot_nt(qs, kc)
        m = jnp.maximum(jnp.max(s_lat, axis=-1, keepdims=True), jnp.max(s_ctx, axis=-1, keepdims=True))
        p_lat = jnp.exp2(s_lat - m)
        p_ctx = jnp.exp2(s_ctx - m)
        l = jnp.sum(p_lat, axis=-1, keepdims=True) + jnp.sum(p_ctx, axis=-1, keepdims=True)
        o = _dot(p_lat.astype(BF16), vw) + _dot(p_ctx.astype(BF16), vc)
        o = o / l
        o_ref[0, i * GRID_W:(i + 1) * GRID_W, :] = _pair_merge(o, GRID_W).astype(BF16)


def _na_attention(q, k, v, kc, vc, bias, score_bound, *, rows_per_step=8):
    bound = score_bound + jnp.max(jnp.where(bias > 0.5 * NEG_INF, jnp.abs(bias), 0.0))
    return lax.cond(bound <= EXP2_SAFE_RANGE,
                    lambda: _na_call(q, k, v, kc, vc, bias, rows_per_step=rows_per_step, direct=True),
                    lambda: _na_call(q, k, v, kc, vc, bias, rows_per_step=rows_per_step, direct=False))


def _na_call(q, k, v, kc, vc, bias, *, rows_per_step, direct):
    b, s, _ = q.shape
    c = kc.shape[1]
    n_rows = s // GRID_W
    tq = rows_per_step * GRID_W
    return pl.pallas_call(
        functools.partial(_na_kernel, rows_per_step=rows_per_step, n_rows=n_rows, direct=direct),
        out_shape=jax.ShapeDtypeStruct((b, s, 512), BF16),
        grid=(b, 4, n_rows // rows_per_step),
        in_specs=[
            pl.BlockSpec((1, tq, LANES), lambda bi, j, i: (bi, i, j)),
            pl.BlockSpec((1, s, LANES), lambda bi, j, i: (bi, 0, j)),
            pl.BlockSpec((1, s, LANES), lambda bi, j, i: (bi, 0, j)),
            pl.BlockSpec((1, c, LANES), lambda bi, j, i: (bi, 0, j)),
            pl.BlockSpec((1, c, LANES), lambda bi, j, i: (bi, 0, j)),
            pl.BlockSpec((NA_WIN_H, None, LANES, NA_WIN_H * GRID_W), lambda bi, j, i: (0, j, 0, 0)),
        ],
        out_specs=pl.BlockSpec((1, tq, LANES), lambda bi, j, i: (bi, i, j)),
        compiler_params=_cparams(("parallel", "parallel", "parallel")),
        name="na_attention" + ("" if direct else "_online"),
    )(q, k, v, kc, vc, bias)


def _na_bias_table(rpb):
    w = GRID_W
    col = jnp.arange(w)
    c0 = jnp.clip(col - NA_WIN_W // 2, 0, w - NA_WIN_W)
    col_in = (col[None, :] >= c0[:, None]) & (col[None, :] < c0[:, None] + NA_WIN_W)
    left = (w - 1) - (NA_WIN_W - 1)
    ext = jnp.pad(rpb, ((0, 0), (0, 0), (left, 2 * w - left - (2 * NA_WIN_W - 1))), mode="edge")
    h, nr, _ = rpb.shape
    flat = jnp.broadcast_to(ext[:, :, None, :], (h, nr, w, 2 * w)).reshape(h, nr, w * 2 * w)
    toep = flat[:, :, :w * (2 * w - 1)].reshape(h, nr, w, 2 * w - 1)[:, :, :, w - 1:]
    toep = jnp.where(col_in[None, None], toep * LOG2E, NEG_INF)
    variants = []
    for v in range(NA_WIN_H):
        tv = toep[:, NA_WIN_H - 1 - v:2 * NA_WIN_H - 1 - v]
        variants.append(tv.transpose(0, 2, 1, 3).reshape(NA_HEADS // 2, 2 * w, NA_WIN_H * w))
    return jnp.stack(variants, axis=0).astype(F32)


def _flash_kernel(*refs, mode, online, tq, tk, s_len, c_len, lam_init):
    if mode == "diff":
        q_ref, k_ref, v_ref, kc_ref, vc_ref, lam_ref, sg_ref, o_ref = refs
    else:
        q_ref, k_ref, v_ref, kc_ref, vc_ref, o_ref = refs

    q = q_ref[0]
    if mode == "mla":
        q_parts = (q[:, :LANES], q[:, LANES:])
    else:
        qs = _pair_stack(q)

    def scores(kt):
        if mode == "mla":
            return jnp.concatenate([_dot_nt(q_parts[0], kt[:, :LANES]), _dot_nt(q_parts[1], kt[:, LANES:])], axis=0)
        return _dot_nt(qs, kt)

    def step(kt, vt, carry):
        m, l, acc = carry
        s = scores(kt)
        m_new = jnp.maximum(m, jnp.max(s, axis=-1, keepdims=True))
        alpha = jnp.exp2(m - m_new)
        p = jnp.exp2(s - m_new)
        l = alpha * l + jnp.sum(p, axis=-1, keepdims=True)
        acc = alpha * acc + _dot(p.astype(BF16), vt)
        return m_new, l, acc

    def body(t, carry):
        start = pl.multiple_of(t * tk, tk)
        return step(k_ref[0, pl.ds(start, tk), :], v_ref[0, pl.ds(start, tk), :], carry)

    def direct(kt, vt, acc):
        return acc + _dot(jnp.exp2(scores(kt)).astype(BF16), _with_ones(vt))

    if online:
        carry = (jnp.full((2 * tq, 1), -jnp.inf, F32), jnp.zeros((2 * tq, 1), F32),
                 jnp.zeros((2 * tq, LANES), F32))
        if s_len:
            carry = lax.fori_loop(0, s_len // tk, body, carry)
        if c_len:
            carry = step(kc_ref[0], vc_ref[0], carry)
        _, l, acc = carry
        o = acc / l
    else:
        acc = jnp.zeros((2 * tq, 2 * LANES), F32)
        for t in range(s_len // tk):
            acc = direct(k_ref[0, t * tk:(t + 1) * tk, :], v_ref[0, t * tk:(t + 1) * tk, :], acc)
        if c_len:
            acc = direct(kc_ref[0], vc_ref[0], acc)
        o = acc[:, :LANES] / acc[:, LANES:]
    if mode == "diff":
        lp = lam_ref[...]
        lam = (jnp.exp(jnp.sum(lp[0:1] * lp[1:2], axis=-1, keepdims=True))
               - jnp.exp(jnp.sum(lp[2:3] * lp[3:4], axis=-1, keepdims=True)) + lam_init)
        d = o[:tq] - lam * o[tq:]
        ms = jnp.mean(d * d, axis=-1, keepdims=True)
        o_ref[0] = (d * lax.rsqrt(ms + NORM_EPS) * sg_ref[...] * (1.0 - lam_init)).astype(BF16)
    else:
        o_ref[0] = _pair_merge(o, tq).astype(BF16)


def _flash(q, k, v, kc, vc, *, score_bound=None, **kw):
    if score_bound is None:
        return _flash_call(q, k, v, kc, vc, online=True, **kw)
    return lax.cond(score_bound <= EXP2_SAFE_RANGE,
                    lambda: _flash_call(q, k, v, kc, vc, online=False, **kw),
                    lambda: _flash_call(q, k, v, kc, vc, online=True, **kw))


def _flash_call(q, k, v, kc, vc, *, mode, online, kv_map, n_chunks, tq, tk, extra=(), lam_init=0.0):
    b, s, _ = q.shape
    qw = 2 * LANES if mode == "mla" else LANES
    c_len = kc.shape[1]
    if k is None:
        k, v, s_len = kc, vc, 0
    else:
        s_len = k.shape[1]
    kk = k.shape[1]
    in_specs = [
        pl.BlockSpec((1, tq, qw), lambda bi, j, i: (bi, i, j)),
        pl.BlockSpec((1, kk, qw), lambda bi, j, i: (bi, 0, kv_map(j) if mode != "mla" else j)),
        pl.BlockSpec((1, kk, LANES), lambda bi, j, i: (bi, 0, kv_map(j))),
        pl.BlockSpec((1, c_len, qw), lambda bi, j, i: (bi, 0, kv_map(j) if mode != "mla" else j)),
        pl.BlockSpec((1, c_len, LANES), lambda bi, j, i: (bi, 0, kv_map(j))),
    ]
    for e in extra:
        in_specs.append(pl.BlockSpec(e.shape, lambda bi, j, i: (0, 0)))
    return pl.pallas_call(
        functools.partial(_flash_kernel, mode=mode, online=online, tq=tq, tk=tk, s_len=s_len, c_len=c_len,
                          lam_init=lam_init),
        out_shape=jax.ShapeDtypeStruct((b, s, n_chunks * LANES), BF16),
        grid=(b, n_chunks, s // tq),
        in_specs=in_specs,
        out_specs=pl.BlockSpec((1, tq, LANES), lambda bi, j, i: (bi, i, j)),
        compiler_params=_cparams(("parallel", "parallel", "parallel")),
        name="flash_" + mode + ("_online" if online else ""),
    )(q, k, v, kc, vc, *extra)


def _post_attn_kernel(oa_ref, ob_ref, wa_ref, wb_ref, x_ref, g_ref, ng_ref, sh_ref, sc_ref, x1_ref, h2_ref):
    y = _dot(oa_ref[0], wa_ref[...]) + _dot(ob_ref[0], wb_ref[...])
    x1 = x_ref[0] + g_ref[0] * y
    x1_ref[0] = x1
    h2_ref[0] = _norm_mod(x1, ng_ref[...], sh_ref[0], sc_ref[0]).astype(BF16)


def _post_attn(oa, ob, wa, wb, x, gate, ng, shift, scale, *, tm):
    b, s, d = x.shape
    row = lambda bi, i: (bi, i, 0)
    vec = lambda bi, i: (bi, 0, 0)
    const = lambda bi, i: (0, 0)
    return pl.pallas_call(
        _post_attn_kernel,
        out_shape=(jax.ShapeDtypeStruct((b, s, d), F32), jax.ShapeDtypeStruct((b, s, d), BF16)),
        grid=(b, s // tm),
        in_specs=[
            pl.BlockSpec((1, tm, 512), row),
            pl.BlockSpec((1, tm, 512), row),
            pl.BlockSpec((512, d), const),
            pl.BlockSpec((512, d), const),
            pl.BlockSpec((1, tm, d), row),
            pl.BlockSpec((1, 1, d), vec),
            pl.BlockSpec((1, d), const),
            pl.BlockSpec((1, 1, d), vec),
            pl.BlockSpec((1, 1, d), vec),
        ],
        out_specs=(pl.BlockSpec((1, tm, d), row), pl.BlockSpec((1, tm, d), row)),
        compiler_params=_cparams(("parallel", "parallel")),
        name="post_attn",
    )(oa, ob, wa, wb, x, gate, ng, shift, scale)


def _ffn_kernel(h_ref, x1_ref, g_ref, wg_ref, wu_ref, wd_ref, o_ref, *, tf):
    h = h_ref[0]
    acc = jnp.zeros(o_ref.shape[1:], F32)
    for f in range(wg_ref.shape[1] // tf):
        sl = slice(f * tf, (f + 1) * tf)
        a = _dot(h, wg_ref[:, sl])
        u = _dot(h, wu_ref[:, sl])
        acc = acc + _dot((_silu(a) * u).astype(BF16), wd_ref[sl, :])
    o_ref[0] = x1_ref[0] + g_ref[0] * acc


def _ffn(h2, x1, gate, wg, wu, wd, *, tm, tf=256):
    b, s, d = x1.shape
    f = wg.shape[1]
    row = lambda bi, i: (bi, i, 0)
    vec = lambda bi, i: (bi, 0, 0)
    const = lambda bi, i: (0, 0)
    return pl.pallas_call(
        functools.partial(_ffn_kernel, tf=tf),
        out_shape=jax.ShapeDtypeStruct((b, s, d), F32),
        grid=(b, s // tm),
        in_specs=[
            pl.BlockSpec((1, tm, d), row),
            pl.BlockSpec((1, tm, d), row),
            pl.BlockSpec((1, 1, d), vec),
            pl.BlockSpec((d, f), const),
            pl.BlockSpec((d, f), const),
            pl.BlockSpec((f, d), const),
        ],
        out_specs=pl.BlockSpec((1, tm, d), row),
        compiler_params=_cparams(("parallel", "parallel")),
        name="ffn",
    )(h2, x1, gate, wg, wu, wd)


def _rms(t, g):
    return t * lax.rsqrt(jnp.mean(t * t, axis=-1, keepdims=True) + NORM_EPS) * g


def _head_norm_128(t, a_ref, at_ref, g, inv_n):
    ss = _dot((t * t).astype(BF16), a_ref[...])
    r = lax.rsqrt(ss * inv_n + NORM_EPS)
    r_hi = r.astype(BF16)
    r_lo = (r - r_hi.astype(F32)).astype(BF16)
    rb = _dot(r_hi, at_ref[...]) + _dot(r_lo, at_ref[...])
    return t * rb * g


def _odd_proj_kernel(x_ref, sh_ref, sc_ref, ng_ref, w_ref, wuq_ref, wukv_ref, e_ref, a_ref, at_ref, g_ref,
                     cos_ref, sin_ref, cosm_ref, sinm_ref,
                     qc_ref, kc_ref, vc_ref, qd_ref, kd_ref, vd_ref, *, use_rope):
    h = _norm_mod(x_ref[0], ng_ref[...], sh_ref[0], sc_ref[0])
    p = _dot(h.astype(BF16), w_ref[...])
    g = g_ref[...]
    e = e_ref[...]
    inv_n = 1.0 / HEAD_DIM
    qc = _group_norm(p[:, 0:512], e, g[0:1, 0:512], inv_n)
    kc = _group_norm(p[:, 512:768], e[0:256, 0:256], g[1:2, 0:256], inv_n)
    cq = _rms(p[:, 1024:1280], g[2:3, 0:256])
    ckv = _rms(p[:, 1280:1408], g[3:4, 0:128])
    krc = p[:, 1408:1536]
    qd = _dot(cq.astype(BF16), wuq_ref[...])
    kvv = _dot(ckv.astype(BF16), wukv_ref[...])
    kd = kvv[:, 0:1024] + jnp.concatenate([krc] * MLA_HEADS, axis=1)
    qd = _head_norm_128(qd, a_ref, at_ref, g[4:5], 1.0 / MLA_QK)
    kd = _head_norm_128(kd, a_ref, at_ref, g[5:6], 1.0 / MLA_QK)
    if use_rope:
        lane = lax.broadcasted_iota(jnp.int32, (1, LANES), 1)
        cos, sin = cos_ref[...], sin_ref[...]
        qc = _rope_chunks(qc, cos, sin, (lane & 16) != 0, 16)
        kc = _rope_chunks(kc, cos, sin, (lane & 16) != 0, 16)
        cosm, sinm = cosm_ref[...], sinm_ref[...]
        qd = _rope_chunks(qd, cosm, sinm, (lane & 8) != 0, 8)
        kd = _rope_chunks(kd, cosm, sinm, (lane & 8) != 0, 8)
    qc_ref[0] = qc.astype(BF16)
    kc_ref[0] = kc.astype(BF16)
    vc_ref[0] = p[:, 768:1024].astype(BF16)
    qd_ref[0] = qd.astype(BF16)
    kd_ref[0] = kd.astype(BF16)
    vd_ref[0] = kvv[:, 1024:1536].astype(BF16)


def _odd_proj(x, shift, scale, ng, w, wuq, wukv, e512, a, at, gains, cos, sin, cosm, sinm, *, use_rope, tm):
    b, s, d = x.shape
    row = lambda bi, i: (bi, i, 0)
    vec = lambda bi, i: (bi, 0, 0)
    const = lambda bi, i: (0, 0)
    tab = lambda bi, i: (i, 0)
    widths = (512, 256, 256, 1024, 1024, 512)
    return pl.pallas_call(
        functools.partial(_odd_proj_kernel, use_rope=use_rope),
        out_shape=tuple(jax.ShapeDtypeStruct((b, s, n), BF16) for n in widths),
        grid=(b, s // tm),
        in_specs=[
            pl.BlockSpec((1, tm, d), row),
            pl.BlockSpec((1, 1, d), vec),
            pl.BlockSpec((1, 1, d), vec),
            pl.BlockSpec((1, d), const),
            pl.BlockSpec(w.shape, const),
            pl.BlockSpec(wuq.shape, const),
            pl.BlockSpec(wukv.shape, const),
            pl.BlockSpec(e512.shape, const),
            pl.BlockSpec(a.shape, const),
            pl.BlockSpec(at.shape, const),
            pl.BlockSpec(gains.shape, const),
            pl.BlockSpec((tm, LANES), tab),
            pl.BlockSpec((tm, LANES), tab),
            pl.BlockSpec((tm, LANES), tab),
            pl.BlockSpec((tm, LANES), tab),
        ],
        out_specs=tuple(pl.BlockSpec((1, tm, n), row) for n in widths),
        compiler_params=_cparams(("parallel", "parallel")),
        name="odd_proj",
    )(x, shift, scale, ng, w, wuq, wukv, e512, a, at, gains, cos, sin, cosm, sinm)


def _mla_rope_tables(seq):
    cos, sin = _rope_tables(seq, MLA_ROPE, MLA_ROPE // 4)
    lane = jnp.arange(LANES)
    on = (lane >= MLA_NOPE) & (lane < MLA_QK)
    return jnp.where(on[None, :], cos, 1.0), jnp.where(on[None, :], sin, 0.0)


def _router_kernel(x1_ref, ng_ref, sh_ref, sc_ref, rt_ref, u_ref, ei_ref, gt_ref, rk_ref, cnt_ref, carry_ref):
    first = (pl.program_id(0) == 0) & (pl.program_id(1) == 0)

    @pl.when(first)
    def _():
        carry_ref[...] = jnp.zeros_like(carry_ref)

    h = _norm_mod(x1_ref[0], ng_ref[...], sh_ref[0], sc_ref[0])
    logits = lax.dot_general(rt_ref[...], h, (((1,), (1,)), ((), ())),
                             preferred_element_type=F32, precision=HIGHEST)
    eidx = lax.broadcasted_iota(jnp.int32, logits.shape, 0)
    m1 = jnp.max(logits, axis=0, keepdims=True)
    i1 = jnp.min(jnp.where(logits == m1, eidx, N_EXPERTS), axis=0, keepdims=True)
    rest = jnp.where(eidx == i1, -jnp.inf, logits)
    m2 = jnp.max(rest, axis=0, keepdims=True)
    i2 = jnp.min(jnp.where(rest == m2, eidx, N_EXPERTS), axis=0, keepdims=True)
    e2 = jnp.exp(m2 - m1)
    g1 = 1.0 / (1.0 + e2)
    sel1 = eidx == i1
    sel2 = eidx == i2
    onehot = jnp.where(sel1 | sel2, 1.0, 0.0)
    before = _dot(onehot.astype(BF16), u_ref[...]) + carry_ref[:, 0:1]
    r1 = jnp.sum(jnp.where(sel1, before, 0.0), axis=0, keepdims=True)
    r2 = jnp.sum(jnp.where(sel2, before, 0.0), axis=0, keepdims=True)
    ei_ref[...] = jnp.concatenate([i1, i2], axis=0)
    gt_ref[...] = jnp.concatenate([g1, e2 * g1], axis=0)
    rk_ref[...] = jnp.concatenate([r1, r2], axis=0).astype(jnp.int32)
    total = carry_ref[...] + jnp.sum(onehot, axis=1, keepdims=True)
    carry_ref[...] = total
    cnt_ref[...] = total.astype(jnp.int32)


def _router(x1, ng, shift, scale, router_t, *, tm):
    b, s, d = x1.shape
    n = b * s
    nt = s // tm
    tri = (jnp.arange(tm)[:, None] < jnp.arange(tm)[None, :]).astype(BF16)
    flat = lambda bi, i: (0, bi * nt + i)
    return pl.pallas_call(
        _router_kernel,
        out_shape=(jax.ShapeDtypeStruct((2, n), jnp.int32), jax.ShapeDtypeStruct((2, n), F32),
                   jax.ShapeDtypeStruct((2, n), jnp.int32), jax.ShapeDtypeStruct((N_EXPERTS, LANES), jnp.int32)),
        grid=(b, nt),
        in_specs=[
            pl.BlockSpec((1, tm, d), lambda bi, i: (bi, i, 0)),
            pl.BlockSpec((1, d), lambda bi, i: (0, 0)),
            pl.BlockSpec((1, 1, d), lambda bi, i: (bi, 0, 0)),
            pl.BlockSpec((1, 1, d), lambda bi, i: (bi, 0, 0)),
            pl.BlockSpec((N_EXPERTS, d), lambda bi, i: (0, 0)),
            pl.BlockSpec((tm, tm), lambda bi, i: (0, 0)),
        ],
        out_specs=(pl.BlockSpec((2, tm), flat), pl.BlockSpec((2, tm), flat), pl.BlockSpec((2, tm), flat),
                   pl.BlockSpec((N_EXPERTS, LANES), lambda bi, i: (0, 0))),
        scratch_shapes=[pltpu.VMEM((N_EXPERTS, LANES), F32)],
        compiler_params=_cparams(("arbitrary", "arbitrary")),
        name="router",
    )(x1, ng, shift, scale, router_t, tri)


MOE_SUB = 256
MOE_BLK = 1024
MOE_TC = 256


def _lookup(table, idx):
    hit = idx[..., None] == jnp.arange(table.shape[0])
    return jnp.sum(jnp.where(hit, table, 0), axis=-1)


def _moe_plan(ei, rk, counts, n_tok):
    n_chunks = n_tok // MOE_TC
    cap = 2 * n_tok + N_EXPERTS * MOE_BLK
    nb_sub = cap // MOE_SUB
    nb_blk = cap // MOE_BLK
    padded = ((counts + MOE_BLK - 1) // MOE_BLK) * MOE_BLK
    pad_end = jnp.cumsum(padded)
    pad_start = pad_end - padded
    dest = _lookup(pad_start, ei) + rk

    onehot = ei.reshape(2, n_chunks, MOE_TC)[..., None] == jnp.arange(N_EXPERTS)
    cnt = jnp.sum(onehot, axis=(0, 2)).astype(jnp.int32)
    cum = jnp.concatenate([jnp.zeros((1, N_EXPERTS), jnp.int32), jnp.cumsum(cnt, axis=0)], axis=0)

    lo = pad_start[None, :] + cum[:-1]
    has = cnt > 0
    blk0 = lo // MOE_SUB
    two = has & ((lo + cnt - 1) // MOE_SUB > blk0)
    blk1 = jnp.minimum(blk0 + 1, nb_sub - 1)
    win_ok = jnp.stack([has, two], axis=-1).reshape(n_chunks, 2 * N_EXPERTS)
    win_blk = jnp.stack([blk0, blk1], axis=-1).reshape(n_chunks, 2 * N_EXPERTS)
    win_pos = jnp.cumsum(win_ok, axis=1) - win_ok
    slot_hit = win_ok[:, :, None] & (win_pos[:, :, None] == jnp.arange(2 * N_EXPERTS)[None, None, :])
    c_blocks = jnp.sum(jnp.where(slot_hit, win_blk[:, :, None], 0), axis=1).reshape(-1).astype(jnp.int32)
    c_count = jnp.sum(win_ok, axis=1).astype(jnp.int32)

    sb = jnp.arange(nb_sub)
    e_sb = jnp.minimum(jnp.sum(sb[:, None] * MOE_SUB >= pad_end[None, :], axis=1), N_EXPERTS - 1)
    counts_sb = _lookup(counts, e_sb)
    r0 = sb * MOE_SUB - _lookup(pad_start, e_sb)
    valid_sb = (sb * MOE_SUB < pad_end[-1]) & (r0 < counts_sb)
    r1 = jnp.minimum(r0 + MOE_SUB, counts_sb) - 1
    hit_sb = e_sb[:, None] == jnp.arange(N_EXPERTS)[None, :]
    cum_sb = jnp.sum(jnp.where(hit_sb[:, None, :], cum[None, 1:, :], 0), axis=-1)
    cmin = jnp.sum(cum_sb <= r0[:, None], axis=1)
    cmax = jnp.sum(cum_sb <= r1[:, None], axis=1)
    items = jnp.where(valid_sb, cmax - cmin + 1, 0)
    off_end = jnp.cumsum(items)
    off = off_end - items
    total = off_end[-1]
    w_max = (2 * n_tok) // MOE_SUB + N_EXPERTS + N_EXPERTS * (n_chunks - 1)
    w = jnp.arange(w_max)
    wv = w < total
    wq = jnp.minimum(w, total - 1)
    w_sb = jnp.minimum(jnp.sum(off_end[None, :] <= wq[:, None], axis=1), nb_sub - 1)
    off_w = _lookup(off, w_sb)
    w_chunk = _lookup(cmin, w_sb) + (wq - off_w)
    w_flag = wv.astype(jnp.int32) + 2 * (wq == off_w).astype(jnp.int32)

    bi = jnp.arange(nb_blk)
    e_blk = jnp.minimum(jnp.sum(bi[:, None] * MOE_BLK >= pad_end[None, :], axis=1), N_EXPERTS - 1)
    rows = jnp.where(bi * MOE_BLK < pad_end[-1],
                     _lookup(counts, e_blk) - (bi * MOE_BLK - _lookup(pad_start, e_blk)), 0)
    n_sub = jnp.clip((rows + MOE_SUB - 1) // MOE_SUB, 0, MOE_BLK // MOE_SUB)
    g_count = jnp.sum(items.reshape(nb_blk, MOE_BLK // MOE_SUB), axis=1)
    g_start = jnp.cumsum(g_count) - g_count
    return dict(dest=dest.astype(jnp.int32), cap=cap,
                w_sb=w_sb.astype(jnp.int32), w_chunk=w_chunk.astype(jnp.int32), w_flag=w_flag,
                g_start=g_start.astype(jnp.int32), g_count=g_count.astype(jnp.int32),
                e_blk=e_blk.astype(jnp.int32), n_sub=n_sub.astype(jnp.int32),
                c_blocks=c_blocks, c_count=c_count)


def _moe_gather_kernel(st_ref, cn_ref, ns_ref, wsb_ref, wch_ref, wfl_ref, t_hbm, d_ref, g_ref, xg_ref, gs_ref,
                       tbuf, sem):
    i = pl.program_id(0)
    n = cn_ref[i]
    s0 = st_ref[i]
    subs = MOE_BLK // MOE_SUB

    def chunk_copy(item, slot):
        start = pl.multiple_of(wch_ref[item] * MOE_TC, MOE_TC)
        return pltpu.make_async_copy(t_hbm.at[pl.ds(start, MOE_TC), :], tbuf.at[slot], sem.at[slot])

    @pl.when(n > 0)
    def _():
        chunk_copy(s0, 0).start()

    def body(k, carry):
        item = s0 + k
        slot = lax.rem(k, 2)
        chunk_copy(item, slot).wait()

        @pl.when(k + 1 < n)
        def _():
            chunk_copy(item + 1, 1 - slot).start()

        sb = wsb_ref[item]
        chunk = wch_ref[item]
        d = d_ref[chunk]
        g = g_ref[chunk]
        srow = lax.broadcasted_iota(jnp.int32, (MOE_SUB, MOE_TC), 0) + sb * MOE_SUB
        hit0 = srow == d[0:1]
        hit1 = srow == d[1:2]
        p = jnp.where(hit0 | hit1, 1.0, 0.0).astype(BF16)
        rows = _dot(p, tbuf[slot]).astype(BF16)
        gate = jnp.sum(jnp.where(hit0, g[0:1], 0.0) + jnp.where(hit1, g[1:2], 0.0), axis=-1, keepdims=True)
        r0 = pl.multiple_of((sb - i * subs) * MOE_SUB, MOE_SUB)
        first = wfl_ref[item] >= 2

        @pl.when(first)
        def _():
            xg_ref[pl.ds(r0, MOE_SUB), :] = rows
            gs_ref[pl.ds(r0, MOE_SUB), :] = gate

        @pl.when(jnp.logical_not(first))
        def _():
            xg_ref[pl.ds(r0, MOE_SUB), :] += rows
            gs_ref[pl.ds(r0, MOE_SUB), :] += gate

        return carry

    lax.fori_loop(0, n, body, 0)

    for sub in range(subs):
        @pl.when(sub >= ns_ref[i])
        def _():
            xg_ref[sub * MOE_SUB:(sub + 1) * MOE_SUB, :] = jnp.zeros((MOE_SUB, xg_ref.shape[1]), BF16)
            gs_ref[sub * MOE_SUB:(sub + 1) * MOE_SUB, :] = jnp.zeros((MOE_SUB, 1), F32)


def _moe_gather(t, dest, gates, plan):
    n_tok, d = t.shape
    cap = plan["cap"]
    n_chunks = n_tok // MOE_TC
    by_chunk = lambda a: a.reshape(2, n_chunks, MOE_TC).transpose(1, 0, 2)
    grid_spec = pltpu.PrefetchScalarGridSpec(
        num_scalar_prefetch=6,
        grid=(cap // MOE_BLK,),
        in_specs=[
            pl.BlockSpec(memory_space=pl.ANY),
            pl.BlockSpec((n_chunks, 2, MOE_TC), lambda i, *_: (0, 0, 0)),
            pl.BlockSpec((n_chunks, 2, MOE_TC), lambda i, *_: (0, 0, 0)),
        ],
        out_specs=(pl.BlockSpec((MOE_BLK, d), lambda i, *_: (i, 0)),
                   pl.BlockSpec((MOE_BLK, 1), lambda i, *_: (i, 0))),
        scratch_shapes=[pltpu.VMEM((2, MOE_TC, d), BF16), pltpu.SemaphoreType.DMA((2,))],
    )
    return pl.pallas_call(
        _moe_gather_kernel,
        out_shape=(jax.ShapeDtypeStruct((cap, d), BF16), jax.ShapeDtypeStruct((cap, 1), F32)),
        grid_spec=grid_spec,
        compiler_params=_cparams(("arbitrary",)),
        name="moe_gather",
    )(plan["g_start"], plan["g_count"], plan["n_sub"], plan["w_sb"], plan["w_chunk"], plan["w_flag"],
      t, by_chunk(dest), by_chunk(gates))


def _moe_expert_kernel(eb_ref, ns_ref, x_ref, gs_ref, w1_ref, w3_ref, w2_ref, y_ref):
    n_sub = ns_ref[pl.program_id(0)]
    subs = MOE_BLK // MOE_SUB

    def mlp(rows):
        xs = x_ref[rows, :]
        acc = jnp.zeros((rows.stop - rows.start, y_ref.shape[1]), F32)
        for f in range(w1_ref.shape[1] // MOE_TF):
            cols = slice(f * MOE_TF, (f + 1) * MOE_TF)
            a = _dot(xs, w1_ref[:, cols])
            u = _dot(xs, w3_ref[:, cols])
            acc = acc + _dot((_silu(a) * u).astype(BF16), w2_ref[cols, :])
        y_ref[rows, :] = (acc * gs_ref[rows, :]).astype(BF16)

    @pl.when(n_sub == subs)
    def _():
        mlp(slice(0, MOE_BLK))

    for sub in range(subs):
        rows = slice(sub * MOE_SUB, (sub + 1) * MOE_SUB)

        @pl.when((n_sub < subs) & (sub < n_sub))
        def _():
            mlp(rows)

        @pl.when(sub >= n_sub)
        def _():
            y_ref[rows, :] = jnp.zeros((MOE_SUB, y_ref.shape[1]), BF16)


def _moe_experts(xg, gs, w1, w3, w2, plan):
    cap, d = xg.shape
    f = w1.shape[2]
    once = pl.Buffered(1)
    grid_spec = pltpu.PrefetchScalarGridSpec(
        num_scalar_prefetch=2,
        grid=(cap // MOE_BLK,),
        in_specs=[
            pl.BlockSpec((MOE_BLK, d), lambda i, eb, ns: (i, 0)),
            pl.BlockSpec((MOE_BLK, 1), lambda i, eb, ns: (i, 0)),
            pl.BlockSpec((None, d, f), lambda i, eb, ns: (eb[i], 0, 0), pipeline_mode=once),
            pl.BlockSpec((None, d, f), lambda i, eb, ns: (eb[i], 0, 0), pipeline_mode=once),
            pl.BlockSpec((None, f, d), lambda i, eb, ns: (eb[i], 0, 0), pipeline_mode=once),
        ],
        out_specs=pl.BlockSpec((MOE_BLK, d), lambda i, eb, ns: (i, 0)),
    )
    return pl.pallas_call(
        _moe_expert_kernel,
        out_shape=jax.ShapeDtypeStruct((cap, d), BF16),
        grid_spec=grid_spec,
        compiler_params=_cparams(("arbitrary",)),
        name="moe_experts",
    )(plan["e_blk"], plan["n_sub"], xg, gs, w1, w3, w2)


MOE_WIN = 2 * N_EXPERTS


def _moe_combine_kernel(cn_ref, bl_ref, dt_ref, y_hbm, x1_ref, g_ref, o_ref, ybuf, sem):
    c = pl.program_id(0)
    n = cn_ref[c]
    d = dt_ref[...]

    def block_copy(k, slot):
        start = pl.multiple_of(bl_ref[c * MOE_WIN + k] * MOE_SUB, MOE_SUB)
        return pltpu.make_async_copy(y_hbm.at[pl.ds(start, MOE_SUB), :], ybuf.at[slot], sem.at[slot])

    @pl.when(n > 0)
    def _():
        block_copy(0, 0).start()

    def body(k, acc):
        slot = lax.rem(k, 2)
        block_copy(k, slot).wait()

        @pl.when(k + 1 < n)
        def _():
            block_copy(k + 1, 1 - slot).start()

        scol = lax.broadcasted_iota(jnp.int32, (MOE_TC, MOE_SUB), 1) + bl_ref[c * MOE_WIN + k] * MOE_SUB
        p = jnp.where((scol == d[:, 0:1]) | (scol == d[:, 1:2]), 1.0, 0.0).astype(BF16)
        return acc + _dot(p, ybuf[slot])

    acc = lax.fori_loop(0, n, body, jnp.zeros(o_ref.shape, F32))
    o_ref[...] = x1_ref[...] + g_ref[0] * acc


def _moe_combine(y, dest_t, x1, gate, plan, seq):
    n_tok, d = x1.shape
    n_chunks = n_tok // MOE_TC
    per_b = seq // MOE_TC
    grid_spec = pltpu.PrefetchScalarGridSpec(
        num_scalar_prefetch=2,
        grid=(n_chunks,),
        in_specs=[
            pl.BlockSpec((MOE_TC, 2), lambda c, *_: (c, 0)),
            pl.BlockSpec(memory_space=pl.ANY),
            pl.BlockSpec((MOE_TC, d), lambda c, *_: (c, 0)),
            pl.BlockSpec((1, 1, d), lambda c, *_: (c // per_b, 0, 0)),
        ],
        out_specs=pl.BlockSpec((MOE_TC, d), lambda c, *_: (c, 0)),
        scratch_shapes=[pltpu.VMEM((2, MOE_SUB, d), BF16), pltpu.SemaphoreType.DMA((2,))],
    )
    return pl.pallas_call(
        _moe_combine_kernel,
        out_shape=jax.ShapeDtypeStruct((n_tok, d), F32),
        grid_spec=grid_spec,
        compiler_params=_cparams(("arbitrary",)),
        name="moe_combine",
    )(plan["c_count"], plan["c_blocks"], dest_t, y, x1, gate)


def _split_mod(mod_l, batch):
    d = D_MODEL
    lat = tuple(mod_l[:batch, k * d:(k + 1) * d][:, None, :] for k in range(ADA_CHUNKS))
    ctx = tuple(jnp.broadcast_to(mod_l[batch:batch + 1, k * d:(k + 1) * d][:, None, :], (batch, 1, d))
                for k in range(ADA_CHUNKS))
    return lat, ctx


def _score_bound(q_gain, k_gain, n):
    return 1.02 * n * jnp.max(jnp.abs(q_gain)) * jnp.max(jnp.abs(k_gain))


def _block_ones(n, block):
    idx = jnp.arange(n) // block
    return (idx[:, None] == idx[None, :]).astype(BF16)


def _even_layer(x, xc, mod_l, norm1_g, norm2_g, w_in, w_out, na_q_g, na_k_g, na_rpb, diff_q_g, diff_k_g,
                lq1, lk1, lq2, lk2, subln_g, wg, wu, wd, layer_idx, need_ctx):
    b, s, d = x.shape
    (sh1, sc1, g1, sh2, sc2, g2), (csh1, csc1, cg1, csh2, csc2, cg2) = _split_mod(mod_l, b)
    qscale = HEAD_DIM ** -0.5 * LOG2E
    lam_init = 0.8 - 0.6 * math.exp(-0.3 * layer_idx)
    hg = jnp.stack([jnp.tile(na_q_g, 8) * qscale, jnp.tile(na_k_g, 8),
                    jnp.tile(diff_q_g, 8) * qscale, jnp.tile(diff_k_g, 8)]).astype(F32)
    e512 = _block_ones(512, HEAD_DIM)
    cos, sin = _rope_tables(s, HEAD_DIM, 16)
    ng1 = norm1_g[None, :]
    ng2 = norm2_g[None, :]
    w_in_b = w_in.astype(BF16)
    wo_a = w_out[:512].astype(BF16)
    wo_b = w_out[512:].astype(BF16)
    wg_b, wu_b, wd_b = wg.astype(BF16), wu.astype(BF16), wd.astype(BF16)
    lam_p = jnp.stack([lq1, lk1, lq2, lk2]).astype(F32)
    sg = subln_g[None, :].astype(F32)
    ident = lambda j: j

    tm = min(512, s)
    tc = xc.shape[1]
    qa, ka, va, qb, kb, vb = _even_proj(x, sh1, sc1, ng1, w_in_b, e512, hg, cos, sin, use_rope=True, tm=tm)
    qca, kca, vca, qcb, kcb, vcb = _even_proj(xc, csh1, csc1, ng1, w_in_b, e512, hg, cos, sin, use_rope=False, tm=tc)
    out_a = _na_attention(qa, ka, va, kca, vca, _na_bias_table(na_rpb), _score_bound(hg[0], hg[1], HEAD_DIM))
    out_b = _flash(qb, kb, vb, kcb, vcb, mode="diff", kv_map=ident, n_chunks=4, tq=FLASH_TQ, tk=FLASH_TK,
                   extra=(lam_p, sg), lam_init=lam_init, score_bound=_score_bound(hg[2], hg[3], HEAD_DIM))
    x1, h2 = _post_attn(out_a, out_b, wo_a, wo_b, x, g1, ng2, sh2, sc2, tm=tm)
    x2 = _ffn(h2, x1, g2, wg_b, wu_b, wd_b, tm=tm)
    if not need_ctx:
        return x2, None
    oca = _flash(qca, None, None, kca, vca, mode="pair", kv_map=ident, n_chunks=4, tq=tc, tk=tc)
    ocb = _flash(qcb, None, None, kcb, vcb, mode="diff", kv_map=ident, n_chunks=4, tq=tc, tk=tc,
                 extra=(lam_p, sg), lam_init=lam_init)
    xc1, hc2 = _post_attn(oca, ocb, wo_a, wo_b, xc, cg1, ng2, csh2, csc2, tm=tc)
    xc2 = _ffn(hc2, xc1, cg2, wg_b, wu_b, wd_b, tm=tc)
    return x2, xc2


def _odd_layer(x, xc, mod_l, norm1_g, norm2_g, w_in, w_out, gqa_q_g, gqa_k_g, cq_g, w_uq, ckv_g, w_ukv,
               mla_q_g, mla_k_g, router, w1, w3, w2):
    b, s, d = x.shape
    n_tok = b * s
    (sh1, sc1, g1, sh2, sc2, g2), (csh1, csc1, _, _, _, _) = _split_mod(mod_l, b)
    ng1 = norm1_g[None, :]
    ng2 = norm2_g[None, :]

    z = lambda n: jnp.zeros((d, n), w_in.dtype)
    k0, k1, v0, v1 = (w_in[:, 512 + 64 * i:576 + 64 * i] for i in range(4))
    w_p = jnp.concatenate([w_in[:, 0:512], k0, k0, k1, k1, v0, v0, v1, v1, w_in[:, 768:1024], w_in[:, 1024:1152],
                           z(MLA_NOPE), w_in[:, 1152:1184], z(LANES - MLA_QK)], axis=1).astype(BF16)
    wuq_p = jnp.pad(w_uq.reshape(MLA_Q_RANK, MLA_HEADS, MLA_QK), ((0, 0), (0, 0), (0, LANES - MLA_QK)))
    wuq_p = wuq_p.reshape(MLA_Q_RANK, MLA_HEADS * LANES).astype(BF16)
    ukv = w_ukv.reshape(MLA_KV_RANK, MLA_HEADS, MLA_NOPE + MLA_V)
    uk = jnp.pad(ukv[:, :, :MLA_NOPE], ((0, 0), (0, 0), (0, LANES - MLA_NOPE))).reshape(MLA_KV_RANK, -1)
    uv = ukv[:, :, MLA_NOPE:].reshape(MLA_KV_RANK, -1)
    wukv_p = jnp.concatenate([uk, uv], axis=1).astype(BF16)
    e512 = _block_ones(512, HEAD_DIM)
    a = (jnp.arange(MLA_HEADS * LANES)[:, None] // LANES == jnp.arange(LANES)[None, :]).astype(BF16)
    at = a.T
    qscale = HEAD_DIM ** -0.5 * LOG2E
    mscale = MLA_QK ** -0.5 * LOG2E
    pad_row = lambda v: jnp.pad(v, (0, MLA_HEADS * LANES - v.shape[0]))
    pad_head = lambda v: jnp.tile(jnp.pad(v, (0, LANES - MLA_QK)), MLA_HEADS)
    gains = jnp.stack([pad_row(jnp.tile(gqa_q_g, 8) * qscale), pad_row(jnp.tile(gqa_k_g, 4)),
                       pad_row(cq_g), pad_row(ckv_g), pad_head(mla_q_g) * mscale, pad_head(mla_k_g),
                       jnp.zeros((MLA_HEADS * LANES,), F32), jnp.zeros((MLA_HEADS * LANES,), F32)]).astype(F32)
    cos, sin = _rope_tables(s, HEAD_DIM, 16)
    cosm, sinm = _mla_rope_tables(s)
    wo_a = w_out[:512].astype(BF16)
    wo_b = w_out[512:].astype(BF16)

    tm = min(512, s)
    tc = xc.shape[1]
    proj = functools.partial(_odd_proj, ng=ng1, w=w_p, wuq=wuq_p, wukv=wukv_p, e512=e512, a=a, at=at, gains=gains,
                             cos=cos, sin=sin, cosm=cosm, sinm=sinm)
    qc, kc, vc, qd, kd, vd = proj(x, sh1, sc1, use_rope=True, tm=tm)
    _, kcc, vcc, _, kcd, vcd = proj(xc, csh1, csc1, use_rope=False, tm=tc)
    out_c = _flash(qc, kc, vc, kcc, vcc, mode="pair", kv_map=lambda j: j // 2, n_chunks=4, tq=FLASH_TQ, tk=FLASH_TK,
                   score_bound=_score_bound(gains[0], gains[1], HEAD_DIM))
    out_d = _flash(qd, kd, vd, kcd, vcd, mode="mla", kv_map=lambda j: j, n_chunks=4, tq=FLASH_TQ, tk=FLASH_TK,
                   score_bound=_score_bound(gains[4], gains[5], MLA_QK))
    x1, h2 = _post_attn(out_c, out_d, wo_a, wo_b, x, g1, ng2, sh2, sc2, tm=tm)

    ei, gt, rk, cnt = _router(x1, ng2, sh2, sc2, router.T.astype(F32), tm=tm)
    plan = _moe_plan(ei, rk, cnt[:, 0], n_tok)
    xg, gs = _moe_gather(h2.reshape(n_tok, d), plan["dest"], gt, plan)
    y = _moe_experts(xg, gs, w1.astype(BF16), w3.astype(BF16), w2.astype(BF16), plan)
    out = _moe_combine(y, plan["dest"].T, x1.reshape(n_tok, d), g2, plan, s)
    return out.reshape(b, s, d)


def _mod_vectors(c, c_ctx, ada_w, ada_b):
    b = c.shape[0]
    cc = jnp.zeros((MOD_ROWS, D_MODEL), F32).at[:b].set(c).at[b].set(c_ctx)
    return _modvec(cc, ada_w, ada_b[:, None, :])


def kernel(x, c, ctx, c_ctx, ada_w, ada_b, norm1_g, norm2_g, ev_w_in, ev_w_out, na_q_g, na_k_g, na_rpb,
           diff_q_g, diff_k_g, diff_lq1, diff_lk1, diff_lq2, diff_lk2, diff_subln_g,
           ffn_w_gate, ffn_w_up, ffn_w_down, od_w_in, od_w_out, gqa_q_g, gqa_k_g, mla_cq_g, mla_w_uq,
           mla_ckv_g, mla_w_ukv, mla_q_g, mla_k_g, moe_router, moe_w1, moe_w3, moe_w2):
    mod = _mod_vectors(c, c_ctx, ada_w, ada_b)
    x, xc = _even_layer(x, ctx, mod[0], norm1_g[0], norm2_g[0], ev_w_in[0], ev_w_out[0], na_q_g[0], na_k_g[0],
                        na_rpb[0], diff_q_g[0], diff_k_g[0], diff_lq1[0], diff_lk1[0], diff_lq2[0], diff_lk2[0],
                        diff_subln_g[0], ffn_w_gate[0], ffn_w_up[0], ffn_w_down[0], 0, True)
    return _odd_layer(x, xc, mod[1], norm1_g[1], norm2_g[1], od_w_in[0], od_w_out[0], gqa_q_g[0], gqa_k_g[0],
                      mla_cq_g[0], mla_w_uq[0], mla_ckv_g[0], mla_w_ukv[0], mla_q_g[0], mla_k_g[0],
                      moe_router[0], moe_w1[0], moe_w3[0], moe_w2[0])
```

```python
import functools
import math

import jax
import jax.numpy as jnp
from jax import lax
from jax.experimental import pallas as pl
from jax.experimental.pallas import tpu as pltpu

F32 = jnp.float32
BF16 = jnp.bfloat16
HIGHEST = lax.Precision.HIGHEST

D_MODEL = 1024
GRID_W = 64
HEAD_DIM = 64
ROPE_THETA = 10000.0
NORM_EPS = 1e-6
NEG_INF = -1e30
ADA_CHUNKS = 6
LOG2E = 1.4426950408889634

NA_HEADS = 8
NA_WIN_H = 8
NA_WIN_W = 16
DIFF_HEADS = 4
DIFF_V_DIM = 2 * HEAD_DIM
GQA_Q_HEADS = 8
GQA_KV_HEADS = 2
MLA_HEADS = 8
MLA_NOPE = 64
MLA_ROPE = 32
MLA_QK = MLA_NOPE + MLA_ROPE
MLA_V = 64
MLA_Q_RANK = 256
MLA_KV_RANK = 128
D_FF = 2816
N_EXPERTS = 8
D_FF_EXPERT = 3584

LANES = 128
MXU_DIM = 256
VMEM_LIMIT = 56 * 1024 * 1024
MOD_ROWS = 16
MOE_TF = 512
FLASH_TQ = 512
FLASH_TK = 512
EXP2_SAFE_RANGE = 64.0


def _cparams(sem):
    return pltpu.CompilerParams(dimension_semantics=sem, vmem_limit_bytes=VMEM_LIMIT)


def _dot(a, b):
    return jnp.dot(a, b, preferred_element_type=F32)


def _dot_nt(a, b):
    return lax.dot_general(a, b, (((1,), (1,)), ((), ())), preferred_element_type=F32)


def _silu(x):
    return x * (1.0 / (1.0 + jnp.exp(-x)))


def _modvec_kernel(c_ref, w_ref, b_ref, o_ref):
    s = _silu(c_ref[...])
    o_ref[...] = jnp.dot(s, w_ref[...], preferred_element_type=F32, precision=HIGHEST) + b_ref[...]


def _modvec(cc, ada_w, ada_b):
    n_layers, d, n = ada_w.shape
    tn = 768
    return pl.pallas_call(
        _modvec_kernel,
        out_shape=jax.ShapeDtypeStruct((n_layers, MOD_ROWS, n), F32),
        grid=(n_layers, n // tn),
        in_specs=[
            pl.BlockSpec((MOD_ROWS, d), lambda l, j: (0, 0)),
            pl.BlockSpec((None, d, tn), lambda l, j: (l, 0, j)),
            pl.BlockSpec((None, 1, tn), lambda l, j: (l, 0, j)),
        ],
        out_specs=pl.BlockSpec((None, MOD_ROWS, tn), lambda l, j: (l, 0, j)),
        compiler_params=_cparams(("arbitrary", "arbitrary")),
        name="modvec",
    )(cc, ada_w, ada_b)


def _norm_mod(x, g, shift, scale):
    ms = jnp.mean(x * x, axis=-1, keepdims=True)
    return (x * lax.rsqrt(ms + NORM_EPS) * g) * (1.0 + scale) + shift


def _group_norm(t, e, g, inv_n):
    ss = _dot((t * t).astype(BF16), e)
    return t * lax.rsqrt(ss * inv_n + NORM_EPS) * g


def _rope_chunks(t, cos, sin, lane_hi, shift):
    outs = []
    for c in range(t.shape[1] // LANES):
        tc = t[:, c * LANES:(c + 1) * LANES]
        up = pltpu.roll(tc, LANES - shift, 1)
        dn = pltpu.roll(tc, shift, 1)
        outs.append(tc * cos + jnp.where(lane_hi, dn, up) * sin)
    return outs[0] if len(outs) == 1 else jnp.concatenate(outs, axis=1)


def _rope_tables(seq, unit, pair_shift):
    pos = jnp.arange(seq)
    rows, cols = pos // GRID_W, pos % GRID_W
    quarter = unit // 4
    assert quarter == pair_shift
    freqs = ROPE_THETA ** (-jnp.arange(quarter, dtype=F32) / quarter)
    lane = jnp.arange(LANES)
    u = lane % unit
    use_col = (u // (unit // 2)) == 1
    fi = u % quarter
    p = jnp.where(use_col[None, :], cols[:, None], rows[:, None]).astype(F32)
    ang = p * freqs[fi][None, :]
    second = ((u % (unit // 2)) // quarter) == 1
    cos = jnp.cos(ang)
    sin = jnp.where(second[None, :], jnp.sin(ang), -jnp.sin(ang))
    return cos, sin


def _even_proj_kernel(x_ref, sh_ref, sc_ref, ng_ref, w_ref, e_ref, hg_ref, cos_ref, sin_ref,
                      qa_ref, ka_ref, va_ref, qb_ref, kb_ref, vb_ref, *, use_rope):
    h = _norm_mod(x_ref[0], ng_ref[...], sh_ref[0], sc_ref[0])
    p = _dot(h.astype(BF16), w_ref[...])
    hg = hg_ref[...]
    e = e_ref[...]
    inv_n = 1.0 / HEAD_DIM
    qa = _group_norm(p[:, 0:512], e, hg[0:1], inv_n)
    ka = _group_norm(p[:, 512:1024], e, hg[1:2], inv_n)
    qb = _group_norm(p[:, 1536:2048], e, hg[2:3], inv_n)
    kb = _group_norm(p[:, 2048:2560], e, hg[3:4], inv_n)
    if use_rope:
        lane = lax.broadcasted_iota(jnp.int32, (1, LANES), 1)
        lane_hi = (lane & 16) != 0
        cos, sin = cos_ref[...], sin_ref[...]
        qb = _rope_chunks(qb, cos, sin, lane_hi, 16)
        kb = _rope_chunks(kb, cos, sin, lane_hi, 16)
    qa_ref[0] = qa.astype(BF16)
    ka_ref[0] = ka.astype(BF16)
    va_ref[0] = p[:, 1024:1536].astype(BF16)
    qb_ref[0] = qb.astype(BF16)
    kb_ref[0] = kb.astype(BF16)
    vb_ref[0] = p[:, 2560:3072].astype(BF16)


def _even_proj(x, shift, scale, ng, w, e512, hg, cos, sin, *, use_rope, tm):
    b, s, d = x.shape
    n = w.shape[1]
    row = lambda bi, i: (bi, i, 0)
    vec = lambda bi, i: (bi, 0, 0)
    const = lambda bi, i: (0, 0)
    out = jax.ShapeDtypeStruct((b, s, 512), BF16)
    return pl.pallas_call(
        functools.partial(_even_proj_kernel, use_rope=use_rope),
        out_shape=(out,) * 6,
        grid=(b, s // tm),
        in_specs=[
            pl.BlockSpec((1, tm, d), row),
            pl.BlockSpec((1, 1, d), vec),
            pl.BlockSpec((1, 1, d), vec),
            pl.BlockSpec((1, d), const),
            pl.BlockSpec((d, n), const),
            pl.BlockSpec((512, 512), const),
            pl.BlockSpec((4, 512), const),
            pl.BlockSpec((tm, LANES), lambda bi, i: (i, 0)),
            pl.BlockSpec((tm, LANES), lambda bi, i: (i, 0)),
        ],
        out_specs=(pl.BlockSpec((1, tm, 512), row),) * 6,
        compiler_params=_cparams(("parallel", "parallel")),
        name="even_proj",
    )(x, shift, scale, ng, w, e512, hg, cos, sin)


def _pair_stack(q):
    lane = lax.broadcasted_iota(jnp.int32, q.shape, 1)
    zero = jnp.zeros_like(q)
    return jnp.concatenate([jnp.where(lane < HEAD_DIM, q, zero), jnp.where(lane >= HEAD_DIM, q, zero)], axis=0)


def _pair_merge(o, tq):
    lane = lax.broadcasted_iota(jnp.int32, (tq, LANES), 1)
    return jnp.where(lane < HEAD_DIM, o[:tq], o[tq:])


def _with_ones(v):
    return jnp.concatenate([v, jnp.ones(v.shape, BF16)], axis=1)


def _na_kernel(q_ref, k_ref, v_ref, kc_ref, vc_ref, bias_ref, o_ref, *, rows_per_step, n_rows, direct):
    rb = pl.program_id(2)
    kc = kc_ref[0]
    vc = vc_ref[0]
    n_lat = NA_WIN_H * GRID_W
    if direct:
        nq = rows_per_step * GRID_W
        qs_all = _pair_stack(q_ref[0])
        o_ctx = _dot(jnp.exp2(_dot_nt(qs_all, kc)).astype(BF16), _with_ones(vc))
        for i in range(rows_per_step):
            r = rb * rows_per_step + i
            r0 = jnp.clip(r - NA_WIN_H // 2, 0, n_rows - NA_WIN_H)
            start = pl.multiple_of(r0 * GRID_W, GRID_W)
            kw = k_ref[0, pl.ds(start, n_lat), :]
            vw = v_ref[0, pl.ds(start, n_lat), :]
            lo = slice(i * GRID_W, (i + 1) * GRID_W)
            hi = slice(nq + i * GRID_W, nq + (i + 1) * GRID_W)
            qs = jnp.concatenate([qs_all[lo], qs_all[hi]], axis=0)
            p = jnp.exp2(_dot_nt(qs, kw) + bias_ref[r - r0]).astype(BF16)
            tot = _dot(p, _with_ones(vw)) + jnp.concatenate([o_ctx[lo], o_ctx[hi]], axis=0)
            o = tot[:, :LANES] / tot[:, LANES:]
            o_ref[0, lo, :] = _pair_merge(o, GRID_W).astype(BF16)
        return
    for i in range(rows_per_step):
        r = rb * rows_per_step + i
        r0 = jnp.clip(r - NA_WIN_H // 2, 0, n_rows - NA_WIN_H)
        start = pl.multiple_of(r0 * GRID_W, GRID_W)
        kw = k_ref[0, pl.ds(start, n_lat), :]
        vw = v_ref[0, pl.ds(start, n_lat), :]
        qs = _pair_stack(q_ref[0, i * GRID_W:(i + 1) * GRID_W, :])
        s_lat = _dot_nt(qs, kw) + bias_ref[r - r0]
        s_ctx = _dot_nt(qs, kc)
        m = jnp.maximum(jnp.max(s_lat, axis=-1, keepdims=True), jnp.max(s_ctx, axis=-1, keepdims=True))
        p_lat = jnp.exp2(s_lat - m)
        p_ctx = jnp.exp2(s_ctx - m)
        l = jnp.sum(p_lat, axis=-1, keepdims=True) + jnp.sum(p_ctx, axis=-1, keepdims=True)
        o = _dot(p_lat.astype(BF16), vw) + _dot(p_ctx.astype(BF16), vc)
        o = o / l
        o_ref[0, i * GRID_W:(i + 1) * GRID_W, :] = _pair_merge(o, GRID_W).astype(BF16)


def _na_attention(q, k, v, kc, vc, bias, score_bound, *, rows_per_step=8):
    bound = score_bound + jnp.max(jnp.where(bias > 0.5 * NEG_INF, jnp.abs(bias), 0.0))
    return lax.cond(bound <= EXP2_SAFE_RANGE,
                    lambda: _na_call(q, k, v, kc, vc, bias, rows_per_step=rows_per_step, direct=True),
                    lambda: _na_call(q, k, v, kc, vc, bias, rows_per_step=rows_per_step, direct=False))


def _na_call(q, k, v, kc, vc, bias, *, rows_per_step, direct):
    b, s, _ = q.shape
    c = kc.shape[1]
    n_rows = s // GRID_W
    tq = rows_per_step * GRID_W
    return pl.pallas_call(
        functools.partial(_na_kernel, rows_per_step=rows_per_step, n_rows=n_rows, direct=direct),
        out_shape=jax.ShapeDtypeStruct((b, s, 512), BF16),
        grid=(b, 4, n_rows // rows_per_step),
        in_specs=[
            pl.BlockSpec((1, tq, LANES), lambda bi, j, i: (bi, i, j)),
            pl.BlockSpec((1, s, LANES), lambda bi, j, i: (bi, 0, j)),
            pl.BlockSpec((1, s, LANES), lambda bi, j, i: (bi, 0, j)),
            pl.BlockSpec((1, c, LANES), lambda bi, j, i: (bi, 0, j)),
            pl.BlockSpec((1, c, LANES), lambda bi, j, i: (bi, 0, j)),
            pl.BlockSpec((NA_WIN_H, None, LANES, NA_WIN_H * GRID_W), lambda bi, j, i: (0, j, 0, 0)),
        ],
        out_specs=pl.BlockSpec((1, tq, LANES), lambda bi, j, i: (bi, i, j)),
        compiler_params=_cparams(("parallel", "parallel", "parallel")),
        name="na_attention" + ("" if direct else "_online"),
    )(q, k, v, kc, vc, bias)


def _na_bias_table(rpb):
    w = GRID_W
    col = jnp.arange(w)
    c0 = jnp.clip(col - NA_WIN_W // 2, 0, w - NA_WIN_W)
    col_in = (col[None, :] >= c0[:, None]) & (col[None, :] < c0[:, None] + NA_WIN_W)
    left = (w - 1) - (NA_WIN_W - 1)
    ext = jnp.pad(rpb, ((0, 0), (0, 0), (left, 2 * w - left - (2 * NA_WIN_W - 1))), mode="edge")
    h, nr, _ = rpb.shape
    flat = jnp.broadcast_to(ext[:, :, None, :], (h, nr, w, 2 * w)).reshape(h, nr, w * 2 * w)
    toep = flat[:, :, :w * (2 * w - 1)].reshape(h, nr, w, 2 * w - 1)[:, :, :, w - 1:]
    toep = jnp.where(col_in[None, None], toep * LOG2E, NEG_INF)
    variants = []
    for v in range(NA_WIN_H):
        tv = toep[:, NA_WIN_H - 1 - v:2 * NA_WIN_H - 1 - v]
        variants.append(tv.transpose(0, 2, 1, 3).reshape(NA_HEADS // 2, 2 * w, NA_WIN_H * w))
    return jnp.stack(variants, axis=0).astype(F32)


def _flash_kernel(*refs, mode, online, tq, tk, s_len, c_len, lam_init):
    if mode == "diff":
        q_ref, k_ref, v_ref, kc_ref, vc_ref, lam_ref, sg_ref, o_ref = refs
    else:
        q_ref, k_ref, v_ref, kc_ref, vc_ref, o_ref = refs

    q = q_ref[0]
    if mode == "mla":
        q_parts = (q[:, :LANES], q[:, LANES:])
    else:
        qs = _pair_stack(q)

    def scores(kt):
        if mode == "mla":
            return jnp.concatenate([_dot_nt(q_parts[0], kt[:, :LANES]), _dot_nt(q_parts[1], kt[:, LANES:])], axis=0)
        return _dot_nt(qs, kt)

    def step(kt, vt, carry):
        m, l, acc = carry
        s = scores(kt)
        m_new = jnp.maximum(m, jnp.max(s, axis=-1, keepdims=True))
        alpha = jnp.exp2(m - m_new)
        p = jnp.exp2(s - m_new)
        l = alpha * l + jnp.sum(p, axis=-1, keepdims=True)
        acc = alpha * acc + _dot(p.astype(BF16), vt)
        return m_new, l, acc

    def body(t, carry):
        start = pl.multiple_of(t * tk, tk)
        return step(k_ref[0, pl.ds(start, tk), :], v_ref[0, pl.ds(start, tk), :], carry)

    def direct(kt, vt, acc):
        return acc + _dot(jnp.exp2(scores(kt)).astype(BF16), _with_ones(vt))

    if online:
        carry = (jnp.full((2 * tq, 1), -jnp.inf, F32), jnp.zeros((2 * tq, 1), F32),
                 jnp.zeros((2 * tq, LANES), F32))
        if s_len:
            carry = lax.fori_loop(0, s_len // tk, body, carry)
        if c_len:
            carry = step(kc_ref[0], vc_ref[0], carry)
        _, l, acc = carry
        o = acc / l
    else:
        acc = jnp.zeros((2 * tq, 2 * LANES), F32)
        for t in range(s_len // tk):
            acc = direct(k_ref[0, t * tk:(t + 1) * tk, :], v_ref[0, t * tk:(t + 1) * tk, :], acc)
        if c_len:
            acc = direct(kc_ref[0], vc_ref[0], acc)
        o = acc[:, :LANES] / acc[:, LANES:]
    if mode == "diff":
        lp = lam_ref[...]
        lam = (jnp.exp(jnp.sum(lp[0:1] * lp[1:2], axis=-1, keepdims=True))
               - jnp.exp(jnp.sum(lp[2:3] * lp[3:4], axis=-1, keepdims=True)) + lam_init)
        d = o[:tq] - lam * o[tq:]
        ms = jnp.mean(d * d, axis=-1, keepdims=True)
        o_ref[0] = (d * lax.rsqrt(ms + NORM_EPS) * sg_ref[...] * (1.0 - lam_init)).astype(BF16)
    else:
        o_ref[0] = _pair_merge(o, tq).astype(BF16)


def _flash(q, k, v, kc, vc, *, score_bound=None, **kw):
    if score_bound is None:
        return _flash_call(q, k, v, kc, vc, online=True, **kw)
    return lax.cond(score_bound <= EXP2_SAFE_RANGE,
                    lambda: _flash_call(q, k, v, kc, vc, online=False, **kw),
                    lambda: _flash_call(q, k, v, kc, vc, online=True, **kw))


def _flash_call(q, k, v, kc, vc, *, mode, online, kv_map, n_chunks, tq, tk, extra=(), lam_init=0.0):
    b, s, _ = q.shape
    qw = 2 * LANES if mode == "mla" else LANES
    c_len = kc.shape[1]
    if k is None:
        k, v, s_len = kc, vc, 0
    else:
        s_len = k.shape[1]
    kk = k.shape[1]
    in_specs = [
        pl.BlockSpec((1, tq, qw), lambda bi, j, i: (bi, i, j)),
        pl.BlockSpec((1, kk, qw), lambda bi, j, i: (bi, 0, kv_map(j) if mode != "mla" else j)),
        pl.BlockSpec((1, kk, LANES), lambda bi, j, i: (bi, 0, kv_map(j))),
        pl.BlockSpec((1, c_len, qw), lambda bi, j, i: (bi, 0, kv_map(j) if mode != "mla" else j)),
        pl.BlockSpec((1, c_len, LANES), lambda bi, j, i: (bi, 0, kv_map(j))),
    ]
    for e in extra:
        in_specs.append(pl.BlockSpec(e.shape, lambda bi, j, i: (0, 0)))
    return pl.pallas_call(
        functools.partial(_flash_kernel, mode=mode, online=online, tq=tq, tk=tk, s_len=s_len, c_len=c_len,
                          lam_init=lam_init),
        out_shape=jax.ShapeDtypeStruct((b, s, n_chunks * LANES), BF16),
        grid=(b, n_chunks, s // tq),
        in_specs=in_specs,
        out_specs=pl.BlockSpec((1, tq, LANES), lambda bi, j, i: (bi, i, j)),
        compiler_params=_cparams(("parallel", "parallel", "parallel")),
        name="flash_" + mode + ("_online" if online else ""),
    )(q, k, v, kc, vc, *extra)


def _post_attn_kernel(oa_ref, ob_ref, wa_ref, wb_ref, x_ref, g_ref, ng_ref, sh_ref, sc_ref, x1_ref, h2_ref):
    y = _dot(oa_ref[0], wa_ref[...]) + _dot(ob_ref[0], wb_ref[...])
    x1 = x_ref[0] + g_ref[0] * y
    x1_ref[0] = x1
    h2_ref[0] = _norm_mod(x1, ng_ref[...], sh_ref[0], sc_ref[0]).astype(BF16)


def _post_attn(oa, ob, wa, wb, x, gate, ng, shift, scale, *, tm):
    b, s, d = x.shape
    row = lambda bi, i: (bi, i, 0)
    vec = lambda bi, i: (bi, 0, 0)
    const = lambda bi, i: (0, 0)
    return pl.pallas_call(
        _post_attn_kernel,
        out_shape=(jax.ShapeDtypeStruct((b, s, d), F32), jax.ShapeDtypeStruct((b, s, d), BF16)),
        grid=(b, s // tm),
        in_specs=[
            pl.BlockSpec((1, tm, 512), row),
            pl.BlockSpec((1, tm, 512), row),
            pl.BlockSpec((512, d), const),
            pl.BlockSpec((512, d), const),
            pl.BlockSpec((1, tm, d), row),
            pl.BlockSpec((1, 1, d), vec),
            pl.BlockSpec((1, d), const),
            pl.BlockSpec((1, 1, d), vec),
            pl.BlockSpec((1, 1, d), vec),
        ],
        out_specs=(pl.BlockSpec((1, tm, d), row), pl.BlockSpec((1, tm, d), row)),
        compiler_params=_cparams(("parallel", "parallel")),
        name="post_attn",
    )(oa, ob, wa, wb, x, gate, ng, shift, scale)


def _ffn_kernel(h_ref, x1_ref, g_ref, wg_ref, wu_ref, wd_ref, o_ref, *, tf):
    h = h_ref[0]
    acc = jnp.zeros(o_ref.shape[1:], F32)
    for f in range(wg_ref.shape[1] // tf):
        sl = slice(f * tf, (f + 1) * tf)
        a = _dot(h, wg_ref[:, sl])
        u = _dot(h, wu_ref[:, sl])
        acc = acc + _dot((_silu(a) * u).astype(BF16), wd_ref[sl, :])
    o_ref[0] = x1_ref[0] + g_ref[0] * acc


def _ffn(h2, x1, gate, wg, wu, wd, *, tm, tf=256):
    b, s, d = x1.shape
    f = wg.shape[1]
    row = lambda bi, i: (bi, i, 0)
    vec = lambda bi, i: (bi, 0, 0)
    const = lambda bi, i: (0, 0)
    return pl.pallas_call(
        functools.partial(_ffn_kernel, tf=tf),
        out_shape=jax.ShapeDtypeStruct((b, s, d), F32),
        grid=(b, s // tm),
        in_specs=[
            pl.BlockSpec((1, tm, d), row),
            pl.BlockSpec((1, tm, d), row),
            pl.BlockSpec((1, 1, d), vec),
            pl.BlockSpec((d, f), const),
            pl.BlockSpec((d, f), const),
            pl.BlockSpec((f, d), const),
        ],
        out_specs=pl.BlockSpec((1, tm, d), row),
        compiler_params=_cparams(("parallel", "parallel")),
        name="ffn",
    )(h2, x1, gate, wg, wu, wd)


def _rms(t, g):
    return t * lax.rsqrt(jnp.mean(t * t, axis=-1, keepdims=True) + NORM_EPS) * g


def _head_norm_128(t, a_ref, at_ref, g, inv_n):
    ss = _dot((t * t).astype(BF16), a_ref[...])
    r = lax.rsqrt(ss * inv_n + NORM_EPS)
    r_hi = r.astype(BF16)
    r_lo = (r - r_hi.astype(F32)).astype(BF16)
    rb = _dot(r_hi, at_ref[...]) + _dot(r_lo, at_ref[...])
    return t * rb * g


def _odd_proj_kernel(x_ref, sh_ref, sc_ref, ng_ref, w_ref, wuq_ref, wukv_ref, e_ref, a_ref, at_ref, g_ref,
                     cos_ref, sin_ref, cosm_ref, sinm_ref,
                     qc_ref, kc_ref, vc_ref, qd_ref, kd_ref, vd_ref, *, use_rope):
    h = _norm_mod(x_ref[0], ng_ref[...], sh_ref[0], sc_ref[0])
    p = _dot(h.astype(BF16), w_ref[...])
    g = g_ref[...]
    e = e_ref[...]
    inv_n = 1.0 / HEAD_DIM
    qc = _group_norm(p[:, 0:512], e, g[0:1, 0:512], inv_n)
    kc = _group_norm(p[:, 512:768], e[0:256, 0:256], g[1:2, 0:256], inv_n)
    cq = _rms(p[:, 1024:1280], g[2:3, 0:256])
    ckv = _rms(p[:, 1280:1408], g[3:4, 0:128])
    krc = p[:, 1408:1536]
    qd = _dot(cq.astype(BF16), wuq_ref[...])
    kvv = _dot(ckv.astype(BF16), wukv_ref[...])
    kd = kvv[:, 0:1024] + jnp.concatenate([krc] * MLA_HEADS, axis=1)
    qd = _head_norm_128(qd, a_ref, at_ref, g[4:5], 1.0 / MLA_QK)
    kd = _head_norm_128(kd, a_ref, at_ref, g[5:6], 1.0 / MLA_QK)
    if use_rope:
        lane = lax.broadcasted_iota(jnp.int32, (1, LANES), 1)
        cos, sin = cos_ref[...], sin_ref[...]
        qc = _rope_chunks(qc, cos, sin, (lane & 16) != 0, 16)
        kc = _rope_chunks(kc, cos, sin, (lane & 16) != 0, 16)
        cosm, sinm = cosm_ref[...], sinm_ref[...]
        qd = _rope_chunks(qd, cosm, sinm, (lane & 8) != 0, 8)
        kd = _rope_chunks(kd, cosm, sinm, (lane & 8) != 0, 8)
    qc_ref[0] = qc.astype(BF16)
    kc_ref[0] = kc.astype(BF16)
    vc_ref[0] = p[:, 768:1024].astype(BF16)
    qd_ref[0] = qd.astype(BF16)
    kd_ref[0] = kd.astype(BF16)
    vd_ref[0] = kvv[:, 1024:1536].astype(BF16)


def _odd_proj(x, shift, scale, ng, w, wuq, wukv, e512, a, at, gains, cos, sin, cosm, sinm, *, use_rope, tm):
    b, s, d = x.shape
    row = lambda bi, i: (bi, i, 0)
    vec = lambda bi, i: (bi, 0, 0)
    const = lambda bi, i: (0, 0)
    tab = lambda bi, i: (i, 0)
    widths = (512, 256, 256, 1024, 1024, 512)
    return pl.pallas_call(
        functools.partial(_odd_proj_kernel, use_rope=use_rope),
        out_shape=tuple(jax.ShapeDtypeStruct((b, s, n), BF16) for n in widths),
        grid=(b, s // tm),
        in_specs=[
            pl.BlockSpec((1, tm, d), row),
            pl.BlockSpec((1, 1, d), vec),
            pl.BlockSpec((1, 1, d), vec),
            pl.BlockSpec((1, d), const),
            pl.BlockSpec(w.shape, const),
            pl.BlockSpec(wuq.shape, const),
            pl.BlockSpec(wukv.shape, const),
            pl.BlockSpec(e512.shape, const),
            pl.BlockSpec(a.shape, const),
            pl.BlockSpec(at.shape, const),
            pl.BlockSpec(gains.shape, const),
            pl.BlockSpec((tm, LANES), tab),
            pl.BlockSpec((tm, LANES), tab),
            pl.BlockSpec((tm, LANES), tab),
            pl.BlockSpec((tm, LANES), tab),
        ],
        out_specs=tuple(pl.BlockSpec((1, tm, n), row) for n in widths),
        compiler_params=_cparams(("parallel", "parallel")),
        name="odd_proj",
    )(x, shift, scale, ng, w, wuq, wukv, e512, a, at, gains, cos, sin, cosm, sinm)


def _mla_rope_tables(seq):
    cos, sin = _rope_tables(seq, MLA_ROPE, MLA_ROPE // 4)
    lane = jnp.arange(LANES)
    on = (lane >= MLA_NOPE) & (lane < MLA_QK)
    return jnp.where(on[None, :], cos, 1.0), jnp.where(on[None, :], sin, 0.0)


def _router_kernel(x1_ref, ng_ref, sh_ref, sc_ref, rt_ref, u_ref, ei_ref, gt_ref, rk_ref, cnt_ref, carry_ref):
    first = (pl.program_id(0) == 0) & (pl.program_id(1) == 0)

    @pl.when(first)
    def _():
        carry_ref[...] = jnp.zeros_like(carry_ref)

    h = _norm_mod(x1_ref[0], ng_ref[...], sh_ref[0], sc_ref[0])
    logits = lax.dot_general(rt_ref[...], h, (((1,), (1,)), ((), ())),
                             preferred_element_type=F32, precision=HIGHEST)
    eidx = lax.broadcasted_iota(jnp.int32, logits.shape, 0)
    m1 = jnp.max(logits, axis=0, keepdims=True)
    i1 = jnp.min(jnp.where(logits == m1, eidx, N_EXPERTS), axis=0, keepdims=True)
    rest = jnp.where(eidx == i1, -jnp.inf, logits)
    m2 = jnp.max(rest, axis=0, keepdims=True)
    i2 = jnp.min(jnp.where(rest == m2, eidx, N_EXPERTS), axis=0, keepdims=True)
    e2 = jnp.exp(m2 - m1)
    g1 = 1.0 / (1.0 + e2)
    sel1 = eidx == i1
    sel2 = eidx == i2
    onehot = jnp.where(sel1 | sel2, 1.0, 0.0)
    before = _dot(onehot.astype(BF16), u_ref[...]) + carry_ref[:, 0:1]
    r1 = jnp.sum(jnp.where(sel1, before, 0.0), axis=0, keepdims=True)
    r2 = jnp.sum(jnp.where(sel2, before, 0.0), axis=0, keepdims=True)
    ei_ref[...] = jnp.concatenate([i1, i2], axis=0)
    gt_ref[...] = jnp.concatenate([g1, e2 * g1], axis=0)
    rk_ref[...] = jnp.concatenate([r1, r2], axis=0).astype(jnp.int32)
    total = carry_ref[...] + jnp.sum(onehot, axis=1, keepdims=True)
    carry_ref[...] = total
    cnt_ref[...] = total.astype(jnp.int32)


def _router(x1, ng, shift, scale, router_t, *, tm):
    b, s, d = x1.shape
    n = b * s
    nt = s // tm
    tri = (jnp.arange(tm)[:, None] < jnp.arange(tm)[None, :]).astype(BF16)
    flat = lambda bi, i: (0, bi * nt + i)
    return pl.pallas_call(
        _router_kernel,
        out_shape=(jax.ShapeDtypeStruct((2, n), jnp.int32), jax.ShapeDtypeStruct((2, n), F32),
                   jax.ShapeDtypeStruct((2, n), jnp.int32), jax.ShapeDtypeStruct((N_EXPERTS, LANES), jnp.int32)),
        grid=(b, nt),
        in_specs=[
            pl.BlockSpec((1, tm, d), lambda bi, i: (bi, i, 0)),
            pl.BlockSpec((1, d), lambda bi, i: (0, 0)),
            pl.BlockSpec((1, 1, d), lambda bi, i: (bi, 0, 0)),
            pl.BlockSpec((1, 1, d), lambda bi, i: (bi, 0, 0)),
            pl.BlockSpec((N_EXPERTS, d), lambda bi, i: (0, 0)),
            pl.BlockSpec((tm, tm), lambda bi, i: (0, 0)),
        ],
        out_specs=(pl.BlockSpec((2, tm), flat), pl.BlockSpec((2, tm), flat), pl.BlockSpec((2, tm), flat),
                   pl.BlockSpec((N_EXPERTS, LANES), lambda bi, i: (0, 0))),
        scratch_shapes=[pltpu.VMEM((N_EXPERTS, LANES), F32)],
        compiler_params=_cparams(("arbitrary", "arbitrary")),
        name="router",
    )(x1, ng, shift, scale, router_t, tri)


MOE_SUB = 256
MOE_BLK = 1024
MOE_TC = 256
MOE_NBUF = 8
MOE_AHEAD = MOE_NBUF - 1


def _lookup(table, idx):
    hit = idx[..., None] == jnp.arange(table.shape[0])
    return jnp.sum(jnp.where(hit, table, 0), axis=-1)


def _moe_plan(ei, rk, counts, n_tok):
    n_chunks = n_tok // MOE_TC
    cap = 2 * n_tok + N_EXPERTS * MOE_BLK
    nb_sub = cap // MOE_SUB
    nb_blk = cap // MOE_BLK
    padded = ((counts + MOE_BLK - 1) // MOE_BLK) * MOE_BLK
    pad_end = jnp.cumsum(padded)
    pad_start = pad_end - padded
    dest = _lookup(pad_start, ei) + rk

    onehot = ei.reshape(2, n_chunks, MOE_TC)[..., None] == jnp.arange(N_EXPERTS)
    cnt = jnp.sum(onehot, axis=(0, 2)).astype(jnp.int32)
    cum = jnp.concatenate([jnp.zeros((1, N_EXPERTS), jnp.int32), jnp.cumsum(cnt, axis=0)], axis=0)

    lo = pad_start[None, :] + cum[:-1]
    has = cnt > 0
    blk0 = lo // MOE_SUB
    two = has & ((lo + cnt - 1) // MOE_SUB > blk0)
    blk1 = jnp.minimum(blk0 + 1, nb_sub - 1)
    win_ok = jnp.stack([has, two], axis=-1).reshape(n_chunks, 2 * N_EXPERTS)
    win_blk = jnp.stack([blk0, blk1], axis=-1).reshape(n_chunks, 2 * N_EXPERTS)
    c_count = jnp.sum(win_ok, axis=1).astype(jnp.int32)
    c_start = (jnp.cumsum(c_count) - c_count).astype(jnp.int32)
    win_at = c_start[:, None] + jnp.cumsum(win_ok, axis=1) - win_ok
    q = jnp.arange(n_chunks * 2 * N_EXPERTS)
    hit = win_ok[None] & (win_at[None] == q[:, None, None])
    c_blocks = jnp.sum(jnp.where(hit, win_blk[None], 0), axis=(1, 2)).astype(jnp.int32)

    sb = jnp.arange(nb_sub)
    e_sb = jnp.minimum(jnp.sum(sb[:, None] * MOE_SUB >= pad_end[None, :], axis=1), N_EXPERTS - 1)
    counts_sb = _lookup(counts, e_sb)
    r0 = sb * MOE_SUB - _lookup(pad_start, e_sb)
    valid_sb = (sb * MOE_SUB < pad_end[-1]) & (r0 < counts_sb)
    r1 = jnp.minimum(r0 + MOE_SUB, counts_sb) - 1
    hit_sb = e_sb[:, None] == jnp.arange(N_EXPERTS)[None, :]
    cum_sb = jnp.sum(jnp.where(hit_sb[:, None, :], cum[None, 1:, :], 0), axis=-1)
    cmin = jnp.sum(cum_sb <= r0[:, None], axis=1)
    cmax = jnp.sum(cum_sb <= r1[:, None], axis=1)
    items = jnp.where(valid_sb, cmax - cmin + 1, 0)
    off_end = jnp.cumsum(items)
    off = off_end - items
    total = off_end[-1]
    w_max = (2 * n_tok) // MOE_SUB + N_EXPERTS + N_EXPERTS * (n_chunks - 1)
    w = jnp.arange(w_max)
    wv = w < total
    wq = jnp.minimum(w, total - 1)
    w_sb = jnp.minimum(jnp.sum(off_end[None, :] <= wq[:, None], axis=1), nb_sub - 1)
    off_w = _lookup(off, w_sb)
    w_chunk = _lookup(cmin, w_sb) + (wq - off_w)
    w_flag = wv.astype(jnp.int32) + 2 * (wq == off_w).astype(jnp.int32)

    bi = jnp.arange(nb_blk)
    e_blk = jnp.minimum(jnp.sum(bi[:, None] * MOE_BLK >= pad_end[None, :], axis=1), N_EXPERTS - 1)
    rows = jnp.where(bi * MOE_BLK < pad_end[-1],
                     _lookup(counts, e_blk) - (bi * MOE_BLK - _lookup(pad_start, e_blk)), 0)
    n_sub = jnp.clip((rows + MOE_SUB - 1) // MOE_SUB, 0, MOE_BLK // MOE_SUB)
    g_count = jnp.sum(items.reshape(nb_blk, MOE_BLK // MOE_SUB), axis=1)
    g_start = jnp.cumsum(g_count) - g_count
    return dict(dest=dest.astype(jnp.int32), cap=cap,
                w_sb=w_sb.astype(jnp.int32), w_chunk=w_chunk.astype(jnp.int32), w_flag=w_flag,
                g_start=g_start.astype(jnp.int32), g_count=g_count.astype(jnp.int32),
                e_blk=e_blk.astype(jnp.int32), n_sub=n_sub.astype(jnp.int32),
                c_blocks=c_blocks, c_count=c_count, c_start=c_start)


def _moe_gather_kernel(st_ref, cn_ref, ns_ref, wsb_ref, wch_ref, wfl_ref, t_hbm, d_ref, g_ref, xg_ref, gs_ref,
                       tbuf, sem):
    i = pl.program_id(0)
    n = cn_ref[i]
    s0 = st_ref[i]
    total = st_ref[pl.num_programs(0) - 1] + cn_ref[pl.num_programs(0) - 1]
    subs = MOE_BLK // MOE_SUB

    def chunk_copy(item):
        slot = lax.rem(item, MOE_NBUF)
        start = pl.multiple_of(wch_ref[item] * MOE_TC, MOE_TC)
        return pltpu.make_async_copy(t_hbm.at[pl.ds(start, MOE_TC), :], tbuf.at[slot], sem.at[slot])

    @pl.when(i == 0)
    def _():
        for a in range(MOE_AHEAD):
            @pl.when(a < total)
            def _():
                chunk_copy(a).start()

    def body(k, carry):
        item = s0 + k
        slot = lax.rem(item, MOE_NBUF)
        chunk_copy(item).wait()

        @pl.when(item + MOE_AHEAD < total)
        def _():
            chunk_copy(item + MOE_AHEAD).start()

        sb = wsb_ref[item]
        chunk = wch_ref[item]
        d = d_ref[chunk]
        g = g_ref[chunk]
        srow = lax.broadcasted_iota(jnp.int32, (MOE_SUB, MOE_TC), 0) + sb * MOE_SUB
        hit0 = srow == d[0:1]
        hit1 = srow == d[1:2]
        p = jnp.where(hit0 | hit1, 1.0, 0.0).astype(BF16)
        rows = _dot(p, tbuf[slot]).astype(BF16)
        gate = jnp.sum(jnp.where(hit0, g[0:1], 0.0) + jnp.where(hit1, g[1:2], 0.0), axis=-1, keepdims=True)
        r0 = pl.multiple_of((sb - i * subs) * MOE_SUB, MOE_SUB)
        first = wfl_ref[item] >= 2

        @pl.when(first)
        def _():
            xg_ref[pl.ds(r0, MOE_SUB), :] = rows
            gs_ref[pl.ds(r0, MOE_SUB), :] = gate

        @pl.when(jnp.logical_not(first))
        def _():
            xg_ref[pl.ds(r0, MOE_SUB), :] += rows
            gs_ref[pl.ds(r0, MOE_SUB), :] += gate

        return carry

    lax.fori_loop(0, n, body, 0)

    for sub in range(subs):
        @pl.when(sub >= ns_ref[i])
        def _():
            xg_ref[sub * MOE_SUB:(sub + 1) * MOE_SUB, :] = jnp.zeros((MOE_SUB, xg_ref.shape[1]), BF16)
            gs_ref[sub * MOE_SUB:(sub + 1) * MOE_SUB, :] = jnp.zeros((MOE_SUB, 1), F32)


def _moe_gather(t, dest, gates, plan):
    n_tok, d = t.shape
    cap = plan["cap"]
    n_chunks = n_tok // MOE_TC
    by_chunk = lambda a: a.reshape(2, n_chunks, MOE_TC).transpose(1, 0, 2)
    grid_spec = pltpu.PrefetchScalarGridSpec(
        num_scalar_prefetch=6,
        grid=(cap // MOE_BLK,),
        in_specs=[
            pl.BlockSpec(memory_space=pl.ANY),
            pl.BlockSpec((n_chunks, 2, MOE_TC), lambda i, *_: (0, 0, 0)),
            pl.BlockSpec((n_chunks, 2, MOE_TC), lambda i, *_: (0, 0, 0)),
        ],
        out_specs=(pl.BlockSpec((MOE_BLK, d), lambda i, *_: (i, 0)),
                   pl.BlockSpec((MOE_BLK, 1), lambda i, *_: (i, 0))),
        scratch_shapes=[pltpu.VMEM((MOE_NBUF, MOE_TC, d), BF16), pltpu.SemaphoreType.DMA((MOE_NBUF,))],
    )
    return pl.pallas_call(
        _moe_gather_kernel,
        out_shape=(jax.ShapeDtypeStruct((cap, d), BF16), jax.ShapeDtypeStruct((cap, 1), F32)),
        grid_spec=grid_spec,
        compiler_params=_cparams(("arbitrary",)),
        name="moe_gather",
    )(plan["g_start"], plan["g_count"], plan["n_sub"], plan["w_sb"], plan["w_chunk"], plan["w_flag"],
      t, by_chunk(dest), by_chunk(gates))


def _moe_expert_kernel(eb_ref, ns_ref, x_ref, gs_ref, w1_ref, w3_ref, w2_ref, y_ref):
    n_sub = ns_ref[pl.program_id(0)]
    subs = MOE_BLK // MOE_SUB

    def mlp(rows):
        xs = x_ref[rows, :]
        acc = jnp.zeros((rows.stop - rows.start, y_ref.shape[1]), F32)
        for f in range(w1_ref.shape[1] // MOE_TF):
            cols = slice(f * MOE_TF, (f + 1) * MOE_TF)
            a = _dot(xs, w1_ref[:, cols])
            u = _dot(xs, w3_ref[:, cols])
            acc = acc + _dot((_silu(a) * u).astype(BF16), w2_ref[cols, :])
        y_ref[rows, :] = (acc * gs_ref[rows, :]).astype(BF16)

    @pl.when(n_sub == subs)
    def _():
        mlp(slice(0, MOE_BLK))

    for sub in range(subs):
        rows = slice(sub * MOE_SUB, (sub + 1) * MOE_SUB)

        @pl.when((n_sub < subs) & (sub < n_sub))
        def _():
            mlp(rows)

        @pl.when(sub >= n_sub)
        def _():
            y_ref[rows, :] = jnp.zeros((MOE_SUB, y_ref.shape[1]), BF16)


def _moe_experts(xg, gs, w1, w3, w2, plan):
    cap, d = xg.shape
    f = w1.shape[2]
    once = pl.Buffered(1)
    grid_spec = pltpu.PrefetchScalarGridSpec(
        num_scalar_prefetch=2,
        grid=(cap // MOE_BLK,),
        in_specs=[
            pl.BlockSpec((MOE_BLK, d), lambda i, eb, ns: (i, 0)),
            pl.BlockSpec((MOE_BLK, 1), lambda i, eb, ns: (i, 0)),
            pl.BlockSpec((None, d, f), lambda i, eb, ns: (eb[i], 0, 0), pipeline_mode=once),
            pl.BlockSpec((None, d, f), lambda i, eb, ns: (eb[i], 0, 0), pipeline_mode=once),
            pl.BlockSpec((None, f, d), lambda i, eb, ns: (eb[i], 0, 0), pipeline_mode=once),
        ],
        out_specs=pl.BlockSpec((MOE_BLK, d), lambda i, eb, ns: (i, 0)),
    )
    return pl.pallas_call(
        _moe_expert_kernel,
        out_shape=jax.ShapeDtypeStruct((cap, d), BF16),
        grid_spec=grid_spec,
        compiler_params=_cparams(("arbitrary",)),
        name="moe_experts",
    )(plan["e_blk"], plan["n_sub"], xg, gs, w1, w3, w2)


MOE_WIN = 2 * N_EXPERTS


def _moe_combine_kernel(st_ref, cn_ref, bl_ref, dt_ref, y_hbm, x1_ref, g_ref, o_ref, ybuf, sem):
    c = pl.program_id(0)
    n = cn_ref[c]
    s0 = st_ref[c]
    total = st_ref[pl.num_programs(0) - 1] + cn_ref[pl.num_programs(0) - 1]
    d = dt_ref[...]

    def block_copy(item):
        slot = lax.rem(item, MOE_NBUF)
        start = pl.multiple_of(bl_ref[item] * MOE_SUB, MOE_SUB)
        return pltpu.make_async_copy(y_hbm.at[pl.ds(start, MOE_SUB), :], ybuf.at[slot], sem.at[slot])

    @pl.when(c == 0)
    def _():
        for a in range(MOE_AHEAD):
            @pl.when(a < total)
            def _():
                block_copy(a).start()

    def body(k, acc):
        item = s0 + k
        block_copy(item).wait()

        @pl.when(item + MOE_AHEAD < total)
        def _():
            block_copy(item + MOE_AHEAD).start()

        scol = lax.broadcasted_iota(jnp.int32, (MOE_TC, MOE_SUB), 1) + bl_ref[item] * MOE_SUB
        p = jnp.where((scol == d[:, 0:1]) | (scol == d[:, 1:2]), 1.0, 0.0).astype(BF16)
        return acc + _dot(p, ybuf[lax.rem(item, MOE_NBUF)])

    acc = lax.fori_loop(0, n, body, jnp.zeros(o_ref.shape, F32))
    o_ref[...] = x1_ref[...] + g_ref[0] * acc


def _moe_combine(y, dest_t, x1, gate, plan, seq):
    n_tok, d = x1.shape
    n_chunks = n_tok // MOE_TC
    per_b = seq // MOE_TC
    grid_spec = pltpu.PrefetchScalarGridSpec(
        num_scalar_prefetch=3,
        grid=(n_chunks,),
        in_specs=[
            pl.BlockSpec((MOE_TC, 2), lambda c, *_: (c, 0)),
            pl.BlockSpec(memory_space=pl.ANY),
            pl.BlockSpec((MOE_TC, d), lambda c, *_: (c, 0)),
            pl.BlockSpec((1, 1, d), lambda c, *_: (c // per_b, 0, 0)),
        ],
        out_specs=pl.BlockSpec((MOE_TC, d), lambda c, *_: (c, 0)),
        scratch_shapes=[pltpu.VMEM((MOE_NBUF, MOE_SUB, d), BF16), pltpu.SemaphoreType.DMA((MOE_NBUF,))],
    )
    return pl.pallas_call(
        _moe_combine_kernel,
        out_shape=jax.ShapeDtypeStruct((n_tok, d), F32),
        grid_spec=grid_spec,
        compiler_params=_cparams(("arbitrary",)),
        name="moe_combine",
    )(plan["c_start"], plan["c_count"], plan["c_blocks"], dest_t, y, x1, gate)


def _split_mod(mod_l, batch):
    d = D_MODEL
    lat = tuple(mod_l[:batch, k * d:(k + 1) * d][:, None, :] for k in range(ADA_CHUNKS))
    ctx = tuple(jnp.broadcast_to(mod_l[batch:batch + 1, k * d:(k + 1) * d][:, None, :], (batch, 1, d))
                for k in range(ADA_CHUNKS))
    return lat, ctx


def _score_bound(q_gain, k_gain, n):
    return 1.02 * n * jnp.max(jnp.abs(q_gain)) * jnp.max(jnp.abs(k_gain))


def _block_ones(n, block):
    idx = jnp.arange(n) // block
    return (idx[:, None] == idx[None, :]).astype(BF16)


def _even_layer(x, xc, mod_l, norm1_g, norm2_g, w_in, w_out, na_q_g, na_k_g, na_rpb, diff_q_g, diff_k_g,
                lq1, lk1, lq2, lk2, subln_g, wg, wu, wd, layer_idx, need_ctx):
    b, s, d = x.shape
    (sh1, sc1, g1, sh2, sc2, g2), (csh1, csc1, cg1, csh2, csc2, cg2) = _split_mod(mod_l, b)
    qscale = HEAD_DIM ** -0.5 * LOG2E
    lam_init = 0.8 - 0.6 * math.exp(-0.3 * layer_idx)
    hg = jnp.stack([jnp.tile(na_q_g, 8) * qscale, jnp.tile(na_k_g, 8),
                    jnp.tile(diff_q_g, 8) * qscale, jnp.tile(diff_k_g, 8)]).astype(F32)
    e512 = _block_ones(512, HEAD_DIM)
    cos, sin = _rope_tables(s, HEAD_DIM, 16)
    ng1 = norm1_g[None, :]
    ng2 = norm2_g[None, :]
    w_in_b = w_in.astype(BF16)
    wo_a = w_out[:512].astype(BF16)
    wo_b = w_out[512:].astype(BF16)
    wg_b, wu_b, wd_b = wg.astype(BF16), wu.astype(BF16), wd.astype(BF16)
    lam_p = jnp.stack([lq1, lk1, lq2, lk2]).astype(F32)
    sg = subln_g[None, :].astype(F32)
    ident = lambda j: j

    tm = min(512, s)
    tc = xc.shape[1]
    qa, ka, va, qb, kb, vb = _even_proj(x, sh1, sc1, ng1, w_in_b, e512, hg, cos, sin, use_rope=True, tm=tm)
    qca, kca, vca, qcb, kcb, vcb = _even_proj(xc, csh1, csc1, ng1, w_in_b, e512, hg, cos, sin, use_rope=False, tm=tc)
    out_a = _na_attention(qa, ka, va, kca, vca, _na_bias_table(na_rpb), _score_bound(hg[0], hg[1], HEAD_DIM))
    out_b = _flash(qb, kb, vb, kcb, vcb, mode="diff", kv_map=ident, n_chunks=4, tq=FLASH_TQ, tk=FLASH_TK,
                   extra=(lam_p, sg), lam_init=lam_init, score_bound=_score_bound(hg[2], hg[3], HEAD_DIM))
    x1, h2 = _post_attn(out_a, out_b, wo_a, wo_b, x, g1, ng2, sh2, sc2, tm=tm)
    x2 = _ffn(h2, x1, g2, wg_b, wu_b, wd_b, tm=tm)
    if not need_ctx:
        return x2, None
    oca = _flash(qca, None, None, kca, vca, mode="pair", kv_map=ident, n_chunks=4, tq=tc, tk=tc)
    ocb = _flash(qcb, None, None, kcb, vcb, mode="diff", kv_map=ident, n_chunks=4, tq=tc, tk=tc,
                 extra=(lam_p, sg), lam_init=lam_init)
    xc1, hc2 = _post_attn(oca, ocb, wo_a, wo_b, xc, cg1, ng2, csh2, csc2, tm=tc)
    xc2 = _ffn(hc2, xc1, cg2, wg_b, wu_b, wd_b, tm=tc)
    return x2, xc2


def _odd_layer(x, xc, mod_l, norm1_g, norm2_g, w_in, w_out, gqa_q_g, gqa_k_g, cq_g, w_uq, ckv_g, w_ukv,
               mla_q_g, mla_k_g, router, w1, w3, w2):
    b, s, d = x.shape
    n_tok = b * s
    (sh1, sc1, g1, sh2, sc2, g2), (csh1, csc1, _, _, _, _) = _split_mod(mod_l, b)
    ng1 = norm1_g[None, :]
    ng2 = norm2_g[None, :]

    z = lambda n: jnp.zeros((d, n), w_in.dtype)
    k0, k1, v0, v1 = (w_in[:, 512 + 64 * i:576 + 64 * i] for i in range(4))
    w_p = jnp.concatenate([w_in[:, 0:512], k0, k0, k1, k1, v0, v0, v1, v1, w_in[:, 768:1024], w_in[:, 1024:1152],
                           z(MLA_NOPE), w_in[:, 1152:1184], z(LANES - MLA_QK)], axis=1).astype(BF16)
    wuq_p = jnp.pad(w_uq.reshape(MLA_Q_RANK, MLA_HEADS, MLA_QK), ((0, 0), (0, 0), (0, LANES - MLA_QK)))
    wuq_p = wuq_p.reshape(MLA_Q_RANK, MLA_HEADS * LANES).astype(BF16)
    ukv = w_ukv.reshape(MLA_KV_RANK, MLA_HEADS, MLA_NOPE + MLA_V)
    uk = jnp.pad(ukv[:, :, :MLA_NOPE], ((0, 0), (0, 0), (0, LANES - MLA_NOPE))).reshape(MLA_KV_RANK, -1)
    uv = ukv[:, :, MLA_NOPE:].reshape(MLA_KV_RANK, -1)
    wukv_p = jnp.concatenate([uk, uv], axis=1).astype(BF16)
    e512 = _block_ones(512, HEAD_DIM)
    a = (jnp.arange(MLA_HEADS * LANES)[:, None] // LANES == jnp.arange(LANES)[None, :]).astype(BF16)
    at = a.T
    qscale = HEAD_DIM ** -0.5 * LOG2E
    mscale = MLA_QK ** -0.5 * LOG2E
    pad_row = lambda v: jnp.pad(v, (0, MLA_HEADS * LANES - v.shape[0]))
    pad_head = lambda v: jnp.tile(jnp.pad(v, (0, LANES - MLA_QK)), MLA_HEADS)
    gains = jnp.stack([pad_row(jnp.tile(gqa_q_g, 8) * qscale), pad_row(jnp.tile(gqa_k_g, 4)),
                       pad_row(cq_g), pad_row(ckv_g), pad_head(mla_q_g) * mscale, pad_head(mla_k_g),
                       jnp.zeros((MLA_HEADS * LANES,), F32), jnp.zeros((MLA_HEADS * LANES,), F32)]).astype(F32)
    cos, sin = _rope_tables(s, HEAD_DIM, 16)
    cosm, sinm = _mla_rope_tables(s)
    wo_a = w_out[:512].astype(BF16)
    wo_b = w_out[512:].astype(BF16)

    tm = min(512, s)
    tc = xc.shape[1]
    proj = functools.partial(_odd_proj, ng=ng1, w=w_p, wuq=wuq_p, wukv=wukv_p, e512=e512, a=a, at=at, gains=gains,
                             cos=cos, sin=sin, cosm=cosm, sinm=sinm)
    qc, kc, vc, qd, kd, vd = proj(x, sh1, sc1, use_rope=True, tm=tm)
    _, kcc, vcc, _, kcd, vcd = proj(xc, csh1, csc1, use_rope=False, tm=tc)
    out_c = _flash(qc, kc, vc, kcc, vcc, mode="pair", kv_map=lambda j: j // 2, n_chunks=4, tq=FLASH_TQ, tk=FLASH_TK,
                   score_bound=_score_bound(gains[0], gains[1], HEAD_DIM))
    out_d = _flash(qd, kd, vd, kcd, vcd, mode="mla", kv_map=lambda j: j, n_chunks=4, tq=FLASH_TQ, tk=FLASH_TK,
                   score_bound=_score_bound(gains[4], gains[5], MLA_QK))
    x1, h2 = _post_attn(out_c, out_d, wo_a, wo_b, x, g1, ng2, sh2, sc2, tm=tm)

    ei, gt, rk, cnt = _router(x1, ng2, sh2, sc2, router.T.astype(F32), tm=tm)
    plan = _moe_plan(ei, rk, cnt[:, 0], n_tok)
    xg, gs = _moe_gather(h2.reshape(n_tok, d), plan["dest"], gt, plan)
    y = _moe_experts(xg, gs, w1.astype(BF16), w3.astype(BF16), w2.astype(BF16), plan)
    out = _moe_combine(y, plan["dest"].T, x1.reshape(n_tok, d), g2, plan, s)
    return out.reshape(b, s, d)


def _mod_vectors(c, c_ctx, ada_w, ada_b):
    b = c.shape[0]
    cc = jnp.zeros((MOD_ROWS, D_MODEL), F32).at[:b].set(c).at[b].set(c_ctx)
    return _modvec(cc, ada_w, ada_b[:, None, :])


def kernel(x, c, ctx, c_ctx, ada_w, ada_b, norm1_g, norm2_g, ev_w_in, ev_w_out, na_q_g, na_k_g, na_rpb,
           diff_q_g, diff_k_g, diff_lq1, diff_lk1, diff_lq2, diff_lk2, diff_subln_g,
           ffn_w_gate, ffn_w_up, ffn_w_down, od_w_in, od_w_out, gqa_q_g, gqa_k_g, mla_cq_g, mla_w_uq,
           mla_ckv_g, mla_w_ukv, mla_q_g, mla_k_g, moe_router, moe_w1, moe_w3, moe_w2):
    mod = _mod_vectors(c, c_ctx, ada_w, ada_b)
    x, xc = _even_layer(x, ctx, mod[0], norm1_g[0], norm2_g[0], ev_w_in[0], ev_w_out[0], na_q_g[0], na_k_g[0],
                        na_rpb[0], diff_q_g[0], diff_k_g[0], diff_lq1[0], diff_lk1[0], diff_lq2[0], diff_lk2[0],
                        diff_subln_g[0], ffn_w_gate[0], ffn_w_up[0], ffn_w_down[0], 0, True)
    return _odd_layer(x, xc, mod[1], norm1_g[1], norm2_g[1], od_w_in[0], od_w_out[0], gqa_q_g[0], gqa_k_g[0],
                      mla_cq_g[0], mla_w_uq[0], mla_ckv_g[0], mla_w_ukv[0], mla_q_g[0], mla_k_g[0],
                      moe_router[0], moe_w1[0], moe_w3[0], moe_w2[0])
```

```python
import functools
import math

import jax
import jax.numpy as jnp
from jax import lax
from jax.experimental import pallas as pl
from jax.experimental.pallas import tpu as pltpu

F32 = jnp.float32
BF16 = jnp.bfloat16
HIGHEST = lax.Precision.HIGHEST

D_MODEL = 1024
GRID_W = 64
HEAD_DIM = 64
ROPE_THETA = 10000.0
NORM_EPS = 1e-6
NEG_INF = -1e30
ADA_CHUNKS = 6
LOG2E = 1.4426950408889634

NA_HEADS = 8
NA_WIN_H = 8
NA_WIN_W = 16
DIFF_HEADS = 4
DIFF_V_DIM = 2 * HEAD_DIM
GQA_Q_HEADS = 8
GQA_KV_HEADS = 2
MLA_HEADS = 8
MLA_NOPE = 64
MLA_ROPE = 32
MLA_QK = MLA_NOPE + MLA_ROPE
MLA_V = 64
MLA_Q_RANK = 256
MLA_KV_RANK = 128
D_FF = 2816
N_EXPERTS = 8
D_FF_EXPERT = 3584

LANES = 128
MXU_DIM = 256
VMEM_LIMIT = 56 * 1024 * 1024
MOD_ROWS = 16
MOE_TF = 512
ROW_TM = 1024
FLASH_TQ = 512
FLASH_TK = 512
EXP2_SAFE_RANGE = 64.0


def _cparams(sem):
    return pltpu.CompilerParams(dimension_semantics=sem, vmem_limit_bytes=VMEM_LIMIT)


def _dot(a, b):
    return jnp.dot(a, b, preferred_element_type=F32)


def _dot_nt(a, b):
    return lax.dot_general(a, b, (((1,), (1,)), ((), ())), preferred_element_type=F32)


def _silu(x):
    return x * (1.0 / (1.0 + jnp.exp(-x)))


def _modvec_kernel(c_ref, w_ref, b_ref, o_ref):
    s = _silu(c_ref[...])
    o_ref[...] = jnp.dot(s, w_ref[...], preferred_element_type=F32, precision=HIGHEST) + b_ref[...]


def _modvec(cc, ada_w, ada_b):
    n_layers, d, n = ada_w.shape
    tn = 768
    return pl.pallas_call(
        _modvec_kernel,
        out_shape=jax.ShapeDtypeStruct((n_layers, MOD_ROWS, n), F32),
        grid=(n_layers, n // tn),
        in_specs=[
            pl.BlockSpec((MOD_ROWS, d), lambda l, j: (0, 0)),
            pl.BlockSpec((None, d, tn), lambda l, j: (l, 0, j)),
            pl.BlockSpec((None, 1, tn), lambda l, j: (l, 0, j)),
        ],
        out_specs=pl.BlockSpec((None, MOD_ROWS, tn), lambda l, j: (l, 0, j)),
        compiler_params=_cparams(("arbitrary", "arbitrary")),
        name="modvec",
    )(cc, ada_w, ada_b)


def _norm_mod(x, g, shift, scale):
    ms = jnp.mean(x * x, axis=-1, keepdims=True)
    return (x * lax.rsqrt(ms + NORM_EPS) * g) * (1.0 + scale) + shift


def _group_norm(t, e, g, inv_n):
    ss = _dot((t * t).astype(BF16), e)
    return t * lax.rsqrt(ss * inv_n + NORM_EPS) * g


def _rope_chunks(t, cos, sin, lane_hi, shift):
    outs = []
    for c in range(t.shape[1] // LANES):
        tc = t[:, c * LANES:(c + 1) * LANES]
        up = pltpu.roll(tc, LANES - shift, 1)
        dn = pltpu.roll(tc, shift, 1)
        outs.append(tc * cos + jnp.where(lane_hi, dn, up) * sin)
    return outs[0] if len(outs) == 1 else jnp.concatenate(outs, axis=1)


def _rope_tables(seq, unit, pair_shift):
    pos = jnp.arange(seq)
    rows, cols = pos // GRID_W, pos % GRID_W
    quarter = unit // 4
    assert quarter == pair_shift
    freqs = ROPE_THETA ** (-jnp.arange(quarter, dtype=F32) / quarter)
    lane = jnp.arange(LANES)
    u = lane % unit
    use_col = (u // (unit // 2)) == 1
    fi = u % quarter
    p = jnp.where(use_col[None, :], cols[:, None], rows[:, None]).astype(F32)
    ang = p * freqs[fi][None, :]
    second = ((u % (unit // 2)) // quarter) == 1
    cos = jnp.cos(ang)
    sin = jnp.where(second[None, :], jnp.sin(ang), -jnp.sin(ang))
    return cos, sin


def _even_proj_kernel(x_ref, sh_ref, sc_ref, ng_ref, w_ref, e_ref, hg_ref, cos_ref, sin_ref,
                      qa_ref, ka_ref, va_ref, qb_ref, kb_ref, vb_ref, *, use_rope):
    h = _norm_mod(x_ref[0], ng_ref[...], sh_ref[0], sc_ref[0])
    p = _dot(h.astype(BF16), w_ref[...])
    hg = hg_ref[...]
    e = e_ref[...]
    inv_n = 1.0 / HEAD_DIM
    qa = _group_norm(p[:, 0:512], e, hg[0:1], inv_n)
    ka = _group_norm(p[:, 512:1024], e, hg[1:2], inv_n)
    qb = _group_norm(p[:, 1536:2048], e, hg[2:3], inv_n)
    kb = _group_norm(p[:, 2048:2560], e, hg[3:4], inv_n)
    if use_rope:
        lane = lax.broadcasted_iota(jnp.int32, (1, LANES), 1)
        lane_hi = (lane & 16) != 0
        cos, sin = cos_ref[...], sin_ref[...]
        qb = _rope_chunks(qb, cos, sin, lane_hi, 16)
        kb = _rope_chunks(kb, cos, sin, lane_hi, 16)
    qa_ref[0] = qa.astype(BF16)
    ka_ref[0] = ka.astype(BF16)
    va_ref[0] = p[:, 1024:1536].astype(BF16)
    qb_ref[0] = qb.astype(BF16)
    kb_ref[0] = kb.astype(BF16)
    vb_ref[0] = p[:, 2560:3072].astype(BF16)


def _even_proj(x, shift, scale, ng, w, e512, hg, cos, sin, *, use_rope, tm):
    b, s, d = x.shape
    n = w.shape[1]
    row = lambda bi, i: (bi, i, 0)
    vec = lambda bi, i: (bi, 0, 0)
    const = lambda bi, i: (0, 0)
    out = jax.ShapeDtypeStruct((b, s, 512), BF16)
    return pl.pallas_call(
        functools.partial(_even_proj_kernel, use_rope=use_rope),
        out_shape=(out,) * 6,
        grid=(b, s // tm),
        in_specs=[
            pl.BlockSpec((1, tm, d), row),
            pl.BlockSpec((1, 1, d), vec),
            pl.BlockSpec((1, 1, d), vec),
            pl.BlockSpec((1, d), const),
            pl.BlockSpec((d, n), const),
            pl.BlockSpec((512, 512), const),
            pl.BlockSpec((4, 512), const),
            pl.BlockSpec((tm, LANES), lambda bi, i: (i, 0)),
            pl.BlockSpec((tm, LANES), lambda bi, i: (i, 0)),
        ],
        out_specs=(pl.BlockSpec((1, tm, 512), row),) * 6,
        compiler_params=_cparams(("parallel", "parallel")),
        name="even_proj",
    )(x, shift, scale, ng, w, e512, hg, cos, sin)


def _pair_stack(q):
    lane = lax.broadcasted_iota(jnp.int32, q.shape, 1)
    zero = jnp.zeros_like(q)
    return jnp.concatenate([jnp.where(lane < HEAD_DIM, q, zero), jnp.where(lane >= HEAD_DIM, q, zero)], axis=0)


def _pair_merge(o, tq):
    lane = lax.broadcasted_iota(jnp.int32, (tq, LANES), 1)
    return jnp.where(lane < HEAD_DIM, o[:tq], o[tq:])


def _with_ones(v):
    return jnp.concatenate([v, jnp.ones(v.shape, BF16)], axis=1)


def _na_kernel(q_ref, k_ref, v_ref, kc_ref, vc_ref, bias_ref, o_ref, *, rows_per_step, n_rows, direct):
    rb = pl.program_id(2)
    kc = kc_ref[0]
    vc = vc_ref[0]
    n_lat = NA_WIN_H * GRID_W
    if direct:
        nq = rows_per_step * GRID_W
        qs_all = _pair_stack(q_ref[0])
        o_ctx = _dot(jnp.exp2(_dot_nt(qs_all, kc)).astype(BF16), _with_ones(vc))
        for i in range(rows_per_step):
            r = rb * rows_per_step + i
            r0 = jnp.clip(r - NA_WIN_H // 2, 0, n_rows - NA_WIN_H)
            start = pl.multiple_of(r0 * GRID_W, GRID_W)
            kw = k_ref[0, pl.ds(start, n_lat), :]
            vw = v_ref[0, pl.ds(start, n_lat), :]
            lo = slice(i * GRID_W, (i + 1) * GRID_W)
            hi = slice(nq + i * GRID_W, nq + (i + 1) * GRID_W)
            qs = jnp.concatenate([qs_all[lo], qs_all[hi]], axis=0)
            p = jnp.exp2(_dot_nt(qs, kw) + bias_ref[r - r0]).astype(BF16)
            tot = _dot(p, _with_ones(vw)) + jnp.concatenate([o_ctx[lo], o_ctx[hi]], axis=0)
            o = tot[:, :LANES] / tot[:, LANES:]
            o_ref[0, lo, :] = _pair_merge(o, GRID_W).astype(BF16)
        return
    for i in range(rows_per_step):
        r = rb * rows_per_step + i
        r0 = jnp.clip(r - NA_WIN_H // 2, 0, n_rows - NA_WIN_H)
        start = pl.multiple_of(r0 * GRID_W, GRID_W)
        kw = k_ref[0, pl.ds(start, n_lat), :]
        vw = v_ref[0, pl.ds(start, n_lat), :]
        qs = _pair_stack(q_ref[0, i * GRID_W:(i + 1) * GRID_W, :])
        s_lat = _dot_nt(qs, kw) + bias_ref[r - r0]
        s_ctx = _dot_nt(qs, kc)
        m = jnp.maximum(jnp.max(s_lat, axis=-1, keepdims=True), jnp.max(s_ctx, axis=-1, keepdims=True))
        p_lat = jnp.exp2(s_lat - m)
        p_ctx = jnp.exp2(s_ctx - m)
        l = jnp.sum(p_lat, axis=-1, keepdims=True) + jnp.sum(p_ctx, axis=-1, keepdims=True)
        o = _dot(p_lat.astype(BF16), vw) + _dot(p_ctx.astype(BF16), vc)
        o = o / l
        o_ref[0, i * GRID_W:(i + 1) * GRID_W, :] = _pair_merge(o, GRID_W).astype(BF16)


def _na_attention(q, k, v, kc, vc, bias, score_bound, *, rows_per_step=8):
    bound = score_bound + jnp.max(jnp.where(bias > 0.5 * NEG_INF, jnp.abs(bias), 0.0))
    return lax.cond(bound <= EXP2_SAFE_RANGE,
                    lambda: _na_call(q, k, v, kc, vc, bias, rows_per_step=rows_per_step, direct=True),
                    lambda: _na_call(q, k, v, kc, vc, bias, rows_per_step=rows_per_step, direct=False))


def _na_call(q, k, v, kc, vc, bias, *, rows_per_step, direct):
    b, s, _ = q.shape
    c = kc.shape[1]
    n_rows = s // GRID_W
    tq = rows_per_step * GRID_W
    return pl.pallas_call(
        functools.partial(_na_kernel, rows_per_step=rows_per_step, n_rows=n_rows, direct=direct),
        out_shape=jax.ShapeDtypeStruct((b, s, 512), BF16),
        grid=(b, 4, n_rows // rows_per_step),
        in_specs=[
            pl.BlockSpec((1, tq, LANES), lambda bi, j, i: (bi, i, j)),
            pl.BlockSpec((1, s, LANES), lambda bi, j, i: (bi, 0, j)),
            pl.BlockSpec((1, s, LANES), lambda bi, j, i: (bi, 0, j)),
            pl.BlockSpec((1, c, LANES), lambda bi, j, i: (bi, 0, j)),
            pl.BlockSpec((1, c, LANES), lambda bi, j, i: (bi, 0, j)),
            pl.BlockSpec((NA_WIN_H, None, LANES, NA_WIN_H * GRID_W), lambda bi, j, i: (0, j, 0, 0)),
        ],
        out_specs=pl.BlockSpec((1, tq, LANES), lambda bi, j, i: (bi, i, j)),
        compiler_params=_cparams(("parallel", "parallel", "parallel")),
        name="na_attention" + ("" if direct else "_online"),
    )(q, k, v, kc, vc, bias)


def _na_bias_table(rpb):
    w = GRID_W
    col = jnp.arange(w)
    c0 = jnp.clip(col - NA_WIN_W // 2, 0, w - NA_WIN_W)
    col_in = (col[None, :] >= c0[:, None]) & (col[None, :] < c0[:, None] + NA_WIN_W)
    left = (w - 1) - (NA_WIN_W - 1)
    ext = jnp.pad(rpb, ((0, 0), (0, 0), (left, 2 * w - left - (2 * NA_WIN_W - 1))), mode="edge")
    h, nr, _ = rpb.shape
    flat = jnp.broadcast_to(ext[:, :, None, :], (h, nr, w, 2 * w)).reshape(h, nr, w * 2 * w)
    toep = flat[:, :, :w * (2 * w - 1)].reshape(h, nr, w, 2 * w - 1)[:, :, :, w - 1:]
    toep = jnp.where(col_in[None, None], toep * LOG2E, NEG_INF)
    variants = []
    for v in range(NA_WIN_H):
        tv = toep[:, NA_WIN_H - 1 - v:2 * NA_WIN_H - 1 - v]
        variants.append(tv.transpose(0, 2, 1, 3).reshape(NA_HEADS // 2, 2 * w, NA_WIN_H * w))
    return jnp.stack(variants, axis=0).astype(F32)


def _flash_kernel(*refs, mode, online, tq, tk, s_len, c_len, lam_init):
    if mode == "diff":
        q_ref, k_ref, v_ref, kc_ref, vc_ref, lam_ref, sg_ref, o_ref = refs
    else:
        q_ref, k_ref, v_ref, kc_ref, vc_ref, o_ref = refs

    q = q_ref[0]
    if mode == "mla":
        q_parts = (q[:, :LANES], q[:, LANES:])
    else:
        qs = _pair_stack(q)

    def scores(kt):
        if mode == "mla":
            return jnp.concatenate([_dot_nt(q_parts[0], kt[:, :LANES]), _dot_nt(q_parts[1], kt[:, LANES:])], axis=0)
        return _dot_nt(qs, kt)

    def step(kt, vt, carry):
        m, l, acc = carry
        s = scores(kt)
        m_new = jnp.maximum(m, jnp.max(s, axis=-1, keepdims=True))
        alpha = jnp.exp2(m - m_new)
        p = jnp.exp2(s - m_new)
        l = alpha * l + jnp.sum(p, axis=-1, keepdims=True)
        acc = alpha * acc + _dot(p.astype(BF16), vt)
        return m_new, l, acc

    def body(t, carry):
        start = pl.multiple_of(t * tk, tk)
        return step(k_ref[0, pl.ds(start, tk), :], v_ref[0, pl.ds(start, tk), :], carry)

    def direct(kt, vt, acc):
        return acc + _dot(jnp.exp2(scores(kt)).astype(BF16), _with_ones(vt))

    if online:
        carry = (jnp.full((2 * tq, 1), -jnp.inf, F32), jnp.zeros((2 * tq, 1), F32),
                 jnp.zeros((2 * tq, LANES), F32))
        if s_len:
            carry = lax.fori_loop(0, s_len // tk, body, carry)
        if c_len:
            carry = step(kc_ref[0], vc_ref[0], carry)
        _, l, acc = carry
        o = acc / l
    else:
        acc = jnp.zeros((2 * tq, 2 * LANES), F32)
        for t in range(s_len // tk):
            acc = direct(k_ref[0, t * tk:(t + 1) * tk, :], v_ref[0, t * tk:(t + 1) * tk, :], acc)
        if c_len:
            acc = direct(kc_ref[0], vc_ref[0], acc)
        o = acc[:, :LANES] / acc[:, LANES:]
    if mode == "diff":
        lp = lam_ref[...]
        lam = (jnp.exp(jnp.sum(lp[0:1] * lp[1:2], axis=-1, keepdims=True))
               - jnp.exp(jnp.sum(lp[2:3] * lp[3:4], axis=-1, keepdims=True)) + lam_init)
        d = o[:tq] - lam * o[tq:]
        ms = jnp.mean(d * d, axis=-1, keepdims=True)
        o_ref[0] = (d * lax.rsqrt(ms + NORM_EPS) * sg_ref[...] * (1.0 - lam_init)).astype(BF16)
    else:
        o_ref[0] = _pair_merge(o, tq).astype(BF16)


def _flash(q, k, v, kc, vc, *, score_bound=None, **kw):
    if score_bound is None:
        return _flash_call(q, k, v, kc, vc, online=True, **kw)
    return lax.cond(score_bound <= EXP2_SAFE_RANGE,
                    lambda: _flash_call(q, k, v, kc, vc, online=False, **kw),
                    lambda: _flash_call(q, k, v, kc, vc, online=True, **kw))


def _flash_call(q, k, v, kc, vc, *, mode, online, kv_map, n_chunks, tq, tk, extra=(), lam_init=0.0):
    b, s, _ = q.shape
    qw = 2 * LANES if mode == "mla" else LANES
    c_len = kc.shape[1]
    if k is None:
        k, v, s_len = kc, vc, 0
    else:
        s_len = k.shape[1]
    kk = k.shape[1]
    in_specs = [
        pl.BlockSpec((1, tq, qw), lambda bi, j, i: (bi, i, j)),
        pl.BlockSpec((1, kk, qw), lambda bi, j, i: (bi, 0, kv_map(j) if mode != "mla" else j)),
        pl.BlockSpec((1, kk, LANES), lambda bi, j, i: (bi, 0, kv_map(j))),
        pl.BlockSpec((1, c_len, qw), lambda bi, j, i: (bi, 0, kv_map(j) if mode != "mla" else j)),
        pl.BlockSpec((1, c_len, LANES), lambda bi, j, i: (bi, 0, kv_map(j))),
    ]
    for e in extra:
        in_specs.append(pl.BlockSpec(e.shape, lambda bi, j, i: (0, 0)))
    return pl.pallas_call(
        functools.partial(_flash_kernel, mode=mode, online=online, tq=tq, tk=tk, s_len=s_len, c_len=c_len,
                          lam_init=lam_init),
        out_shape=jax.ShapeDtypeStruct((b, s, n_chunks * LANES), BF16),
        grid=(b, n_chunks, s // tq),
        in_specs=in_specs,
        out_specs=pl.BlockSpec((1, tq, LANES), lambda bi, j, i: (bi, i, j)),
        compiler_params=_cparams(("parallel", "parallel", "parallel")),
        name="flash_" + mode + ("_online" if online else ""),
    )(q, k, v, kc, vc, *extra)


def _post_attn_kernel(oa_ref, ob_ref, wa_ref, wb_ref, x_ref, g_ref, ng_ref, sh_ref, sc_ref, x1_ref, h2_ref):
    y = _dot(oa_ref[0], wa_ref[...]) + _dot(ob_ref[0], wb_ref[...])
    x1 = x_ref[0] + g_ref[0] * y
    x1_ref[0] = x1
    h2_ref[0] = _norm_mod(x1, ng_ref[...], sh_ref[0], sc_ref[0]).astype(BF16)


def _post_attn(oa, ob, wa, wb, x, gate, ng, shift, scale, *, tm):
    b, s, d = x.shape
    row = lambda bi, i: (bi, i, 0)
    vec = lambda bi, i: (bi, 0, 0)
    const = lambda bi, i: (0, 0)
    return pl.pallas_call(
        _post_attn_kernel,
        out_shape=(jax.ShapeDtypeStruct((b, s, d), F32), jax.ShapeDtypeStruct((b, s, d), BF16)),
        grid=(b, s // tm),
        in_specs=[
            pl.BlockSpec((1, tm, 512), row),
            pl.BlockSpec((1, tm, 512), row),
            pl.BlockSpec((512, d), const),
            pl.BlockSpec((512, d), const),
            pl.BlockSpec((1, tm, d), row),
            pl.BlockSpec((1, 1, d), vec),
            pl.BlockSpec((1, d), const),
            pl.BlockSpec((1, 1, d), vec),
            pl.BlockSpec((1, 1, d), vec),
        ],
        out_specs=(pl.BlockSpec((1, tm, d), row), pl.BlockSpec((1, tm, d), row)),
        compiler_params=_cparams(("parallel", "parallel")),
        name="post_attn",
    )(oa, ob, wa, wb, x, gate, ng, shift, scale)


def _ffn_kernel(h_ref, x1_ref, g_ref, wg_ref, wu_ref, wd_ref, o_ref, *, tf):
    h = h_ref[0]
    acc = jnp.zeros(o_ref.shape[1:], F32)
    for f in range(wg_ref.shape[1] // tf):
        sl = slice(f * tf, (f + 1) * tf)
        a = _dot(h, wg_ref[:, sl])
        u = _dot(h, wu_ref[:, sl])
        acc = acc + _dot((_silu(a) * u).astype(BF16), wd_ref[sl, :])
    o_ref[0] = x1_ref[0] + g_ref[0] * acc


def _ffn(h2, x1, gate, wg, wu, wd, *, tm, tf=256):
    b, s, d = x1.shape
    f = wg.shape[1]
    row = lambda bi, i: (bi, i, 0)
    vec = lambda bi, i: (bi, 0, 0)
    const = lambda bi, i: (0, 0)
    return pl.pallas_call(
        functools.partial(_ffn_kernel, tf=tf),
        out_shape=jax.ShapeDtypeStruct((b, s, d), F32),
        grid=(b, s // tm),
        in_specs=[
            pl.BlockSpec((1, tm, d), row),
            pl.BlockSpec((1, tm, d), row),
            pl.BlockSpec((1, 1, d), vec),
            pl.BlockSpec((d, f), const),
            pl.BlockSpec((d, f), const),
            pl.BlockSpec((f, d), const),
        ],
        out_specs=pl.BlockSpec((1, tm, d), row),
        compiler_params=_cparams(("parallel", "parallel")),
        name="ffn",
    )(h2, x1, gate, wg, wu, wd)


def _rms(t, g):
    return t * lax.rsqrt(jnp.mean(t * t, axis=-1, keepdims=True) + NORM_EPS) * g


def _head_norm_128(t, a_ref, at_ref, g, inv_n):
    ss = _dot((t * t).astype(BF16), a_ref[...])
    r = lax.rsqrt(ss * inv_n + NORM_EPS)
    r_hi = r.astype(BF16)
    r_lo = (r - r_hi.astype(F32)).astype(BF16)
    rb = _dot(r_hi, at_ref[...]) + _dot(r_lo, at_ref[...])
    return t * rb * g


def _odd_proj_kernel(x_ref, sh_ref, sc_ref, ng_ref, w_ref, wuq_ref, wukv_ref, e_ref, a_ref, at_ref, g_ref,
                     cos_ref, sin_ref, cosm_ref, sinm_ref,
                     qc_ref, kc_ref, vc_ref, qd_ref, kd_ref, vd_ref, *, use_rope):
    h = _norm_mod(x_ref[0], ng_ref[...], sh_ref[0], sc_ref[0])
    p = _dot(h.astype(BF16), w_ref[...])
    g = g_ref[...]
    e = e_ref[...]
    inv_n = 1.0 / HEAD_DIM
    qc = _group_norm(p[:, 0:512], e, g[0:1, 0:512], inv_n)
    kc = _group_norm(p[:, 512:768], e[0:256, 0:256], g[1:2, 0:256], inv_n)
    cq = _rms(p[:, 1024:1280], g[2:3, 0:256])
    ckv = _rms(p[:, 1280:1408], g[3:4, 0:128])
    krc = p[:, 1408:1536]
    qd = _dot(cq.astype(BF16), wuq_ref[...])
    kvv = _dot(ckv.astype(BF16), wukv_ref[...])
    kd = kvv[:, 0:1024] + jnp.concatenate([krc] * MLA_HEADS, axis=1)
    qd = _head_norm_128(qd, a_ref, at_ref, g[4:5], 1.0 / MLA_QK)
    kd = _head_norm_128(kd, a_ref, at_ref, g[5:6], 1.0 / MLA_QK)
    if use_rope:
        lane = lax.broadcasted_iota(jnp.int32, (1, LANES), 1)
        cos, sin = cos_ref[...], sin_ref[...]
        qc = _rope_chunks(qc, cos, sin, (lane & 16) != 0, 16)
        kc = _rope_chunks(kc, cos, sin, (lane & 16) != 0, 16)
        cosm, sinm = cosm_ref[...], sinm_ref[...]
        qd = _rope_chunks(qd, cosm, sinm, (lane & 8) != 0, 8)
        kd = _rope_chunks(kd, cosm, sinm, (lane & 8) != 0, 8)
    qc_ref[0] = qc.astype(BF16)
    kc_ref[0] = kc.astype(BF16)
    vc_ref[0] = p[:, 768:1024].astype(BF16)
    qd_ref[0] = qd.astype(BF16)
    kd_ref[0] = kd.astype(BF16)
    vd_ref[0] = kvv[:, 1024:1536].astype(BF16)


def _odd_proj(x, shift, scale, ng, w, wuq, wukv, e512, a, at, gains, cos, sin, cosm, sinm, *, use_rope, tm):
    b, s, d = x.shape
    row = lambda bi, i: (bi, i, 0)
    vec = lambda bi, i: (bi, 0, 0)
    const = lambda bi, i: (0, 0)
    tab = lambda bi, i: (i, 0)
    widths = (512, 256, 256, 1024, 1024, 512)
    return pl.pallas_call(
        functools.partial(_odd_proj_kernel, use_rope=use_rope),
        out_shape=tuple(jax.ShapeDtypeStruct((b, s, n), BF16) for n in widths),
        grid=(b, s // tm),
        in_specs=[
            pl.BlockSpec((1, tm, d), row),
            pl.BlockSpec((1, 1, d), vec),
            pl.BlockSpec((1, 1, d), vec),
            pl.BlockSpec((1, d), const),
            pl.BlockSpec(w.shape, const),
            pl.BlockSpec(wuq.shape, const),
            pl.BlockSpec(wukv.shape, const),
            pl.BlockSpec(e512.shape, const),
            pl.BlockSpec(a.shape, const),
            pl.BlockSpec(at.shape, const),
            pl.BlockSpec(gains.shape, const),
            pl.BlockSpec((tm, LANES), tab),
            pl.BlockSpec((tm, LANES), tab),
            pl.BlockSpec((tm, LANES), tab),
            pl.BlockSpec((tm, LANES), tab),
        ],
        out_specs=tuple(pl.BlockSpec((1, tm, n), row) for n in widths),
        compiler_params=_cparams(("parallel", "parallel")),
        name="odd_proj",
    )(x, shift, scale, ng, w, wuq, wukv, e512, a, at, gains, cos, sin, cosm, sinm)


def _mla_rope_tables(seq):
    cos, sin = _rope_tables(seq, MLA_ROPE, MLA_ROPE // 4)
    lane = jnp.arange(LANES)
    on = (lane >= MLA_NOPE) & (lane < MLA_QK)
    return jnp.where(on[None, :], cos, 1.0), jnp.where(on[None, :], sin, 0.0)


def _router_kernel(x1_ref, ng_ref, sh_ref, sc_ref, rt_ref, u_ref, ei_ref, gt_ref, rk_ref, cnt_ref, carry_ref):
    first = (pl.program_id(0) == 0) & (pl.program_id(1) == 0)

    @pl.when(first)
    def _():
        carry_ref[...] = jnp.zeros_like(carry_ref)

    h = _norm_mod(x1_ref[0], ng_ref[...], sh_ref[0], sc_ref[0])
    logits = lax.dot_general(rt_ref[...], h, (((1,), (1,)), ((), ())),
                             preferred_element_type=F32, precision=HIGHEST)
    eidx = lax.broadcasted_iota(jnp.int32, logits.shape, 0)
    m1 = jnp.max(logits, axis=0, keepdims=True)
    i1 = jnp.min(jnp.where(logits == m1, eidx, N_EXPERTS), axis=0, keepdims=True)
    rest = jnp.where(eidx == i1, -jnp.inf, logits)
    m2 = jnp.max(rest, axis=0, keepdims=True)
    i2 = jnp.min(jnp.where(rest == m2, eidx, N_EXPERTS), axis=0, keepdims=True)
    e2 = jnp.exp(m2 - m1)
    g1 = 1.0 / (1.0 + e2)
    sel1 = eidx == i1
    sel2 = eidx == i2
    onehot = jnp.where(sel1 | sel2, 1.0, 0.0)
    before = _dot(onehot.astype(BF16), u_ref[...]) + carry_ref[:, 0:1]
    r1 = jnp.sum(jnp.where(sel1, before, 0.0), axis=0, keepdims=True)
    r2 = jnp.sum(jnp.where(sel2, before, 0.0), axis=0, keepdims=True)
    ei_ref[...] = jnp.concatenate([i1, i2], axis=0)
    gt_ref[...] = jnp.concatenate([g1, e2 * g1], axis=0)
    rk_ref[...] = jnp.concatenate([r1, r2], axis=0).astype(jnp.int32)
    total = carry_ref[...] + jnp.sum(onehot, axis=1, keepdims=True)
    carry_ref[...] = total
    cnt_ref[...] = total.astype(jnp.int32)


def _router(x1, ng, shift, scale, router_t, *, tm):
    b, s, d = x1.shape
    n = b * s
    nt = s // tm
    tri = (jnp.arange(tm)[:, None] < jnp.arange(tm)[None, :]).astype(BF16)
    flat = lambda bi, i: (0, bi * nt + i)
    return pl.pallas_call(
        _router_kernel,
        out_shape=(jax.ShapeDtypeStruct((2, n), jnp.int32), jax.ShapeDtypeStruct((2, n), F32),
                   jax.ShapeDtypeStruct((2, n), jnp.int32), jax.ShapeDtypeStruct((N_EXPERTS, LANES), jnp.int32)),
        grid=(b, nt),
        in_specs=[
            pl.BlockSpec((1, tm, d), lambda bi, i: (bi, i, 0)),
            pl.BlockSpec((1, d), lambda bi, i: (0, 0)),
            pl.BlockSpec((1, 1, d), lambda bi, i: (bi, 0, 0)),
            pl.BlockSpec((1, 1, d), lambda bi, i: (bi, 0, 0)),
            pl.BlockSpec((N_EXPERTS, d), lambda bi, i: (0, 0)),
            pl.BlockSpec((tm, tm), lambda bi, i: (0, 0)),
        ],
        out_specs=(pl.BlockSpec((2, tm), flat), pl.BlockSpec((2, tm), flat), pl.BlockSpec((2, tm), flat),
                   pl.BlockSpec((N_EXPERTS, LANES), lambda bi, i: (0, 0))),
        scratch_shapes=[pltpu.VMEM((N_EXPERTS, LANES), F32)],
        compiler_params=_cparams(("arbitrary", "arbitrary")),
        name="router",
    )(x1, ng, shift, scale, router_t, tri)


MOE_SUB = 256
MOE_BLK = 1024
MOE_TC = 256
MOE_NBUF = 8
MOE_AHEAD = MOE_NBUF - 2


def _lookup(table, idx):
    hit = idx[..., None] == jnp.arange(table.shape[0])
    return jnp.sum(jnp.where(hit, table, 0), axis=-1)


def _moe_plan(ei, rk, counts, n_tok):
    n_chunks = n_tok // MOE_TC
    cap = 2 * n_tok + N_EXPERTS * MOE_BLK
    nb_sub = cap // MOE_SUB
    nb_blk = cap // MOE_BLK
    padded = ((counts + MOE_BLK - 1) // MOE_BLK) * MOE_BLK
    pad_end = jnp.cumsum(padded)
    pad_start = pad_end - padded
    dest = _lookup(pad_start, ei) + rk

    onehot = ei.reshape(2, n_chunks, MOE_TC)[..., None] == jnp.arange(N_EXPERTS)
    cnt = jnp.sum(onehot, axis=(0, 2)).astype(jnp.int32)
    cum = jnp.concatenate([jnp.zeros((1, N_EXPERTS), jnp.int32), jnp.cumsum(cnt, axis=0)], axis=0)

    lo = pad_start[None, :] + cum[:-1]
    has = cnt > 0
    blk0 = lo // MOE_SUB
    two = has & ((lo + cnt - 1) // MOE_SUB > blk0)
    blk1 = jnp.minimum(blk0 + 1, nb_sub - 1)
    win_ok = jnp.stack([has, two], axis=-1).reshape(n_chunks, 2 * N_EXPERTS)
    win_blk = jnp.stack([blk0, blk1], axis=-1).reshape(n_chunks, 2 * N_EXPERTS)
    c_count = jnp.sum(win_ok, axis=1).astype(jnp.int32)
    c_start = (jnp.cumsum(c_count) - c_count).astype(jnp.int32)
    win_at = c_start[:, None] + jnp.cumsum(win_ok, axis=1) - win_ok
    q = jnp.arange(n_chunks * 2 * N_EXPERTS)
    hit = win_ok[None] & (win_at[None] == q[:, None, None])
    c_blocks = jnp.sum(jnp.where(hit, win_blk[None], 0), axis=(1, 2)).astype(jnp.int32)

    sb = jnp.arange(nb_sub)
    e_sb = jnp.minimum(jnp.sum(sb[:, None] * MOE_SUB >= pad_end[None, :], axis=1), N_EXPERTS - 1)
    counts_sb = _lookup(counts, e_sb)
    r0 = sb * MOE_SUB - _lookup(pad_start, e_sb)
    valid_sb = (sb * MOE_SUB < pad_end[-1]) & (r0 < counts_sb)
    r1 = jnp.minimum(r0 + MOE_SUB, counts_sb) - 1
    hit_sb = e_sb[:, None] == jnp.arange(N_EXPERTS)[None, :]
    cum_sb = jnp.sum(jnp.where(hit_sb[:, None, :], cum[None, 1:, :], 0), axis=-1)
    cmin = jnp.sum(cum_sb <= r0[:, None], axis=1)
    cmax = jnp.sum(cum_sb <= r1[:, None], axis=1)
    items = jnp.where(valid_sb, cmax - cmin + 1, 0)
    off_end = jnp.cumsum(items)
    off = off_end - items
    total = off_end[-1]
    w_max = (2 * n_tok) // MOE_SUB + N_EXPERTS + N_EXPERTS * (n_chunks - 1)
    w = jnp.arange(w_max)
    wv = w < total
    wq = jnp.minimum(w, total - 1)
    w_sb = jnp.minimum(jnp.sum(off_end[None, :] <= wq[:, None], axis=1), nb_sub - 1)
    off_w = _lookup(off, w_sb)
    w_chunk = _lookup(cmin, w_sb) + (wq - off_w)
    w_flag = wv.astype(jnp.int32) + 2 * (wq == off_w).astype(jnp.int32)

    bi = jnp.arange(nb_blk)
    e_blk = jnp.minimum(jnp.sum(bi[:, None] * MOE_BLK >= pad_end[None, :], axis=1), N_EXPERTS - 1)
    rows = jnp.where(bi * MOE_BLK < pad_end[-1],
                     _lookup(counts, e_blk) - (bi * MOE_BLK - _lookup(pad_start, e_blk)), 0)
    n_sub = jnp.clip((rows + MOE_SUB - 1) // MOE_SUB, 0, MOE_BLK // MOE_SUB)
    g_count = jnp.sum(items.reshape(nb_blk, MOE_BLK // MOE_SUB), axis=1)
    g_start = jnp.cumsum(g_count) - g_count
    return dict(dest=dest.astype(jnp.int32), cap=cap,
                w_sb=w_sb.astype(jnp.int32), w_chunk=w_chunk.astype(jnp.int32), w_flag=w_flag,
                g_start=g_start.astype(jnp.int32), g_count=g_count.astype(jnp.int32),
                e_blk=e_blk.astype(jnp.int32), n_sub=n_sub.astype(jnp.int32),
                c_blocks=c_blocks, c_count=c_count, c_start=c_start)


def _moe_gather_kernel(st_ref, cn_ref, wsb_ref, wch_ref, t_hbm, d_ref, g_ref, xg_ref, gs_ref, tbuf, sem):
    i = pl.program_id(0)
    n = cn_ref[i]
    s0 = st_ref[i]
    total = st_ref[pl.num_programs(0) - 1] + cn_ref[pl.num_programs(0) - 1]
    subs = MOE_BLK // MOE_SUB

    def chunk_copy(item):
        slot = lax.rem(item, MOE_NBUF)
        start = pl.multiple_of(wch_ref[item] * MOE_TC, MOE_TC)
        return pltpu.make_async_copy(t_hbm.at[pl.ds(start, MOE_TC), :], tbuf.at[slot], sem.at[slot])

    @pl.when(i == 0)
    def _():
        for a in range(MOE_AHEAD):
            @pl.when(a < total)
            def _():
                chunk_copy(a).start()

    xg_ref[...] = jnp.zeros(xg_ref.shape, BF16)
    gs_ref[...] = jnp.zeros(gs_ref.shape, F32)

    def arrive(item):
        chunk_copy(item).wait()

        @pl.when(item + MOE_AHEAD < total)
        def _():
            chunk_copy(item + MOE_AHEAD).start()

    def contribute(item):
        sb = wsb_ref[item]
        chunk = wch_ref[item]
        d = d_ref[chunk]
        g = g_ref[chunk]
        srow = lax.broadcasted_iota(jnp.int32, (MOE_SUB, MOE_TC), 0) + sb * MOE_SUB
        hit0 = srow == d[0:1]
        hit1 = srow == d[1:2]
        p = jnp.where(hit0 | hit1, 1.0, 0.0).astype(BF16)
        rows = _dot(p, tbuf[lax.rem(item, MOE_NBUF)]).astype(BF16)
        gate = jnp.sum(jnp.where(hit0, g[0:1], 0.0) + jnp.where(hit1, g[1:2], 0.0), axis=-1, keepdims=True)
        r0 = pl.multiple_of((sb - i * subs) * MOE_SUB, MOE_SUB)
        xg_ref[pl.ds(r0, MOE_SUB), :] += rows
        gs_ref[pl.ds(r0, MOE_SUB), :] += gate

    def pair(k, carry):
        item = s0 + 2 * k
        arrive(item)
        arrive(item + 1)
        contribute(item)
        contribute(item + 1)
        return carry

    lax.fori_loop(0, n // 2, pair, 0)

    @pl.when(lax.rem(n, 2) == 1)
    def _():
        arrive(s0 + n - 1)
        contribute(s0 + n - 1)


def _moe_gather(t, dest, gates, plan):
    n_tok, d = t.shape
    cap = plan["cap"]
    n_chunks = n_tok // MOE_TC
    by_chunk = lambda a: a.reshape(2, n_chunks, MOE_TC).transpose(1, 0, 2)
    grid_spec = pltpu.PrefetchScalarGridSpec(
        num_scalar_prefetch=4,
        grid=(cap // MOE_BLK,),
        in_specs=[
            pl.BlockSpec(memory_space=pl.ANY),
            pl.BlockSpec((n_chunks, 2, MOE_TC), lambda i, *_: (0, 0, 0)),
            pl.BlockSpec((n_chunks, 2, MOE_TC), lambda i, *_: (0, 0, 0)),
        ],
        out_specs=(pl.BlockSpec((MOE_BLK, d), lambda i, *_: (i, 0)),
                   pl.BlockSpec((MOE_BLK, 1), lambda i, *_: (i, 0))),
        scratch_shapes=[pltpu.VMEM((MOE_NBUF, MOE_TC, d), BF16), pltpu.SemaphoreType.DMA((MOE_NBUF,))],
    )
    return pl.pallas_call(
        _moe_gather_kernel,
        out_shape=(jax.ShapeDtypeStruct((cap, d), BF16), jax.ShapeDtypeStruct((cap, 1), F32)),
        grid_spec=grid_spec,
        compiler_params=_cparams(("arbitrary",)),
        name="moe_gather",
    )(plan["g_start"], plan["g_count"], plan["w_sb"], plan["w_chunk"], t, by_chunk(dest), by_chunk(gates))


def _moe_expert_kernel(eb_ref, ns_ref, x_ref, gs_ref, w1_ref, w3_ref, w2_ref, y_ref):
    n_sub = ns_ref[pl.program_id(0)]
    subs = MOE_BLK // MOE_SUB

    def mlp(rows):
        xs = x_ref[rows, :]
        acc = jnp.zeros((rows.stop - rows.start, y_ref.shape[1]), F32)
        for f in range(w1_ref.shape[1] // MOE_TF):
            cols = slice(f * MOE_TF, (f + 1) * MOE_TF)
            a = _dot(xs, w1_ref[:, cols])
            u = _dot(xs, w3_ref[:, cols])
            acc = acc + _dot((_silu(a) * u).astype(BF16), w2_ref[cols, :])
        y_ref[rows, :] = (acc * gs_ref[rows, :]).astype(BF16)

    @pl.when(n_sub == subs)
    def _():
        mlp(slice(0, MOE_BLK))

    for sub in range(subs):
        rows = slice(sub * MOE_SUB, (sub + 1) * MOE_SUB)

        @pl.when((n_sub < subs) & (sub < n_sub))
        def _():
            mlp(rows)

        @pl.when(sub >= n_sub)
        def _():
            y_ref[rows, :] = jnp.zeros((MOE_SUB, y_ref.shape[1]), BF16)


def _moe_experts(xg, gs, w1, w3, w2, plan):
    cap, d = xg.shape
    f = w1.shape[2]
    once = pl.Buffered(1)
    grid_spec = pltpu.PrefetchScalarGridSpec(
        num_scalar_prefetch=2,
        grid=(cap // MOE_BLK,),
        in_specs=[
            pl.BlockSpec((MOE_BLK, d), lambda i, eb, ns: (i, 0)),
            pl.BlockSpec((MOE_BLK, 1), lambda i, eb, ns: (i, 0)),
            pl.BlockSpec((None, d, f), lambda i, eb, ns: (eb[i], 0, 0), pipeline_mode=once),
            pl.BlockSpec((None, d, f), lambda i, eb, ns: (eb[i], 0, 0), pipeline_mode=once),
            pl.BlockSpec((None, f, d), lambda i, eb, ns: (eb[i], 0, 0), pipeline_mode=once),
        ],
        out_specs=pl.BlockSpec((MOE_BLK, d), lambda i, eb, ns: (i, 0)),
    )
    return pl.pallas_call(
        _moe_expert_kernel,
        out_shape=jax.ShapeDtypeStruct((cap, d), BF16),
        grid_spec=grid_spec,
        compiler_params=_cparams(("arbitrary",)),
        name="moe_experts",
    )(plan["e_blk"], plan["n_sub"], xg, gs, w1, w3, w2)


MOE_WIN = 2 * N_EXPERTS


def _moe_combine_kernel(st_ref, cn_ref, bl_ref, dt_ref, y_hbm, x1_ref, g_ref, o_ref, ybuf, sem, acc_ref):
    c = pl.program_id(0)
    n = cn_ref[c]
    s0 = st_ref[c]
    total = st_ref[pl.num_programs(0) - 1] + cn_ref[pl.num_programs(0) - 1]
    d = dt_ref[...]

    def block_copy(item):
        slot = lax.rem(item, MOE_NBUF)
        start = pl.multiple_of(bl_ref[item] * MOE_SUB, MOE_SUB)
        return pltpu.make_async_copy(y_hbm.at[pl.ds(start, MOE_SUB), :], ybuf.at[slot], sem.at[slot])

    @pl.when(c == 0)
    def _():
        for a in range(MOE_AHEAD):
            @pl.when(a < total)
            def _():
                block_copy(a).start()

    def arrive(item):
        block_copy(item).wait()

        @pl.when(item + MOE_AHEAD < total)
        def _():
            block_copy(item + MOE_AHEAD).start()

    def rows_of(item):
        scol = lax.broadcasted_iota(jnp.int32, (MOE_TC, MOE_SUB), 1) + bl_ref[item] * MOE_SUB
        p = jnp.where((scol == d[:, 0:1]) | (scol == d[:, 1:2]), 1.0, 0.0).astype(BF16)
        return _dot(p, ybuf[lax.rem(item, MOE_NBUF)])

    def pair(k, acc):
        item = s0 + 2 * k
        arrive(item)
        arrive(item + 1)
        return acc + rows_of(item) + rows_of(item + 1)

    acc_ref[...] = lax.fori_loop(0, n // 2, pair, jnp.zeros(o_ref.shape, F32))

    @pl.when(lax.rem(n, 2) == 1)
    def _():
        arrive(s0 + n - 1)
        acc_ref[...] += rows_of(s0 + n - 1)

    o_ref[...] = x1_ref[...] + g_ref[0] * acc_ref[...]


def _moe_combine(y, dest_t, x1, gate, plan, seq):
    n_tok, d = x1.shape
    n_chunks = n_tok // MOE_TC
    per_b = seq // MOE_TC
    grid_spec = pltpu.PrefetchScalarGridSpec(
        num_scalar_prefetch=3,
        grid=(n_chunks,),
        in_specs=[
            pl.BlockSpec((MOE_TC, 2), lambda c, *_: (c, 0)),
            pl.BlockSpec(memory_space=pl.ANY),
            pl.BlockSpec((MOE_TC, d), lambda c, *_: (c, 0)),
            pl.BlockSpec((1, 1, d), lambda c, *_: (c // per_b, 0, 0)),
        ],
        out_specs=pl.BlockSpec((MOE_TC, d), lambda c, *_: (c, 0)),
        scratch_shapes=[pltpu.VMEM((MOE_NBUF, MOE_SUB, d), BF16), pltpu.SemaphoreType.DMA((MOE_NBUF,)),
                        pltpu.VMEM((MOE_TC, d), F32)],
    )
    return pl.pallas_call(
        _moe_combine_kernel,
        out_shape=jax.ShapeDtypeStruct((n_tok, d), F32),
        grid_spec=grid_spec,
        compiler_params=_cparams(("arbitrary",)),
        name="moe_combine",
    )(plan["c_start"], plan["c_count"], plan["c_blocks"], dest_t, y, x1, gate)


def _split_mod(mod_l, batch):
    d = D_MODEL
    lat = tuple(mod_l[:batch, k * d:(k + 1) * d][:, None, :] for k in range(ADA_CHUNKS))
    ctx = tuple(jnp.broadcast_to(mod_l[batch:batch + 1, k * d:(k + 1) * d][:, None, :], (batch, 1, d))
                for k in range(ADA_CHUNKS))
    return lat, ctx


def _score_bound(q_gain, k_gain, n):
    return 1.02 * n * jnp.max(jnp.abs(q_gain)) * jnp.max(jnp.abs(k_gain))


def _block_ones(n, block):
    idx = jnp.arange(n) // block
    return (idx[:, None] == idx[None, :]).astype(BF16)


def _even_layer(x, xc, mod_l, norm1_g, norm2_g, w_in, w_out, na_q_g, na_k_g, na_rpb, diff_q_g, diff_k_g,
                lq1, lk1, lq2, lk2, subln_g, wg, wu, wd, layer_idx, need_ctx):
    b, s, d = x.shape
    (sh1, sc1, g1, sh2, sc2, g2), (csh1, csc1, cg1, csh2, csc2, cg2) = _split_mod(mod_l, b)
    qscale = HEAD_DIM ** -0.5 * LOG2E
    lam_init = 0.8 - 0.6 * math.exp(-0.3 * layer_idx)
    hg = jnp.stack([jnp.tile(na_q_g, 8) * qscale, jnp.tile(na_k_g, 8),
                    jnp.tile(diff_q_g, 8) * qscale, jnp.tile(diff_k_g, 8)]).astype(F32)
    e512 = _block_ones(512, HEAD_DIM)
    cos, sin = _rope_tables(s, HEAD_DIM, 16)
    ng1 = norm1_g[None, :]
    ng2 = norm2_g[None, :]
    w_in_b = w_in.astype(BF16)
    wo_a = w_out[:512].astype(BF16)
    wo_b = w_out[512:].astype(BF16)
    wg_b, wu_b, wd_b = wg.astype(BF16), wu.astype(BF16), wd.astype(BF16)
    lam_p = jnp.stack([lq1, lk1, lq2, lk2]).astype(F32)
    sg = subln_g[None, :].astype(F32)
    ident = lambda j: j

    tm = min(ROW_TM, s)
    tc = xc.shape[1]
    qa, ka, va, qb, kb, vb = _even_proj(x, sh1, sc1, ng1, w_in_b, e512, hg, cos, sin, use_rope=True, tm=tm)
    qca, kca, vca, qcb, kcb, vcb = _even_proj(xc, csh1, csc1, ng1, w_in_b, e512, hg, cos, sin, use_rope=False, tm=tc)
    out_a = _na_attention(qa, ka, va, kca, vca, _na_bias_table(na_rpb), _score_bound(hg[0], hg[1], HEAD_DIM))
    out_b = _flash(qb, kb, vb, kcb, vcb, mode="diff", kv_map=ident, n_chunks=4, tq=FLASH_TQ, tk=FLASH_TK,
                   extra=(lam_p, sg), lam_init=lam_init, score_bound=_score_bound(hg[2], hg[3], HEAD_DIM))
    x1, h2 = _post_attn(out_a, out_b, wo_a, wo_b, x, g1, ng2, sh2, sc2, tm=tm)
    x2 = _ffn(h2, x1, g2, wg_b, wu_b, wd_b, tm=tm)
    if not need_ctx:
        return x2, None
    oca = _flash(qca, None, None, kca, vca, mode="pair", kv_map=ident, n_chunks=4, tq=tc, tk=tc)
    ocb = _flash(qcb, None, None, kcb, vcb, mode="diff", kv_map=ident, n_chunks=4, tq=tc, tk=tc,
                 extra=(lam_p, sg), lam_init=lam_init)
    xc1, hc2 = _post_attn(oca, ocb, wo_a, wo_b, xc, cg1, ng2, csh2, csc2, tm=tc)
    xc2 = _ffn(hc2, xc1, cg2, wg_b, wu_b, wd_b, tm=tc)
    return x2, xc2


def _odd_layer(x, xc, mod_l, norm1_g, norm2_g, w_in, w_out, gqa_q_g, gqa_k_g, cq_g, w_uq, ckv_g, w_ukv,
               mla_q_g, mla_k_g, router, w1, w3, w2):
    b, s, d = x.shape
    n_tok = b * s
    (sh1, sc1, g1, sh2, sc2, g2), (csh1, csc1, _, _, _, _) = _split_mod(mod_l, b)
    ng1 = norm1_g[None, :]
    ng2 = norm2_g[None, :]

    z = lambda n: jnp.zeros((d, n), w_in.dtype)
    k0, k1, v0, v1 = (w_in[:, 512 + 64 * i:576 + 64 * i] for i in range(4))
    w_p = jnp.concatenate([w_in[:, 0:512], k0, k0, k1, k1, v0, v0, v1, v1, w_in[:, 768:1024], w_in[:, 1024:1152],
                           z(MLA_NOPE), w_in[:, 1152:1184], z(LANES - MLA_QK)], axis=1).astype(BF16)
    wuq_p = jnp.pad(w_uq.reshape(MLA_Q_RANK, MLA_HEADS, MLA_QK), ((0, 0), (0, 0), (0, LANES - MLA_QK)))
    wuq_p = wuq_p.reshape(MLA_Q_RANK, MLA_HEADS * LANES).astype(BF16)
    ukv = w_ukv.reshape(MLA_KV_RANK, MLA_HEADS, MLA_NOPE + MLA_V)
    uk = jnp.pad(ukv[:, :, :MLA_NOPE], ((0, 0), (0, 0), (0, LANES - MLA_NOPE))).reshape(MLA_KV_RANK, -1)
    uv = ukv[:, :, MLA_NOPE:].reshape(MLA_KV_RANK, -1)
    wukv_p = jnp.concatenate([uk, uv], axis=1).astype(BF16)
    e512 = _block_ones(512, HEAD_DIM)
    a = (jnp.arange(MLA_HEADS * LANES)[:, None] // LANES == jnp.arange(LANES)[None, :]).astype(BF16)
    at = a.T
    qscale = HEAD_DIM ** -0.5 * LOG2E
    mscale = MLA_QK ** -0.5 * LOG2E
    pad_row = lambda v: jnp.pad(v, (0, MLA_HEADS * LANES - v.shape[0]))
    pad_head = lambda v: jnp.tile(jnp.pad(v, (0, LANES - MLA_QK)), MLA_HEADS)
    gains = jnp.stack([pad_row(jnp.tile(gqa_q_g, 8) * qscale), pad_row(jnp.tile(gqa_k_g, 4)),
                       pad_row(cq_g), pad_row(ckv_g), pad_head(mla_q_g) * mscale, pad_head(mla_k_g),
                       jnp.zeros((MLA_HEADS * LANES,), F32), jnp.zeros((MLA_HEADS * LANES,), F32)]).astype(F32)
    cos, sin = _rope_tables(s, HEAD_DIM, 16)
    cosm, sinm = _mla_rope_tables(s)
    wo_a = w_out[:512].astype(BF16)
    wo_b = w_out[512:].astype(BF16)

    tm = min(ROW_TM, s)
    tc = xc.shape[1]
    proj = functools.partial(_odd_proj, ng=ng1, w=w_p, wuq=wuq_p, wukv=wukv_p, e512=e512, a=a, at=at, gains=gains,
                             cos=cos, sin=sin, cosm=cosm, sinm=sinm)
    qc, kc, vc, qd, kd, vd = proj(x, sh1, sc1, use_rope=True, tm=tm)
    _, kcc, vcc, _, kcd, vcd = proj(xc, csh1, csc1, use_rope=False, tm=tc)
    out_c = _flash(qc, kc, vc, kcc, vcc, mode="pair", kv_map=lambda j: j // 2, n_chunks=4, tq=FLASH_TQ, tk=FLASH_TK,
                   score_bound=_score_bound(gains[0], gains[1], HEAD_DIM))
    out_d = _flash(qd, kd, vd, kcd, vcd, mode="mla", kv_map=lambda j: j, n_chunks=4, tq=FLASH_TQ, tk=FLASH_TK,
                   score_bound=_score_bound(gains[4], gains[5], MLA_QK))
    x1, h2 = _post_attn(out_c, out_d, wo_a, wo_b, x, g1, ng2, sh2, sc2, tm=tm)

    ei, gt, rk, cnt = _router(x1, ng2, sh2, sc2, router.T.astype(F32), tm=tm)
    plan = _moe_plan(ei, rk, cnt[:, 0], n_tok)
    xg, gs = _moe_gather(h2.reshape(n_tok, d), plan["dest"], gt, plan)
    y = _moe_experts(xg, gs, w1.astype(BF16), w3.astype(BF16), w2.astype(BF16), plan)
    out = _moe_combine(y, plan["dest"].T, x1.reshape(n_tok, d), g2, plan, s)
    return out.reshape(b, s, d)


def _mod_vectors(c, c_ctx, ada_w, ada_b):
    b = c.shape[0]
    cc = jnp.zeros((MOD_ROWS, D_MODEL), F32).at[:b].set(c).at[b].set(c_ctx)
    return _modvec(cc, ada_w, ada_b[:, None, :])


def kernel(x, c, ctx, c_ctx, ada_w, ada_b, norm1_g, norm2_g, ev_w_in, ev_w_out, na_q_g, na_k_g, na_rpb,
           diff_q_g, diff_k_g, diff_lq1, diff_lk1, diff_lq2, diff_lk2, diff_subln_g,
           ffn_w_gate, ffn_w_up, ffn_w_down, od_w_in, od_w_out, gqa_q_g, gqa_k_g, mla_cq_g, mla_w_uq,
           mla_ckv_g, mla_w_ukv, mla_q_g, mla_k_g, moe_router, moe_w1, moe_w3, moe_w2):
    mod = _mod_vectors(c, c_ctx, ada_w, ada_b)
    x, xc = _even_layer(x, ctx, mod[0], norm1_g[0], norm2_g[0], ev_w_in[0], ev_w_out[0], na_q_g[0], na_k_g[0],
                        na_rpb[0], diff_q_g[0], diff_k_g[0], diff_lq1[0], diff_lk1[0], diff_lq2[0], diff_lk2[0],
                        diff_subln_g[0], ffn_w_gate[0], ffn_w_up[0], ffn_w_down[0], 0, True)
    return _odd_layer(x, xc, mod[1], norm1_g[1], norm2_g[1], od_w_in[0], od_w_out[0], gqa_q_g[0], gqa_k_g[0],
                      mla_cq_g[0], mla_w_uq[0], mla_ckv_g[0], mla_w_ukv[0], mla_q_g[0], mla_k_g[0],
                      moe_router[0], moe_w1[0], moe_w3[0], moe_w2[0])
```

```python
import functools
import math

import jax
import jax.numpy as jnp
from jax import lax
from jax.experimental import pallas as pl
from jax.experimental.pallas import tpu as pltpu

F32 = jnp.float32
BF16 = jnp.bfloat16
HIGHEST = lax.Precision.HIGHEST

D_MODEL = 1024
GRID_W = 64
HEAD_DIM = 64
ROPE_THETA = 10000.0
NORM_EPS = 1e-6
NEG_INF = -1e30
ADA_CHUNKS = 6
LOG2E = 1.4426950408889634

NA_HEADS = 8
NA_WIN_H = 8
NA_WIN_W = 16
DIFF_HEADS = 4
DIFF_V_DIM = 2 * HEAD_DIM
GQA_Q_HEADS = 8
GQA_KV_HEADS = 2
MLA_HEADS = 8
MLA_NOPE = 64
MLA_ROPE = 32
MLA_QK = MLA_NOPE + MLA_ROPE
MLA_V = 64
MLA_Q_RANK = 256
MLA_KV_RANK = 128
D_FF = 2816
N_EXPERTS = 8
D_FF_EXPERT = 3584

LANES = 128
MXU_DIM = 256
VMEM_LIMIT = 56 * 1024 * 1024
MOD_ROWS = 16
MOE_TF = 512
SUM_ROWS = 16
ROW_TM = 1024
FLASH_TQ = 512
FLASH_TK = 512
EXP2_SAFE_RANGE = 64.0


def _cparams(sem):
    return pltpu.CompilerParams(dimension_semantics=sem, vmem_limit_bytes=VMEM_LIMIT)


def _dot(a, b):
    return jnp.dot(a, b, preferred_element_type=F32)


def _dot_nt(a, b):
    return lax.dot_general(a, b, (((1,), (1,)), ((), ())), preferred_element_type=F32)


def _silu(x):
    return x * (1.0 / (1.0 + jnp.exp(-x)))


def _modvec_kernel(c_ref, w_ref, b_ref, o_ref):
    s = _silu(c_ref[...])
    o_ref[...] = jnp.dot(s, w_ref[...], preferred_element_type=F32, precision=HIGHEST) + b_ref[...]


def _modvec(cc, ada_w, ada_b):
    n_layers, d, n = ada_w.shape
    tn = 768
    return pl.pallas_call(
        _modvec_kernel,
        out_shape=jax.ShapeDtypeStruct((n_layers, MOD_ROWS, n), F32),
        grid=(n_layers, n // tn),
        in_specs=[
            pl.BlockSpec((MOD_ROWS, d), lambda l, j: (0, 0)),
            pl.BlockSpec((None, d, tn), lambda l, j: (l, 0, j)),
            pl.BlockSpec((None, 1, tn), lambda l, j: (l, 0, j)),
        ],
        out_specs=pl.BlockSpec((None, MOD_ROWS, tn), lambda l, j: (l, 0, j)),
        compiler_params=_cparams(("arbitrary", "arbitrary")),
        name="modvec",
    )(cc, ada_w, ada_b)


def _norm_mod(x, g, shift, scale):
    ms = jnp.mean(x * x, axis=-1, keepdims=True)
    return (x * lax.rsqrt(ms + NORM_EPS) * g) * (1.0 + scale) + shift


def _group_norm(t, e, g, inv_n):
    ss = _dot((t * t).astype(BF16), e)
    return t * lax.rsqrt(ss * inv_n + NORM_EPS) * g


def _rope_chunks(t, cos, sin, lane_hi, shift):
    outs = []
    for c in range(t.shape[1] // LANES):
        tc = t[:, c * LANES:(c + 1) * LANES]
        up = pltpu.roll(tc, LANES - shift, 1)
        dn = pltpu.roll(tc, shift, 1)
        outs.append(tc * cos + jnp.where(lane_hi, dn, up) * sin)
    return outs[0] if len(outs) == 1 else jnp.concatenate(outs, axis=1)


def _rope_tables(seq, unit, pair_shift):
    pos = jnp.arange(seq)
    rows, cols = pos // GRID_W, pos % GRID_W
    quarter = unit // 4
    assert quarter == pair_shift
    freqs = ROPE_THETA ** (-jnp.arange(quarter, dtype=F32) / quarter)
    lane = jnp.arange(LANES)
    u = lane % unit
    use_col = (u // (unit // 2)) == 1
    fi = u % quarter
    p = jnp.where(use_col[None, :], cols[:, None], rows[:, None]).astype(F32)
    ang = p * freqs[fi][None, :]
    second = ((u % (unit // 2)) // quarter) == 1
    cos = jnp.cos(ang)
    sin = jnp.where(second[None, :], jnp.sin(ang), -jnp.sin(ang))
    return cos, sin


def _even_proj_kernel(x_ref, sh_ref, sc_ref, ng_ref, w_ref, e_ref, hg_ref, cos_ref, sin_ref,
                      qa_ref, ka_ref, va_ref, qb_ref, kb_ref, vb_ref, *, use_rope):
    h = _norm_mod(x_ref[0], ng_ref[...], sh_ref[0], sc_ref[0])
    p = _dot(h.astype(BF16), w_ref[...])
    hg = hg_ref[...]
    e = e_ref[...]
    inv_n = 1.0 / HEAD_DIM
    qa = _group_norm(p[:, 0:512], e, hg[0:1], inv_n)
    ka = _group_norm(p[:, 512:1024], e, hg[1:2], inv_n)
    qb = _group_norm(p[:, 1536:2048], e, hg[2:3], inv_n)
    kb = _group_norm(p[:, 2048:2560], e, hg[3:4], inv_n)
    if use_rope:
        lane = lax.broadcasted_iota(jnp.int32, (1, LANES), 1)
        lane_hi = (lane & 16) != 0
        cos, sin = cos_ref[...], sin_ref[...]
        qb = _rope_chunks(qb, cos, sin, lane_hi, 16)
        kb = _rope_chunks(kb, cos, sin, lane_hi, 16)
    qa_ref[0] = qa.astype(BF16)
    ka_ref[0] = ka.astype(BF16)
    va_ref[0] = p[:, 1024:1536].astype(BF16)
    qb_ref[0] = qb.astype(BF16)
    kb_ref[0] = kb.astype(BF16)
    vb_ref[0] = p[:, 2560:3072].astype(BF16)


def _even_proj(x, shift, scale, ng, w, e512, hg, cos, sin, *, use_rope, tm):
    b, s, d = x.shape
    n = w.shape[1]
    row = lambda bi, i: (bi, i, 0)
    vec = lambda bi, i: (bi, 0, 0)
    const = lambda bi, i: (0, 0)
    out = jax.ShapeDtypeStruct((b, s, 512), BF16)
    return pl.pallas_call(
        functools.partial(_even_proj_kernel, use_rope=use_rope),
        out_shape=(out,) * 6,
        grid=(b, s // tm),
        in_specs=[
            pl.BlockSpec((1, tm, d), row),
            pl.BlockSpec((1, 1, d), vec),
            pl.BlockSpec((1, 1, d), vec),
            pl.BlockSpec((1, d), const),
            pl.BlockSpec((d, n), const),
            pl.BlockSpec((512, 512), const),
            pl.BlockSpec((4, 512), const),
            pl.BlockSpec((tm, LANES), lambda bi, i: (i, 0)),
            pl.BlockSpec((tm, LANES), lambda bi, i: (i, 0)),
        ],
        out_specs=(pl.BlockSpec((1, tm, 512), row),) * 6,
        compiler_params=_cparams(("parallel", "parallel")),
        name="even_proj",
    )(x, shift, scale, ng, w, e512, hg, cos, sin)


def _pair_stack(q):
    lane = lax.broadcasted_iota(jnp.int32, q.shape, 1)
    zero = jnp.zeros_like(q)
    return jnp.concatenate([jnp.where(lane < HEAD_DIM, q, zero), jnp.where(lane >= HEAD_DIM, q, zero)], axis=0)


def _pair_merge(o, tq):
    lane = lax.broadcasted_iota(jnp.int32, (tq, LANES), 1)
    return jnp.where(lane < HEAD_DIM, o[:tq], o[tq:])


def _with_ones(v):
    return jnp.concatenate([v, jnp.ones(v.shape, BF16)], axis=1)


def _na_kernel(q_ref, k_ref, v_ref, kc_ref, vc_ref, bias_ref, o_ref, *, rows_per_step, n_rows, direct):
    rb = pl.program_id(2)
    kc = kc_ref[0]
    vc = vc_ref[0]
    n_lat = NA_WIN_H * GRID_W
    if direct:
        nq = rows_per_step * GRID_W
        qs_all = _pair_stack(q_ref[0])
        o_ctx = _dot(jnp.exp2(_dot_nt(qs_all, kc)).astype(BF16), _with_ones(vc))
        for i in range(rows_per_step):
            r = rb * rows_per_step + i
            r0 = jnp.clip(r - NA_WIN_H // 2, 0, n_rows - NA_WIN_H)
            start = pl.multiple_of(r0 * GRID_W, GRID_W)
            kw = k_ref[0, pl.ds(start, n_lat), :]
            vw = v_ref[0, pl.ds(start, n_lat), :]
            lo = slice(i * GRID_W, (i + 1) * GRID_W)
            hi = slice(nq + i * GRID_W, nq + (i + 1) * GRID_W)
            qs = jnp.concatenate([qs_all[lo], qs_all[hi]], axis=0)
            p = jnp.exp2(_dot_nt(qs, kw) + bias_ref[r - r0]).astype(BF16)
            tot = _dot(p, _with_ones(vw)) + jnp.concatenate([o_ctx[lo], o_ctx[hi]], axis=0)
            o = tot[:, :LANES] / tot[:, LANES:]
            o_ref[0, lo, :] = _pair_merge(o, GRID_W).astype(BF16)
        return
    for i in range(rows_per_step):
        r = rb * rows_per_step + i
        r0 = jnp.clip(r - NA_WIN_H // 2, 0, n_rows - NA_WIN_H)
        start = pl.multiple_of(r0 * GRID_W, GRID_W)
        kw = k_ref[0, pl.ds(start, n_lat), :]
        vw = v_ref[0, pl.ds(start, n_lat), :]
        qs = _pair_stack(q_ref[0, i * GRID_W:(i + 1) * GRID_W, :])
        s_lat = _dot_nt(qs, kw) + bias_ref[r - r0]
        s_ctx = _dot_nt(qs, kc)
        m = jnp.maximum(jnp.max(s_lat, axis=-1, keepdims=True), jnp.max(s_ctx, axis=-1, keepdims=True))
        p_lat = jnp.exp2(s_lat - m)
        p_ctx = jnp.exp2(s_ctx - m)
        l = jnp.sum(p_lat, axis=-1, keepdims=True) + jnp.sum(p_ctx, axis=-1, keepdims=True)
        o = _dot(p_lat.astype(BF16), vw) + _dot(p_ctx.astype(BF16), vc)
        o = o / l
        o_ref[0, i * GRID_W:(i + 1) * GRID_W, :] = _pair_merge(o, GRID_W).astype(BF16)


def _na_attention(q, k, v, kc, vc, bias, score_bound, *, rows_per_step=8):
    bound = score_bound + jnp.max(jnp.where(bias > 0.5 * NEG_INF, jnp.abs(bias), 0.0))
    return lax.cond(bound <= EXP2_SAFE_RANGE,
                    lambda: _na_call(q, k, v, kc, vc, bias, rows_per_step=rows_per_step, direct=True),
                    lambda: _na_call(q, k, v, kc, vc, bias, rows_per_step=rows_per_step, direct=False))


def _na_call(q, k, v, kc, vc, bias, *, rows_per_step, direct):
    b, s, _ = q.shape
    c = kc.shape[1]
    n_rows = s // GRID_W
    tq = rows_per_step * GRID_W
    return pl.pallas_call(
        functools.partial(_na_kernel, rows_per_step=rows_per_step, n_rows=n_rows, direct=direct),
        out_shape=jax.ShapeDtypeStruct((b, s, 512), BF16),
        grid=(b, 4, n_rows // rows_per_step),
        in_specs=[
            pl.BlockSpec((1, tq, LANES), lambda bi, j, i: (bi, i, j)),
            pl.BlockSpec((1, s, LANES), lambda bi, j, i: (bi, 0, j)),
            pl.BlockSpec((1, s, LANES), lambda bi, j, i: (bi, 0, j)),
            pl.BlockSpec((1, c, LANES), lambda bi, j, i: (bi, 0, j)),
            pl.BlockSpec((1, c, LANES), lambda bi, j, i: (bi, 0, j)),
            pl.BlockSpec((NA_WIN_H, None, LANES, NA_WIN_H * GRID_W), lambda bi, j, i: (0, j, 0, 0)),
        ],
        out_specs=pl.BlockSpec((1, tq, LANES), lambda bi, j, i: (bi, i, j)),
        compiler_params=_cparams(("parallel", "parallel", "parallel")),
        name="na_attention" + ("" if direct else "_online"),
    )(q, k, v, kc, vc, bias)


def _na_bias_table(rpb):
    w = GRID_W
    col = jnp.arange(w)
    c0 = jnp.clip(col - NA_WIN_W // 2, 0, w - NA_WIN_W)
    col_in = (col[None, :] >= c0[:, None]) & (col[None, :] < c0[:, None] + NA_WIN_W)
    left = (w - 1) - (NA_WIN_W - 1)
    ext = jnp.pad(rpb, ((0, 0), (0, 0), (left, 2 * w - left - (2 * NA_WIN_W - 1))), mode="edge")
    h, nr, _ = rpb.shape
    flat = jnp.broadcast_to(ext[:, :, None, :], (h, nr, w, 2 * w)).reshape(h, nr, w * 2 * w)
    toep = flat[:, :, :w * (2 * w - 1)].reshape(h, nr, w, 2 * w - 1)[:, :, :, w - 1:]
    toep = jnp.where(col_in[None, None], toep * LOG2E, NEG_INF)
    variants = []
    for v in range(NA_WIN_H):
        tv = toep[:, NA_WIN_H - 1 - v:2 * NA_WIN_H - 1 - v]
        variants.append(tv.transpose(0, 2, 1, 3).reshape(NA_HEADS // 2, 2 * w, NA_WIN_H * w))
    return jnp.stack(variants, axis=0).astype(F32)


def _flash_kernel(*refs, mode, online, tq, tk, s_len, c_len, lam_init):
    if mode == "diff":
        q_ref, k_ref, v_ref, kc_ref, vc_ref, lam_ref, sg_ref, o_ref = refs
    else:
        q_ref, k_ref, v_ref, kc_ref, vc_ref, o_ref = refs

    q = q_ref[0]
    if mode == "mla":
        q_parts = (q[:, :LANES], q[:, LANES:])
    else:
        qs = _pair_stack(q)

    def scores(kt):
        if mode == "mla":
            return jnp.concatenate([_dot_nt(q_parts[0], kt[:, :LANES]), _dot_nt(q_parts[1], kt[:, LANES:])], axis=0)
        return _dot_nt(qs, kt)

    def step(kt, vt, carry):
        m, l, acc = carry
        s = scores(kt)
        m_new = jnp.maximum(m, jnp.max(s, axis=-1, keepdims=True))
        alpha = jnp.exp2(m - m_new)
        p = jnp.exp2(s - m_new)
        l = alpha * l + jnp.sum(p, axis=-1, keepdims=True)
        acc = alpha * acc + _dot(p.astype(BF16), vt)
        return m_new, l, acc

    def body(t, carry):
        start = pl.multiple_of(t * tk, tk)
        return step(k_ref[0, pl.ds(start, tk), :], v_ref[0, pl.ds(start, tk), :], carry)

    def scores_t(kt):
        if mode == "mla":
            return jnp.concatenate([_dot_nt(kt[:, :LANES], q_parts[0]), _dot_nt(kt[:, LANES:], q_parts[1])], axis=1)
        return _dot_nt(kt, qs)

    def direct(kt, vtt, acc):
        p = jnp.exp2(scores_t(kt)).astype(BF16)
        v1 = jnp.concatenate([vtt, jnp.ones((SUM_ROWS, vtt.shape[1]), BF16)], axis=0)
        return acc + _dot(v1, p)

    if online:
        carry = (jnp.full((2 * tq, 1), -jnp.inf, F32), jnp.zeros((2 * tq, 1), F32),
                 jnp.zeros((2 * tq, LANES), F32))
        if s_len:
            carry = lax.fori_loop(0, s_len // tk, body, carry)
        if c_len:
            carry = step(kc_ref[0], vc_ref[0], carry)
        _, l, acc = carry
        o = acc / l
    else:
        acc = jnp.zeros((LANES + SUM_ROWS, 2 * tq), F32)
        for t in range(s_len // tk):
            acc = direct(k_ref[0, t * tk:(t + 1) * tk, :], v_ref[0, :, t * tk:(t + 1) * tk], acc)
        if c_len:
            acc = direct(kc_ref[0], vc_ref[0], acc)
        o_t = acc[:LANES] / acc[LANES:LANES + 1]
    if mode == "diff":
        lp = lam_ref[...]
        lam = (jnp.exp(jnp.sum(lp[0:1] * lp[1:2], axis=-1, keepdims=True))
               - jnp.exp(jnp.sum(lp[2:3] * lp[3:4], axis=-1, keepdims=True)) + lam_init)
        if online:
            d = o[:tq] - lam * o[tq:]
            dn = d * lax.rsqrt(jnp.mean(d * d, axis=-1, keepdims=True) + NORM_EPS)
        else:
            d_t = o_t[:, :tq] - lam * o_t[:, tq:]
            dn = (d_t * lax.rsqrt(jnp.mean(d_t * d_t, axis=0, keepdims=True) + NORM_EPS)).T
        o_ref[0] = (dn * sg_ref[...] * (1.0 - lam_init)).astype(BF16)
    elif online:
        o_ref[0] = _pair_merge(o, tq).astype(BF16)
    else:
        row = lax.broadcasted_iota(jnp.int32, (LANES, tq), 0)
        o_ref[0] = jnp.where(row < HEAD_DIM, o_t[:, :tq], o_t[:, tq:]).T.astype(BF16)


def _flash(q, k, v, kc, vc, *, score_bound=None, **kw):
    if score_bound is None:
        return _flash_call(q, k, v, kc, vc, online=True, **kw)
    return lax.cond(score_bound <= EXP2_SAFE_RANGE,
                    lambda: _flash_call(q, k, jnp.swapaxes(v, 1, 2), kc, jnp.swapaxes(vc, 1, 2), online=False, **kw),
                    lambda: _flash_call(q, k, v, kc, vc, online=True, **kw))


def _flash_call(q, k, v, kc, vc, *, mode, online, kv_map, n_chunks, tq, tk, extra=(), lam_init=0.0):
    b, s, _ = q.shape
    qw = 2 * LANES if mode == "mla" else LANES
    c_len = kc.shape[1]
    if k is None:
        k, v, s_len = kc, vc, 0
    else:
        s_len = k.shape[1]
    kk = k.shape[1]
    if online:
        v_spec = lambda n: pl.BlockSpec((1, n, LANES), lambda bi, j, i: (bi, 0, kv_map(j)))
    else:
        v_spec = lambda n: pl.BlockSpec((1, LANES, n), lambda bi, j, i: (bi, kv_map(j), 0))
    in_specs = [
        pl.BlockSpec((1, tq, qw), lambda bi, j, i: (bi, i, j)),
        pl.BlockSpec((1, kk, qw), lambda bi, j, i: (bi, 0, kv_map(j) if mode != "mla" else j)),
        v_spec(kk),
        pl.BlockSpec((1, c_len, qw), lambda bi, j, i: (bi, 0, kv_map(j) if mode != "mla" else j)),
        v_spec(c_len),
    ]
    for e in extra:
        in_specs.append(pl.BlockSpec(e.shape, lambda bi, j, i: (0, 0)))
    return pl.pallas_call(
        functools.partial(_flash_kernel, mode=mode, online=online, tq=tq, tk=tk, s_len=s_len, c_len=c_len,
                          lam_init=lam_init),
        out_shape=jax.ShapeDtypeStruct((b, s, n_chunks * LANES), BF16),
        grid=(b, n_chunks, s // tq),
        in_specs=in_specs,
        out_specs=pl.BlockSpec((1, tq, LANES), lambda bi, j, i: (bi, i, j)),
        compiler_params=_cparams(("parallel", "parallel", "parallel")),
        name="flash_" + mode + ("_online" if online else ""),
    )(q, k, v, kc, vc, *extra)


def _post_attn_kernel(oa_ref, ob_ref, wa_ref, wb_ref, x_ref, g_ref, ng_ref, sh_ref, sc_ref, x1_ref, h2_ref):
    y = _dot(oa_ref[0], wa_ref[...]) + _dot(ob_ref[0], wb_ref[...])
    x1 = x_ref[0] + g_ref[0] * y
    x1_ref[0] = x1
    h2_ref[0] = _norm_mod(x1, ng_ref[...], sh_ref[0], sc_ref[0]).astype(BF16)


def _post_attn(oa, ob, wa, wb, x, gate, ng, shift, scale, *, tm):
    b, s, d = x.shape
    row = lambda bi, i: (bi, i, 0)
    vec = lambda bi, i: (bi, 0, 0)
    const = lambda bi, i: (0, 0)
    return pl.pallas_call(
        _post_attn_kernel,
        out_shape=(jax.ShapeDtypeStruct((b, s, d), F32), jax.ShapeDtypeStruct((b, s, d), BF16)),
        grid=(b, s // tm),
        in_specs=[
            pl.BlockSpec((1, tm, 512), row),
            pl.BlockSpec((1, tm, 512), row),
            pl.BlockSpec((512, d), const),
            pl.BlockSpec((512, d), const),
            pl.BlockSpec((1, tm, d), row),
            pl.BlockSpec((1, 1, d), vec),
            pl.BlockSpec((1, d), const),
            pl.BlockSpec((1, 1, d), vec),
            pl.BlockSpec((1, 1, d), vec),
        ],
        out_specs=(pl.BlockSpec((1, tm, d), row), pl.BlockSpec((1, tm, d), row)),
        compiler_params=_cparams(("parallel", "parallel")),
        name="post_attn",
    )(oa, ob, wa, wb, x, gate, ng, shift, scale)


def _ffn_kernel(h_ref, x1_ref, g_ref, wg_ref, wu_ref, wd_ref, o_ref, *, tf):
    h = h_ref[0]
    acc = jnp.zeros(o_ref.shape[1:], F32)
    for f in range(wg_ref.shape[1] // tf):
        sl = slice(f * tf, (f + 1) * tf)
        a = _dot(h, wg_ref[:, sl])
        u = _dot(h, wu_ref[:, sl])
        acc = acc + _dot((_silu(a) * u).astype(BF16), wd_ref[sl, :])
    o_ref[0] = x1_ref[0] + g_ref[0] * acc


def _ffn(h2, x1, gate, wg, wu, wd, *, tm, tf=256):
    b, s, d = x1.shape
    f = wg.shape[1]
    row = lambda bi, i: (bi, i, 0)
    vec = lambda bi, i: (bi, 0, 0)
    const = lambda bi, i: (0, 0)
    return pl.pallas_call(
        functools.partial(_ffn_kernel, tf=tf),
        out_shape=jax.ShapeDtypeStruct((b, s, d), F32),
        grid=(b, s // tm),
        in_specs=[
            pl.BlockSpec((1, tm, d), row),
            pl.BlockSpec((1, tm, d), row),
            pl.BlockSpec((1, 1, d), vec),
            pl.BlockSpec((d, f), const),
            pl.BlockSpec((d, f), const),
            pl.BlockSpec((f, d), const),
        ],
        out_specs=pl.BlockSpec((1, tm, d), row),
        compiler_params=_cparams(("parallel", "parallel")),
        name="ffn",
    )(h2, x1, gate, wg, wu, wd)


def _rms(t, g):
    return t * lax.rsqrt(jnp.mean(t * t, axis=-1, keepdims=True) + NORM_EPS) * g


def _head_norm_128(t, a_ref, at_ref, g, inv_n):
    ss = _dot((t * t).astype(BF16), a_ref[...])
    r = lax.rsqrt(ss * inv_n + NORM_EPS)
    r_hi = r.astype(BF16)
    r_lo = (r - r_hi.astype(F32)).astype(BF16)
    rb = _dot(r_hi, at_ref[...]) + _dot(r_lo, at_ref[...])
    return t * rb * g


def _odd_proj_kernel(x_ref, sh_ref, sc_ref, ng_ref, w_ref, wuq_ref, wukv_ref, e_ref, a_ref, at_ref, g_ref,
                     cos_ref, sin_ref, cosm_ref, sinm_ref,
                     qc_ref, kc_ref, vc_ref, qd_ref, kd_ref, vd_ref, *, use_rope):
    h = _norm_mod(x_ref[0], ng_ref[...], sh_ref[0], sc_ref[0])
    p = _dot(h.astype(BF16), w_ref[...])
    g = g_ref[...]
    e = e_ref[...]
    inv_n = 1.0 / HEAD_DIM
    qc = _group_norm(p[:, 0:512], e, g[0:1, 0:512], inv_n)
    kc = _group_norm(p[:, 512:768], e[0:256, 0:256], g[1:2, 0:256], inv_n)
    cq = _rms(p[:, 1024:1280], g[2:3, 0:256])
    ckv = _rms(p[:, 1280:1408], g[3:4, 0:128])
    krc = p[:, 1408:1536]
    qd = _dot(cq.astype(BF16), wuq_ref[...])
    kvv = _dot(ckv.astype(BF16), wukv_ref[...])
    kd = kvv[:, 0:1024] + jnp.concatenate([krc] * MLA_HEADS, axis=1)
    qd = _head_norm_128(qd, a_ref, at_ref, g[4:5], 1.0 / MLA_QK)
    kd = _head_norm_128(kd, a_ref, at_ref, g[5:6], 1.0 / MLA_QK)
    if use_rope:
        lane = lax.broadcasted_iota(jnp.int32, (1, LANES), 1)
        cos, sin = cos_ref[...], sin_ref[...]
        qc = _rope_chunks(qc, cos, sin, (lane & 16) != 0, 16)
        kc = _rope_chunks(kc, cos, sin, (lane & 16) != 0, 16)
        cosm, sinm = cosm_ref[...], sinm_ref[...]
        qd = _rope_chunks(qd, cosm, sinm, (lane & 8) != 0, 8)
        kd = _rope_chunks(kd, cosm, sinm, (lane & 8) != 0, 8)
    qc_ref[0] = qc.astype(BF16)
    kc_ref[0] = kc.astype(BF16)
    vc_ref[0] = p[:, 768:1024].astype(BF16)
    qd_ref[0] = qd.astype(BF16)
    kd_ref[0] = kd.astype(BF16)
    vd_ref[0] = kvv[:, 1024:1536].astype(BF16)


def _odd_proj(x, shift, scale, ng, w, wuq, wukv, e512, a, at, gains, cos, sin, cosm, sinm, *, use_rope, tm):
    b, s, d = x.shape
    row = lambda bi, i: (bi, i, 0)
    vec = lambda bi, i: (bi, 0, 0)
    const = lambda bi, i: (0, 0)
    tab = lambda bi, i: (i, 0)
    widths = (512, 256, 256, 1024, 1024, 512)
    return pl.pallas_call(
        functools.partial(_odd_proj_kernel, use_rope=use_rope),
        out_shape=tuple(jax.ShapeDtypeStruct((b, s, n), BF16) for n in widths),
        grid=(b, s // tm),
        in_specs=[
            pl.BlockSpec((1, tm, d), row),
            pl.BlockSpec((1, 1, d), vec),
            pl.BlockSpec((1, 1, d), vec),
            pl.BlockSpec((1, d), const),
            pl.BlockSpec(w.shape, const),
            pl.BlockSpec(wuq.shape, const),
            pl.BlockSpec(wukv.shape, const),
            pl.BlockSpec(e512.shape, const),
            pl.BlockSpec(a.shape, const),
            pl.BlockSpec(at.shape, const),
            pl.BlockSpec(gains.shape, const),
            pl.BlockSpec((tm, LANES), tab),
            pl.BlockSpec((tm, LANES), tab),
            pl.BlockSpec((tm, LANES), tab),
            pl.BlockSpec((tm, LANES), tab),
        ],
        out_specs=tuple(pl.BlockSpec((1, tm, n), row) for n in widths),
        compiler_params=_cparams(("parallel", "parallel")),
        name="odd_proj",
    )(x, shift, scale, ng, w, wuq, wukv, e512, a, at, gains, cos, sin, cosm, sinm)


def _mla_rope_tables(seq):
    cos, sin = _rope_tables(seq, MLA_ROPE, MLA_ROPE // 4)
    lane = jnp.arange(LANES)
    on = (lane >= MLA_NOPE) & (lane < MLA_QK)
    return jnp.where(on[None, :], cos, 1.0), jnp.where(on[None, :], sin, 0.0)


def _router_kernel(x1_ref, ng_ref, sh_ref, sc_ref, rt_ref, u_ref, ei_ref, gt_ref, rk_ref, cnt_ref, carry_ref):
    first = (pl.program_id(0) == 0) & (pl.program_id(1) == 0)

    @pl.when(first)
    def _():
        carry_ref[...] = jnp.zeros_like(carry_ref)

    h = _norm_mod(x1_ref[0], ng_ref[...], sh_ref[0], sc_ref[0])
    logits = lax.dot_general(rt_ref[...], h, (((1,), (1,)), ((), ())),
                             preferred_element_type=F32, precision=HIGHEST)
    eidx = lax.broadcasted_iota(jnp.int32, logits.shape, 0)
    m1 = jnp.max(logits, axis=0, keepdims=True)
    i1 = jnp.min(jnp.where(logits == m1, eidx, N_EXPERTS), axis=0, keepdims=True)
    rest = jnp.where(eidx == i1, -jnp.inf, logits)
    m2 = jnp.max(rest, axis=0, keepdims=True)
    i2 = jnp.min(jnp.where(rest == m2, eidx, N_EXPERTS), axis=0, keepdims=True)
    e2 = jnp.exp(m2 - m1)
    g1 = 1.0 / (1.0 + e2)
    sel1 = eidx == i1
    sel2 = eidx == i2
    onehot = jnp.where(sel1 | sel2, 1.0, 0.0)
    before = _dot(onehot.astype(BF16), u_ref[...]) + carry_ref[:, 0:1]
    r1 = jnp.sum(jnp.where(sel1, before, 0.0), axis=0, keepdims=True)
    r2 = jnp.sum(jnp.where(sel2, before, 0.0), axis=0, keepdims=True)
    ei_ref[...] = jnp.concatenate([i1, i2], axis=0)
    gt_ref[...] = jnp.concatenate([g1, e2 * g1], axis=0)
    rk_ref[...] = jnp.concatenate([r1, r2], axis=0).astype(jnp.int32)
    total = carry_ref[...] + jnp.sum(onehot, axis=1, keepdims=True)
    carry_ref[...] = total
    cnt_ref[...] = total.astype(jnp.int32)


def _router(x1, ng, shift, scale, router_t, *, tm):
    b, s, d = x1.shape
    n = b * s
    nt = s // tm
    tri = (jnp.arange(tm)[:, None] < jnp.arange(tm)[None, :]).astype(BF16)
    flat = lambda bi, i: (0, bi * nt + i)
    return pl.pallas_call(
        _router_kernel,
        out_shape=(jax.ShapeDtypeStruct((2, n), jnp.int32), jax.ShapeDtypeStruct((2, n), F32),
                   jax.ShapeDtypeStruct((2, n), jnp.int32), jax.ShapeDtypeStruct((N_EXPERTS, LANES), jnp.int32)),
        grid=(b, nt),
        in_specs=[
            pl.BlockSpec((1, tm, d), lambda bi, i: (bi, i, 0)),
            pl.BlockSpec((1, d), lambda bi, i: (0, 0)),
            pl.BlockSpec((1, 1, d), lambda bi, i: (bi, 0, 0)),
            pl.BlockSpec((1, 1, d), lambda bi, i: (bi, 0, 0)),
            pl.BlockSpec((N_EXPERTS, d), lambda bi, i: (0, 0)),
            pl.BlockSpec((tm, tm), lambda bi, i: (0, 0)),
        ],
        out_specs=(pl.BlockSpec((2, tm), flat), pl.BlockSpec((2, tm), flat), pl.BlockSpec((2, tm), flat),
                   pl.BlockSpec((N_EXPERTS, LANES), lambda bi, i: (0, 0))),
        scratch_shapes=[pltpu.VMEM((N_EXPERTS, LANES), F32)],
        compiler_params=_cparams(("arbitrary", "arbitrary")),
        name="router",
    )(x1, ng, shift, scale, router_t, tri)


MOE_SUB = 256
MOE_BLK = 1024
MOE_TC = 256
MOE_NBUF = 8
MOE_AHEAD = MOE_NBUF - 2


def _lookup(table, idx):
    hit = idx[..., None] == jnp.arange(table.shape[0])
    return jnp.sum(jnp.where(hit, table, 0), axis=-1)


def _moe_plan(ei, rk, counts, n_tok):
    n_chunks = n_tok // MOE_TC
    cap = 2 * n_tok + N_EXPERTS * MOE_BLK
    nb_sub = cap // MOE_SUB
    nb_blk = cap // MOE_BLK
    padded = ((counts + MOE_BLK - 1) // MOE_BLK) * MOE_BLK
    pad_end = jnp.cumsum(padded)
    pad_start = pad_end - padded
    dest = _lookup(pad_start, ei) + rk

    onehot = ei.reshape(2, n_chunks, MOE_TC)[..., None] == jnp.arange(N_EXPERTS)
    cnt = jnp.sum(onehot, axis=(0, 2)).astype(jnp.int32)
    cum = jnp.concatenate([jnp.zeros((1, N_EXPERTS), jnp.int32), jnp.cumsum(cnt, axis=0)], axis=0)

    lo = pad_start[None, :] + cum[:-1]
    has = cnt > 0
    blk0 = lo // MOE_SUB
    two = has & ((lo + cnt - 1) // MOE_SUB > blk0)
    blk1 = jnp.minimum(blk0 + 1, nb_sub - 1)
    win_ok = jnp.stack([has, two], axis=-1).reshape(n_chunks, 2 * N_EXPERTS)
    win_blk = jnp.stack([blk0, blk1], axis=-1).reshape(n_chunks, 2 * N_EXPERTS)
    c_count = jnp.sum(win_ok, axis=1).astype(jnp.int32)
    c_start = (jnp.cumsum(c_count) - c_count).astype(jnp.int32)
    win_at = c_start[:, None] + jnp.cumsum(win_ok, axis=1) - win_ok
    q = jnp.arange(n_chunks * 2 * N_EXPERTS)
    hit = win_ok[None] & (win_at[None] == q[:, None, None])
    c_blocks = jnp.sum(jnp.where(hit, win_blk[None], 0), axis=(1, 2)).astype(jnp.int32)

    sb = jnp.arange(nb_sub)
    e_sb = jnp.minimum(jnp.sum(sb[:, None] * MOE_SUB >= pad_end[None, :], axis=1), N_EXPERTS - 1)
    counts_sb = _lookup(counts, e_sb)
    r0 = sb * MOE_SUB - _lookup(pad_start, e_sb)
    valid_sb = (sb * MOE_SUB < pad_end[-1]) & (r0 < counts_sb)
    r1 = jnp.minimum(r0 + MOE_SUB, counts_sb) - 1
    hit_sb = e_sb[:, None] == jnp.arange(N_EXPERTS)[None, :]
    cum_sb = jnp.sum(jnp.where(hit_sb[:, None, :], cum[None, 1:, :], 0), axis=-1)
    cmin = jnp.sum(cum_sb <= r0[:, None], axis=1)
    cmax = jnp.sum(cum_sb <= r1[:, None], axis=1)
    items = jnp.where(valid_sb, cmax - cmin + 1, 0)
    off_end = jnp.cumsum(items)
    off = off_end - items
    total = off_end[-1]
    w_max = (2 * n_tok) // MOE_SUB + N_EXPERTS + N_EXPERTS * (n_chunks - 1)
    w = jnp.arange(w_max)
    wv = w < total
    wq = jnp.minimum(w, total - 1)
    w_sb = jnp.minimum(jnp.sum(off_end[None, :] <= wq[:, None], axis=1), nb_sub - 1)
    off_w = _lookup(off, w_sb)
    w_chunk = _lookup(cmin, w_sb) + (wq - off_w)
    w_flag = wv.astype(jnp.int32) + 2 * (wq == off_w).astype(jnp.int32)

    bi = jnp.arange(nb_blk)
    e_blk = jnp.minimum(jnp.sum(bi[:, None] * MOE_BLK >= pad_end[None, :], axis=1), N_EXPERTS - 1)
    rows = jnp.where(bi * MOE_BLK < pad_end[-1],
                     _lookup(counts, e_blk) - (bi * MOE_BLK - _lookup(pad_start, e_blk)), 0)
    n_sub = jnp.clip((rows + MOE_SUB - 1) // MOE_SUB, 0, MOE_BLK // MOE_SUB)
    g_count = jnp.sum(items.reshape(nb_blk, MOE_BLK // MOE_SUB), axis=1)
    g_start = jnp.cumsum(g_count) - g_count
    return dict(dest=dest.astype(jnp.int32), cap=cap,
                w_sb=w_sb.astype(jnp.int32), w_chunk=w_chunk.astype(jnp.int32), w_flag=w_flag,
                g_start=g_start.astype(jnp.int32), g_count=g_count.astype(jnp.int32),
                e_blk=e_blk.astype(jnp.int32), n_sub=n_sub.astype(jnp.int32),
                c_blocks=c_blocks, c_count=c_count, c_start=c_start)


def _moe_gather_kernel(st_ref, cn_ref, wsb_ref, wch_ref, t_hbm, d_ref, g_ref, xg_ref, gs_ref, tbuf, sem):
    i = pl.program_id(0)
    n = cn_ref[i]
    s0 = st_ref[i]
    total = st_ref[pl.num_programs(0) - 1] + cn_ref[pl.num_programs(0) - 1]
    subs = MOE_BLK // MOE_SUB

    def chunk_copy(item):
        slot = lax.rem(item, MOE_NBUF)
        start = pl.multiple_of(wch_ref[item] * MOE_TC, MOE_TC)
        return pltpu.make_async_copy(t_hbm.at[pl.ds(start, MOE_TC), :], tbuf.at[slot], sem.at[slot])

    @pl.when(i == 0)
    def _():
        for a in range(MOE_AHEAD):
            @pl.when(a < total)
            def _():
                chunk_copy(a).start()

    xg_ref[...] = jnp.zeros(xg_ref.shape, BF16)
    gs_ref[...] = jnp.zeros(gs_ref.shape, F32)

    def arrive(item):
        chunk_copy(item).wait()

        @pl.when(item + MOE_AHEAD < total)
        def _():
            chunk_copy(item + MOE_AHEAD).start()

    def contribute(item):
        sb = wsb_ref[item]
        chunk = wch_ref[item]
        d = d_ref[chunk]
        g = g_ref[chunk]
        srow = lax.broadcasted_iota(jnp.int32, (MOE_SUB, MOE_TC), 0) + sb * MOE_SUB
        hit0 = srow == d[0:1]
        hit1 = srow == d[1:2]
        p = jnp.where(hit0 | hit1, 1.0, 0.0).astype(BF16)
        rows = _dot(p, tbuf[lax.rem(item, MOE_NBUF)]).astype(BF16)
        gate = jnp.sum(jnp.where(hit0, g[0:1], 0.0) + jnp.where(hit1, g[1:2], 0.0), axis=-1, keepdims=True)
        r0 = pl.multiple_of((sb - i * subs) * MOE_SUB, MOE_SUB)
        xg_ref[pl.ds(r0, MOE_SUB), :] += rows
        gs_ref[pl.ds(r0, MOE_SUB), :] += gate

    def pair(k, carry):
        item = s0 + 2 * k
        arrive(item)
        arrive(item + 1)
        contribute(item)
        contribute(item + 1)
        return carry

    lax.fori_loop(0, n // 2, pair, 0)

    @pl.when(lax.rem(n, 2) == 1)
    def _():
        arrive(s0 + n - 1)
        contribute(s0 + n - 1)


def _moe_gather(t, dest, gates, plan):
    n_tok, d = t.shape
    cap = plan["cap"]
    n_chunks = n_tok // MOE_TC
    by_chunk = lambda a: a.reshape(2, n_chunks, MOE_TC).transpose(1, 0, 2)
    grid_spec = pltpu.PrefetchScalarGridSpec(
        num_scalar_prefetch=4,
        grid=(cap // MOE_BLK,),
        in_specs=[
            pl.BlockSpec(memory_space=pl.ANY),
            pl.BlockSpec((n_chunks, 2, MOE_TC), lambda i, *_: (0, 0, 0)),
            pl.BlockSpec((n_chunks, 2, MOE_TC), lambda i, *_: (0, 0, 0)),
        ],
        out_specs=(pl.BlockSpec((MOE_BLK, d), lambda i, *_: (i, 0)),
                   pl.BlockSpec((MOE_BLK, 1), lambda i, *_: (i, 0))),
        scratch_shapes=[pltpu.VMEM((MOE_NBUF, MOE_TC, d), BF16), pltpu.SemaphoreType.DMA((MOE_NBUF,))],
    )
    return pl.pallas_call(
        _moe_gather_kernel,
        out_shape=(jax.ShapeDtypeStruct((cap, d), BF16), jax.ShapeDtypeStruct((cap, 1), F32)),
        grid_spec=grid_spec,
        compiler_params=_cparams(("arbitrary",)),
        name="moe_gather",
    )(plan["g_start"], plan["g_count"], plan["w_sb"], plan["w_chunk"], t, by_chunk(dest), by_chunk(gates))


def _moe_expert_kernel(eb_ref, ns_ref, x_ref, gs_ref, w1_ref, w3_ref, w2_ref, y_ref):
    n_sub = ns_ref[pl.program_id(0)]
    subs = MOE_BLK // MOE_SUB

    def mlp(rows):
        xs = x_ref[rows, :]
        acc = jnp.zeros((rows.stop - rows.start, y_ref.shape[1]), F32)
        for f in range(w1_ref.shape[1] // MOE_TF):
            cols = slice(f * MOE_TF, (f + 1) * MOE_TF)
            a = _dot(xs, w1_ref[:, cols])
            u = _dot(xs, w3_ref[:, cols])
            acc = acc + _dot((_silu(a) * u).astype(BF16), w2_ref[cols, :])
        y_ref[rows, :] = (acc * gs_ref[rows, :]).astype(BF16)

    @pl.when(n_sub == subs)
    def _():
        mlp(slice(0, MOE_BLK))

    for sub in range(subs):
        rows = slice(sub * MOE_SUB, (sub + 1) * MOE_SUB)

        @pl.when((n_sub < subs) & (sub < n_sub))
        def _():
            mlp(rows)

        @pl.when(sub >= n_sub)
        def _():
            y_ref[rows, :] = jnp.zeros((MOE_SUB, y_ref.shape[1]), BF16)


def _moe_experts(xg, gs, w1, w3, w2, plan):
    cap, d = xg.shape
    f = w1.shape[2]
    once = pl.Buffered(1)
    grid_spec = pltpu.PrefetchScalarGridSpec(
        num_scalar_prefetch=2,
        grid=(cap // MOE_BLK,),
        in_specs=[
            pl.BlockSpec((MOE_BLK, d), lambda i, eb, ns: (i, 0)),
            pl.BlockSpec((MOE_BLK, 1), lambda i, eb, ns: (i, 0)),
            pl.BlockSpec((None, d, f), lambda i, eb, ns: (eb[i], 0, 0), pipeline_mode=once),
            pl.BlockSpec((None, d, f), lambda i, eb, ns: (eb[i], 0, 0), pipeline_mode=once),
            pl.BlockSpec((None, f, d), lambda i, eb, ns: (eb[i], 0, 0), pipeline_mode=once),
        ],
        out_specs=pl.BlockSpec((MOE_BLK, d), lambda i, eb, ns: (i, 0)),
    )
    return pl.pallas_call(
        _moe_expert_kernel,
        out_shape=jax.ShapeDtypeStruct((cap, d), BF16),
        grid_spec=grid_spec,
        compiler_params=_cparams(("arbitrary",)),
        name="moe_experts",
    )(plan["e_blk"], plan["n_sub"], xg, gs, w1, w3, w2)


MOE_WIN = 2 * N_EXPERTS


def _moe_combine_kernel(st_ref, cn_ref, bl_ref, dt_ref, y_hbm, x1_ref, g_ref, o_ref, ybuf, sem, acc_ref):
    c = pl.program_id(0)
    n = cn_ref[c]
    s0 = st_ref[c]
    total = st_ref[pl.num_programs(0) - 1] + cn_ref[pl.num_programs(0) - 1]
    d = dt_ref[...]

    def block_copy(item):
        slot = lax.rem(item, MOE_NBUF)
        start = pl.multiple_of(bl_ref[item] * MOE_SUB, MOE_SUB)
        return pltpu.make_async_copy(y_hbm.at[pl.ds(start, MOE_SUB), :], ybuf.at[slot], sem.at[slot])

    @pl.when(c == 0)
    def _():
        for a in range(MOE_AHEAD):
            @pl.when(a < total)
            def _():
                block_copy(a).start()

    def arrive(item):
        block_copy(item).wait()

        @pl.when(item + MOE_AHEAD < total)
        def _():
            block_copy(item + MOE_AHEAD).start()

    def rows_of(item):
        scol = lax.broadcasted_iota(jnp.int32, (MOE_TC, MOE_SUB), 1) + bl_ref[item] * MOE_SUB
        p = jnp.where((scol == d[:, 0:1]) | (scol == d[:, 1:2]), 1.0, 0.0).astype(BF16)
        return _dot(p, ybuf[lax.rem(item, MOE_NBUF)])

    def pair(k, acc):
        item = s0 + 2 * k
        arrive(item)
        arrive(item + 1)
        return acc + rows_of(item) + rows_of(item + 1)

    acc_ref[...] = lax.fori_loop(0, n // 2, pair, jnp.zeros(o_ref.shape, F32))

    @pl.when(lax.rem(n, 2) == 1)
    def _():
        arrive(s0 + n - 1)
        acc_ref[...] += rows_of(s0 + n - 1)

    o_ref[...] = x1_ref[...] + g_ref[0] * acc_ref[...]


def _moe_combine(y, dest_t, x1, gate, plan, seq):
    n_tok, d = x1.shape
    n_chunks = n_tok // MOE_TC
    per_b = seq // MOE_TC
    grid_spec = pltpu.PrefetchScalarGridSpec(
        num_scalar_prefetch=3,
        grid=(n_chunks,),
        in_specs=[
            pl.BlockSpec((MOE_TC, 2), lambda c, *_: (c, 0)),
            pl.BlockSpec(memory_space=pl.ANY),
            pl.BlockSpec((MOE_TC, d), lambda c, *_: (c, 0)),
            pl.BlockSpec((1, 1, d), lambda c, *_: (c // per_b, 0, 0)),
        ],
        out_specs=pl.BlockSpec((MOE_TC, d), lambda c, *_: (c, 0)),
        scratch_shapes=[pltpu.VMEM((MOE_NBUF, MOE_SUB, d), BF16), pltpu.SemaphoreType.DMA((MOE_NBUF,)),
                        pltpu.VMEM((MOE_TC, d), F32)],
    )
    return pl.pallas_call(
        _moe_combine_kernel,
        out_shape=jax.ShapeDtypeStruct((n_tok, d), F32),
        grid_spec=grid_spec,
        compiler_params=_cparams(("arbitrary",)),
        name="moe_combine",
    )(plan["c_start"], plan["c_count"], plan["c_blocks"], dest_t, y, x1, gate)


def _split_mod(mod_l, batch):
    d = D_MODEL
    lat = tuple(mod_l[:batch, k * d:(k + 1) * d][:, None, :] for k in range(ADA_CHUNKS))
    ctx = tuple(jnp.broadcast_to(mod_l[batch:batch + 1, k * d:(k + 1) * d][:, None, :], (batch, 1, d))
                for k in range(ADA_CHUNKS))
    return lat, ctx


def _score_bound(q_gain, k_gain, n):
    return 1.02 * n * jnp.max(jnp.abs(q_gain)) * jnp.max(jnp.abs(k_gain))


def _block_ones(n, block):
    idx = jnp.arange(n) // block
    return (idx[:, None] == idx[None, :]).astype(BF16)


def _even_layer(x, xc, mod_l, norm1_g, norm2_g, w_in, w_out, na_q_g, na_k_g, na_rpb, diff_q_g, diff_k_g,
                lq1, lk1, lq2, lk2, subln_g, wg, wu, wd, layer_idx, need_ctx):
    b, s, d = x.shape
    (sh1, sc1, g1, sh2, sc2, g2), (csh1, csc1, cg1, csh2, csc2, cg2) = _split_mod(mod_l, b)
    qscale = HEAD_DIM ** -0.5 * LOG2E
    lam_init = 0.8 - 0.6 * math.exp(-0.3 * layer_idx)
    hg = jnp.stack([jnp.tile(na_q_g, 8) * qscale, jnp.tile(na_k_g, 8),
                    jnp.tile(diff_q_g, 8) * qscale, jnp.tile(diff_k_g, 8)]).astype(F32)
    e512 = _block_ones(512, HEAD_DIM)
    cos, sin = _rope_tables(s, HEAD_DIM, 16)
    ng1 = norm1_g[None, :]
    ng2 = norm2_g[None, :]
    w_in_b = w_in.astype(BF16)
    wo_a = w_out[:512].astype(BF16)
    wo_b = w_out[512:].astype(BF16)
    wg_b, wu_b, wd_b = wg.astype(BF16), wu.astype(BF16), wd.astype(BF16)
    lam_p = jnp.stack([lq1, lk1, lq2, lk2]).astype(F32)
    sg = subln_g[None, :].astype(F32)
    ident = lambda j: j

    tm = min(ROW_TM, s)
    tc = xc.shape[1]
    qa, ka, va, qb, kb, vb = _even_proj(x, sh1, sc1, ng1, w_in_b, e512, hg, cos, sin, use_rope=True, tm=tm)
    qca, kca, vca, qcb, kcb, vcb = _even_proj(xc, csh1, csc1, ng1, w_in_b, e512, hg, cos, sin, use_rope=False, tm=tc)
    out_a = _na_attention(qa, ka, va, kca, vca, _na_bias_table(na_rpb), _score_bound(hg[0], hg[1], HEAD_DIM))
    out_b = _flash(qb, kb, vb, kcb, vcb, mode="diff", kv_map=ident, n_chunks=4, tq=FLASH_TQ, tk=FLASH_TK,
                   extra=(lam_p, sg), lam_init=lam_init, score_bound=_score_bound(hg[2], hg[3], HEAD_DIM))
    x1, h2 = _post_attn(out_a, out_b, wo_a, wo_b, x, g1, ng2, sh2, sc2, tm=tm)
    x2 = _ffn(h2, x1, g2, wg_b, wu_b, wd_b, tm=tm)
    if not need_ctx:
        return x2, None
    oca = _flash(qca, None, None, kca, vca, mode="pair", kv_map=ident, n_chunks=4, tq=tc, tk=tc)
    ocb = _flash(qcb, None, None, kcb, vcb, mode="diff", kv_map=ident, n_chunks=4, tq=tc, tk=tc,
                 extra=(lam_p, sg), lam_init=lam_init)
    xc1, hc2 = _post_attn(oca, ocb, wo_a, wo_b, xc, cg1, ng2, csh2, csc2, tm=tc)
    xc2 = _ffn(hc2, xc1, cg2, wg_b, wu_b, wd_b, tm=tc)
    return x2, xc2


def _odd_layer(x, xc, mod_l, norm1_g, norm2_g, w_in, w_out, gqa_q_g, gqa_k_g, cq_g, w_uq, ckv_g, w_ukv,
               mla_q_g, mla_k_g, router, w1, w3, w2):
    b, s, d = x.shape
    n_tok = b * s
    (sh1, sc1, g1, sh2, sc2, g2), (csh1, csc1, _, _, _, _) = _split_mod(mod_l, b)
    ng1 = norm1_g[None, :]
    ng2 = norm2_g[None, :]

    z = lambda n: jnp.zeros((d, n), w_in.dtype)
    k0, k1, v0, v1 = (w_in[:, 512 + 64 * i:576 + 64 * i] for i in range(4))
    w_p = jnp.concatenate([w_in[:, 0:512], k0, k0, k1, k1, v0, v0, v1, v1, w_in[:, 768:1024], w_in[:, 1024:1152],
                           z(MLA_NOPE), w_in[:, 1152:1184], z(LANES - MLA_QK)], axis=1).astype(BF16)
    wuq_p = jnp.pad(w_uq.reshape(MLA_Q_RANK, MLA_HEADS, MLA_QK), ((0, 0), (0, 0), (0, LANES - MLA_QK)))
    wuq_p = wuq_p.reshape(MLA_Q_RANK, MLA_HEADS * LANES).astype(BF16)
    ukv = w_ukv.reshape(MLA_KV_RANK, MLA_HEADS, MLA_NOPE + MLA_V)
    uk = jnp.pad(ukv[:, :, :MLA_NOPE], ((0, 0), (0, 0), (0, LANES - MLA_NOPE))).reshape(MLA_KV_RANK, -1)
    uv = ukv[:, :, MLA_NOPE:].reshape(MLA_KV_RANK, -1)
    wukv_p = jnp.concatenate([uk, uv], axis=1).astype(BF16)
    e512 = _block_ones(512, HEAD_DIM)
    a = (jnp.arange(MLA_HEADS * LANES)[:, None] // LANES == jnp.arange(LANES)[None, :]).astype(BF16)
    at = a.T
    qscale = HEAD_DIM ** -0.5 * LOG2E
    mscale = MLA_QK ** -0.5 * LOG2E
    pad_row = lambda v: jnp.pad(v, (0, MLA_HEADS * LANES - v.shape[0]))
    pad_head = lambda v: jnp.tile(jnp.pad(v, (0, LANES - MLA_QK)), MLA_HEADS)
    gains = jnp.stack([pad_row(jnp.tile(gqa_q_g, 8) * qscale), pad_row(jnp.tile(gqa_k_g, 4)),
                       pad_row(cq_g), pad_row(ckv_g), pad_head(mla_q_g) * mscale, pad_head(mla_k_g),
                       jnp.zeros((MLA_HEADS * LANES,), F32), jnp.zeros((MLA_HEADS * LANES,), F32)]).astype(F32)
    cos, sin = _rope_tables(s, HEAD_DIM, 16)
    cosm, sinm = _mla_rope_tables(s)
    wo_a = w_out[:512].astype(BF16)
    wo_b = w_out[512:].astype(BF16)

    tm = min(ROW_TM, s)
    tc = xc.shape[1]
    proj = functools.partial(_odd_proj, ng=ng1, w=w_p, wuq=wuq_p, wukv=wukv_p, e512=e512, a=a, at=at, gains=gains,
                             cos=cos, sin=sin, cosm=cosm, sinm=sinm)
    qc, kc, vc, qd, kd, vd = proj(x, sh1, sc1, use_rope=True, tm=tm)
    _, kcc, vcc, _, kcd, vcd = proj(xc, csh1, csc1, use_rope=False, tm=tc)
    out_c = _flash(qc, kc, vc, kcc, vcc, mode="pair", kv_map=lambda j: j // 2, n_chunks=4, tq=FLASH_TQ, tk=FLASH_TK,
                   score_bound=_score_bound(gains[0], gains[1], HEAD_DIM))
    out_d = _flash(qd, kd, vd, kcd, vcd, mode="mla", kv_map=lambda j: j, n_chunks=4, tq=FLASH_TQ, tk=FLASH_TK,
                   score_bound=_score_bound(gains[4], gains[5], MLA_QK))
    x1, h2 = _post_attn(out_c, out_d, wo_a, wo_b, x, g1, ng2, sh2, sc2, tm=tm)

    ei, gt, rk, cnt = _router(x1, ng2, sh2, sc2, router.T.astype(F32), tm=tm)
    plan = _moe_plan(ei, rk, cnt[:, 0], n_tok)
    xg, gs = _moe_gather(h2.reshape(n_tok, d), plan["dest"], gt, plan)
    y = _moe_experts(xg, gs, w1.astype(BF16), w3.astype(BF16), w2.astype(BF16), plan)
    out = _moe_combine(y, plan["dest"].T, x1.reshape(n_tok, d), g2, plan, s)
    return out.reshape(b, s, d)


def _mod_vectors(c, c_ctx, ada_w, ada_b):
    b = c.shape[0]
    cc = jnp.zeros((MOD_ROWS, D_MODEL), F32).at[:b].set(c).at[b].set(c_ctx)
    return _modvec(cc, ada_w, ada_b[:, None, :])


def kernel(x, c, ctx, c_ctx, ada_w, ada_b, norm1_g, norm2_g, ev_w_in, ev_w_out, na_q_g, na_k_g, na_rpb,
           diff_q_g, diff_k_g, diff_lq1, diff_lk1, diff_lq2, diff_lk2, diff_subln_g,
           ffn_w_gate, ffn_w_up, ffn_w_down, od_w_in, od_w_out, gqa_q_g, gqa_k_g, mla_cq_g, mla_w_uq,
           mla_ckv_g, mla_w_ukv, mla_q_g, mla_k_g, moe_router, moe_w1, moe_w3, moe_w2):
    mod = _mod_vectors(c, c_ctx, ada_w, ada_b)
    x, xc = _even_layer(x, ctx, mod[0], norm1_g[0], norm2_g[0], ev_w_in[0], ev_w_out[0], na_q_g[0], na_k_g[0],
                        na_rpb[0], diff_q_g[0], diff_k_g[0], diff_lq1[0], diff_lk1[0], diff_lq2[0], diff_lk2[0],
                        diff_subln_g[0], ffn_w_gate[0], ffn_w_up[0], ffn_w_down[0], 0, True)
    return _odd_layer(x, xc, mod[1], norm1_g[1], norm2_g[1], od_w_in[0], od_w_out[0], gqa_q_g[0], gqa_k_g[0],
                      mla_cq_g[0], mla_w_uq[0], mla_ckv_g[0], mla_w_ukv[0], mla_q_g[0], mla_k_g[0],
                      moe_router[0], moe_w1[0], moe_w3[0], moe_w2[0])
```

```python
import functools
import math

import jax
import jax.numpy as jnp
from jax import lax
from jax.experimental import pallas as pl
from jax.experimental.pallas import tpu as pltpu

F32 = jnp.float32
BF16 = jnp.bfloat16
HIGHEST = lax.Precision.HIGHEST

D_MODEL = 1024
GRID_W = 64
HEAD_DIM = 64
ROPE_THETA = 10000.0
NORM_EPS = 1e-6
NEG_INF = -1e30
ADA_CHUNKS = 6
LOG2E = 1.4426950408889634

NA_HEADS = 8
NA_WIN_H = 8
NA_WIN_W = 16
DIFF_HEADS = 4
DIFF_V_DIM = 2 * HEAD_DIM
GQA_Q_HEADS = 8
GQA_KV_HEADS = 2
MLA_HEADS = 8
MLA_NOPE = 64
MLA_ROPE = 32
MLA_QK = MLA_NOPE + MLA_ROPE
MLA_V = 64
MLA_Q_RANK = 256
MLA_KV_RANK = 128
D_FF = 2816
N_EXPERTS = 8
D_FF_EXPERT = 3584

LANES = 128
MXU_DIM = 256
VMEM_LIMIT = 56 * 1024 * 1024
MOD_ROWS = 16
MOE_TF = 512
SUM_ROWS = 16
ROW_TM = 1024
FLASH_TQ = 1024
FLASH_TK = 512
EXP2_SAFE_RANGE = 64.0


def _cparams(sem):
    return pltpu.CompilerParams(dimension_semantics=sem, vmem_limit_bytes=VMEM_LIMIT)


def _dot(a, b):
    return jnp.dot(a, b, preferred_element_type=F32)


def _dot_nt(a, b):
    return lax.dot_general(a, b, (((1,), (1,)), ((), ())), preferred_element_type=F32)


def _silu(x):
    return x * (1.0 / (1.0 + jnp.exp(-x)))


def _modvec_kernel(c_ref, w_ref, b_ref, o_ref):
    s = _silu(c_ref[...])
    o_ref[...] = jnp.dot(s, w_ref[...], preferred_element_type=F32, precision=HIGHEST) + b_ref[...]


def _modvec(cc, ada_w, ada_b):
    n_layers, d, n = ada_w.shape
    tn = 768
    return pl.pallas_call(
        _modvec_kernel,
        out_shape=jax.ShapeDtypeStruct((n_layers, MOD_ROWS, n), F32),
        grid=(n_layers, n // tn),
        in_specs=[
            pl.BlockSpec((MOD_ROWS, d), lambda l, j: (0, 0)),
            pl.BlockSpec((None, d, tn), lambda l, j: (l, 0, j)),
            pl.BlockSpec((None, 1, tn), lambda l, j: (l, 0, j)),
        ],
        out_specs=pl.BlockSpec((None, MOD_ROWS, tn), lambda l, j: (l, 0, j)),
        compiler_params=_cparams(("arbitrary", "arbitrary")),
        name="modvec",
    )(cc, ada_w, ada_b)


def _norm_mod(x, g, shift, scale):
    ms = jnp.mean(x * x, axis=-1, keepdims=True)
    return (x * lax.rsqrt(ms + NORM_EPS) * g) * (1.0 + scale) + shift


def _row_blocks(n, size=256):
    size = min(size, n)
    return [slice(r, r + size) for r in range(0, n, size)]


def _group_norm(t, e, g, inv_n):
    outs = []
    for c in range(t.shape[1] // MXU_DIM):
        cols = slice(c * MXU_DIM, (c + 1) * MXU_DIM)
        tc = t[:, cols]
        ss = _dot((tc * tc).astype(BF16), e)
        outs.append(tc * lax.rsqrt(ss * inv_n + NORM_EPS) * g[:, cols])
    return outs[0] if len(outs) == 1 else jnp.concatenate(outs, axis=1)


def _rope_chunks(t, cos, sin, lane_hi, shift):
    outs = []
    for c in range(t.shape[1] // LANES):
        tc = t[:, c * LANES:(c + 1) * LANES]
        up = pltpu.roll(tc, LANES - shift, 1)
        dn = pltpu.roll(tc, shift, 1)
        outs.append(tc * cos + jnp.where(lane_hi, dn, up) * sin)
    return outs[0] if len(outs) == 1 else jnp.concatenate(outs, axis=1)


def _rope_tables(seq, unit, pair_shift):
    pos = jnp.arange(seq)
    rows, cols = pos // GRID_W, pos % GRID_W
    quarter = unit // 4
    assert quarter == pair_shift
    freqs = ROPE_THETA ** (-jnp.arange(quarter, dtype=F32) / quarter)
    lane = jnp.arange(LANES)
    u = lane % unit
    use_col = (u // (unit // 2)) == 1
    fi = u % quarter
    p = jnp.where(use_col[None, :], cols[:, None], rows[:, None]).astype(F32)
    ang = p * freqs[fi][None, :]
    second = ((u % (unit // 2)) // quarter) == 1
    cos = jnp.cos(ang)
    sin = jnp.where(second[None, :], jnp.sin(ang), -jnp.sin(ang))
    return cos, sin


def _even_proj_kernel(x_ref, sh_ref, sc_ref, ng_ref, w_ref, e_ref, hg_ref, cos_ref, sin_ref,
                      qa_ref, ka_ref, va_ref, qb_ref, kb_ref, vb_ref, *, use_rope):
    hg = hg_ref[...]
    e = e_ref[...]
    inv_n = 1.0 / HEAD_DIM
    for rows in _row_blocks(x_ref.shape[1]):
        h = _norm_mod(x_ref[0, rows, :], ng_ref[...], sh_ref[0], sc_ref[0])
        p = _dot(h.astype(BF16), w_ref[...])
        qa = _group_norm(p[:, 0:512], e, hg[0:1], inv_n)
        ka = _group_norm(p[:, 512:1024], e, hg[1:2], inv_n)
        qb = _group_norm(p[:, 1536:2048], e, hg[2:3], inv_n)
        kb = _group_norm(p[:, 2048:2560], e, hg[3:4], inv_n)
        if use_rope:
            lane = lax.broadcasted_iota(jnp.int32, (1, LANES), 1)
            lane_hi = (lane & 16) != 0
            cos, sin = cos_ref[rows, :], sin_ref[rows, :]
            qb = _rope_chunks(qb, cos, sin, lane_hi, 16)
            kb = _rope_chunks(kb, cos, sin, lane_hi, 16)
        qa_ref[0, rows, :] = qa.astype(BF16)
        ka_ref[0, rows, :] = ka.astype(BF16)
        va_ref[0, rows, :] = p[:, 1024:1536].astype(BF16)
        qb_ref[0, rows, :] = qb.astype(BF16)
        kb_ref[0, rows, :] = kb.astype(BF16)
        vb_ref[0, rows, :] = p[:, 2560:3072].astype(BF16)


def _even_proj(x, shift, scale, ng, w, e64, hg, cos, sin, *, use_rope, tm):
    b, s, d = x.shape
    n = w.shape[1]
    row = lambda bi, i: (bi, i, 0)
    vec = lambda bi, i: (bi, 0, 0)
    const = lambda bi, i: (0, 0)
    out = jax.ShapeDtypeStruct((b, s, 512), BF16)
    return pl.pallas_call(
        functools.partial(_even_proj_kernel, use_rope=use_rope),
        out_shape=(out,) * 6,
        grid=(b, s // tm),
        in_specs=[
            pl.BlockSpec((1, tm, d), row),
            pl.BlockSpec((1, 1, d), vec),
            pl.BlockSpec((1, 1, d), vec),
            pl.BlockSpec((1, d), const),
            pl.BlockSpec((d, n), const),
            pl.BlockSpec(e64.shape, const),
            pl.BlockSpec((4, 512), const),
            pl.BlockSpec((tm, LANES), lambda bi, i: (i, 0)),
            pl.BlockSpec((tm, LANES), lambda bi, i: (i, 0)),
        ],
        out_specs=(pl.BlockSpec((1, tm, 512), row),) * 6,
        compiler_params=_cparams(("parallel", "parallel")),
        name="even_proj",
    )(x, shift, scale, ng, w, e64, hg, cos, sin)


def _pair_stack(q):
    lane = lax.broadcasted_iota(jnp.int32, q.shape, 1)
    zero = jnp.zeros_like(q)
    return jnp.concatenate([jnp.where(lane < HEAD_DIM, q, zero), jnp.where(lane >= HEAD_DIM, q, zero)], axis=0)


def _pair_merge(o, tq):
    lane = lax.broadcasted_iota(jnp.int32, (tq, LANES), 1)
    return jnp.where(lane < HEAD_DIM, o[:tq], o[tq:])


def _with_ones(v):
    return jnp.concatenate([v, jnp.ones(v.shape, BF16)], axis=1)


def _na_kernel(q_ref, k_ref, v_ref, kc_ref, vc_ref, bias_ref, o_ref, *, rows_per_step, n_rows, direct):
    rb = pl.program_id(2)
    kc = kc_ref[0]
    vc = vc_ref[0]
    n_lat = NA_WIN_H * GRID_W
    if direct:
        nq = rows_per_step * GRID_W
        qs_all = _pair_stack(q_ref[0])
        o_ctx = _dot(jnp.exp2(_dot_nt(qs_all, kc)).astype(BF16), _with_ones(vc))
        for i in range(rows_per_step):
            r = rb * rows_per_step + i
            r0 = jnp.clip(r - NA_WIN_H // 2, 0, n_rows - NA_WIN_H)
            start = pl.multiple_of(r0 * GRID_W, GRID_W)
            kw = k_ref[0, pl.ds(start, n_lat), :]
            vw = v_ref[0, pl.ds(start, n_lat), :]
            lo = slice(i * GRID_W, (i + 1) * GRID_W)
            hi = slice(nq + i * GRID_W, nq + (i + 1) * GRID_W)
            qs = jnp.concatenate([qs_all[lo], qs_all[hi]], axis=0)
            p = jnp.exp2(_dot_nt(qs, kw) + bias_ref[r - r0]).astype(BF16)
            tot = _dot(p, _with_ones(vw)) + jnp.concatenate([o_ctx[lo], o_ctx[hi]], axis=0)
            o = tot[:, :LANES] / tot[:, LANES:]
            o_ref[0, lo, :] = _pair_merge(o, GRID_W).astype(BF16)
        return
    for i in range(rows_per_step):
        r = rb * rows_per_step + i
        r0 = jnp.clip(r - NA_WIN_H // 2, 0, n_rows - NA_WIN_H)
        start = pl.multiple_of(r0 * GRID_W, GRID_W)
        kw = k_ref[0, pl.ds(start, n_lat), :]
        vw = v_ref[0, pl.ds(start, n_lat), :]
        qs = _pair_stack(q_ref[0, i * GRID_W:(i + 1) * GRID_W, :])
        s_lat = _dot_nt(qs, kw) + bias_ref[r - r0]
        s_ctx = _dot_nt(qs, kc)
        m = jnp.maximum(jnp.max(s_lat, axis=-1, keepdims=True), jnp.max(s_ctx, axis=-1, keepdims=True))
        p_lat = jnp.exp2(s_lat - m)
        p_ctx = jnp.exp2(s_ctx - m)
        l = jnp.sum(p_lat, axis=-1, keepdims=True) + jnp.sum(p_ctx, axis=-1, keepdims=True)
        o = _dot(p_lat.astype(BF16), vw) + _dot(p_ctx.astype(BF16), vc)
        o = o / l
        o_ref[0, i * GRID_W:(i + 1) * GRID_W, :] = _pair_merge(o, GRID_W).astype(BF16)


def _na_attention(q, k, v, kc, vc, bias, score_bound, *, rows_per_step=8):
    bound = score_bound + jnp.max(jnp.where(bias > 0.5 * NEG_INF, jnp.abs(bias), 0.0))
    return lax.cond(bound <= EXP2_SAFE_RANGE,
                    lambda: _na_call(q, k, v, kc, vc, bias, rows_per_step=rows_per_step, direct=True),
                    lambda: _na_call(q, k, v, kc, vc, bias, rows_per_step=rows_per_step, direct=False))


def _na_call(q, k, v, kc, vc, bias, *, rows_per_step, direct):
    b, s, _ = q.shape
    c = kc.shape[1]
    n_rows = s // GRID_W
    tq = rows_per_step * GRID_W
    return pl.pallas_call(
        functools.partial(_na_kernel, rows_per_step=rows_per_step, n_rows=n_rows, direct=direct),
        out_shape=jax.ShapeDtypeStruct((b, s, 512), BF16),
        grid=(b, 4, n_rows // rows_per_step),
        in_specs=[
            pl.BlockSpec((1, tq, LANES), lambda bi, j, i: (bi, i, j)),
            pl.BlockSpec((1, s, LANES), lambda bi, j, i: (bi, 0, j)),
            pl.BlockSpec((1, s, LANES), lambda bi, j, i: (bi, 0, j)),
            pl.BlockSpec((1, c, LANES), lambda bi, j, i: (bi, 0, j)),
            pl.BlockSpec((1, c, LANES), lambda bi, j, i: (bi, 0, j)),
            pl.BlockSpec((NA_WIN_H, None, LANES, NA_WIN_H * GRID_W), lambda bi, j, i: (0, j, 0, 0)),
        ],
        out_specs=pl.BlockSpec((1, tq, LANES), lambda bi, j, i: (bi, i, j)),
        compiler_params=_cparams(("parallel", "parallel", "parallel")),
        name="na_attention" + ("" if direct else "_online"),
    )(q, k, v, kc, vc, bias)


def _na_bias_table(rpb):
    w = GRID_W
    col = jnp.arange(w)
    c0 = jnp.clip(col - NA_WIN_W // 2, 0, w - NA_WIN_W)
    col_in = (col[None, :] >= c0[:, None]) & (col[None, :] < c0[:, None] + NA_WIN_W)
    left = (w - 1) - (NA_WIN_W - 1)
    ext = jnp.pad(rpb, ((0, 0), (0, 0), (left, 2 * w - left - (2 * NA_WIN_W - 1))), mode="edge")
    h, nr, _ = rpb.shape
    flat = jnp.broadcast_to(ext[:, :, None, :], (h, nr, w, 2 * w)).reshape(h, nr, w * 2 * w)
    toep = flat[:, :, :w * (2 * w - 1)].reshape(h, nr, w, 2 * w - 1)[:, :, :, w - 1:]
    toep = jnp.where(col_in[None, None], toep * LOG2E, NEG_INF)
    variants = []
    for v in range(NA_WIN_H):
        tv = toep[:, NA_WIN_H - 1 - v:2 * NA_WIN_H - 1 - v]
        variants.append(tv.transpose(0, 2, 1, 3).reshape(NA_HEADS // 2, 2 * w, NA_WIN_H * w))
    return jnp.stack(variants, axis=0).astype(F32)


def _flash_kernel(*refs, mode, online, tq, tk, s_len, c_len, lam_init):
    if mode == "diff":
        q_ref, k_ref, v_ref, kc_ref, vc_ref, lam_ref, sg_ref, o_ref = refs
    else:
        q_ref, k_ref, v_ref, kc_ref, vc_ref, o_ref = refs

    q = q_ref[0]
    if mode == "mla":
        q_parts = (q[:, :LANES], q[:, LANES:])
    else:
        qs = _pair_stack(q)

    def scores(kt):
        if mode == "mla":
            return jnp.concatenate([_dot_nt(q_parts[0], kt[:, :LANES]), _dot_nt(q_parts[1], kt[:, LANES:])], axis=0)
        return _dot_nt(qs, kt)

    def step(kt, vt, carry):
        m, l, acc = carry
        s = scores(kt)
        m_new = jnp.maximum(m, jnp.max(s, axis=-1, keepdims=True))
        alpha = jnp.exp2(m - m_new)
        p = jnp.exp2(s - m_new)
        l = alpha * l + jnp.sum(p, axis=-1, keepdims=True)
        acc = alpha * acc + _dot(p.astype(BF16), vt)
        return m_new, l, acc

    def body(t, carry):
        start = pl.multiple_of(t * tk, tk)
        return step(k_ref[0, pl.ds(start, tk), :], v_ref[0, pl.ds(start, tk), :], carry)

    def scores_t(kt):
        if mode == "mla":
            return jnp.concatenate([_dot_nt(kt[:, :LANES], q_parts[0]), _dot_nt(kt[:, LANES:], q_parts[1])], axis=1)
        return _dot_nt(kt, qs)

    def direct(kt, vtt, acc):
        p = jnp.exp2(scores_t(kt)).astype(BF16)
        v1 = jnp.concatenate([vtt, jnp.ones((SUM_ROWS, vtt.shape[1]), BF16)], axis=0)
        return acc + _dot(v1, p)

    if online:
        carry = (jnp.full((2 * tq, 1), -jnp.inf, F32), jnp.zeros((2 * tq, 1), F32),
                 jnp.zeros((2 * tq, LANES), F32))
        if s_len:
            carry = lax.fori_loop(0, s_len // tk, body, carry)
        if c_len:
            carry = step(kc_ref[0], vc_ref[0], carry)
        _, l, acc = carry
        o = acc / l
    else:
        acc = jnp.zeros((LANES + SUM_ROWS, 2 * tq), F32)
        for t in range(s_len // tk):
            acc = direct(k_ref[0, t * tk:(t + 1) * tk, :], v_ref[0, :, t * tk:(t + 1) * tk], acc)
        if c_len:
            acc = direct(kc_ref[0], vc_ref[0], acc)
        o_t = acc[:LANES] / acc[LANES:LANES + 1]
    if mode == "diff":
        lp = lam_ref[...]
        lam = (jnp.exp(jnp.sum(lp[0:1] * lp[1:2], axis=-1, keepdims=True))
               - jnp.exp(jnp.sum(lp[2:3] * lp[3:4], axis=-1, keepdims=True)) + lam_init)
        if online:
            d = o[:tq] - lam * o[tq:]
            dn = d * lax.rsqrt(jnp.mean(d * d, axis=-1, keepdims=True) + NORM_EPS)
        else:
            d_t = o_t[:, :tq] - lam * o_t[:, tq:]
            dn = (d_t * lax.rsqrt(jnp.mean(d_t * d_t, axis=0, keepdims=True) + NORM_EPS)).T
        o_ref[0] = (dn * sg_ref[...] * (1.0 - lam_init)).astype(BF16)
    elif online:
        o_ref[0] = _pair_merge(o, tq).astype(BF16)
    else:
        row = lax.broadcasted_iota(jnp.int32, (LANES, tq), 0)
        o_ref[0] = jnp.where(row < HEAD_DIM, o_t[:, :tq], o_t[:, tq:]).T.astype(BF16)


def _flash(q, k, v, kc, vc, *, score_bound=None, **kw):
    if score_bound is None:
        return _flash_call(q, k, v, kc, vc, online=True, **kw)
    return lax.cond(score_bound <= EXP2_SAFE_RANGE,
                    lambda: _flash_call(q, k, jnp.swapaxes(v, 1, 2), kc, jnp.swapaxes(vc, 1, 2), online=False, **kw),
                    lambda: _flash_call(q, k, v, kc, vc, online=True, **kw))


def _flash_call(q, k, v, kc, vc, *, mode, online, kv_map, n_chunks, tq, tk, extra=(), lam_init=0.0):
    b, s, _ = q.shape
    qw = 2 * LANES if mode == "mla" else LANES
    c_len = kc.shape[1]
    if k is None:
        k, v, s_len = kc, vc, 0
    else:
        s_len = k.shape[1]
    kk = k.shape[1]
    if online:
        v_spec = lambda n: pl.BlockSpec((1, n, LANES), lambda bi, j, i: (bi, 0, kv_map(j)))
    else:
        v_spec = lambda n: pl.BlockSpec((1, LANES, n), lambda bi, j, i: (bi, kv_map(j), 0))
    in_specs = [
        pl.BlockSpec((1, tq, qw), lambda bi, j, i: (bi, i, j)),
        pl.BlockSpec((1, kk, qw), lambda bi, j, i: (bi, 0, kv_map(j) if mode != "mla" else j)),
        v_spec(kk),
        pl.BlockSpec((1, c_len, qw), lambda bi, j, i: (bi, 0, kv_map(j) if mode != "mla" else j)),
        v_spec(c_len),
    ]
    for e in extra:
        in_specs.append(pl.BlockSpec(e.shape, lambda bi, j, i: (0, 0)))
    return pl.pallas_call(
        functools.partial(_flash_kernel, mode=mode, online=online, tq=tq, tk=tk, s_len=s_len, c_len=c_len,
                          lam_init=lam_init),
        out_shape=jax.ShapeDtypeStruct((b, s, n_chunks * LANES), BF16),
        grid=(b, n_chunks, s // tq),
        in_specs=in_specs,
        out_specs=pl.BlockSpec((1, tq, LANES), lambda bi, j, i: (bi, i, j)),
        compiler_params=_cparams(("parallel", "parallel", "parallel")),
        name="flash_" + mode + ("_online" if online else ""),
    )(q, k, v, kc, vc, *extra)


def _post_attn_kernel(oa_ref, ob_ref, wa_ref, wb_ref, x_ref, g_ref, ng_ref, sh_ref, sc_ref, x1_ref, h2_ref):
    for rows in _row_blocks(x_ref.shape[1]):
        y = _dot(oa_ref[0, rows, :], wa_ref[...]) + _dot(ob_ref[0, rows, :], wb_ref[...])
        x1 = x_ref[0, rows, :] + g_ref[0] * y
        x1_ref[0, rows, :] = x1
        h2_ref[0, rows, :] = _norm_mod(x1, ng_ref[...], sh_ref[0], sc_ref[0]).astype(BF16)


def _post_attn(oa, ob, wa, wb, x, gate, ng, shift, scale, *, tm):
    b, s, d = x.shape
    row = lambda bi, i: (bi, i, 0)
    vec = lambda bi, i: (bi, 0, 0)
    const = lambda bi, i: (0, 0)
    return pl.pallas_call(
        _post_attn_kernel,
        out_shape=(jax.ShapeDtypeStruct((b, s, d), F32), jax.ShapeDtypeStruct((b, s, d), BF16)),
        grid=(b, s // tm),
        in_specs=[
            pl.BlockSpec((1, tm, 512), row),
            pl.BlockSpec((1, tm, 512), row),
            pl.BlockSpec((512, d), const),
            pl.BlockSpec((512, d), const),
            pl.BlockSpec((1, tm, d), row),
            pl.BlockSpec((1, 1, d), vec),
            pl.BlockSpec((1, d), const),
            pl.BlockSpec((1, 1, d), vec),
            pl.BlockSpec((1, 1, d), vec),
        ],
        out_specs=(pl.BlockSpec((1, tm, d), row), pl.BlockSpec((1, tm, d), row)),
        compiler_params=_cparams(("parallel", "parallel")),
        name="post_attn",
    )(oa, ob, wa, wb, x, gate, ng, shift, scale)


def _ffn_kernel(h_ref, x1_ref, g_ref, wg_ref, wu_ref, wd_ref, o_ref, *, tf):
    h = h_ref[0]
    acc = jnp.zeros(o_ref.shape[1:], F32)
    for f in range(wg_ref.shape[1] // tf):
        sl = slice(f * tf, (f + 1) * tf)
        a = _dot(h, wg_ref[:, sl])
        u = _dot(h, wu_ref[:, sl])
        acc = acc + _dot((_silu(a) * u).astype(BF16), wd_ref[sl, :])
    o_ref[0] = x1_ref[0] + g_ref[0] * acc


def _ffn(h2, x1, gate, wg, wu, wd, *, tm, tf=256):
    b, s, d = x1.shape
    f = wg.shape[1]
    row = lambda bi, i: (bi, i, 0)
    vec = lambda bi, i: (bi, 0, 0)
    const = lambda bi, i: (0, 0)
    return pl.pallas_call(
        functools.partial(_ffn_kernel, tf=tf),
        out_shape=jax.ShapeDtypeStruct((b, s, d), F32),
        grid=(b, s // tm),
        in_specs=[
            pl.BlockSpec((1, tm, d), row),
            pl.BlockSpec((1, tm, d), row),
            pl.BlockSpec((1, 1, d), vec),
            pl.BlockSpec((d, f), const),
            pl.BlockSpec((d, f), const),
            pl.BlockSpec((f, d), const),
        ],
        out_specs=pl.BlockSpec((1, tm, d), row),
        compiler_params=_cparams(("parallel", "parallel")),
        name="ffn",
    )(h2, x1, gate, wg, wu, wd)


def _rms(t, g):
    return t * lax.rsqrt(jnp.mean(t * t, axis=-1, keepdims=True) + NORM_EPS) * g


def _odd_proj_kernel(x_ref, sh_ref, sc_ref, ng_ref, w_ref, wuq_ref, wukv_ref, e_ref, e2_ref, g_ref,
                     cos_ref, sin_ref, cosm_ref, sinm_ref,
                     qc_ref, kc_ref, vc_ref, qd_ref, kd_ref, vd_ref, *, use_rope):
    g = g_ref[...]
    e = e_ref[...]
    e2 = e2_ref[...]
    inv_n = 1.0 / HEAD_DIM
    for rows in _row_blocks(x_ref.shape[1]):
        h = _norm_mod(x_ref[0, rows, :], ng_ref[...], sh_ref[0], sc_ref[0])
        p = _dot(h.astype(BF16), w_ref[...])
        qc = _group_norm(p[:, 0:512], e, g[0:1, 0:512], inv_n)
        kc = _group_norm(p[:, 512:768], e, g[1:2, 0:256], inv_n)
        cq = _rms(p[:, 1024:1280], g[2:3, 0:256])
        ckv = _rms(p[:, 1280:1408], g[3:4, 0:128])
        krc = p[:, 1408:1536]
        qd = _dot(cq.astype(BF16), wuq_ref[...])
        kvv = _dot(ckv.astype(BF16), wukv_ref[...])
        kd = kvv[:, 0:1024] + jnp.concatenate([krc] * MLA_HEADS, axis=1)
        qd = _group_norm(qd, e2, g[4:5], 1.0 / MLA_QK)
        kd = _group_norm(kd, e2, g[5:6], 1.0 / MLA_QK)
        if use_rope:
            lane = lax.broadcasted_iota(jnp.int32, (1, LANES), 1)
            cos, sin = cos_ref[rows, :], sin_ref[rows, :]
            qc = _rope_chunks(qc, cos, sin, (lane & 16) != 0, 16)
            kc = _rope_chunks(kc, cos, sin, (lane & 16) != 0, 16)
            cosm, sinm = cosm_ref[rows, :], sinm_ref[rows, :]
            qd = _rope_chunks(qd, cosm, sinm, (lane & 8) != 0, 8)
            kd = _rope_chunks(kd, cosm, sinm, (lane & 8) != 0, 8)
        qc_ref[0, rows, :] = qc.astype(BF16)
        kc_ref[0, rows, :] = kc.astype(BF16)
        vc_ref[0, rows, :] = p[:, 768:1024].astype(BF16)
        qd_ref[0, rows, :] = qd.astype(BF16)
        kd_ref[0, rows, :] = kd.astype(BF16)
        vd_ref[0, rows, :] = kvv[:, 1024:1536].astype(BF16)


def _odd_proj(x, shift, scale, ng, w, wuq, wukv, e64, e128, gains, cos, sin, cosm, sinm, *, use_rope, tm):
    b, s, d = x.shape
    row = lambda bi, i: (bi, i, 0)
    vec = lambda bi, i: (bi, 0, 0)
    const = lambda bi, i: (0, 0)
    tab = lambda bi, i: (i, 0)
    widths = (512, 256, 256, 1024, 1024, 512)
    return pl.pallas_call(
        functools.partial(_odd_proj_kernel, use_rope=use_rope),
        out_shape=tuple(jax.ShapeDtypeStruct((b, s, n), BF16) for n in widths),
        grid=(b, s // tm),
        in_specs=[
            pl.BlockSpec((1, tm, d), row),
            pl.BlockSpec((1, 1, d), vec),
            pl.BlockSpec((1, 1, d), vec),
            pl.BlockSpec((1, d), const),
            pl.BlockSpec(w.shape, const),
            pl.BlockSpec(wuq.shape, const),
            pl.BlockSpec(wukv.shape, const),
            pl.BlockSpec(e64.shape, const),
            pl.BlockSpec(e128.shape, const),
            pl.BlockSpec(gains.shape, const),
            pl.BlockSpec((tm, LANES), tab),
            pl.BlockSpec((tm, LANES), tab),
            pl.BlockSpec((tm, LANES), tab),
            pl.BlockSpec((tm, LANES), tab),
        ],
        out_specs=tuple(pl.BlockSpec((1, tm, n), row) for n in widths),
        compiler_params=_cparams(("parallel", "parallel")),
        name="odd_proj",
    )(x, shift, scale, ng, w, wuq, wukv, e64, e128, gains, cos, sin, cosm, sinm)


def _mla_rope_tables(seq):
    cos, sin = _rope_tables(seq, MLA_ROPE, MLA_ROPE // 4)
    lane = jnp.arange(LANES)
    on = (lane >= MLA_NOPE) & (lane < MLA_QK)
    return jnp.where(on[None, :], cos, 1.0), jnp.where(on[None, :], sin, 0.0)


def _router_kernel(x1_ref, ng_ref, sh_ref, sc_ref, rt_ref, u_ref, ei_ref, gt_ref, rk_ref, cnt_ref, carry_ref):
    first = (pl.program_id(0) == 0) & (pl.program_id(1) == 0)

    @pl.when(first)
    def _():
        carry_ref[...] = jnp.zeros_like(carry_ref)

    h = _norm_mod(x1_ref[0], ng_ref[...], sh_ref[0], sc_ref[0])
    logits = lax.dot_general(rt_ref[...], h, (((1,), (1,)), ((), ())),
                             preferred_element_type=F32, precision=HIGHEST)
    eidx = lax.broadcasted_iota(jnp.int32, logits.shape, 0)
    m1 = jnp.max(logits, axis=0, keepdims=True)
    i1 = jnp.min(jnp.where(logits == m1, eidx, N_EXPERTS), axis=0, keepdims=True)
    rest = jnp.where(eidx == i1, -jnp.inf, logits)
    m2 = jnp.max(rest, axis=0, keepdims=True)
    i2 = jnp.min(jnp.where(rest == m2, eidx, N_EXPERTS), axis=0, keepdims=True)
    e2 = jnp.exp(m2 - m1)
    g1 = 1.0 / (1.0 + e2)
    sel1 = eidx == i1
    sel2 = eidx == i2
    onehot = jnp.where(sel1 | sel2, 1.0, 0.0)
    before = _dot(onehot.astype(BF16), u_ref[...]) + carry_ref[:, 0:1]
    r1 = jnp.sum(jnp.where(sel1, before, 0.0), axis=0, keepdims=True)
    r2 = jnp.sum(jnp.where(sel2, before, 0.0), axis=0, keepdims=True)
    ei_ref[...] = jnp.concatenate([i1, i2], axis=0)
    gt_ref[...] = jnp.concatenate([g1, e2 * g1], axis=0)
    rk_ref[...] = jnp.concatenate([r1, r2], axis=0).astype(jnp.int32)
    total = carry_ref[...] + jnp.sum(onehot, axis=1, keepdims=True)
    carry_ref[...] = total
    cnt_ref[...] = total.astype(jnp.int32)


def _router(x1, ng, shift, scale, router_t, *, tm):
    b, s, d = x1.shape
    n = b * s
    nt = s // tm
    tri = (jnp.arange(tm)[:, None] < jnp.arange(tm)[None, :]).astype(BF16)
    flat = lambda bi, i: (0, bi * nt + i)
    return pl.pallas_call(
        _router_kernel,
        out_shape=(jax.ShapeDtypeStruct((2, n), jnp.int32), jax.ShapeDtypeStruct((2, n), F32),
                   jax.ShapeDtypeStruct((2, n), jnp.int32), jax.ShapeDtypeStruct((N_EXPERTS, LANES), jnp.int32)),
        grid=(b, nt),
        in_specs=[
            pl.BlockSpec((1, tm, d), lambda bi, i: (bi, i, 0)),
            pl.BlockSpec((1, d), lambda bi, i: (0, 0)),
            pl.BlockSpec((1, 1, d), lambda bi, i: (bi, 0, 0)),
            pl.BlockSpec((1, 1, d), lambda bi, i: (bi, 0, 0)),
            pl.BlockSpec((N_EXPERTS, d), lambda bi, i: (0, 0)),
            pl.BlockSpec((tm, tm), lambda bi, i: (0, 0)),
        ],
        out_specs=(pl.BlockSpec((2, tm), flat), pl.BlockSpec((2, tm), flat), pl.BlockSpec((2, tm), flat),
                   pl.BlockSpec((N_EXPERTS, LANES), lambda bi, i: (0, 0))),
        scratch_shapes=[pltpu.VMEM((N_EXPERTS, LANES), F32)],
        compiler_params=_cparams(("arbitrary", "arbitrary")),
        name="router",
    )(x1, ng, shift, scale, router_t, tri)


MOE_SUB = 256
MOE_BLK = 1024
MOE_TC = 256
MOE_NBUF = 8
MOE_AHEAD = MOE_NBUF - 2


def _lookup(table, idx):
    hit = idx[..., None] == jnp.arange(table.shape[0])
    return jnp.sum(jnp.where(hit, table, 0), axis=-1)


def _moe_plan(ei, rk, counts, n_tok):
    n_chunks = n_tok // MOE_TC
    cap = 2 * n_tok + N_EXPERTS * MOE_BLK
    nb_sub = cap // MOE_SUB
    nb_blk = cap // MOE_BLK
    padded = ((counts + MOE_BLK - 1) // MOE_BLK) * MOE_BLK
    pad_end = jnp.cumsum(padded)
    pad_start = pad_end - padded
    dest = _lookup(pad_start, ei) + rk

    onehot = ei.reshape(2, n_chunks, MOE_TC)[..., None] == jnp.arange(N_EXPERTS)
    cnt = jnp.sum(onehot, axis=(0, 2)).astype(jnp.int32)
    cum = jnp.concatenate([jnp.zeros((1, N_EXPERTS), jnp.int32), jnp.cumsum(cnt, axis=0)], axis=0)

    lo = pad_start[None, :] + cum[:-1]
    has = cnt > 0
    blk0 = lo // MOE_SUB
    two = has & ((lo + cnt - 1) // MOE_SUB > blk0)
    blk1 = jnp.minimum(blk0 + 1, nb_sub - 1)
    win_ok = jnp.stack([has, two], axis=-1).reshape(n_chunks, 2 * N_EXPERTS)
    win_blk = jnp.stack([blk0, blk1], axis=-1).reshape(n_chunks, 2 * N_EXPERTS)
    c_count = jnp.sum(win_ok, axis=1).astype(jnp.int32)
    c_start = (jnp.cumsum(c_count) - c_count).astype(jnp.int32)
    win_at = c_start[:, None] + jnp.cumsum(win_ok, axis=1) - win_ok
    q = jnp.arange(n_chunks * 2 * N_EXPERTS)
    hit = win_ok[None] & (win_at[None] == q[:, None, None])
    c_blocks = jnp.sum(jnp.where(hit, win_blk[None], 0), axis=(1, 2)).astype(jnp.int32)

    sb = jnp.arange(nb_sub)
    e_sb = jnp.minimum(jnp.sum(sb[:, None] * MOE_SUB >= pad_end[None, :], axis=1), N_EXPERTS - 1)
    counts_sb = _lookup(counts, e_sb)
    r0 = sb * MOE_SUB - _lookup(pad_start, e_sb)
    valid_sb = (sb * MOE_SUB < pad_end[-1]) & (r0 < counts_sb)
    r1 = jnp.minimum(r0 + MOE_SUB, counts_sb) - 1
    hit_sb = e_sb[:, None] == jnp.arange(N_EXPERTS)[None, :]
    cum_sb = jnp.sum(jnp.where(hit_sb[:, None, :], cum[None, 1:, :], 0), axis=-1)
    cmin = jnp.sum(cum_sb <= r0[:, None], axis=1)
    cmax = jnp.sum(cum_sb <= r1[:, None], axis=1)
    items = jnp.where(valid_sb, cmax - cmin + 1, 0)
    off_end = jnp.cumsum(items)
    off = off_end - items
    total = off_end[-1]
    w_max = (2 * n_tok) // MOE_SUB + N_EXPERTS + N_EXPERTS * (n_chunks - 1)
    w = jnp.arange(w_max)
    wv = w < total
    wq = jnp.minimum(w, total - 1)
    w_sb = jnp.minimum(jnp.sum(off_end[None, :] <= wq[:, None], axis=1), nb_sub - 1)
    off_w = _lookup(off, w_sb)
    w_chunk = _lookup(cmin, w_sb) + (wq - off_w)
    w_flag = wv.astype(jnp.int32) + 2 * (wq == off_w).astype(jnp.int32)

    bi = jnp.arange(nb_blk)
    e_blk = jnp.minimum(jnp.sum(bi[:, None] * MOE_BLK >= pad_end[None, :], axis=1), N_EXPERTS - 1)
    rows = jnp.where(bi * MOE_BLK < pad_end[-1],
                     _lookup(counts, e_blk) - (bi * MOE_BLK - _lookup(pad_start, e_blk)), 0)
    n_sub = jnp.clip((rows + MOE_SUB - 1) // MOE_SUB, 0, MOE_BLK // MOE_SUB)
    g_count = jnp.sum(items.reshape(nb_blk, MOE_BLK // MOE_SUB), axis=1)
    g_start = jnp.cumsum(g_count) - g_count
    return dict(dest=dest.astype(jnp.int32), cap=cap,
                w_sb=w_sb.astype(jnp.int32), w_chunk=w_chunk.astype(jnp.int32), w_flag=w_flag,
                g_start=g_start.astype(jnp.int32), g_count=g_count.astype(jnp.int32),
                e_blk=e_blk.astype(jnp.int32), n_sub=n_sub.astype(jnp.int32),
                c_blocks=c_blocks, c_count=c_count, c_start=c_start)


def _moe_gather_kernel(st_ref, cn_ref, wsb_ref, wch_ref, t_hbm, d_ref, g_ref, xg_ref, gs_ref, tbuf, sem):
    i = pl.program_id(0)
    n = cn_ref[i]
    s0 = st_ref[i]
    total = st_ref[pl.num_programs(0) - 1] + cn_ref[pl.num_programs(0) - 1]
    subs = MOE_BLK // MOE_SUB

    def chunk_copy(item):
        slot = lax.rem(item, MOE_NBUF)
        start = pl.multiple_of(wch_ref[item] * MOE_TC, MOE_TC)
        return pltpu.make_async_copy(t_hbm.at[pl.ds(start, MOE_TC), :], tbuf.at[slot], sem.at[slot])

    @pl.when(i == 0)
    def _():
        for a in range(MOE_AHEAD):
            @pl.when(a < total)
            def _():
                chunk_copy(a).start()

    xg_ref[...] = jnp.zeros(xg_ref.shape, BF16)
    gs_ref[...] = jnp.zeros(gs_ref.shape, F32)

    def arrive(item):
        chunk_copy(item).wait()

        @pl.when(item + MOE_AHEAD < total)
        def _():
            chunk_copy(item + MOE_AHEAD).start()

    def contribute(item):
        sb = wsb_ref[item]
        chunk = wch_ref[item]
        d = d_ref[chunk]
        g = g_ref[chunk]
        srow = lax.broadcasted_iota(jnp.int32, (MOE_SUB, MOE_TC), 0) + sb * MOE_SUB
        hit0 = srow == d[0:1]
        hit1 = srow == d[1:2]
        p = jnp.where(hit0 | hit1, 1.0, 0.0).astype(BF16)
        rows = _dot(p, tbuf[lax.rem(item, MOE_NBUF)]).astype(BF16)
        gate = jnp.sum(jnp.where(hit0, g[0:1], 0.0) + jnp.where(hit1, g[1:2], 0.0), axis=-1, keepdims=True)
        r0 = pl.multiple_of((sb - i * subs) * MOE_SUB, MOE_SUB)
        xg_ref[pl.ds(r0, MOE_SUB), :] += rows
        gs_ref[pl.ds(r0, MOE_SUB), :] += gate

    def pair(k, carry):
        item = s0 + 2 * k
        arrive(item)
        arrive(item + 1)
        contribute(item)
        contribute(item + 1)
        return carry

    lax.fori_loop(0, n // 2, pair, 0)

    @pl.when(lax.rem(n, 2) == 1)
    def _():
        arrive(s0 + n - 1)
        contribute(s0 + n - 1)


def _moe_gather(t, dest, gates, plan):
    n_tok, d = t.shape
    cap = plan["cap"]
    n_chunks = n_tok // MOE_TC
    by_chunk = lambda a: a.reshape(2, n_chunks, MOE_TC).transpose(1, 0, 2)
    grid_spec = pltpu.PrefetchScalarGridSpec(
        num_scalar_prefetch=4,
        grid=(cap // MOE_BLK,),
        in_specs=[
            pl.BlockSpec(memory_space=pl.ANY),
            pl.BlockSpec((n_chunks, 2, MOE_TC), lambda i, *_: (0, 0, 0)),
            pl.BlockSpec((n_chunks, 2, MOE_TC), lambda i, *_: (0, 0, 0)),
        ],
        out_specs=(pl.BlockSpec((MOE_BLK, d), lambda i, *_: (i, 0)),
                   pl.BlockSpec((MOE_BLK, 1), lambda i, *_: (i, 0))),
        scratch_shapes=[pltpu.VMEM((MOE_NBUF, MOE_TC, d), BF16), pltpu.SemaphoreType.DMA((MOE_NBUF,))],
    )
    return pl.pallas_call(
        _moe_gather_kernel,
        out_shape=(jax.ShapeDtypeStruct((cap, d), BF16), jax.ShapeDtypeStruct((cap, 1), F32)),
        grid_spec=grid_spec,
        compiler_params=_cparams(("arbitrary",)),
        name="moe_gather",
    )(plan["g_start"], plan["g_count"], plan["w_sb"], plan["w_chunk"], t, by_chunk(dest), by_chunk(gates))


def _moe_expert_kernel(eb_ref, ns_ref, x_ref, gs_ref, w1_ref, w3_ref, w2_ref, y_ref):
    n_sub = ns_ref[pl.program_id(0)]
    subs = MOE_BLK // MOE_SUB

    def mlp(rows):
        xs = x_ref[rows, :]
        acc = jnp.zeros((rows.stop - rows.start, y_ref.shape[1]), F32)
        for f in range(w1_ref.shape[1] // MOE_TF):
            cols = slice(f * MOE_TF, (f + 1) * MOE_TF)
            a = _dot(xs, w1_ref[:, cols])
            u = _dot(xs, w3_ref[:, cols])
            acc = acc + _dot((_silu(a) * u).astype(BF16), w2_ref[cols, :])
        y_ref[rows, :] = (acc * gs_ref[rows, :]).astype(BF16)

    @pl.when(n_sub == subs)
    def _():
        mlp(slice(0, MOE_BLK))

    for sub in range(subs):
        rows = slice(sub * MOE_SUB, (sub + 1) * MOE_SUB)

        @pl.when((n_sub < subs) & (sub < n_sub))
        def _():
            mlp(rows)

        @pl.when(sub >= n_sub)
        def _():
            y_ref[rows, :] = jnp.zeros((MOE_SUB, y_ref.shape[1]), BF16)


def _moe_experts(xg, gs, w1, w3, w2, plan):
    cap, d = xg.shape
    f = w1.shape[2]
    once = pl.Buffered(1)
    grid_spec = pltpu.PrefetchScalarGridSpec(
        num_scalar_prefetch=2,
        grid=(cap // MOE_BLK,),
        in_specs=[
            pl.BlockSpec((MOE_BLK, d), lambda i, eb, ns: (i, 0)),
            pl.BlockSpec((MOE_BLK, 1), lambda i, eb, ns: (i, 0)),
            pl.BlockSpec((None, d, f), lambda i, eb, ns: (eb[i], 0, 0), pipeline_mode=once),
            pl.BlockSpec((None, d, f), lambda i, eb, ns: (eb[i], 0, 0), pipeline_mode=once),
            pl.BlockSpec((None, f, d), lambda i, eb, ns: (eb[i], 0, 0), pipeline_mode=once),
        ],
        out_specs=pl.BlockSpec((MOE_BLK, d), lambda i, eb, ns: (i, 0)),
    )
    return pl.pallas_call(
        _moe_expert_kernel,
        out_shape=jax.ShapeDtypeStruct((cap, d), BF16),
        grid_spec=grid_spec,
        compiler_params=_cparams(("arbitrary",)),
        name="moe_experts",
    )(plan["e_blk"], plan["n_sub"], xg, gs, w1, w3, w2)


MOE_WIN = 2 * N_EXPERTS


def _moe_combine_kernel(st_ref, cn_ref, bl_ref, dt_ref, y_hbm, x1_ref, g_ref, o_ref, ybuf, sem, acc_ref):
    c = pl.program_id(0)
    n = cn_ref[c]
    s0 = st_ref[c]
    total = st_ref[pl.num_programs(0) - 1] + cn_ref[pl.num_programs(0) - 1]
    d = dt_ref[...]

    def block_copy(item):
        slot = lax.rem(item, MOE_NBUF)
        start = pl.multiple_of(bl_ref[item] * MOE_SUB, MOE_SUB)
        return pltpu.make_async_copy(y_hbm.at[pl.ds(start, MOE_SUB), :], ybuf.at[slot], sem.at[slot])

    @pl.when(c == 0)
    def _():
        for a in range(MOE_AHEAD):
            @pl.when(a < total)
            def _():
                block_copy(a).start()

    def arrive(item):
        block_copy(item).wait()

        @pl.when(item + MOE_AHEAD < total)
        def _():
            block_copy(item + MOE_AHEAD).start()

    def rows_of(item):
        scol = lax.broadcasted_iota(jnp.int32, (MOE_TC, MOE_SUB), 1) + bl_ref[item] * MOE_SUB
        p = jnp.where((scol == d[:, 0:1]) | (scol == d[:, 1:2]), 1.0, 0.0).astype(BF16)
        return _dot(p, ybuf[lax.rem(item, MOE_NBUF)])

    def pair(k, acc):
        item = s0 + 2 * k
        arrive(item)
        arrive(item + 1)
        return acc + rows_of(item) + rows_of(item + 1)

    acc_ref[...] = lax.fori_loop(0, n // 2, pair, jnp.zeros(o_ref.shape, F32))

    @pl.when(lax.rem(n, 2) == 1)
    def _():
        arrive(s0 + n - 1)
        acc_ref[...] += rows_of(s0 + n - 1)

    o_ref[...] = x1_ref[...] + g_ref[0] * acc_ref[...]


def _moe_combine(y, dest_t, x1, gate, plan, seq):
    n_tok, d = x1.shape
    n_chunks = n_tok // MOE_TC
    per_b = seq // MOE_TC
    grid_spec = pltpu.PrefetchScalarGridSpec(
        num_scalar_prefetch=3,
        grid=(n_chunks,),
        in_specs=[
            pl.BlockSpec((MOE_TC, 2), lambda c, *_: (c, 0)),
            pl.BlockSpec(memory_space=pl.ANY),
            pl.BlockSpec((MOE_TC, d), lambda c, *_: (c, 0)),
            pl.BlockSpec((1, 1, d), lambda c, *_: (c // per_b, 0, 0)),
        ],
        out_specs=pl.BlockSpec((MOE_TC, d), lambda c, *_: (c, 0)),
        scratch_shapes=[pltpu.VMEM((MOE_NBUF, MOE_SUB, d), BF16), pltpu.SemaphoreType.DMA((MOE_NBUF,)),
                        pltpu.VMEM((MOE_TC, d), F32)],
    )
    return pl.pallas_call(
        _moe_combine_kernel,
        out_shape=jax.ShapeDtypeStruct((n_tok, d), F32),
        grid_spec=grid_spec,
        compiler_params=_cparams(("arbitrary",)),
        name="moe_combine",
    )(plan["c_start"], plan["c_count"], plan["c_blocks"], dest_t, y, x1, gate)


def _split_mod(mod_l, batch):
    d = D_MODEL
    lat = tuple(mod_l[:batch, k * d:(k + 1) * d][:, None, :] for k in range(ADA_CHUNKS))
    ctx = tuple(jnp.broadcast_to(mod_l[batch:batch + 1, k * d:(k + 1) * d][:, None, :], (batch, 1, d))
                for k in range(ADA_CHUNKS))
    return lat, ctx


def _score_bound(q_gain, k_gain, n):
    return 1.02 * n * jnp.max(jnp.abs(q_gain)) * jnp.max(jnp.abs(k_gain))


def _block_ones(n, block):
    idx = jnp.arange(n) // block
    return (idx[:, None] == idx[None, :]).astype(BF16)


def _even_layer(x, xc, mod_l, norm1_g, norm2_g, w_in, w_out, na_q_g, na_k_g, na_rpb, diff_q_g, diff_k_g,
                lq1, lk1, lq2, lk2, subln_g, wg, wu, wd, layer_idx, need_ctx):
    b, s, d = x.shape
    (sh1, sc1, g1, sh2, sc2, g2), (csh1, csc1, cg1, csh2, csc2, cg2) = _split_mod(mod_l, b)
    qscale = HEAD_DIM ** -0.5 * LOG2E
    lam_init = 0.8 - 0.6 * math.exp(-0.3 * layer_idx)
    hg = jnp.stack([jnp.tile(na_q_g, 8) * qscale, jnp.tile(na_k_g, 8),
                    jnp.tile(diff_q_g, 8) * qscale, jnp.tile(diff_k_g, 8)]).astype(F32)
    e64 = _block_ones(MXU_DIM, HEAD_DIM)
    cos, sin = _rope_tables(s, HEAD_DIM, 16)
    ng1 = norm1_g[None, :]
    ng2 = norm2_g[None, :]
    w_in_b = w_in.astype(BF16)
    wo_a = w_out[:512].astype(BF16)
    wo_b = w_out[512:].astype(BF16)
    wg_b, wu_b, wd_b = wg.astype(BF16), wu.astype(BF16), wd.astype(BF16)
    lam_p = jnp.stack([lq1, lk1, lq2, lk2]).astype(F32)
    sg = subln_g[None, :].astype(F32)
    ident = lambda j: j

    tm = min(ROW_TM, s)
    tc = xc.shape[1]
    qa, ka, va, qb, kb, vb = _even_proj(x, sh1, sc1, ng1, w_in_b, e64, hg, cos, sin, use_rope=True, tm=tm)
    qca, kca, vca, qcb, kcb, vcb = _even_proj(xc, csh1, csc1, ng1, w_in_b, e64, hg, cos, sin, use_rope=False, tm=tc)
    out_a = _na_attention(qa, ka, va, kca, vca, _na_bias_table(na_rpb), _score_bound(hg[0], hg[1], HEAD_DIM))
    out_b = _flash(qb, kb, vb, kcb, vcb, mode="diff", kv_map=ident, n_chunks=4, tq=FLASH_TQ, tk=FLASH_TK,
                   extra=(lam_p, sg), lam_init=lam_init, score_bound=_score_bound(hg[2], hg[3], HEAD_DIM))
    x1, h2 = _post_attn(out_a, out_b, wo_a, wo_b, x, g1, ng2, sh2, sc2, tm=tm)
    x2 = _ffn(h2, x1, g2, wg_b, wu_b, wd_b, tm=tm)
    if not need_ctx:
        return x2, None
    oca = _flash(qca, None, None, kca, vca, mode="pair", kv_map=ident, n_chunks=4, tq=tc, tk=tc)
    ocb = _flash(qcb, None, None, kcb, vcb, mode="diff", kv_map=ident, n_chunks=4, tq=tc, tk=tc,
                 extra=(lam_p, sg), lam_init=lam_init)
    xc1, hc2 = _post_attn(oca, ocb, wo_a, wo_b, xc, cg1, ng2, csh2, csc2, tm=tc)
    xc2 = _ffn(hc2, xc1, cg2, wg_b, wu_b, wd_b, tm=tc)
    return x2, xc2


def _odd_layer(x, xc, mod_l, norm1_g, norm2_g, w_in, w_out, gqa_q_g, gqa_k_g, cq_g, w_uq, ckv_g, w_ukv,
               mla_q_g, mla_k_g, router, w1, w3, w2):
    b, s, d = x.shape
    n_tok = b * s
    (sh1, sc1, g1, sh2, sc2, g2), (csh1, csc1, _, _, _, _) = _split_mod(mod_l, b)
    ng1 = norm1_g[None, :]
    ng2 = norm2_g[None, :]

    z = lambda n: jnp.zeros((d, n), w_in.dtype)
    k0, k1, v0, v1 = (w_in[:, 512 + 64 * i:576 + 64 * i] for i in range(4))
    w_p = jnp.concatenate([w_in[:, 0:512], k0, k0, k1, k1, v0, v0, v1, v1, w_in[:, 768:1024], w_in[:, 1024:1152],
                           z(MLA_NOPE), w_in[:, 1152:1184], z(LANES - MLA_QK)], axis=1).astype(BF16)
    wuq_p = jnp.pad(w_uq.reshape(MLA_Q_RANK, MLA_HEADS, MLA_QK), ((0, 0), (0, 0), (0, LANES - MLA_QK)))
    wuq_p = wuq_p.reshape(MLA_Q_RANK, MLA_HEADS * LANES).astype(BF16)
    ukv = w_ukv.reshape(MLA_KV_RANK, MLA_HEADS, MLA_NOPE + MLA_V)
    uk = jnp.pad(ukv[:, :, :MLA_NOPE], ((0, 0), (0, 0), (0, LANES - MLA_NOPE))).reshape(MLA_KV_RANK, -1)
    uv = ukv[:, :, MLA_NOPE:].reshape(MLA_KV_RANK, -1)
    wukv_p = jnp.concatenate([uk, uv], axis=1).astype(BF16)
    e64 = _block_ones(MXU_DIM, HEAD_DIM)
    e128 = _block_ones(MXU_DIM, LANES)
    qscale = HEAD_DIM ** -0.5 * LOG2E
    mscale = MLA_QK ** -0.5 * LOG2E
    pad_row = lambda v: jnp.pad(v, (0, MLA_HEADS * LANES - v.shape[0]))
    pad_head = lambda v: jnp.tile(jnp.pad(v, (0, LANES - MLA_QK)), MLA_HEADS)
    gains = jnp.stack([pad_row(jnp.tile(gqa_q_g, 8) * qscale), pad_row(jnp.tile(gqa_k_g, 4)),
                       pad_row(cq_g), pad_row(ckv_g), pad_head(mla_q_g) * mscale, pad_head(mla_k_g),
                       jnp.zeros((MLA_HEADS * LANES,), F32), jnp.zeros((MLA_HEADS * LANES,), F32)]).astype(F32)
    cos, sin = _rope_tables(s, HEAD_DIM, 16)
    cosm, sinm = _mla_rope_tables(s)
    wo_a = w_out[:512].astype(BF16)
    wo_b = w_out[512:].astype(BF16)

    tm = min(ROW_TM, s)
    tc = xc.shape[1]
    proj = functools.partial(_odd_proj, ng=ng1, w=w_p, wuq=wuq_p, wukv=wukv_p, e64=e64, e128=e128, gains=gains,
                             cos=cos, sin=sin, cosm=cosm, sinm=sinm)
    qc, kc, vc, qd, kd, vd = proj(x, sh1, sc1, use_rope=True, tm=tm)
    _, kcc, vcc, _, kcd, vcd = proj(xc, csh1, csc1, use_rope=False, tm=tc)
    out_c = _flash(qc, kc, vc, kcc, vcc, mode="pair", kv_map=lambda j: j // 2, n_chunks=4, tq=FLASH_TQ, tk=FLASH_TK,
                   score_bound=_score_bound(gains[0], gains[1], HEAD_DIM))
    out_d = _flash(qd, kd, vd, kcd, vcd, mode="mla", kv_map=lambda j: j, n_chunks=4, tq=FLASH_TQ, tk=FLASH_TK,
                   score_bound=_score_bound(gains[4], gains[5], MLA_QK))
    x1, h2 = _post_attn(out_c, out_d, wo_a, wo_b, x, g1, ng2, sh2, sc2, tm=tm)

    ei, gt, rk, cnt = _router(x1, ng2, sh2, sc2, router.T.astype(F32), tm=tm)
    plan = _moe_plan(ei, rk, cnt[:, 0], n_tok)
    xg, gs = _moe_gather(h2.reshape(n_tok, d), plan["dest"], gt, plan)
    y = _moe_experts(xg, gs, w1.astype(BF16), w3.astype(BF16), w2.astype(BF16), plan)
    out = _moe_combine(y, plan["dest"].T, x1.reshape(n_tok, d), g2, plan, s)
    return out.reshape(b, s, d)


def _mod_vectors(c, c_ctx, ada_w, ada_b):
    b = c.shape[0]
    cc = jnp.zeros((MOD_ROWS, D_MODEL), F32).at[:b].set(c).at[b].set(c_ctx)
    return _modvec(cc, ada_w, ada_b[:, None, :])


def kernel(x, c, ctx, c_ctx, ada_w, ada_b, norm1_g, norm2_g, ev_w_in, ev_w_out, na_q_g, na_k_g, na_rpb,
           diff_q_g, diff_k_g, diff_lq1, diff_lk1, diff_lq2, diff_lk2, diff_subln_g,
           ffn_w_gate, ffn_w_up, ffn_w_down, od_w_in, od_w_out, gqa_q_g, gqa_k_g, mla_cq_g, mla_w_uq,
           mla_ckv_g, mla_w_ukv, mla_q_g, mla_k_g, moe_router, moe_w1, moe_w3, moe_w2):
    mod = _mod_vectors(c, c_ctx, ada_w, ada_b)
    x, xc = _even_layer(x, ctx, mod[0], norm1_g[0], norm2_g[0], ev_w_in[0], ev_w_out[0], na_q_g[0], na_k_g[0],
                        na_rpb[0], diff_q_g[0], diff_k_g[0], diff_lq1[0], diff_lk1[0], diff_lq2[0], diff_lk2[0],
                        diff_subln_g[0], ffn_w_gate[0], ffn_w_up[0], ffn_w_down[0], 0, True)
    return _odd_layer(x, xc, mod[1], norm1_g[1], norm2_g[1], od_w_in[0], od_w_out[0], gqa_q_g[0], gqa_k_g[0],
                      mla_cq_g[0], mla_w_uq[0], mla_ckv_g[0], mla_w_ukv[0], mla_q_g[0], mla_k_g[0],
                      moe_router[0], moe_w1[0], moe_w3[0], moe_w2[0])
```

```python
import functools
import math

import jax
import jax.numpy as jnp
from jax import lax
from jax.experimental import pallas as pl
from jax.experimental.pallas import tpu as pltpu

F32 = jnp.float32
BF16 = jnp.bfloat16
HIGHEST = lax.Precision.HIGHEST

D_MODEL = 1024
GRID_W = 64
HEAD_DIM = 64
ROPE_THETA = 10000.0
NORM_EPS = 1e-6
NEG_INF = -1e30
ADA_CHUNKS = 6
LOG2E = 1.4426950408889634

NA_HEADS = 8
NA_WIN_H = 8
NA_WIN_W = 16
DIFF_HEADS = 4
DIFF_V_DIM = 2 * HEAD_DIM
GQA_Q_HEADS = 8
GQA_KV_HEADS = 2
MLA_HEADS = 8
MLA_NOPE = 64
MLA_ROPE = 32
MLA_QK = MLA_NOPE + MLA_ROPE
MLA_V = 64
MLA_Q_RANK = 256
MLA_KV_RANK = 128
D_FF = 2816
N_EXPERTS = 8
D_FF_EXPERT = 3584

LANES = 128
MXU_DIM = 256
VMEM_LIMIT = 56 * 1024 * 1024
MOD_ROWS = 16
MOE_TF = 512
SUM_ROWS = 16
ROW_TM = 1024
FLASH_TQ = 1024
FLASH_TK = 512
EXP2_SAFE_RANGE = 64.0


def _cparams(sem):
    return pltpu.CompilerParams(dimension_semantics=sem, vmem_limit_bytes=VMEM_LIMIT)


def _dot(a, b):
    return jnp.dot(a, b, preferred_element_type=F32)


def _dot_nt(a, b):
    return lax.dot_general(a, b, (((1,), (1,)), ((), ())), preferred_element_type=F32)


def _silu(x):
    return x * (1.0 / (1.0 + jnp.exp(-x)))


def _modvec_kernel(c_ref, w_ref, b_ref, o_ref):
    s = _silu(c_ref[...])
    o_ref[...] = jnp.dot(s, w_ref[...], preferred_element_type=F32, precision=HIGHEST) + b_ref[...]


def _modvec(cc, ada_w, ada_b):
    n_layers, d, n = ada_w.shape
    tn = 768
    return pl.pallas_call(
        _modvec_kernel,
        out_shape=jax.ShapeDtypeStruct((n_layers, MOD_ROWS, n), F32),
        grid=(n_layers, n // tn),
        in_specs=[
            pl.BlockSpec((MOD_ROWS, d), lambda l, j: (0, 0)),
            pl.BlockSpec((None, d, tn), lambda l, j: (l, 0, j)),
            pl.BlockSpec((None, 1, tn), lambda l, j: (l, 0, j)),
        ],
        out_specs=pl.BlockSpec((None, MOD_ROWS, tn), lambda l, j: (l, 0, j)),
        compiler_params=_cparams(("arbitrary", "arbitrary")),
        name="modvec",
    )(cc, ada_w, ada_b)


def _norm_mod(x, g, shift, scale):
    ms = jnp.mean(x * x, axis=-1, keepdims=True)
    return (x * lax.rsqrt(ms + NORM_EPS) * g) * (1.0 + scale) + shift


def _row_blocks(n, size=256):
    size = min(size, n)
    return [slice(r, r + size) for r in range(0, n, size)]


def _group_norm(t, e, g, inv_n):
    outs = []
    for c in range(t.shape[1] // MXU_DIM):
        cols = slice(c * MXU_DIM, (c + 1) * MXU_DIM)
        tc = t[:, cols]
        ss = _dot((tc * tc).astype(BF16), e)
        outs.append(tc * lax.rsqrt(ss * inv_n + NORM_EPS) * g[:, cols])
    return outs[0] if len(outs) == 1 else jnp.concatenate(outs, axis=1)


def _rope_chunks(t, cos, sin, lane_hi, shift):
    outs = []
    for c in range(t.shape[1] // LANES):
        tc = t[:, c * LANES:(c + 1) * LANES]
        up = pltpu.roll(tc, LANES - shift, 1)
        dn = pltpu.roll(tc, shift, 1)
        outs.append(tc * cos + jnp.where(lane_hi, dn, up) * sin)
    return outs[0] if len(outs) == 1 else jnp.concatenate(outs, axis=1)


def _rope_tables(seq, unit, pair_shift):
    pos = jnp.arange(seq)
    rows, cols = pos // GRID_W, pos % GRID_W
    quarter = unit // 4
    assert quarter == pair_shift
    freqs = ROPE_THETA ** (-jnp.arange(quarter, dtype=F32) / quarter)
    lane = jnp.arange(LANES)
    u = lane % unit
    use_col = (u // (unit // 2)) == 1
    fi = u % quarter
    p = jnp.where(use_col[None, :], cols[:, None], rows[:, None]).astype(F32)
    ang = p * freqs[fi][None, :]
    second = ((u % (unit // 2)) // quarter) == 1
    cos = jnp.cos(ang)
    sin = jnp.where(second[None, :], jnp.sin(ang), -jnp.sin(ang))
    return cos, sin


def _even_proj_kernel(x_ref, sh_ref, sc_ref, ng_ref, w_ref, e_ref, hg_ref, cos_ref, sin_ref,
                      qa_ref, ka_ref, va_ref, qb_ref, kb_ref, vb_ref, *, use_rope):
    hg = hg_ref[...]
    e = e_ref[...]
    inv_n = 1.0 / HEAD_DIM
    for rows in _row_blocks(x_ref.shape[1]):
        h = _norm_mod(x_ref[0, rows, :], ng_ref[...], sh_ref[0], sc_ref[0])
        p = _dot(h.astype(BF16), w_ref[...])
        qa = _group_norm(p[:, 0:512], e, hg[0:1], inv_n)
        ka = _group_norm(p[:, 512:1024], e, hg[1:2], inv_n)
        qb = _group_norm(p[:, 1536:2048], e, hg[2:3], inv_n)
        kb = _group_norm(p[:, 2048:2560], e, hg[3:4], inv_n)
        if use_rope:
            lane = lax.broadcasted_iota(jnp.int32, (1, LANES), 1)
            lane_hi = (lane & 16) != 0
            cos, sin = cos_ref[rows, :], sin_ref[rows, :]
            qb = _rope_chunks(qb, cos, sin, lane_hi, 16)
            kb = _rope_chunks(kb, cos, sin, lane_hi, 16)
        qa_ref[0, rows, :] = qa.astype(BF16)
        ka_ref[0, rows, :] = ka.astype(BF16)
        va_ref[0, rows, :] = p[:, 1024:1536].astype(BF16)
        qb_ref[0, rows, :] = qb.astype(BF16)
        kb_ref[0, rows, :] = kb.astype(BF16)
        vb_ref[0, rows, :] = p[:, 2560:3072].astype(BF16)


def _even_proj(x, shift, scale, ng, w, e64, hg, cos, sin, *, use_rope, tm):
    b, s, d = x.shape
    n = w.shape[1]
    row = lambda bi, i: (bi, i, 0)
    vec = lambda bi, i: (bi, 0, 0)
    const = lambda bi, i: (0, 0)
    out = jax.ShapeDtypeStruct((b, s, 512), BF16)
    return pl.pallas_call(
        functools.partial(_even_proj_kernel, use_rope=use_rope),
        out_shape=(out,) * 6,
        grid=(b, s // tm),
        in_specs=[
            pl.BlockSpec((1, tm, d), row),
            pl.BlockSpec((1, 1, d), vec),
            pl.BlockSpec((1, 1, d), vec),
            pl.BlockSpec((1, d), const),
            pl.BlockSpec((d, n), const),
            pl.BlockSpec(e64.shape, const),
            pl.BlockSpec((4, 512), const),
            pl.BlockSpec((tm, LANES), lambda bi, i: (i, 0)),
            pl.BlockSpec((tm, LANES), lambda bi, i: (i, 0)),
        ],
        out_specs=(pl.BlockSpec((1, tm, 512), row),) * 6,
        compiler_params=_cparams(("parallel", "parallel")),
        name="even_proj",
    )(x, shift, scale, ng, w, e64, hg, cos, sin)


def _pair_stack(q):
    lane = lax.broadcasted_iota(jnp.int32, q.shape, 1)
    zero = jnp.zeros_like(q)
    return jnp.concatenate([jnp.where(lane < HEAD_DIM, q, zero), jnp.where(lane >= HEAD_DIM, q, zero)], axis=0)


def _pair_merge(o, tq):
    lane = lax.broadcasted_iota(jnp.int32, (tq, LANES), 1)
    return jnp.where(lane < HEAD_DIM, o[:tq], o[tq:])


def _with_ones(v):
    return jnp.concatenate([v, jnp.ones(v.shape, BF16)], axis=1)


def _na_kernel(q_ref, k_ref, v_ref, kc_ref, vc_ref, bias_ref, o_ref, *, rows_per_step, n_rows, direct):
    rb = pl.program_id(2)
    kc = kc_ref[0]
    vc = vc_ref[0]
    n_lat = NA_WIN_H * GRID_W
    if direct:
        nq = rows_per_step * GRID_W
        qs_all = _pair_stack(q_ref[0])
        o_ctx = _dot(jnp.exp2(_dot_nt(qs_all, kc)).astype(BF16), _with_ones(vc))
        for i in range(rows_per_step):
            r = rb * rows_per_step + i
            r0 = jnp.clip(r - NA_WIN_H // 2, 0, n_rows - NA_WIN_H)
            start = pl.multiple_of(r0 * GRID_W, GRID_W)
            kw = k_ref[0, pl.ds(start, n_lat), :]
            vw = v_ref[0, pl.ds(start, n_lat), :]
            lo = slice(i * GRID_W, (i + 1) * GRID_W)
            hi = slice(nq + i * GRID_W, nq + (i + 1) * GRID_W)
            qs = jnp.concatenate([qs_all[lo], qs_all[hi]], axis=0)
            p = jnp.exp2(_dot_nt(qs, kw) + bias_ref[r - r0]).astype(BF16)
            tot = _dot(p, _with_ones(vw)) + jnp.concatenate([o_ctx[lo], o_ctx[hi]], axis=0)
            o = tot[:, :LANES] / tot[:, LANES:]
            o_ref[0, lo, :] = _pair_merge(o, GRID_W).astype(BF16)
        return
    for i in range(rows_per_step):
        r = rb * rows_per_step + i
        r0 = jnp.clip(r - NA_WIN_H // 2, 0, n_rows - NA_WIN_H)
        start = pl.multiple_of(r0 * GRID_W, GRID_W)
        kw = k_ref[0, pl.ds(start, n_lat), :]
        vw = v_ref[0, pl.ds(start, n_lat), :]
        qs = _pair_stack(q_ref[0, i * GRID_W:(i + 1) * GRID_W, :])
        s_lat = _dot_nt(qs, kw) + bias_ref[r - r0]
        s_ctx = _dot_nt(qs, kc)
        m = jnp.maximum(jnp.max(s_lat, axis=-1, keepdims=True), jnp.max(s_ctx, axis=-1, keepdims=True))
        p_lat = jnp.exp2(s_lat - m)
        p_ctx = jnp.exp2(s_ctx - m)
        l = jnp.sum(p_lat, axis=-1, keepdims=True) + jnp.sum(p_ctx, axis=-1, keepdims=True)
        o = _dot(p_lat.astype(BF16), vw) + _dot(p_ctx.astype(BF16), vc)
        o = o / l
        o_ref[0, i * GRID_W:(i + 1) * GRID_W, :] = _pair_merge(o, GRID_W).astype(BF16)


def _na_attention(q, k, v, kc, vc, bias, score_bound, *, rows_per_step=16):
    bound = score_bound + jnp.max(jnp.where(bias > 0.5 * NEG_INF, jnp.abs(bias), 0.0))
    return lax.cond(bound <= EXP2_SAFE_RANGE,
                    lambda: _na_call(q, k, v, kc, vc, bias, rows_per_step=rows_per_step, direct=True),
                    lambda: _na_call(q, k, v, kc, vc, bias, rows_per_step=rows_per_step, direct=False))


def _na_call(q, k, v, kc, vc, bias, *, rows_per_step, direct):
    b, s, _ = q.shape
    c = kc.shape[1]
    n_rows = s // GRID_W
    tq = rows_per_step * GRID_W
    return pl.pallas_call(
        functools.partial(_na_kernel, rows_per_step=rows_per_step, n_rows=n_rows, direct=direct),
        out_shape=jax.ShapeDtypeStruct((b, s, 512), BF16),
        grid=(b, 4, n_rows // rows_per_step),
        in_specs=[
            pl.BlockSpec((1, tq, LANES), lambda bi, j, i: (bi, i, j)),
            pl.BlockSpec((1, s, LANES), lambda bi, j, i: (bi, 0, j)),
            pl.BlockSpec((1, s, LANES), lambda bi, j, i: (bi, 0, j)),
            pl.BlockSpec((1, c, LANES), lambda bi, j, i: (bi, 0, j)),
            pl.BlockSpec((1, c, LANES), lambda bi, j, i: (bi, 0, j)),
            pl.BlockSpec((NA_WIN_H, None, LANES, NA_WIN_H * GRID_W), lambda bi, j, i: (0, j, 0, 0)),
        ],
        out_specs=pl.BlockSpec((1, tq, LANES), lambda bi, j, i: (bi, i, j)),
        compiler_params=_cparams(("parallel", "parallel", "parallel")),
        name="na_attention" + ("" if direct else "_online"),
    )(q, k, v, kc, vc, bias)


def _na_bias_table(rpb):
    w = GRID_W
    col = jnp.arange(w)
    c0 = jnp.clip(col - NA_WIN_W // 2, 0, w - NA_WIN_W)
    col_in = (col[None, :] >= c0[:, None]) & (col[None, :] < c0[:, None] + NA_WIN_W)
    left = (w - 1) - (NA_WIN_W - 1)
    ext = jnp.pad(rpb, ((0, 0), (0, 0), (left, 2 * w - left - (2 * NA_WIN_W - 1))), mode="edge")
    h, nr, _ = rpb.shape
    flat = jnp.broadcast_to(ext[:, :, None, :], (h, nr, w, 2 * w)).reshape(h, nr, w * 2 * w)
    toep = flat[:, :, :w * (2 * w - 1)].reshape(h, nr, w, 2 * w - 1)[:, :, :, w - 1:]
    toep = jnp.where(col_in[None, None], toep * LOG2E, NEG_INF)
    variants = []
    for v in range(NA_WIN_H):
        tv = toep[:, NA_WIN_H - 1 - v:2 * NA_WIN_H - 1 - v]
        variants.append(tv.transpose(0, 2, 1, 3).reshape(NA_HEADS // 2, 2 * w, NA_WIN_H * w))
    return jnp.stack(variants, axis=0).astype(F32)


def _flash_kernel(*refs, mode, online, tq, tk, s_len, c_len, lam_init):
    if mode == "diff":
        q_ref, k_ref, v_ref, kc_ref, vc_ref, lam_ref, sg_ref, o_ref = refs
    else:
        q_ref, k_ref, v_ref, kc_ref, vc_ref, o_ref = refs

    q = q_ref[0]
    if mode == "mla":
        q_parts = (q[:, :LANES], q[:, LANES:])
    else:
        qs = _pair_stack(q)

    def scores(kt):
        if mode == "mla":
            return jnp.concatenate([_dot_nt(q_parts[0], kt[:, :LANES]), _dot_nt(q_parts[1], kt[:, LANES:])], axis=0)
        return _dot_nt(qs, kt)

    def step(kt, vt, carry):
        m, l, acc = carry
        s = scores(kt)
        m_new = jnp.maximum(m, jnp.max(s, axis=-1, keepdims=True))
        alpha = jnp.exp2(m - m_new)
        p = jnp.exp2(s - m_new)
        l = alpha * l + jnp.sum(p, axis=-1, keepdims=True)
        acc = alpha * acc + _dot(p.astype(BF16), vt)
        return m_new, l, acc

    def body(t, carry):
        start = pl.multiple_of(t * tk, tk)
        return step(k_ref[0, pl.ds(start, tk), :], v_ref[0, pl.ds(start, tk), :], carry)

    def scores_t(kt):
        if mode == "mla":
            return jnp.concatenate([_dot_nt(kt[:, :LANES], q_parts[0]), _dot_nt(kt[:, LANES:], q_parts[1])], axis=1)
        return _dot_nt(kt, qs)

    def direct(kt, vtt, acc):
        p = jnp.exp2(scores_t(kt)).astype(BF16)
        v1 = jnp.concatenate([vtt, jnp.ones((SUM_ROWS, vtt.shape[1]), BF16)], axis=0)
        return acc + _dot(v1, p)

    if online:
        carry = (jnp.full((2 * tq, 1), -jnp.inf, F32), jnp.zeros((2 * tq, 1), F32),
                 jnp.zeros((2 * tq, LANES), F32))
        if s_len:
            carry = lax.fori_loop(0, s_len // tk, body, carry)
        if c_len:
            carry = step(kc_ref[0], vc_ref[0], carry)
        _, l, acc = carry
        o = acc / l
    else:
        acc = jnp.zeros((LANES + SUM_ROWS, 2 * tq), F32)
        for t in range(s_len // tk):
            acc = direct(k_ref[0, t * tk:(t + 1) * tk, :], v_ref[0, :, t * tk:(t + 1) * tk], acc)
        if c_len:
            acc = direct(kc_ref[0], vc_ref[0], acc)
        o_t = acc[:LANES] / acc[LANES:LANES + 1]
    if mode == "diff":
        lp = lam_ref[...]
        lam = (jnp.exp(jnp.sum(lp[0:1] * lp[1:2], axis=-1, keepdims=True))
               - jnp.exp(jnp.sum(lp[2:3] * lp[3:4], axis=-1, keepdims=True)) + lam_init)
        if online:
            d = o[:tq] - lam * o[tq:]
            dn = d * lax.rsqrt(jnp.mean(d * d, axis=-1, keepdims=True) + NORM_EPS)
        else:
            d_t = o_t[:, :tq] - lam * o_t[:, tq:]
            dn = (d_t * lax.rsqrt(jnp.mean(d_t * d_t, axis=0, keepdims=True) + NORM_EPS)).T
        o_ref[0] = (dn * sg_ref[...] * (1.0 - lam_init)).astype(BF16)
    elif online:
        o_ref[0] = _pair_merge(o, tq).astype(BF16)
    else:
        row = lax.broadcasted_iota(jnp.int32, (LANES, tq), 0)
        o_ref[0] = jnp.where(row < HEAD_DIM, o_t[:, :tq], o_t[:, tq:]).T.astype(BF16)


def _flash(q, k, v, kc, vc, *, score_bound=None, **kw):
    if score_bound is None:
        return _flash_call(q, k, v, kc, vc, online=True, **kw)
    return lax.cond(score_bound <= EXP2_SAFE_RANGE,
                    lambda: _flash_call(q, k, jnp.swapaxes(v, 1, 2), kc, jnp.swapaxes(vc, 1, 2), online=False, **kw),
                    lambda: _flash_call(q, k, v, kc, vc, online=True, **kw))


def _flash_call(q, k, v, kc, vc, *, mode, online, kv_map, n_chunks, tq, tk, extra=(), lam_init=0.0):
    b, s, _ = q.shape
    qw = 2 * LANES if mode == "mla" else LANES
    c_len = kc.shape[1]
    if k is None:
        k, v, s_len = kc, vc, 0
    else:
        s_len = k.shape[1]
    kk = k.shape[1]
    if online:
        v_spec = lambda n: pl.BlockSpec((1, n, LANES), lambda bi, j, i: (bi, 0, kv_map(j)))
    else:
        v_spec = lambda n: pl.BlockSpec((1, LANES, n), lambda bi, j, i: (bi, kv_map(j), 0))
    in_specs = [
        pl.BlockSpec((1, tq, qw), lambda bi, j, i: (bi, i, j)),
        pl.BlockSpec((1, kk, qw), lambda bi, j, i: (bi, 0, kv_map(j) if mode != "mla" else j)),
        v_spec(kk),
        pl.BlockSpec((1, c_len, qw), lambda bi, j, i: (bi, 0, kv_map(j) if mode != "mla" else j)),
        v_spec(c_len),
    ]
    for e in extra:
        in_specs.append(pl.BlockSpec(e.shape, lambda bi, j, i: (0, 0)))
    return pl.pallas_call(
        functools.partial(_flash_kernel, mode=mode, online=online, tq=tq, tk=tk, s_len=s_len, c_len=c_len,
                          lam_init=lam_init),
        out_shape=jax.ShapeDtypeStruct((b, s, n_chunks * LANES), BF16),
        grid=(b, n_chunks, s // tq),
        in_specs=in_specs,
        out_specs=pl.BlockSpec((1, tq, LANES), lambda bi, j, i: (bi, i, j)),
        compiler_params=_cparams(("parallel", "parallel", "parallel")),
        name="flash_" + mode + ("_online" if online else ""),
    )(q, k, v, kc, vc, *extra)


def _post_attn_kernel(oa_ref, ob_ref, wa_ref, wb_ref, x_ref, g_ref, ng_ref, sh_ref, sc_ref, x1_ref, h2_ref):
    for rows in _row_blocks(x_ref.shape[1]):
        y = _dot(oa_ref[0, rows, :], wa_ref[...]) + _dot(ob_ref[0, rows, :], wb_ref[...])
        x1 = x_ref[0, rows, :] + g_ref[0] * y
        x1_ref[0, rows, :] = x1
        h2_ref[0, rows, :] = _norm_mod(x1, ng_ref[...], sh_ref[0], sc_ref[0]).astype(BF16)


def _post_attn(oa, ob, wa, wb, x, gate, ng, shift, scale, *, tm):
    b, s, d = x.shape
    row = lambda bi, i: (bi, i, 0)
    vec = lambda bi, i: (bi, 0, 0)
    const = lambda bi, i: (0, 0)
    return pl.pallas_call(
        _post_attn_kernel,
        out_shape=(jax.ShapeDtypeStruct((b, s, d), F32), jax.ShapeDtypeStruct((b, s, d), BF16)),
        grid=(b, s // tm),
        in_specs=[
            pl.BlockSpec((1, tm, 512), row),
            pl.BlockSpec((1, tm, 512), row),
            pl.BlockSpec((512, d), const),
            pl.BlockSpec((512, d), const),
            pl.BlockSpec((1, tm, d), row),
            pl.BlockSpec((1, 1, d), vec),
            pl.BlockSpec((1, d), const),
            pl.BlockSpec((1, 1, d), vec),
            pl.BlockSpec((1, 1, d), vec),
        ],
        out_specs=(pl.BlockSpec((1, tm, d), row), pl.BlockSpec((1, tm, d), row)),
        compiler_params=_cparams(("parallel", "parallel")),
        name="post_attn",
    )(oa, ob, wa, wb, x, gate, ng, shift, scale)


def _ffn_kernel(h_ref, x1_ref, g_ref, wg_ref, wu_ref, wd_ref, o_ref, *, tf):
    h = h_ref[0]
    acc = jnp.zeros(o_ref.shape[1:], F32)
    for f in range(wg_ref.shape[1] // tf):
        sl = slice(f * tf, (f + 1) * tf)
        a = _dot(h, wg_ref[:, sl])
        u = _dot(h, wu_ref[:, sl])
        acc = acc + _dot((_silu(a) * u).astype(BF16), wd_ref[sl, :])
    o_ref[0] = x1_ref[0] + g_ref[0] * acc


def _ffn(h2, x1, gate, wg, wu, wd, *, tm, tf=256):
    b, s, d = x1.shape
    f = wg.shape[1]
    row = lambda bi, i: (bi, i, 0)
    vec = lambda bi, i: (bi, 0, 0)
    const = lambda bi, i: (0, 0)
    return pl.pallas_call(
        functools.partial(_ffn_kernel, tf=tf),
        out_shape=jax.ShapeDtypeStruct((b, s, d), F32),
        grid=(b, s // tm),
        in_specs=[
            pl.BlockSpec((1, tm, d), row),
            pl.BlockSpec((1, tm, d), row),
            pl.BlockSpec((1, 1, d), vec),
            pl.BlockSpec((d, f), const),
            pl.BlockSpec((d, f), const),
            pl.BlockSpec((f, d), const),
        ],
        out_specs=pl.BlockSpec((1, tm, d), row),
        compiler_params=_cparams(("parallel", "parallel")),
        name="ffn",
    )(h2, x1, gate, wg, wu, wd)


def _rms(t, g):
    return t * lax.rsqrt(jnp.mean(t * t, axis=-1, keepdims=True) + NORM_EPS) * g


def _odd_proj_kernel(x_ref, sh_ref, sc_ref, ng_ref, w_ref, wuq_ref, wukv_ref, e_ref, e2_ref, g_ref,
                     cos_ref, sin_ref, cosm_ref, sinm_ref,
                     qc_ref, kc_ref, vc_ref, qd_ref, kd_ref, vd_ref, *, use_rope):
    g = g_ref[...]
    e = e_ref[...]
    e2 = e2_ref[...]
    inv_n = 1.0 / HEAD_DIM
    for rows in _row_blocks(x_ref.shape[1]):
        h = _norm_mod(x_ref[0, rows, :], ng_ref[...], sh_ref[0], sc_ref[0])
        p = _dot(h.astype(BF16), w_ref[...])
        qc = _group_norm(p[:, 0:512], e, g[0:1, 0:512], inv_n)
        kc = _group_norm(p[:, 512:768], e, g[1:2, 0:256], inv_n)
        cq = _rms(p[:, 1024:1280], g[2:3, 0:256])
        ckv = _rms(p[:, 1280:1408], g[3:4, 0:128])
        krc = p[:, 1408:1536]
        qd = _dot(cq.astype(BF16), wuq_ref[...])
        kvv = _dot(ckv.astype(BF16), wukv_ref[...])
        kd = kvv[:, 0:1024] + jnp.concatenate([krc] * MLA_HEADS, axis=1)
        qd = _group_norm(qd, e2, g[4:5], 1.0 / MLA_QK)
        kd = _group_norm(kd, e2, g[5:6], 1.0 / MLA_QK)
        if use_rope:
            lane = lax.broadcasted_iota(jnp.int32, (1, LANES), 1)
            cos, sin = cos_ref[rows, :], sin_ref[rows, :]
            qc = _rope_chunks(qc, cos, sin, (lane & 16) != 0, 16)
            kc = _rope_chunks(kc, cos, sin, (lane & 16) != 0, 16)
            cosm, sinm = cosm_ref[rows, :], sinm_ref[rows, :]
            qd = _rope_chunks(qd, cosm, sinm, (lane & 8) != 0, 8)
            kd = _rope_chunks(kd, cosm, sinm, (lane & 8) != 0, 8)
        qc_ref[0, rows, :] = qc.astype(BF16)
        kc_ref[0, rows, :] = kc.astype(BF16)
        vc_ref[0, rows, :] = p[:, 768:1024].astype(BF16)
        qd_ref[0, rows, :] = qd.astype(BF16)
        kd_ref[0, rows, :] = kd.astype(BF16)
        vd_ref[0, rows, :] = kvv[:, 1024:1536].astype(BF16)


def _odd_proj(x, shift, scale, ng, w, wuq, wukv, e64, e128, gains, cos, sin, cosm, sinm, *, use_rope, tm):
    b, s, d = x.shape
    row = lambda bi, i: (bi, i, 0)
    vec = lambda bi, i: (bi, 0, 0)
    const = lambda bi, i: (0, 0)
    tab = lambda bi, i: (i, 0)
    widths = (512, 256, 256, 1024, 1024, 512)
    return pl.pallas_call(
        functools.partial(_odd_proj_kernel, use_rope=use_rope),
        out_shape=tuple(jax.ShapeDtypeStruct((b, s, n), BF16) for n in widths),
        grid=(b, s // tm),
        in_specs=[
            pl.BlockSpec((1, tm, d), row),
            pl.BlockSpec((1, 1, d), vec),
            pl.BlockSpec((1, 1, d), vec),
            pl.BlockSpec((1, d), const),
            pl.BlockSpec(w.shape, const),
            pl.BlockSpec(wuq.shape, const),
            pl.BlockSpec(wukv.shape, const),
            pl.BlockSpec(e64.shape, const),
            pl.BlockSpec(e128.shape, const),
            pl.BlockSpec(gains.shape, const),
            pl.BlockSpec((tm, LANES), tab),
            pl.BlockSpec((tm, LANES), tab),
            pl.BlockSpec((tm, LANES), tab),
            pl.BlockSpec((tm, LANES), tab),
        ],
        out_specs=tuple(pl.BlockSpec((1, tm, n), row) for n in widths),
        compiler_params=_cparams(("parallel", "parallel")),
        name="odd_proj",
    )(x, shift, scale, ng, w, wuq, wukv, e64, e128, gains, cos, sin, cosm, sinm)


def _mla_rope_tables(seq):
    cos, sin = _rope_tables(seq, MLA_ROPE, MLA_ROPE // 4)
    lane = jnp.arange(LANES)
    on = (lane >= MLA_NOPE) & (lane < MLA_QK)
    return jnp.where(on[None, :], cos, 1.0), jnp.where(on[None, :], sin, 0.0)


def _route_block(h, cols, rt_ref, u_ref, ei_ref, gt_ref, rk_ref, before0):
    h_hi = h.astype(BF16)
    h_lo = (h - h_hi.astype(F32)).astype(BF16)
    both = _dot(h_hi, rt_ref[...]) + _dot(h_lo, rt_ref[...])
    logits = (both[:, :LANES] + both[:, LANES:]).T[:N_EXPERTS]
    eidx = lax.broadcasted_iota(jnp.int32, logits.shape, 0)
    m1 = jnp.max(logits, axis=0, keepdims=True)
    i1 = jnp.min(jnp.where(logits == m1, eidx, N_EXPERTS), axis=0, keepdims=True)
    rest = jnp.where(eidx == i1, -jnp.inf, logits)
    m2 = jnp.max(rest, axis=0, keepdims=True)
    i2 = jnp.min(jnp.where(rest == m2, eidx, N_EXPERTS), axis=0, keepdims=True)
    e2 = jnp.exp(m2 - m1)
    g1 = 1.0 / (1.0 + e2)
    sel1 = eidx == i1
    sel2 = eidx == i2
    onehot = jnp.where(sel1 | sel2, 1.0, 0.0)
    before = _dot(onehot.astype(BF16), u_ref[...]) + before0
    r1 = jnp.sum(jnp.where(sel1, before, 0.0), axis=0, keepdims=True)
    r2 = jnp.sum(jnp.where(sel2, before, 0.0), axis=0, keepdims=True)
    ei_ref[:, cols] = jnp.concatenate([i1, i2], axis=0)
    gt_ref[:, cols] = jnp.concatenate([g1, e2 * g1], axis=0)
    rk_ref[:, cols] = jnp.concatenate([r1, r2], axis=0).astype(jnp.int32)
    return before0 + jnp.sum(onehot, axis=1, keepdims=True)


def _post_attn_router_kernel(oa_ref, ob_ref, wa_ref, wb_ref, x_ref, g_ref, ng_ref, sh_ref, sc_ref, rt_ref, u_ref,
                             x1_ref, h2_ref, ei_ref, gt_ref, rk_ref, cnt_ref, carry_ref):
    @pl.when((pl.program_id(0) == 0) & (pl.program_id(1) == 0))
    def _():
        carry_ref[...] = jnp.zeros_like(carry_ref)

    routed = carry_ref[:, 0:1]
    for rows in _row_blocks(x_ref.shape[1]):
        y = _dot(oa_ref[0, rows, :], wa_ref[...]) + _dot(ob_ref[0, rows, :], wb_ref[...])
        x1 = x_ref[0, rows, :] + g_ref[0] * y
        x1_ref[0, rows, :] = x1
        h = _norm_mod(x1, ng_ref[...], sh_ref[0], sc_ref[0])
        h2_ref[0, rows, :] = h.astype(BF16)
        routed = _route_block(h, rows, rt_ref, u_ref, ei_ref, gt_ref, rk_ref, routed)
    carry_ref[...] = jnp.broadcast_to(routed, carry_ref.shape)
    cnt_ref[...] = jnp.broadcast_to(routed, cnt_ref.shape).astype(jnp.int32)


def _post_attn_router(oa, ob, wa, wb, x, gate, ng, shift, scale, router_t, *, tm):
    b, s, d = x.shape
    n = b * s
    nt = s // tm
    blk = _row_blocks(tm)[0].stop
    tri = (jnp.arange(blk)[:, None] < jnp.arange(blk)[None, :]).astype(BF16)
    row = lambda bi, i: (bi, i, 0)
    vec = lambda bi, i: (bi, 0, 0)
    const = lambda bi, i: (0, 0)
    flat = lambda bi, i: (0, bi * nt + i)
    return pl.pallas_call(
        _post_attn_router_kernel,
        out_shape=(jax.ShapeDtypeStruct((b, s, d), F32), jax.ShapeDtypeStruct((b, s, d), BF16),
                   jax.ShapeDtypeStruct((2, n), jnp.int32), jax.ShapeDtypeStruct((2, n), F32),
                   jax.ShapeDtypeStruct((2, n), jnp.int32), jax.ShapeDtypeStruct((N_EXPERTS, LANES), jnp.int32)),
        grid=(b, nt),
        in_specs=[
            pl.BlockSpec((1, tm, 512), row),
            pl.BlockSpec((1, tm, 512), row),
            pl.BlockSpec((512, d), const),
            pl.BlockSpec((512, d), const),
            pl.BlockSpec((1, tm, d), row),
            pl.BlockSpec((1, 1, d), vec),
            pl.BlockSpec((1, d), const),
            pl.BlockSpec((1, 1, d), vec),
            pl.BlockSpec((1, 1, d), vec),
            pl.BlockSpec((d, 2 * LANES), const),
            pl.BlockSpec((blk, blk), const),
        ],
        out_specs=(pl.BlockSpec((1, tm, d), row), pl.BlockSpec((1, tm, d), row),
                   pl.BlockSpec((2, tm), flat), pl.BlockSpec((2, tm), flat), pl.BlockSpec((2, tm), flat),
                   pl.BlockSpec((N_EXPERTS, LANES), const)),
        scratch_shapes=[pltpu.VMEM((N_EXPERTS, LANES), F32)],
        compiler_params=_cparams(("arbitrary", "arbitrary")),
        name="post_attn_router",
    )(oa, ob, wa, wb, x, gate, ng, shift, scale, router_t, tri)


MOE_SUB = 256
MOE_BLK = 1024
MOE_TC = 256
MOE_NBUF = 8
MOE_AHEAD = MOE_NBUF - 2


def _lookup(table, idx):
    hit = idx[..., None] == jnp.arange(table.shape[0])
    return jnp.sum(jnp.where(hit, table, 0), axis=-1)


def _lookup_cols(rows, idx):
    hit = idx[:, None] == jnp.arange(rows.shape[1])[None, :]
    return jnp.sum(jnp.where(hit, rows, 0), axis=1)


def _moe_plan(ei, rk, counts, n_tok):
    n_chunks = n_tok // MOE_TC
    cap = 2 * n_tok + N_EXPERTS * MOE_BLK
    nb_sub = cap // MOE_SUB
    nb_blk = cap // MOE_BLK
    padded = ((counts + MOE_BLK - 1) // MOE_BLK) * MOE_BLK
    pad_end = jnp.cumsum(padded)
    pad_start = pad_end - padded
    dest = _lookup(pad_start, ei) + rk

    onehot = ei.reshape(2, n_chunks, MOE_TC)[..., None] == jnp.arange(N_EXPERTS)
    cnt = jnp.sum(onehot, axis=(0, 2)).astype(jnp.int32)
    cum = jnp.concatenate([jnp.zeros((1, N_EXPERTS), jnp.int32), jnp.cumsum(cnt, axis=0)], axis=0)

    lo = pad_start[None, :] + cum[:-1]
    has = cnt > 0
    blk0 = lo // MOE_SUB
    two = has & ((lo + cnt - 1) // MOE_SUB > blk0)
    blk1 = jnp.minimum(blk0 + 1, nb_sub - 1)
    win_ok = jnp.stack([has, two], axis=-1).reshape(n_chunks, 2 * N_EXPERTS)
    win_blk = jnp.stack([blk0, blk1], axis=-1).reshape(n_chunks, 2 * N_EXPERTS)
    c_count = jnp.sum(win_ok, axis=1).astype(jnp.int32)
    c_start = (jnp.cumsum(c_count) - c_count).astype(jnp.int32)
    win_pos = jnp.cumsum(win_ok, axis=1) - win_ok
    front = win_ok[:, :, None] & (win_pos[:, :, None] == jnp.arange(2 * N_EXPERTS)[None, None, :])
    packed = jnp.sum(jnp.where(front, win_blk[:, :, None], 0), axis=1)
    q = jnp.arange(n_chunks * 2 * N_EXPERTS)
    c_of_q = jnp.minimum(jnp.sum((c_start + c_count)[None, :] <= q[:, None], axis=1), n_chunks - 1)
    row_q = jnp.dot((c_of_q[:, None] == jnp.arange(n_chunks)[None, :]).astype(F32), packed.astype(F32),
                    precision=HIGHEST)
    p_of_q = q - _lookup(c_start, c_of_q)
    c_blocks = _lookup_cols(row_q, p_of_q).astype(jnp.int32)

    sb = jnp.arange(nb_sub)
    e_sb = jnp.minimum(jnp.sum(sb[:, None] * MOE_SUB >= pad_end[None, :], axis=1), N_EXPERTS - 1)
    counts_sb = _lookup(counts, e_sb)
    r0 = sb * MOE_SUB - _lookup(pad_start, e_sb)
    valid_sb = (sb * MOE_SUB < pad_end[-1]) & (r0 < counts_sb)
    r1 = jnp.minimum(r0 + MOE_SUB, counts_sb) - 1
    hit_sb = e_sb[:, None] == jnp.arange(N_EXPERTS)[None, :]
    cum_sb = jnp.sum(jnp.where(hit_sb[:, None, :], cum[None, 1:, :], 0), axis=-1)
    cmin = jnp.sum(cum_sb <= r0[:, None], axis=1)
    cmax = jnp.sum(cum_sb <= r1[:, None], axis=1)
    items = jnp.where(valid_sb, cmax - cmin + 1, 0)
    off_end = jnp.cumsum(items)
    off = off_end - items
    total = off_end[-1]
    w_max = (2 * n_tok) // MOE_SUB + N_EXPERTS + N_EXPERTS * (n_chunks - 1)
    w = jnp.arange(w_max)
    wv = w < total
    wq = jnp.minimum(w, total - 1)
    w_sb = jnp.minimum(jnp.sum(off_end[None, :] <= wq[:, None], axis=1), nb_sub - 1)
    off_w = _lookup(off, w_sb)
    w_chunk = _lookup(cmin, w_sb) + (wq - off_w)
    w_flag = wv.astype(jnp.int32) + 2 * (wq == off_w).astype(jnp.int32)

    bi = jnp.arange(nb_blk)
    e_blk = jnp.minimum(jnp.sum(bi[:, None] * MOE_BLK >= pad_end[None, :], axis=1), N_EXPERTS - 1)
    rows = jnp.where(bi * MOE_BLK < pad_end[-1],
                     _lookup(counts, e_blk) - (bi * MOE_BLK - _lookup(pad_start, e_blk)), 0)
    n_sub = jnp.clip((rows + MOE_SUB - 1) // MOE_SUB, 0, MOE_BLK // MOE_SUB)
    g_count = jnp.sum(items.reshape(nb_blk, MOE_BLK // MOE_SUB), axis=1)
    g_start = jnp.cumsum(g_count) - g_count
    return dict(dest=dest.astype(jnp.int32), cap=cap,
                w_sb=w_sb.astype(jnp.int32), w_chunk=w_chunk.astype(jnp.int32), w_flag=w_flag,
                g_start=g_start.astype(jnp.int32), g_count=g_count.astype(jnp.int32),
                e_blk=e_blk.astype(jnp.int32), n_sub=n_sub.astype(jnp.int32),
                c_blocks=c_blocks, c_count=c_count, c_start=c_start)


def _moe_gather_kernel(st_ref, cn_ref, wsb_ref, wch_ref, t_hbm, d_ref, g_ref, xg_ref, gs_ref, tbuf, sem):
    i = pl.program_id(0)
    n = cn_ref[i]
    s0 = st_ref[i]
    total = st_ref[pl.num_programs(0) - 1] + cn_ref[pl.num_programs(0) - 1]
    subs = MOE_BLK // MOE_SUB

    def chunk_copy(item):
        slot = lax.rem(item, MOE_NBUF)
        start = pl.multiple_of(wch_ref[item] * MOE_TC, MOE_TC)
        return pltpu.make_async_copy(t_hbm.at[pl.ds(start, MOE_TC), :], tbuf.at[slot], sem.at[slot])

    @pl.when(i == 0)
    def _():
        for a in range(MOE_AHEAD):
            @pl.when(a < total)
            def _():
                chunk_copy(a).start()

    xg_ref[...] = jnp.zeros(xg_ref.shape, BF16)
    gs_ref[...] = jnp.zeros(gs_ref.shape, F32)

    def arrive(item):
        chunk_copy(item).wait()

        @pl.when(item + MOE_AHEAD < total)
        def _():
            chunk_copy(item + MOE_AHEAD).start()

    def contribute(item):
        sb = wsb_ref[item]
        chunk = wch_ref[item]
        d = d_ref[chunk]
        g = g_ref[chunk]
        srow = lax.broadcasted_iota(jnp.int32, (MOE_SUB, MOE_TC), 0) + sb * MOE_SUB
        hit0 = srow == d[0:1]
        hit1 = srow == d[1:2]
        p = jnp.where(hit0 | hit1, 1.0, 0.0).astype(BF16)
        rows = _dot(p, tbuf[lax.rem(item, MOE_NBUF)]).astype(BF16)
        gate = jnp.sum(jnp.where(hit0, g[0:1], 0.0) + jnp.where(hit1, g[1:2], 0.0), axis=-1, keepdims=True)
        r0 = pl.multiple_of((sb - i * subs) * MOE_SUB, MOE_SUB)
        xg_ref[pl.ds(r0, MOE_SUB), :] += rows
        gs_ref[pl.ds(r0, MOE_SUB), :] += gate

    def pair(k, carry):
        item = s0 + 2 * k
        arrive(item)
        arrive(item + 1)
        contribute(item)
        contribute(item + 1)
        return carry

    lax.fori_loop(0, n // 2, pair, 0)

    @pl.when(lax.rem(n, 2) == 1)
    def _():
        arrive(s0 + n - 1)
        contribute(s0 + n - 1)


def _moe_gather(t, dest, gates, plan):
    n_tok, d = t.shape
    cap = plan["cap"]
    n_chunks = n_tok // MOE_TC
    by_chunk = lambda a: a.reshape(2, n_chunks, MOE_TC).transpose(1, 0, 2)
    grid_spec = pltpu.PrefetchScalarGridSpec(
        num_scalar_prefetch=4,
        grid=(cap // MOE_BLK,),
        in_specs=[
            pl.BlockSpec(memory_space=pl.ANY),
            pl.BlockSpec((n_chunks, 2, MOE_TC), lambda i, *_: (0, 0, 0)),
            pl.BlockSpec((n_chunks, 2, MOE_TC), lambda i, *_: (0, 0, 0)),
        ],
        out_specs=(pl.BlockSpec((MOE_BLK, d), lambda i, *_: (i, 0)),
                   pl.BlockSpec((MOE_BLK, 1), lambda i, *_: (i, 0))),
        scratch_shapes=[pltpu.VMEM((MOE_NBUF, MOE_TC, d), BF16), pltpu.SemaphoreType.DMA((MOE_NBUF,))],
    )
    return pl.pallas_call(
        _moe_gather_kernel,
        out_shape=(jax.ShapeDtypeStruct((cap, d), BF16), jax.ShapeDtypeStruct((cap, 1), F32)),
        grid_spec=grid_spec,
        compiler_params=_cparams(("arbitrary",)),
        name="moe_gather",
    )(plan["g_start"], plan["g_count"], plan["w_sb"], plan["w_chunk"], t, by_chunk(dest), by_chunk(gates))


def _moe_expert_kernel(eb_ref, ns_ref, x_ref, gs_ref, w1_ref, w3_ref, w2_ref, y_ref):
    n_sub = ns_ref[pl.program_id(0)]
    subs = MOE_BLK // MOE_SUB

    def mlp(rows):
        xs = x_ref[rows, :]
        acc = jnp.zeros((rows.stop - rows.start, y_ref.shape[1]), F32)
        for f in range(w1_ref.shape[1] // MOE_TF):
            cols = slice(f * MOE_TF, (f + 1) * MOE_TF)
            a = _dot(xs, w1_ref[:, cols])
            u = _dot(xs, w3_ref[:, cols])
            acc = acc + _dot((_silu(a) * u).astype(BF16), w2_ref[cols, :])
        y_ref[rows, :] = (acc * gs_ref[rows, :]).astype(BF16)

    @pl.when(n_sub == subs)
    def _():
        mlp(slice(0, MOE_BLK))

    for sub in range(subs):
        rows = slice(sub * MOE_SUB, (sub + 1) * MOE_SUB)

        @pl.when((n_sub < subs) & (sub < n_sub))
        def _():
            mlp(rows)

        @pl.when(sub >= n_sub)
        def _():
            y_ref[rows, :] = jnp.zeros((MOE_SUB, y_ref.shape[1]), BF16)


def _moe_experts(xg, gs, w1, w3, w2, plan):
    cap, d = xg.shape
    f = w1.shape[2]
    once = pl.Buffered(1)
    grid_spec = pltpu.PrefetchScalarGridSpec(
        num_scalar_prefetch=2,
        grid=(cap // MOE_BLK,),
        in_specs=[
            pl.BlockSpec((MOE_BLK, d), lambda i, eb, ns: (i, 0)),
            pl.BlockSpec((MOE_BLK, 1), lambda i, eb, ns: (i, 0)),
            pl.BlockSpec((None, d, f), lambda i, eb, ns: (eb[i], 0, 0), pipeline_mode=once),
            pl.BlockSpec((None, d, f), lambda i, eb, ns: (eb[i], 0, 0), pipeline_mode=once),
            pl.BlockSpec((None, f, d), lambda i, eb, ns: (eb[i], 0, 0), pipeline_mode=once),
        ],
        out_specs=pl.BlockSpec((MOE_BLK, d), lambda i, eb, ns: (i, 0)),
    )
    return pl.pallas_call(
        _moe_expert_kernel,
        out_shape=jax.ShapeDtypeStruct((cap, d), BF16),
        grid_spec=grid_spec,
        compiler_params=_cparams(("arbitrary",)),
        name="moe_experts",
    )(plan["e_blk"], plan["n_sub"], xg, gs, w1, w3, w2)


MOE_WIN = 2 * N_EXPERTS


def _moe_combine_kernel(st_ref, cn_ref, bl_ref, dt_ref, y_hbm, x1_ref, g_ref, o_ref, ybuf, sem, acc_ref):
    c = pl.program_id(0)
    n = cn_ref[c]
    s0 = st_ref[c]
    total = st_ref[pl.num_programs(0) - 1] + cn_ref[pl.num_programs(0) - 1]
    d = dt_ref[...]

    def block_copy(item):
        slot = lax.rem(item, MOE_NBUF)
        start = pl.multiple_of(bl_ref[item] * MOE_SUB, MOE_SUB)
        return pltpu.make_async_copy(y_hbm.at[pl.ds(start, MOE_SUB), :], ybuf.at[slot], sem.at[slot])

    @pl.when(c == 0)
    def _():
        for a in range(MOE_AHEAD):
            @pl.when(a < total)
            def _():
                block_copy(a).start()

    def arrive(item):
        block_copy(item).wait()

        @pl.when(item + MOE_AHEAD < total)
        def _():
            block_copy(item + MOE_AHEAD).start()

    def rows_of(item):
        scol = lax.broadcasted_iota(jnp.int32, (MOE_TC, MOE_SUB), 1) + bl_ref[item] * MOE_SUB
        p = jnp.where((scol == d[:, 0:1]) | (scol == d[:, 1:2]), 1.0, 0.0).astype(BF16)
        return _dot(p, ybuf[lax.rem(item, MOE_NBUF)])

    def pair(k, acc):
        item = s0 + 2 * k
        arrive(item)
        arrive(item + 1)
        return acc + rows_of(item) + rows_of(item + 1)

    acc_ref[...] = lax.fori_loop(0, n // 2, pair, jnp.zeros(o_ref.shape, F32))

    @pl.when(lax.rem(n, 2) == 1)
    def _():
        arrive(s0 + n - 1)
        acc_ref[...] += rows_of(s0 + n - 1)

    o_ref[...] = x1_ref[...] + g_ref[0] * acc_ref[...]


def _moe_combine(y, dest_t, x1, gate, plan, seq):
    n_tok, d = x1.shape
    n_chunks = n_tok // MOE_TC
    per_b = seq // MOE_TC
    grid_spec = pltpu.PrefetchScalarGridSpec(
        num_scalar_prefetch=3,
        grid=(n_chunks,),
        in_specs=[
            pl.BlockSpec((MOE_TC, 2), lambda c, *_: (c, 0)),
            pl.BlockSpec(memory_space=pl.ANY),
            pl.BlockSpec((MOE_TC, d), lambda c, *_: (c, 0)),
            pl.BlockSpec((1, 1, d), lambda c, *_: (c // per_b, 0, 0)),
        ],
        out_specs=pl.BlockSpec((MOE_TC, d), lambda c, *_: (c, 0)),
        scratch_shapes=[pltpu.VMEM((MOE_NBUF, MOE_SUB, d), BF16), pltpu.SemaphoreType.DMA((MOE_NBUF,)),
                        pltpu.VMEM((MOE_TC, d), F32)],
    )
    return pl.pallas_call(
        _moe_combine_kernel,
        out_shape=jax.ShapeDtypeStruct((n_tok, d), F32),
        grid_spec=grid_spec,
        compiler_params=_cparams(("arbitrary",)),
        name="moe_combine",
    )(plan["c_start"], plan["c_count"], plan["c_blocks"], dest_t, y, x1, gate)


def _split_mod(mod_l, batch):
    d = D_MODEL
    lat = tuple(mod_l[:batch, k * d:(k + 1) * d][:, None, :] for k in range(ADA_CHUNKS))
    ctx = tuple(jnp.broadcast_to(mod_l[batch:batch + 1, k * d:(k + 1) * d][:, None, :], (batch, 1, d))
                for k in range(ADA_CHUNKS))
    return lat, ctx


def _score_bound(q_gain, k_gain, n):
    return 1.02 * n * jnp.max(jnp.abs(q_gain)) * jnp.max(jnp.abs(k_gain))


def _block_ones(n, block):
    idx = jnp.arange(n) // block
    return (idx[:, None] == idx[None, :]).astype(BF16)


def _even_layer(x, xc, mod_l, norm1_g, norm2_g, w_in, w_out, na_q_g, na_k_g, na_rpb, diff_q_g, diff_k_g,
                lq1, lk1, lq2, lk2, subln_g, wg, wu, wd, layer_idx, need_ctx):
    b, s, d = x.shape
    (sh1, sc1, g1, sh2, sc2, g2), (csh1, csc1, cg1, csh2, csc2, cg2) = _split_mod(mod_l, b)
    qscale = HEAD_DIM ** -0.5 * LOG2E
    lam_init = 0.8 - 0.6 * math.exp(-0.3 * layer_idx)
    hg = jnp.stack([jnp.tile(na_q_g, 8) * qscale, jnp.tile(na_k_g, 8),
                    jnp.tile(diff_q_g, 8) * qscale, jnp.tile(diff_k_g, 8)]).astype(F32)
    e64 = _block_ones(MXU_DIM, HEAD_DIM)
    cos, sin = _rope_tables(s, HEAD_DIM, 16)
    ng1 = norm1_g[None, :]
    ng2 = norm2_g[None, :]
    w_in_b = w_in.astype(BF16)
    wo_a = w_out[:512].astype(BF16)
    wo_b = w_out[512:].astype(BF16)
    wg_b, wu_b, wd_b = wg.astype(BF16), wu.astype(BF16), wd.astype(BF16)
    lam_p = jnp.stack([lq1, lk1, lq2, lk2]).astype(F32)
    sg = subln_g[None, :].astype(F32)
    ident = lambda j: j

    tm = min(ROW_TM, s)
    tc = xc.shape[1]
    qa, ka, va, qb, kb, vb = _even_proj(x, sh1, sc1, ng1, w_in_b, e64, hg, cos, sin, use_rope=True, tm=tm)
    qca, kca, vca, qcb, kcb, vcb = _even_proj(xc, csh1, csc1, ng1, w_in_b, e64, hg, cos, sin, use_rope=False, tm=tc)
    out_a = _na_attention(qa, ka, va, kca, vca, _na_bias_table(na_rpb), _score_bound(hg[0], hg[1], HEAD_DIM))
    out_b = _flash(qb, kb, vb, kcb, vcb, mode="diff", kv_map=ident, n_chunks=4, tq=FLASH_TQ, tk=FLASH_TK,
                   extra=(lam_p, sg), lam_init=lam_init, score_bound=_score_bound(hg[2], hg[3], HEAD_DIM))
    x1, h2 = _post_attn(out_a, out_b, wo_a, wo_b, x, g1, ng2, sh2, sc2, tm=tm)
    x2 = _ffn(h2, x1, g2, wg_b, wu_b, wd_b, tm=tm)
    if not need_ctx:
        return x2, None
    oca = _flash(qca, None, None, kca, vca, mode="pair", kv_map=ident, n_chunks=4, tq=tc, tk=tc)
    ocb = _flash(qcb, None, None, kcb, vcb, mode="diff", kv_map=ident, n_chunks=4, tq=tc, tk=tc,
                 extra=(lam_p, sg), lam_init=lam_init)
    xc1, hc2 = _post_attn(oca, ocb, wo_a, wo_b, xc, cg1, ng2, csh2, csc2, tm=tc)
    xc2 = _ffn(hc2, xc1, cg2, wg_b, wu_b, wd_b, tm=tc)
    return x2, xc2


def _odd_layer(x, xc, mod_l, norm1_g, norm2_g, w_in, w_out, gqa_q_g, gqa_k_g, cq_g, w_uq, ckv_g, w_ukv,
               mla_q_g, mla_k_g, router, w1, w3, w2):
    b, s, d = x.shape
    n_tok = b * s
    (sh1, sc1, g1, sh2, sc2, g2), (csh1, csc1, _, _, _, _) = _split_mod(mod_l, b)
    ng1 = norm1_g[None, :]
    ng2 = norm2_g[None, :]

    z = lambda n: jnp.zeros((d, n), w_in.dtype)
    k0, k1, v0, v1 = (w_in[:, 512 + 64 * i:576 + 64 * i] for i in range(4))
    w_p = jnp.concatenate([w_in[:, 0:512], k0, k0, k1, k1, v0, v0, v1, v1, w_in[:, 768:1024], w_in[:, 1024:1152],
                           z(MLA_NOPE), w_in[:, 1152:1184], z(LANES - MLA_QK)], axis=1).astype(BF16)
    wuq_p = jnp.pad(w_uq.reshape(MLA_Q_RANK, MLA_HEADS, MLA_QK), ((0, 0), (0, 0), (0, LANES - MLA_QK)))
    wuq_p = wuq_p.reshape(MLA_Q_RANK, MLA_HEADS * LANES).astype(BF16)
    ukv = w_ukv.reshape(MLA_KV_RANK, MLA_HEADS, MLA_NOPE + MLA_V)
    uk = jnp.pad(ukv[:, :, :MLA_NOPE], ((0, 0), (0, 0), (0, LANES - MLA_NOPE))).reshape(MLA_KV_RANK, -1)
    uv = ukv[:, :, MLA_NOPE:].reshape(MLA_KV_RANK, -1)
    wukv_p = jnp.concatenate([uk, uv], axis=1).astype(BF16)
    e64 = _block_ones(MXU_DIM, HEAD_DIM)
    e128 = _block_ones(MXU_DIM, LANES)
    qscale = HEAD_DIM ** -0.5 * LOG2E
    mscale = MLA_QK ** -0.5 * LOG2E
    pad_row = lambda v: jnp.pad(v, (0, MLA_HEADS * LANES - v.shape[0]))
    pad_head = lambda v: jnp.tile(jnp.pad(v, (0, LANES - MLA_QK)), MLA_HEADS)
    gains = jnp.stack([pad_row(jnp.tile(gqa_q_g, 8) * qscale), pad_row(jnp.tile(gqa_k_g, 4)),
                       pad_row(cq_g), pad_row(ckv_g), pad_head(mla_q_g) * mscale, pad_head(mla_k_g),
                       jnp.zeros((MLA_HEADS * LANES,), F32), jnp.zeros((MLA_HEADS * LANES,), F32)]).astype(F32)
    cos, sin = _rope_tables(s, HEAD_DIM, 16)
    cosm, sinm = _mla_rope_tables(s)
    wo_a = w_out[:512].astype(BF16)
    wo_b = w_out[512:].astype(BF16)

    tm = min(ROW_TM, s)
    tc = xc.shape[1]
    proj = functools.partial(_odd_proj, ng=ng1, w=w_p, wuq=wuq_p, wukv=wukv_p, e64=e64, e128=e128, gains=gains,
                             cos=cos, sin=sin, cosm=cosm, sinm=sinm)
    qc, kc, vc, qd, kd, vd = proj(x, sh1, sc1, use_rope=True, tm=tm)
    _, kcc, vcc, _, kcd, vcd = proj(xc, csh1, csc1, use_rope=False, tm=tc)
    out_c = _flash(qc, kc, vc, kcc, vcc, mode="pair", kv_map=lambda j: j // 2, n_chunks=4, tq=FLASH_TQ, tk=FLASH_TK,
                   score_bound=_score_bound(gains[0], gains[1], HEAD_DIM))
    out_d = _flash(qd, kd, vd, kcd, vcd, mode="mla", kv_map=lambda j: j, n_chunks=4, tq=FLASH_TQ, tk=FLASH_TK,
                   score_bound=_score_bound(gains[4], gains[5], MLA_QK))
    r_hi = router.astype(BF16)
    r_lo = (router - r_hi.astype(F32)).astype(BF16)
    pad_e = lambda m: jnp.pad(m, ((0, 0), (0, LANES - N_EXPERTS)))
    x1, h2, ei, gt, rk, cnt = _post_attn_router(out_c, out_d, wo_a, wo_b, x, g1, ng2, sh2, sc2,
                                                jnp.concatenate([pad_e(r_hi), pad_e(r_lo)], axis=1), tm=tm)
    plan = _moe_plan(ei, rk, cnt[:, 0], n_tok)
    xg, gs = _moe_gather(h2.reshape(n_tok, d), plan["dest"], gt, plan)
    y = _moe_experts(xg, gs, w1.astype(BF16), w3.astype(BF16), w2.astype(BF16), plan)
    out = _moe_combine(y, plan["dest"].T, x1.reshape(n_tok, d), g2, plan, s)
    return out.reshape(b, s, d)


def _mod_vectors(c, c_ctx, ada_w, ada_b):
    b = c.shape[0]
    cc = jnp.zeros((MOD_ROWS, D_MODEL), F32).at[:b].set(c).at[b].set(c_ctx)
    return _modvec(cc, ada_w, ada_b[:, None, :])


def kernel(x, c, ctx, c_ctx, ada_w, ada_b, norm1_g, norm2_g, ev_w_in, ev_w_out, na_q_g, na_k_g, na_rpb,
           diff_q_g, diff_k_g, diff_lq1, diff_lk1, diff_lq2, diff_lk2, diff_subln_g,
           ffn_w_gate, ffn_w_up, ffn_w_down, od_w_in, od_w_out, gqa_q_g, gqa_k_g, mla_cq_g, mla_w_uq,
           mla_ckv_g, mla_w_ukv, mla_q_g, mla_k_g, moe_router, moe_w1, moe_w3, moe_w2):
    mod = _mod_vectors(c, c_ctx, ada_w, ada_b)
    x, xc = _even_layer(x, ctx, mod[0], norm1_g[0], norm2_g[0], ev_w_in[0], ev_w_out[0], na_q_g[0], na_k_g[0],
                        na_rpb[0], diff_q_g[0], diff_k_g[0], diff_lq1[0], diff_lk1[0], diff_lq2[0], diff_lk2[0],
                        diff_subln_g[0], ffn_w_gate[0], ffn_w_up[0], ffn_w_down[0], 0, True)
    return _odd_layer(x, xc, mod[1], norm1_g[1], norm2_g[1], od_w_in[0], od_w_out[0], gqa_q_g[0], gqa_k_g[0],
                      mla_cq_g[0], mla_w_uq[0], mla_ckv_g[0], mla_w_ukv[0], mla_q_g[0], mla_k_g[0],
                      moe_router[0], moe_w1[0], moe_w3[0], moe_w2[0])
```

```python
import functools
import math

import jax
import jax.numpy as jnp
from jax import lax
from jax.experimental import pallas as pl
from jax.experimental.pallas import tpu as pltpu

F32 = jnp.float32
BF16 = jnp.bfloat16
HIGHEST = lax.Precision.HIGHEST

D_MODEL = 1024
GRID_W = 64
HEAD_DIM = 64
ROPE_THETA = 10000.0
NORM_EPS = 1e-6
NEG_INF = -1e30
ADA_CHUNKS = 6
LOG2E = 1.4426950408889634

NA_HEADS = 8
NA_WIN_H = 8
NA_WIN_W = 16
DIFF_HEADS = 4
DIFF_V_DIM = 2 * HEAD_DIM
GQA_Q_HEADS = 8
GQA_KV_HEADS = 2
MLA_HEADS = 8
MLA_NOPE = 64
MLA_ROPE = 32
MLA_QK = MLA_NOPE + MLA_ROPE
MLA_V = 64
MLA_Q_RANK = 256
MLA_KV_RANK = 128
D_FF = 2816
N_EXPERTS = 8
D_FF_EXPERT = 3584

LANES = 128
MXU_DIM = 256
VMEM_LIMIT = 56 * 1024 * 1024
MOD_ROWS = 16
MOE_TF = 512
SUM_ROWS = 16
ROW_TM = 1024
FLASH_TQ = 1024
FLASH_TK = 512
EXP2_SAFE_RANGE = 64.0


def _cparams(sem):
    return pltpu.CompilerParams(dimension_semantics=sem, vmem_limit_bytes=VMEM_LIMIT)


def _dot(a, b):
    return jnp.dot(a, b, preferred_element_type=F32)


def _dot_nt(a, b):
    return lax.dot_general(a, b, (((1,), (1,)), ((), ())), preferred_element_type=F32)


def _silu(x):
    return x * (1.0 / (1.0 + jnp.exp(-x)))


def _modvec_kernel(c_ref, w_ref, b_ref, o_ref):
    s = _silu(c_ref[...])
    o_ref[...] = jnp.dot(s, w_ref[...], preferred_element_type=F32, precision=HIGHEST) + b_ref[...]


def _modvec(cc, ada_w, ada_b):
    n_layers, d, n = ada_w.shape
    tn = 768
    return pl.pallas_call(
        _modvec_kernel,
        out_shape=jax.ShapeDtypeStruct((n_layers, MOD_ROWS, n), F32),
        grid=(n_layers, n // tn),
        in_specs=[
            pl.BlockSpec((MOD_ROWS, d), lambda l, j: (0, 0)),
            pl.BlockSpec((None, d, tn), lambda l, j: (l, 0, j)),
            pl.BlockSpec((None, 1, tn), lambda l, j: (l, 0, j)),
        ],
        out_specs=pl.BlockSpec((None, MOD_ROWS, tn), lambda l, j: (l, 0, j)),
        compiler_params=_cparams(("arbitrary", "arbitrary")),
        name="modvec",
    )(cc, ada_w, ada_b)


def _norm_mod(x, g, shift, scale):
    ms = jnp.mean(x * x, axis=-1, keepdims=True)
    return (x * lax.rsqrt(ms + NORM_EPS) * g) * (1.0 + scale) + shift


def _row_blocks(n, size=256):
    size = min(size, n)
    return [slice(r, r + size) for r in range(0, n, size)]


def _group_norm(t, e, g, inv_n):
    outs = []
    for c in range(t.shape[1] // MXU_DIM):
        cols = slice(c * MXU_DIM, (c + 1) * MXU_DIM)
        tc = t[:, cols]
        ss = _dot((tc * tc).astype(BF16), e)
        outs.append(tc * lax.rsqrt(ss * inv_n + NORM_EPS) * g[:, cols])
    return outs[0] if len(outs) == 1 else jnp.concatenate(outs, axis=1)


def _rope_chunks(t, cos, sin, lane_hi, shift):
    outs = []
    for c in range(t.shape[1] // LANES):
        tc = t[:, c * LANES:(c + 1) * LANES]
        up = pltpu.roll(tc, LANES - shift, 1)
        dn = pltpu.roll(tc, shift, 1)
        outs.append(tc * cos + jnp.where(lane_hi, dn, up) * sin)
    return outs[0] if len(outs) == 1 else jnp.concatenate(outs, axis=1)


def _rope_tables(seq, unit, pair_shift):
    pos = jnp.arange(seq)
    rows, cols = pos // GRID_W, pos % GRID_W
    quarter = unit // 4
    assert quarter == pair_shift
    freqs = ROPE_THETA ** (-jnp.arange(quarter, dtype=F32) / quarter)
    lane = jnp.arange(LANES)
    u = lane % unit
    use_col = (u // (unit // 2)) == 1
    fi = u % quarter
    p = jnp.where(use_col[None, :], cols[:, None], rows[:, None]).astype(F32)
    ang = p * freqs[fi][None, :]
    second = ((u % (unit // 2)) // quarter) == 1
    cos = jnp.cos(ang)
    sin = jnp.where(second[None, :], jnp.sin(ang), -jnp.sin(ang))
    return cos, sin


def _even_proj_kernel(x_ref, sh_ref, sc_ref, ng_ref, w_ref, e_ref, hg_ref, cos_ref, sin_ref,
                      qa_ref, ka_ref, va_ref, qb_ref, kb_ref, vb_ref, *, use_rope):
    hg = hg_ref[...]
    e = e_ref[...]
    inv_n = 1.0 / HEAD_DIM
    for rows in _row_blocks(x_ref.shape[1]):
        h = _norm_mod(x_ref[0, rows, :], ng_ref[...], sh_ref[0], sc_ref[0])
        p = _dot(h.astype(BF16), w_ref[...])
        qa = _group_norm(p[:, 0:512], e, hg[0:1], inv_n)
        ka = _group_norm(p[:, 512:1024], e, hg[1:2], inv_n)
        qb = _group_norm(p[:, 1536:2048], e, hg[2:3], inv_n)
        kb = _group_norm(p[:, 2048:2560], e, hg[3:4], inv_n)
        if use_rope:
            lane = lax.broadcasted_iota(jnp.int32, (1, LANES), 1)
            lane_hi = (lane & 16) != 0
            cos, sin = cos_ref[rows, :], sin_ref[rows, :]
            qb = _rope_chunks(qb, cos, sin, lane_hi, 16)
            kb = _rope_chunks(kb, cos, sin, lane_hi, 16)
        qa_ref[0, rows, :] = qa.astype(BF16)
        ka_ref[0, rows, :] = ka.astype(BF16)
        va_ref[0, rows, :] = p[:, 1024:1536].astype(BF16)
        qb_ref[0, rows, :] = qb.astype(BF16)
        kb_ref[0, rows, :] = kb.astype(BF16)
        vb_ref[0, rows, :] = p[:, 2560:3072].astype(BF16)


def _even_proj(x, shift, scale, ng, w, e64, hg, cos, sin, *, use_rope, tm):
    b, s, d = x.shape
    n = w.shape[1]
    row = lambda bi, i: (bi, i, 0)
    vec = lambda bi, i: (bi, 0, 0)
    const = lambda bi, i: (0, 0)
    out = jax.ShapeDtypeStruct((b, s, 512), BF16)
    return pl.pallas_call(
        functools.partial(_even_proj_kernel, use_rope=use_rope),
        out_shape=(out,) * 6,
        grid=(b, s // tm),
        in_specs=[
            pl.BlockSpec((1, tm, d), row),
            pl.BlockSpec((1, 1, d), vec),
            pl.BlockSpec((1, 1, d), vec),
            pl.BlockSpec((1, d), const),
            pl.BlockSpec((d, n), const),
            pl.BlockSpec(e64.shape, const),
            pl.BlockSpec((4, 512), const),
            pl.BlockSpec((tm, LANES), lambda bi, i: (i, 0)),
            pl.BlockSpec((tm, LANES), lambda bi, i: (i, 0)),
        ],
        out_specs=(pl.BlockSpec((1, tm, 512), row),) * 6,
        compiler_params=_cparams(("parallel", "parallel")),
        name="even_proj",
    )(x, shift, scale, ng, w, e64, hg, cos, sin)


def _pair_stack(q):
    lane = lax.broadcasted_iota(jnp.int32, q.shape, 1)
    zero = jnp.zeros_like(q)
    return jnp.concatenate([jnp.where(lane < HEAD_DIM, q, zero), jnp.where(lane >= HEAD_DIM, q, zero)], axis=0)


def _pair_merge(o, tq):
    lane = lax.broadcasted_iota(jnp.int32, (tq, LANES), 1)
    return jnp.where(lane < HEAD_DIM, o[:tq], o[tq:])


def _with_ones(v):
    return jnp.concatenate([v, jnp.ones(v.shape, BF16)], axis=1)


def _na_kernel(q_ref, k_ref, v_ref, kc_ref, vc_ref, bias_ref, o_ref, *, rows_per_step, n_rows, direct):
    rb = pl.program_id(2)
    kc = kc_ref[0]
    vc = vc_ref[0]
    n_lat = NA_WIN_H * GRID_W
    if direct:
        nq = rows_per_step * GRID_W
        qs_all = _pair_stack(q_ref[0])
        o_ctx = _dot(jnp.exp2(_dot_nt(qs_all, kc)).astype(BF16), _with_ones(vc))
        for i in range(rows_per_step):
            r = rb * rows_per_step + i
            r0 = jnp.clip(r - NA_WIN_H // 2, 0, n_rows - NA_WIN_H)
            start = pl.multiple_of(r0 * GRID_W, GRID_W)
            kw = k_ref[0, pl.ds(start, n_lat), :]
            vw = v_ref[0, pl.ds(start, n_lat), :]
            lo = slice(i * GRID_W, (i + 1) * GRID_W)
            hi = slice(nq + i * GRID_W, nq + (i + 1) * GRID_W)
            qs = jnp.concatenate([qs_all[lo], qs_all[hi]], axis=0)
            p = jnp.exp2(_dot_nt(qs, kw) + bias_ref[r - r0]).astype(BF16)
            tot = _dot(p, _with_ones(vw)) + jnp.concatenate([o_ctx[lo], o_ctx[hi]], axis=0)
            o = tot[:, :LANES] / tot[:, LANES:]
            o_ref[0, lo, :] = _pair_merge(o, GRID_W).astype(BF16)
        return
    for i in range(rows_per_step):
        r = rb * rows_per_step + i
        r0 = jnp.clip(r - NA_WIN_H // 2, 0, n_rows - NA_WIN_H)
        start = pl.multiple_of(r0 * GRID_W, GRID_W)
        kw = k_ref[0, pl.ds(start, n_lat), :]
        vw = v_ref[0, pl.ds(start, n_lat), :]
        qs = _pair_stack(q_ref[0, i * GRID_W:(i + 1) * GRID_W, :])
        s_lat = _dot_nt(qs, kw) + bias_ref[r - r0]
        s_ctx = _dot_nt(qs, kc)
        m = jnp.maximum(jnp.max(s_lat, axis=-1, keepdims=True), jnp.max(s_ctx, axis=-1, keepdims=True))
        p_lat = jnp.exp2(s_lat - m)
        p_ctx = jnp.exp2(s_ctx - m)
        l = jnp.sum(p_lat, axis=-1, keepdims=True) + jnp.sum(p_ctx, axis=-1, keepdims=True)
        o = _dot(p_lat.astype(BF16), vw) + _dot(p_ctx.astype(BF16), vc)
        o = o / l
        o_ref[0, i * GRID_W:(i + 1) * GRID_W, :] = _pair_merge(o, GRID_W).astype(BF16)


def _na_attention(q, k, v, kc, vc, bias, score_bound, *, rows_per_step=16):
    bound = score_bound + jnp.max(jnp.where(bias > 0.5 * NEG_INF, jnp.abs(bias), 0.0))
    return lax.cond(bound <= EXP2_SAFE_RANGE,
                    lambda: _na_call(q, k, v, kc, vc, bias, rows_per_step=rows_per_step, direct=True),
                    lambda: _na_call(q, k, v, kc, vc, bias, rows_per_step=rows_per_step, direct=False))


def _na_call(q, k, v, kc, vc, bias, *, rows_per_step, direct):
    b, s, _ = q.shape
    c = kc.shape[1]
    n_rows = s // GRID_W
    tq = rows_per_step * GRID_W
    return pl.pallas_call(
        functools.partial(_na_kernel, rows_per_step=rows_per_step, n_rows=n_rows, direct=direct),
        out_shape=jax.ShapeDtypeStruct((b, s, 512), BF16),
        grid=(b, 4, n_rows // rows_per_step),
        in_specs=[
            pl.BlockSpec((1, tq, LANES), lambda bi, j, i: (bi, i, j)),
            pl.BlockSpec((1, s, LANES), lambda bi, j, i: (bi, 0, j)),
            pl.BlockSpec((1, s, LANES), lambda bi, j, i: (bi, 0, j)),
            pl.BlockSpec((1, c, LANES), lambda bi, j, i: (bi, 0, j)),
            pl.BlockSpec((1, c, LANES), lambda bi, j, i: (bi, 0, j)),
            pl.BlockSpec((NA_WIN_H, None, LANES, NA_WIN_H * GRID_W), lambda bi, j, i: (0, j, 0, 0)),
        ],
        out_specs=pl.BlockSpec((1, tq, LANES), lambda bi, j, i: (bi, i, j)),
        compiler_params=_cparams(("parallel", "parallel", "parallel")),
        name="na_attention" + ("" if direct else "_online"),
    )(q, k, v, kc, vc, bias)


def _na_bias_table(rpb):
    w = GRID_W
    col = jnp.arange(w)
    c0 = jnp.clip(col - NA_WIN_W // 2, 0, w - NA_WIN_W)
    col_in = (col[None, :] >= c0[:, None]) & (col[None, :] < c0[:, None] + NA_WIN_W)
    left = (w - 1) - (NA_WIN_W - 1)
    ext = jnp.pad(rpb, ((0, 0), (0, 0), (left, 2 * w - left - (2 * NA_WIN_W - 1))), mode="edge")
    h, nr, _ = rpb.shape
    flat = jnp.broadcast_to(ext[:, :, None, :], (h, nr, w, 2 * w)).reshape(h, nr, w * 2 * w)
    toep = flat[:, :, :w * (2 * w - 1)].reshape(h, nr, w, 2 * w - 1)[:, :, :, w - 1:]
    toep = jnp.where(col_in[None, None], toep * LOG2E, NEG_INF)
    variants = []
    for v in range(NA_WIN_H):
        tv = toep[:, NA_WIN_H - 1 - v:2 * NA_WIN_H - 1 - v]
        variants.append(tv.transpose(0, 2, 1, 3).reshape(NA_HEADS // 2, 2 * w, NA_WIN_H * w))
    return jnp.stack(variants, axis=0).astype(F32)


def _flash_kernel(*refs, mode, online, tq, tk, s_len, c_len, lam_init):
    if mode == "diff":
        q_ref, k_ref, v_ref, kc_ref, vc_ref, lam_ref, sg_ref, o_ref = refs
    else:
        q_ref, k_ref, v_ref, kc_ref, vc_ref, o_ref = refs

    q = q_ref[0]
    if mode == "mla":
        q_parts = (q[:, :LANES], q[:, LANES:])
    else:
        qs = _pair_stack(q)

    def scores(kt):
        if mode == "mla":
            return jnp.concatenate([_dot_nt(q_parts[0], kt[:, :LANES]), _dot_nt(q_parts[1], kt[:, LANES:])], axis=0)
        return _dot_nt(qs, kt)

    def step(kt, vt, carry):
        m, l, acc = carry
        s = scores(kt)
        m_new = jnp.maximum(m, jnp.max(s, axis=-1, keepdims=True))
        alpha = jnp.exp2(m - m_new)
        p = jnp.exp2(s - m_new)
        l = alpha * l + jnp.sum(p, axis=-1, keepdims=True)
        acc = alpha * acc + _dot(p.astype(BF16), vt)
        return m_new, l, acc

    def body(t, carry):
        start = pl.multiple_of(t * tk, tk)
        return step(k_ref[0, pl.ds(start, tk), :], v_ref[0, pl.ds(start, tk), :], carry)

    def scores_t(kt):
        if mode == "mla":
            return jnp.concatenate([_dot_nt(kt[:, :LANES], q_parts[0]), _dot_nt(kt[:, LANES:], q_parts[1])], axis=1)
        return _dot_nt(kt, qs)

    def direct(kt, vtt, acc):
        p = jnp.exp2(scores_t(kt)).astype(BF16)
        v1 = jnp.concatenate([vtt, jnp.ones((SUM_ROWS, vtt.shape[1]), BF16)], axis=0)
        return acc + _dot(v1, p)

    if online:
        carry = (jnp.full((2 * tq, 1), -jnp.inf, F32), jnp.zeros((2 * tq, 1), F32),
                 jnp.zeros((2 * tq, LANES), F32))
        if s_len:
            carry = lax.fori_loop(0, s_len // tk, body, carry)
        if c_len:
            carry = step(kc_ref[0], vc_ref[0], carry)
        _, l, acc = carry
        o = acc / l
    else:
        acc = jnp.zeros((LANES + SUM_ROWS, 2 * tq), F32)
        for t in range(s_len // tk):
            acc = direct(k_ref[0, t * tk:(t + 1) * tk, :], v_ref[0, :, t * tk:(t + 1) * tk], acc)
        if c_len:
            acc = direct(kc_ref[0], vc_ref[0], acc)
        o_t = acc[:LANES] / acc[LANES:LANES + 1]
    if mode == "diff":
        lp = lam_ref[...]
        lam = (jnp.exp(jnp.sum(lp[0:1] * lp[1:2], axis=-1, keepdims=True))
               - jnp.exp(jnp.sum(lp[2:3] * lp[3:4], axis=-1, keepdims=True)) + lam_init)
        if online:
            d = o[:tq] - lam * o[tq:]
            dn = d * lax.rsqrt(jnp.mean(d * d, axis=-1, keepdims=True) + NORM_EPS)
        else:
            d_t = o_t[:, :tq] - lam * o_t[:, tq:]
            dn = (d_t * lax.rsqrt(jnp.mean(d_t * d_t, axis=0, keepdims=True) + NORM_EPS)).T
        o_ref[0] = (dn * sg_ref[...] * (1.0 - lam_init)).astype(BF16)
    elif online:
        o_ref[0] = _pair_merge(o, tq).astype(BF16)
    else:
        row = lax.broadcasted_iota(jnp.int32, (LANES, tq), 0)
        o_ref[0] = jnp.where(row < HEAD_DIM, o_t[:, :tq], o_t[:, tq:]).T.astype(BF16)


def _flash(q, k, v, kc, vc, *, score_bound=None, **kw):
    if score_bound is None:
        return _flash_call(q, k, v, kc, vc, online=True, **kw)
    return lax.cond(score_bound <= EXP2_SAFE_RANGE,
                    lambda: _flash_call(q, k, jnp.swapaxes(v, 1, 2), kc, jnp.swapaxes(vc, 1, 2), online=False, **kw),
                    lambda: _flash_call(q, k, v, kc, vc, online=True, **kw))


def _flash_call(q, k, v, kc, vc, *, mode, online, kv_map, n_chunks, tq, tk, extra=(), lam_init=0.0):
    b, s, _ = q.shape
    qw = 2 * LANES if mode == "mla" else LANES
    c_len = kc.shape[1]
    if k is None:
        k, v, s_len = kc, vc, 0
    else:
        s_len = k.shape[1]
    kk = k.shape[1]
    if online:
        v_spec = lambda n: pl.BlockSpec((1, n, LANES), lambda bi, j, i: (bi, 0, kv_map(j)))
    else:
        v_spec = lambda n: pl.BlockSpec((1, LANES, n), lambda bi, j, i: (bi, kv_map(j), 0))
    in_specs = [
        pl.BlockSpec((1, tq, qw), lambda bi, j, i: (bi, i, j)),
        pl.BlockSpec((1, kk, qw), lambda bi, j, i: (bi, 0, kv_map(j) if mode != "mla" else j)),
        v_spec(kk),
        pl.BlockSpec((1, c_len, qw), lambda bi, j, i: (bi, 0, kv_map(j) if mode != "mla" else j)),
        v_spec(c_len),
    ]
    for e in extra:
        in_specs.append(pl.BlockSpec(e.shape, lambda bi, j, i: (0, 0)))
    return pl.pallas_call(
        functools.partial(_flash_kernel, mode=mode, online=online, tq=tq, tk=tk, s_len=s_len, c_len=c_len,
                          lam_init=lam_init),
        out_shape=jax.ShapeDtypeStruct((b, s, n_chunks * LANES), BF16),
        grid=(b, n_chunks, s // tq),
        in_specs=in_specs,
        out_specs=pl.BlockSpec((1, tq, LANES), lambda bi, j, i: (bi, i, j)),
        compiler_params=_cparams(("parallel", "parallel", "parallel")),
        name="flash_" + mode + ("_online" if online else ""),
    )(q, k, v, kc, vc, *extra)


def _post_ffn_kernel(oa_ref, ob_ref, wa_ref, wb_ref, x_ref, g1_ref, ng_ref, sh_ref, sc_ref, g2_ref,
                     wg_ref, wu_ref, wd_ref, o_ref, *, tf):
    y = _dot(oa_ref[0], wa_ref[...]) + _dot(ob_ref[0], wb_ref[...])
    x1 = x_ref[0] + g1_ref[0] * y
    h = _norm_mod(x1, ng_ref[...], sh_ref[0], sc_ref[0]).astype(BF16)
    acc = jnp.zeros(o_ref.shape[1:], F32)
    for f in range(wg_ref.shape[1] // tf):
        sl = slice(f * tf, (f + 1) * tf)
        a = _dot(h, wg_ref[:, sl])
        u = _dot(h, wu_ref[:, sl])
        acc = acc + _dot((_silu(a) * u).astype(BF16), wd_ref[sl, :])
    o_ref[0] = x1 + g2_ref[0] * acc


def _post_ffn(oa, ob, wa, wb, x, gate1, ng, shift, scale, gate2, wg, wu, wd, *, tm, tf=256):
    b, s, d = x.shape
    f = wg.shape[1]
    row = lambda bi, i: (bi, i, 0)
    vec = lambda bi, i: (bi, 0, 0)
    const = lambda bi, i: (0, 0)
    return pl.pallas_call(
        functools.partial(_post_ffn_kernel, tf=tf),
        out_shape=jax.ShapeDtypeStruct((b, s, d), F32),
        grid=(b, s // tm),
        in_specs=[
            pl.BlockSpec((1, tm, 512), row),
            pl.BlockSpec((1, tm, 512), row),
            pl.BlockSpec((512, d), const),
            pl.BlockSpec((512, d), const),
            pl.BlockSpec((1, tm, d), row),
            pl.BlockSpec((1, 1, d), vec),
            pl.BlockSpec((1, d), const),
            pl.BlockSpec((1, 1, d), vec),
            pl.BlockSpec((1, 1, d), vec),
            pl.BlockSpec((1, 1, d), vec),
            pl.BlockSpec((d, f), const),
            pl.BlockSpec((d, f), const),
            pl.BlockSpec((f, d), const),
        ],
        out_specs=pl.BlockSpec((1, tm, d), row),
        compiler_params=_cparams(("parallel", "parallel")),
        name="post_ffn",
    )(oa, ob, wa, wb, x, gate1, ng, shift, scale, gate2, wg, wu, wd)


def _rms(t, g):
    return t * lax.rsqrt(jnp.mean(t * t, axis=-1, keepdims=True) + NORM_EPS) * g


def _odd_proj_kernel(x_ref, sh_ref, sc_ref, ng_ref, w_ref, wuq_ref, wukv_ref, e_ref, e2_ref, g_ref,
                     cos_ref, sin_ref, cosm_ref, sinm_ref,
                     qc_ref, kc_ref, vc_ref, qd_ref, kd_ref, vd_ref, *, use_rope):
    g = g_ref[...]
    e = e_ref[...]
    e2 = e2_ref[...]
    inv_n = 1.0 / HEAD_DIM
    for rows in _row_blocks(x_ref.shape[1]):
        h = _norm_mod(x_ref[0, rows, :], ng_ref[...], sh_ref[0], sc_ref[0])
        p = _dot(h.astype(BF16), w_ref[...])
        qc = _group_norm(p[:, 0:512], e, g[0:1, 0:512], inv_n)
        kc = _group_norm(p[:, 512:768], e, g[1:2, 0:256], inv_n)
        cq = _rms(p[:, 1024:1280], g[2:3, 0:256])
        ckv = _rms(p[:, 1280:1408], g[3:4, 0:128])
        krc = p[:, 1408:1536]
        qd = _dot(cq.astype(BF16), wuq_ref[...])
        kvv = _dot(ckv.astype(BF16), wukv_ref[...])
        kd = kvv[:, 0:1024] + jnp.concatenate([krc] * MLA_HEADS, axis=1)
        qd = _group_norm(qd, e2, g[4:5], 1.0 / MLA_QK)
        kd = _group_norm(kd, e2, g[5:6], 1.0 / MLA_QK)
        if use_rope:
            lane = lax.broadcasted_iota(jnp.int32, (1, LANES), 1)
            cos, sin = cos_ref[rows, :], sin_ref[rows, :]
            qc = _rope_chunks(qc, cos, sin, (lane & 16) != 0, 16)
            kc = _rope_chunks(kc, cos, sin, (lane & 16) != 0, 16)
            cosm, sinm = cosm_ref[rows, :], sinm_ref[rows, :]
            qd = _rope_chunks(qd, cosm, sinm, (lane & 8) != 0, 8)
            kd = _rope_chunks(kd, cosm, sinm, (lane & 8) != 0, 8)
        qc_ref[0, rows, :] = qc.astype(BF16)
        kc_ref[0, rows, :] = kc.astype(BF16)
        vc_ref[0, rows, :] = p[:, 768:1024].astype(BF16)
        qd_ref[0, rows, :] = qd.astype(BF16)
        kd_ref[0, rows, :] = kd.astype(BF16)
        vd_ref[0, rows, :] = kvv[:, 1024:1536].astype(BF16)


def _odd_proj(x, shift, scale, ng, w, wuq, wukv, e64, e128, gains, cos, sin, cosm, sinm, *, use_rope, tm):
    b, s, d = x.shape
    row = lambda bi, i: (bi, i, 0)
    vec = lambda bi, i: (bi, 0, 0)
    const = lambda bi, i: (0, 0)
    tab = lambda bi, i: (i, 0)
    widths = (512, 256, 256, 1024, 1024, 512)
    return pl.pallas_call(
        functools.partial(_odd_proj_kernel, use_rope=use_rope),
        out_shape=tuple(jax.ShapeDtypeStruct((b, s, n), BF16) for n in widths),
        grid=(b, s // tm),
        in_specs=[
            pl.BlockSpec((1, tm, d), row),
            pl.BlockSpec((1, 1, d), vec),
            pl.BlockSpec((1, 1, d), vec),
            pl.BlockSpec((1, d), const),
            pl.BlockSpec(w.shape, const),
            pl.BlockSpec(wuq.shape, const),
            pl.BlockSpec(wukv.shape, const),
            pl.BlockSpec(e64.shape, const),
            pl.BlockSpec(e128.shape, const),
            pl.BlockSpec(gains.shape, const),
            pl.BlockSpec((tm, LANES), tab),
            pl.BlockSpec((tm, LANES), tab),
            pl.BlockSpec((tm, LANES), tab),
            pl.BlockSpec((tm, LANES), tab),
        ],
        out_specs=tuple(pl.BlockSpec((1, tm, n), row) for n in widths),
        compiler_params=_cparams(("parallel", "parallel")),
        name="odd_proj",
    )(x, shift, scale, ng, w, wuq, wukv, e64, e128, gains, cos, sin, cosm, sinm)


def _mla_rope_tables(seq):
    cos, sin = _rope_tables(seq, MLA_ROPE, MLA_ROPE // 4)
    lane = jnp.arange(LANES)
    on = (lane >= MLA_NOPE) & (lane < MLA_QK)
    return jnp.where(on[None, :], cos, 1.0), jnp.where(on[None, :], sin, 0.0)


def _route_block(h, cols, rt_ref, u_ref, ei_ref, gt_ref, rk_ref, before0):
    h_hi = h.astype(BF16)
    h_lo = (h - h_hi.astype(F32)).astype(BF16)
    both = _dot(h_hi, rt_ref[...]) + _dot(h_lo, rt_ref[...])
    logits = (both[:, :LANES] + both[:, LANES:]).T[:N_EXPERTS]
    eidx = lax.broadcasted_iota(jnp.int32, logits.shape, 0)
    m1 = jnp.max(logits, axis=0, keepdims=True)
    i1 = jnp.min(jnp.where(logits == m1, eidx, N_EXPERTS), axis=0, keepdims=True)
    rest = jnp.where(eidx == i1, -jnp.inf, logits)
    m2 = jnp.max(rest, axis=0, keepdims=True)
    i2 = jnp.min(jnp.where(rest == m2, eidx, N_EXPERTS), axis=0, keepdims=True)
    e2 = jnp.exp(m2 - m1)
    g1 = 1.0 / (1.0 + e2)
    sel1 = eidx == i1
    sel2 = eidx == i2
    onehot = jnp.where(sel1 | sel2, 1.0, 0.0)
    before = _dot(onehot.astype(BF16), u_ref[...]) + before0
    r1 = jnp.sum(jnp.where(sel1, before, 0.0), axis=0, keepdims=True)
    r2 = jnp.sum(jnp.where(sel2, before, 0.0), axis=0, keepdims=True)
    ei_ref[:, cols] = jnp.concatenate([i1, i2], axis=0)
    gt_ref[:, cols] = jnp.concatenate([g1, e2 * g1], axis=0)
    rk_ref[:, cols] = jnp.concatenate([r1, r2], axis=0).astype(jnp.int32)
    return before0 + jnp.sum(onehot, axis=1, keepdims=True)


def _post_attn_router_kernel(oa_ref, ob_ref, wa_ref, wb_ref, x_ref, g_ref, ng_ref, sh_ref, sc_ref, rt_ref, u_ref,
                             x1_ref, h2_ref, ei_ref, gt_ref, rk_ref, cnt_ref, carry_ref):
    @pl.when((pl.program_id(0) == 0) & (pl.program_id(1) == 0))
    def _():
        carry_ref[...] = jnp.zeros_like(carry_ref)

    routed = carry_ref[:, 0:1]
    for rows in _row_blocks(x_ref.shape[1]):
        y = _dot(oa_ref[0, rows, :], wa_ref[...]) + _dot(ob_ref[0, rows, :], wb_ref[...])
        x1 = x_ref[0, rows, :] + g_ref[0] * y
        x1_ref[0, rows, :] = x1
        h = _norm_mod(x1, ng_ref[...], sh_ref[0], sc_ref[0])
        h2_ref[0, rows, :] = h.astype(BF16)
        routed = _route_block(h, rows, rt_ref, u_ref, ei_ref, gt_ref, rk_ref, routed)
    carry_ref[...] = jnp.broadcast_to(routed, carry_ref.shape)
    cnt_ref[...] = jnp.broadcast_to(routed, cnt_ref.shape).astype(jnp.int32)


def _post_attn_router(oa, ob, wa, wb, x, gate, ng, shift, scale, router_t, *, tm):
    b, s, d = x.shape
    n = b * s
    nt = s // tm
    blk = _row_blocks(tm)[0].stop
    tri = (jnp.arange(blk)[:, None] < jnp.arange(blk)[None, :]).astype(BF16)
    row = lambda bi, i: (bi, i, 0)
    vec = lambda bi, i: (bi, 0, 0)
    const = lambda bi, i: (0, 0)
    flat = lambda bi, i: (0, bi * nt + i)
    return pl.pallas_call(
        _post_attn_router_kernel,
        out_shape=(jax.ShapeDtypeStruct((b, s, d), F32), jax.ShapeDtypeStruct((b, s, d), BF16),
                   jax.ShapeDtypeStruct((2, n), jnp.int32), jax.ShapeDtypeStruct((2, n), F32),
                   jax.ShapeDtypeStruct((2, n), jnp.int32), jax.ShapeDtypeStruct((N_EXPERTS, LANES), jnp.int32)),
        grid=(b, nt),
        in_specs=[
            pl.BlockSpec((1, tm, 512), row),
            pl.BlockSpec((1, tm, 512), row),
            pl.BlockSpec((512, d), const),
            pl.BlockSpec((512, d), const),
            pl.BlockSpec((1, tm, d), row),
            pl.BlockSpec((1, 1, d), vec),
            pl.BlockSpec((1, d), const),
            pl.BlockSpec((1, 1, d), vec),
            pl.BlockSpec((1, 1, d), vec),
            pl.BlockSpec((d, 2 * LANES), const),
            pl.BlockSpec((blk, blk), const),
        ],
        out_specs=(pl.BlockSpec((1, tm, d), row), pl.BlockSpec((1, tm, d), row),
                   pl.BlockSpec((2, tm), flat), pl.BlockSpec((2, tm), flat), pl.BlockSpec((2, tm), flat),
                   pl.BlockSpec((N_EXPERTS, LANES), const)),
        scratch_shapes=[pltpu.VMEM((N_EXPERTS, LANES), F32)],
        compiler_params=_cparams(("arbitrary", "arbitrary")),
        name="post_attn_router",
    )(oa, ob, wa, wb, x, gate, ng, shift, scale, router_t, tri)


MOE_SUB = 256
MOE_BLK = 1024
MOE_TC = 256
MOE_GROUP = 4
MOE_NBUF = 16
MOE_AHEAD = MOE_NBUF - MOE_GROUP


def _lookup(table, idx):
    hit = idx[..., None] == jnp.arange(table.shape[0])
    return jnp.sum(jnp.where(hit, table, 0), axis=-1)


def _lookup_cols(rows, idx):
    hit = idx[:, None] == jnp.arange(rows.shape[1])[None, :]
    return jnp.sum(jnp.where(hit, rows, 0), axis=1)


def _moe_plan(ei, rk, counts, n_tok):
    n_chunks = n_tok // MOE_TC
    cap = 2 * n_tok + N_EXPERTS * MOE_BLK
    nb_sub = cap // MOE_SUB
    nb_blk = cap // MOE_BLK
    padded = ((counts + MOE_BLK - 1) // MOE_BLK) * MOE_BLK
    pad_end = jnp.cumsum(padded)
    pad_start = pad_end - padded
    dest = _lookup(pad_start, ei) + rk

    onehot = ei.reshape(2, n_chunks, MOE_TC)[..., None] == jnp.arange(N_EXPERTS)
    cnt = jnp.sum(onehot, axis=(0, 2)).astype(jnp.int32)
    cum = jnp.concatenate([jnp.zeros((1, N_EXPERTS), jnp.int32), jnp.cumsum(cnt, axis=0)], axis=0)

    lo = pad_start[None, :] + cum[:-1]
    has = cnt > 0
    blk0 = lo // MOE_SUB
    two = has & ((lo + cnt - 1) // MOE_SUB > blk0)
    blk1 = jnp.minimum(blk0 + 1, nb_sub - 1)
    win_ok = jnp.stack([has, two], axis=-1).reshape(n_chunks, 2 * N_EXPERTS)
    win_blk = jnp.stack([blk0, blk1], axis=-1).reshape(n_chunks, 2 * N_EXPERTS)
    c_count = jnp.sum(win_ok, axis=1).astype(jnp.int32)
    c_start = (jnp.cumsum(c_count) - c_count).astype(jnp.int32)
    win_pos = jnp.cumsum(win_ok, axis=1) - win_ok
    front = win_ok[:, :, None] & (win_pos[:, :, None] == jnp.arange(2 * N_EXPERTS)[None, None, :])
    packed = jnp.sum(jnp.where(front, win_blk[:, :, None], 0), axis=1)
    q = jnp.arange(n_chunks * 2 * N_EXPERTS)
    c_of_q = jnp.minimum(jnp.sum((c_start + c_count)[None, :] <= q[:, None], axis=1), n_chunks - 1)
    row_q = jnp.dot((c_of_q[:, None] == jnp.arange(n_chunks)[None, :]).astype(F32), packed.astype(F32),
                    precision=HIGHEST)
    p_of_q = q - _lookup(c_start, c_of_q)
    c_blocks = _lookup_cols(row_q, p_of_q).astype(jnp.int32)

    sb = jnp.arange(nb_sub)
    e_sb = jnp.minimum(jnp.sum(sb[:, None] * MOE_SUB >= pad_end[None, :], axis=1), N_EXPERTS - 1)
    counts_sb = _lookup(counts, e_sb)
    r0 = sb * MOE_SUB - _lookup(pad_start, e_sb)
    valid_sb = (sb * MOE_SUB < pad_end[-1]) & (r0 < counts_sb)
    r1 = jnp.minimum(r0 + MOE_SUB, counts_sb) - 1
    hit_sb = e_sb[:, None] == jnp.arange(N_EXPERTS)[None, :]
    cum_sb = jnp.sum(jnp.where(hit_sb[:, None, :], cum[None, 1:, :], 0), axis=-1)
    cmin = jnp.sum(cum_sb <= r0[:, None], axis=1)
    cmax = jnp.sum(cum_sb <= r1[:, None], axis=1)
    items = jnp.where(valid_sb, cmax - cmin + 1, 0)
    off_end = jnp.cumsum(items)
    off = off_end - items
    total = off_end[-1]
    w_max = (2 * n_tok) // MOE_SUB + N_EXPERTS + N_EXPERTS * (n_chunks - 1)
    w = jnp.arange(w_max)
    wv = w < total
    wq = jnp.minimum(w, total - 1)
    w_sb = jnp.minimum(jnp.sum(off_end[None, :] <= wq[:, None], axis=1), nb_sub - 1)
    off_w = _lookup(off, w_sb)
    w_chunk = _lookup(cmin, w_sb) + (wq - off_w)
    w_flag = wv.astype(jnp.int32) + 2 * (wq == off_w).astype(jnp.int32)

    bi = jnp.arange(nb_blk)
    e_blk = jnp.minimum(jnp.sum(bi[:, None] * MOE_BLK >= pad_end[None, :], axis=1), N_EXPERTS - 1)
    rows = jnp.where(bi * MOE_BLK < pad_end[-1],
                     _lookup(counts, e_blk) - (bi * MOE_BLK - _lookup(pad_start, e_blk)), 0)
    n_sub = jnp.clip((rows + MOE_SUB - 1) // MOE_SUB, 0, MOE_BLK // MOE_SUB)
    g_count = jnp.sum(items.reshape(nb_blk, MOE_BLK // MOE_SUB), axis=1)
    g_start = jnp.cumsum(g_count) - g_count
    return dict(dest=dest.astype(jnp.int32), cap=cap,
                w_sb=w_sb.astype(jnp.int32), w_chunk=w_chunk.astype(jnp.int32), w_flag=w_flag,
                g_start=g_start.astype(jnp.int32), g_count=g_count.astype(jnp.int32),
                e_blk=e_blk.astype(jnp.int32), n_sub=n_sub.astype(jnp.int32),
                c_blocks=c_blocks, c_count=c_count, c_start=c_start)


def _moe_gather_kernel(st_ref, cn_ref, wsb_ref, wch_ref, t_hbm, d_ref, g_ref, xg_ref, gs_ref, tbuf, sem):
    i = pl.program_id(0)
    n = cn_ref[i]
    s0 = st_ref[i]
    total = st_ref[pl.num_programs(0) - 1] + cn_ref[pl.num_programs(0) - 1]
    subs = MOE_BLK // MOE_SUB

    def chunk_copy(item):
        slot = lax.rem(item, MOE_NBUF)
        start = pl.multiple_of(wch_ref[item] * MOE_TC, MOE_TC)
        return pltpu.make_async_copy(t_hbm.at[pl.ds(start, MOE_TC), :], tbuf.at[slot], sem.at[slot])

    @pl.when(i == 0)
    def _():
        for a in range(MOE_AHEAD):
            @pl.when(a < total)
            def _():
                chunk_copy(a).start()

    xg_ref[...] = jnp.zeros(xg_ref.shape, BF16)
    gs_ref[...] = jnp.zeros(gs_ref.shape, F32)

    def arrive(item):
        chunk_copy(item).wait()

        @pl.when(item + MOE_AHEAD < total)
        def _():
            chunk_copy(item + MOE_AHEAD).start()

    def contribute(item):
        sb = wsb_ref[item]
        chunk = wch_ref[item]
        d = d_ref[chunk]
        g = g_ref[chunk]
        srow = lax.broadcasted_iota(jnp.int32, (MOE_SUB, MOE_TC), 0) + sb * MOE_SUB
        hit0 = srow == d[0:1]
        hit1 = srow == d[1:2]
        p = jnp.where(hit0 | hit1, 1.0, 0.0).astype(BF16)
        rows = _dot(p, tbuf[lax.rem(item, MOE_NBUF)]).astype(BF16)
        gate = jnp.sum(jnp.where(hit0, g[0:1], 0.0) + jnp.where(hit1, g[1:2], 0.0), axis=-1, keepdims=True)
        r0 = pl.multiple_of((sb - i * subs) * MOE_SUB, MOE_SUB)
        xg_ref[pl.ds(r0, MOE_SUB), :] += rows
        gs_ref[pl.ds(r0, MOE_SUB), :] += gate

    def group(k, carry):
        item = s0 + MOE_GROUP * k
        for a in range(MOE_GROUP):
            arrive(item + a)
        for a in range(MOE_GROUP):
            contribute(item + a)
        return carry

    def single(k, carry):
        arrive(s0 + grouped + k)
        contribute(s0 + grouped + k)
        return carry

    grouped = (n // MOE_GROUP) * MOE_GROUP
    lax.fori_loop(0, n // MOE_GROUP, group, 0)
    lax.fori_loop(0, n - grouped, single, 0)


def _moe_gather(t, dest, gates, plan):
    n_tok, d = t.shape
    cap = plan["cap"]
    n_chunks = n_tok // MOE_TC
    by_chunk = lambda a: a.reshape(2, n_chunks, MOE_TC).transpose(1, 0, 2)
    grid_spec = pltpu.PrefetchScalarGridSpec(
        num_scalar_prefetch=4,
        grid=(cap // MOE_BLK,),
        in_specs=[
            pl.BlockSpec(memory_space=pl.ANY),
            pl.BlockSpec((n_chunks, 2, MOE_TC), lambda i, *_: (0, 0, 0)),
            pl.BlockSpec((n_chunks, 2, MOE_TC), lambda i, *_: (0, 0, 0)),
        ],
        out_specs=(pl.BlockSpec((MOE_BLK, d), lambda i, *_: (i, 0)),
                   pl.BlockSpec((MOE_BLK, 1), lambda i, *_: (i, 0))),
        scratch_shapes=[pltpu.VMEM((MOE_NBUF, MOE_TC, d), BF16), pltpu.SemaphoreType.DMA((MOE_NBUF,))],
    )
    return pl.pallas_call(
        _moe_gather_kernel,
        out_shape=(jax.ShapeDtypeStruct((cap, d), BF16), jax.ShapeDtypeStruct((cap, 1), F32)),
        grid_spec=grid_spec,
        compiler_params=_cparams(("arbitrary",)),
        name="moe_gather",
    )(plan["g_start"], plan["g_count"], plan["w_sb"], plan["w_chunk"], t, by_chunk(dest), by_chunk(gates))


def _moe_expert_kernel(eb_ref, ns_ref, x_ref, gs_ref, w1_ref, w3_ref, w2_ref, y_ref):
    n_sub = ns_ref[pl.program_id(0)]
    subs = MOE_BLK // MOE_SUB

    def mlp(rows):
        xs = x_ref[rows, :]
        acc = jnp.zeros((rows.stop - rows.start, y_ref.shape[1]), F32)
        for f in range(w1_ref.shape[1] // MOE_TF):
            cols = slice(f * MOE_TF, (f + 1) * MOE_TF)
            a = _dot(xs, w1_ref[:, cols])
            u = _dot(xs, w3_ref[:, cols])
            acc = acc + _dot((_silu(a) * u).astype(BF16), w2_ref[cols, :])
        y_ref[rows, :] = (acc * gs_ref[rows, :]).astype(BF16)

    @pl.when(n_sub == subs)
    def _():
        mlp(slice(0, MOE_BLK))

    for sub in range(subs):
        rows = slice(sub * MOE_SUB, (sub + 1) * MOE_SUB)

        @pl.when((n_sub < subs) & (sub < n_sub))
        def _():
            mlp(rows)

        @pl.when(sub >= n_sub)
        def _():
            y_ref[rows, :] = jnp.zeros((MOE_SUB, y_ref.shape[1]), BF16)


def _moe_experts(xg, gs, w1, w3, w2, plan):
    cap, d = xg.shape
    f = w1.shape[2]
    once = pl.Buffered(1)
    grid_spec = pltpu.PrefetchScalarGridSpec(
        num_scalar_prefetch=2,
        grid=(cap // MOE_BLK,),
        in_specs=[
            pl.BlockSpec((MOE_BLK, d), lambda i, eb, ns: (i, 0)),
            pl.BlockSpec((MOE_BLK, 1), lambda i, eb, ns: (i, 0)),
            pl.BlockSpec((None, d, f), lambda i, eb, ns: (eb[i], 0, 0), pipeline_mode=once),
            pl.BlockSpec((None, d, f), lambda i, eb, ns: (eb[i], 0, 0), pipeline_mode=once),
            pl.BlockSpec((None, f, d), lambda i, eb, ns: (eb[i], 0, 0), pipeline_mode=once),
        ],
        out_specs=pl.BlockSpec((MOE_BLK, d), lambda i, eb, ns: (i, 0)),
    )
    return pl.pallas_call(
        _moe_expert_kernel,
        out_shape=jax.ShapeDtypeStruct((cap, d), BF16),
        grid_spec=grid_spec,
        compiler_params=_cparams(("arbitrary",)),
        name="moe_experts",
    )(plan["e_blk"], plan["n_sub"], xg, gs, w1, w3, w2)


MOE_WIN = 2 * N_EXPERTS


def _moe_combine_kernel(st_ref, cn_ref, bl_ref, dt_ref, y_hbm, x1_ref, g_ref, o_ref, ybuf, sem):
    c = pl.program_id(0)
    n = cn_ref[c]
    s0 = st_ref[c]
    total = st_ref[pl.num_programs(0) - 1] + cn_ref[pl.num_programs(0) - 1]
    d = dt_ref[...]

    def block_copy(item):
        slot = lax.rem(item, MOE_NBUF)
        start = pl.multiple_of(bl_ref[item] * MOE_SUB, MOE_SUB)
        return pltpu.make_async_copy(y_hbm.at[pl.ds(start, MOE_SUB), :], ybuf.at[slot], sem.at[slot])

    @pl.when(c == 0)
    def _():
        for a in range(MOE_AHEAD):
            @pl.when(a < total)
            def _():
                block_copy(a).start()

    def arrive(item):
        block_copy(item).wait()

        @pl.when(item + MOE_AHEAD < total)
        def _():
            block_copy(item + MOE_AHEAD).start()

    def rows_of(item):
        scol = lax.broadcasted_iota(jnp.int32, (MOE_TC, MOE_SUB), 1) + bl_ref[item] * MOE_SUB
        p = jnp.where((scol == d[:, 0:1]) | (scol == d[:, 1:2]), 1.0, 0.0).astype(BF16)
        return _dot(p, ybuf[lax.rem(item, MOE_NBUF)])

    def group(k, acc):
        item = s0 + MOE_GROUP * k
        for a in range(MOE_GROUP):
            arrive(item + a)
        for a in range(MOE_GROUP):
            acc = acc + rows_of(item + a)
        return acc

    def single(k, acc):
        arrive(s0 + grouped + k)
        return acc + rows_of(s0 + grouped + k)

    grouped = (n // MOE_GROUP) * MOE_GROUP
    acc = lax.fori_loop(0, n // MOE_GROUP, group, jnp.zeros(o_ref.shape, F32))
    acc = lax.fori_loop(0, n - grouped, single, acc)
    o_ref[...] = x1_ref[...] + g_ref[0] * acc


def _moe_combine(y, dest_t, x1, gate, plan, seq):
    n_tok, d = x1.shape
    n_chunks = n_tok // MOE_TC
    per_b = seq // MOE_TC
    grid_spec = pltpu.PrefetchScalarGridSpec(
        num_scalar_prefetch=3,
        grid=(n_chunks,),
        in_specs=[
            pl.BlockSpec((MOE_TC, 2), lambda c, *_: (c, 0)),
            pl.BlockSpec(memory_space=pl.ANY),
            pl.BlockSpec((MOE_TC, d), lambda c, *_: (c, 0)),
            pl.BlockSpec((1, 1, d), lambda c, *_: (c // per_b, 0, 0)),
        ],
        out_specs=pl.BlockSpec((MOE_TC, d), lambda c, *_: (c, 0)),
        scratch_shapes=[pltpu.VMEM((MOE_NBUF, MOE_SUB, d), BF16), pltpu.SemaphoreType.DMA((MOE_NBUF,))],
    )
    return pl.pallas_call(
        _moe_combine_kernel,
        out_shape=jax.ShapeDtypeStruct((n_tok, d), F32),
        grid_spec=grid_spec,
        compiler_params=_cparams(("arbitrary",)),
        name="moe_combine",
    )(plan["c_start"], plan["c_count"], plan["c_blocks"], dest_t, y, x1, gate)


def _split_mod(mod_l, batch):
    d = D_MODEL
    lat = tuple(mod_l[:batch, k * d:(k + 1) * d][:, None, :] for k in range(ADA_CHUNKS))
    ctx = tuple(jnp.broadcast_to(mod_l[batch:batch + 1, k * d:(k + 1) * d][:, None, :], (batch, 1, d))
                for k in range(ADA_CHUNKS))
    return lat, ctx


def _score_bound(q_gain, k_gain, n):
    return 1.02 * n * jnp.max(jnp.abs(q_gain)) * jnp.max(jnp.abs(k_gain))


def _block_ones(n, block):
    idx = jnp.arange(n) // block
    return (idx[:, None] == idx[None, :]).astype(BF16)


def _even_layer(x, xc, mod_l, norm1_g, norm2_g, w_in, w_out, na_q_g, na_k_g, na_rpb, diff_q_g, diff_k_g,
                lq1, lk1, lq2, lk2, subln_g, wg, wu, wd, layer_idx, need_ctx):
    b, s, d = x.shape
    (sh1, sc1, g1, sh2, sc2, g2), (csh1, csc1, cg1, csh2, csc2, cg2) = _split_mod(mod_l, b)
    qscale = HEAD_DIM ** -0.5 * LOG2E
    lam_init = 0.8 - 0.6 * math.exp(-0.3 * layer_idx)
    hg = jnp.stack([jnp.tile(na_q_g, 8) * qscale, jnp.tile(na_k_g, 8),
                    jnp.tile(diff_q_g, 8) * qscale, jnp.tile(diff_k_g, 8)]).astype(F32)
    e64 = _block_ones(MXU_DIM, HEAD_DIM)
    cos, sin = _rope_tables(s, HEAD_DIM, 16)
    ng1 = norm1_g[None, :]
    ng2 = norm2_g[None, :]
    w_in_b = w_in.astype(BF16)
    wo_a = w_out[:512].astype(BF16)
    wo_b = w_out[512:].astype(BF16)
    wg_b, wu_b, wd_b = wg.astype(BF16), wu.astype(BF16), wd.astype(BF16)
    lam_p = jnp.stack([lq1, lk1, lq2, lk2]).astype(F32)
    sg = subln_g[None, :].astype(F32)
    ident = lambda j: j

    tm = min(ROW_TM, s)
    tc = xc.shape[1]
    qa, ka, va, qb, kb, vb = _even_proj(x, sh1, sc1, ng1, w_in_b, e64, hg, cos, sin, use_rope=True, tm=tm)
    qca, kca, vca, qcb, kcb, vcb = _even_proj(xc, csh1, csc1, ng1, w_in_b, e64, hg, cos, sin, use_rope=False, tm=tc)
    out_a = _na_attention(qa, ka, va, kca, vca, _na_bias_table(na_rpb), _score_bound(hg[0], hg[1], HEAD_DIM))
    out_b = _flash(qb, kb, vb, kcb, vcb, mode="diff", kv_map=ident, n_chunks=4, tq=FLASH_TQ, tk=FLASH_TK,
                   extra=(lam_p, sg), lam_init=lam_init, score_bound=_score_bound(hg[2], hg[3], HEAD_DIM))
    x2 = _post_ffn(out_a, out_b, wo_a, wo_b, x, g1, ng2, sh2, sc2, g2, wg_b, wu_b, wd_b, tm=tm)
    if not need_ctx:
        return x2, None
    oca = _flash(qca, None, None, kca, vca, mode="pair", kv_map=ident, n_chunks=4, tq=tc, tk=tc)
    ocb = _flash(qcb, None, None, kcb, vcb, mode="diff", kv_map=ident, n_chunks=4, tq=tc, tk=tc,
                 extra=(lam_p, sg), lam_init=lam_init)
    xc2 = _post_ffn(oca, ocb, wo_a, wo_b, xc, cg1, ng2, csh2, csc2, cg2, wg_b, wu_b, wd_b, tm=tc)
    return x2, xc2


def _odd_layer(x, xc, mod_l, norm1_g, norm2_g, w_in, w_out, gqa_q_g, gqa_k_g, cq_g, w_uq, ckv_g, w_ukv,
               mla_q_g, mla_k_g, router, w1, w3, w2):
    b, s, d = x.shape
    n_tok = b * s
    (sh1, sc1, g1, sh2, sc2, g2), (csh1, csc1, _, _, _, _) = _split_mod(mod_l, b)
    ng1 = norm1_g[None, :]
    ng2 = norm2_g[None, :]

    z = lambda n: jnp.zeros((d, n), w_in.dtype)
    k0, k1, v0, v1 = (w_in[:, 512 + 64 * i:576 + 64 * i] for i in range(4))
    w_p = jnp.concatenate([w_in[:, 0:512], k0, k0, k1, k1, v0, v0, v1, v1, w_in[:, 768:1024], w_in[:, 1024:1152],
                           z(MLA_NOPE), w_in[:, 1152:1184], z(LANES - MLA_QK)], axis=1).astype(BF16)
    wuq_p = jnp.pad(w_uq.reshape(MLA_Q_RANK, MLA_HEADS, MLA_QK), ((0, 0), (0, 0), (0, LANES - MLA_QK)))
    wuq_p = wuq_p.reshape(MLA_Q_RANK, MLA_HEADS * LANES).astype(BF16)
    ukv = w_ukv.reshape(MLA_KV_RANK, MLA_HEADS, MLA_NOPE + MLA_V)
    uk = jnp.pad(ukv[:, :, :MLA_NOPE], ((0, 0), (0, 0), (0, LANES - MLA_NOPE))).reshape(MLA_KV_RANK, -1)
    uv = ukv[:, :, MLA_NOPE:].reshape(MLA_KV_RANK, -1)
    wukv_p = jnp.concatenate([uk, uv], axis=1).astype(BF16)
    e64 = _block_ones(MXU_DIM, HEAD_DIM)
    e128 = _block_ones(MXU_DIM, LANES)
    qscale = HEAD_DIM ** -0.5 * LOG2E
    mscale = MLA_QK ** -0.5 * LOG2E
    pad_row = lambda v: jnp.pad(v, (0, MLA_HEADS * LANES - v.shape[0]))
    pad_head = lambda v: jnp.tile(jnp.pad(v, (0, LANES - MLA_QK)), MLA_HEADS)
    gains = jnp.stack([pad_row(jnp.tile(gqa_q_g, 8) * qscale), pad_row(jnp.tile(gqa_k_g, 4)),
                       pad_row(cq_g), pad_row(ckv_g), pad_head(mla_q_g) * mscale, pad_head(mla_k_g),
                       jnp.zeros((MLA_HEADS * LANES,), F32), jnp.zeros((MLA_HEADS * LANES,), F32)]).astype(F32)
    cos, sin = _rope_tables(s, HEAD_DIM, 16)
    cosm, sinm = _mla_rope_tables(s)
    wo_a = w_out[:512].astype(BF16)
    wo_b = w_out[512:].astype(BF16)

    tm = min(ROW_TM, s)
    tc = xc.shape[1]
    proj = functools.partial(_odd_proj, ng=ng1, w=w_p, wuq=wuq_p, wukv=wukv_p, e64=e64, e128=e128, gains=gains,
                             cos=cos, sin=sin, cosm=cosm, sinm=sinm)
    qc, kc, vc, qd, kd, vd = proj(x, sh1, sc1, use_rope=True, tm=tm)
    _, kcc, vcc, _, kcd, vcd = proj(xc, csh1, csc1, use_rope=False, tm=tc)
    out_c = _flash(qc, kc, vc, kcc, vcc, mode="pair", kv_map=lambda j: j // 2, n_chunks=4, tq=FLASH_TQ, tk=FLASH_TK,
                   score_bound=_score_bound(gains[0], gains[1], HEAD_DIM))
    out_d = _flash(qd, kd, vd, kcd, vcd, mode="mla", kv_map=lambda j: j, n_chunks=4, tq=FLASH_TQ, tk=FLASH_TK,
                   score_bound=_score_bound(gains[4], gains[5], MLA_QK))
    r_hi = router.astype(BF16)
    r_lo = (router - r_hi.astype(F32)).astype(BF16)
    pad_e = lambda m: jnp.pad(m, ((0, 0), (0, LANES - N_EXPERTS)))
    x1, h2, ei, gt, rk, cnt = _post_attn_router(out_c, out_d, wo_a, wo_b, x, g1, ng2, sh2, sc2,
                                                jnp.concatenate([pad_e(r_hi), pad_e(r_lo)], axis=1), tm=tm)
    plan = _moe_plan(ei, rk, cnt[:, 0], n_tok)
    xg, gs = _moe_gather(h2.reshape(n_tok, d), plan["dest"], gt, plan)
    y = _moe_experts(xg, gs, w1.astype(BF16), w3.astype(BF16), w2.astype(BF16), plan)
    out = _moe_combine(y, plan["dest"].T, x1.reshape(n_tok, d), g2, plan, s)
    return out.reshape(b, s, d)


def _mod_vectors(c, c_ctx, ada_w, ada_b):
    b = c.shape[0]
    cc = jnp.zeros((MOD_ROWS, D_MODEL), F32).at[:b].set(c).at[b].set(c_ctx)
    return _modvec(cc, ada_w, ada_b[:, None, :])


def kernel(x, c, ctx, c_ctx, ada_w, ada_b, norm1_g, norm2_g, ev_w_in, ev_w_out, na_q_g, na_k_g, na_rpb,
           diff_q_g, diff_k_g, diff_lq1, diff_lk1, diff_lq2, diff_lk2, diff_subln_g,
           ffn_w_gate, ffn_w_up, ffn_w_down, od_w_in, od_w_out, gqa_q_g, gqa_k_g, mla_cq_g, mla_w_uq,
           mla_ckv_g, mla_w_ukv, mla_q_g, mla_k_g, moe_router, moe_w1, moe_w3, moe_w2):
    mod = _mod_vectors(c, c_ctx, ada_w, ada_b)
    x, xc = _even_layer(x, ctx, mod[0], norm1_g[0], norm2_g[0], ev_w_in[0], ev_w_out[0], na_q_g[0], na_k_g[0],
                        na_rpb[0], diff_q_g[0], diff_k_g[0], diff_lq1[0], diff_lk1[0], diff_lq2[0], diff_lk2[0],
                        diff_subln_g[0], ffn_w_gate[0], ffn_w_up[0], ffn_w_down[0], 0, True)
    return _odd_layer(x, xc, mod[1], norm1_g[1], norm2_g[1], od_w_in[0], od_w_out[0], gqa_q_g[0], gqa_k_g[0],
                      mla_cq_g[0], mla_w_uq[0], mla_ckv_g[0], mla_w_ukv[0], mla_q_g[0], mla_k_g[0],
                      moe_router[0], moe_w1[0], moe_w3[0], moe_w2[0])
```

```python
import functools
import math

import jax
import jax.numpy as jnp
from jax import lax
from jax.experimental import pallas as pl
from jax.experimental.pallas import tpu as pltpu

F32 = jnp.float32
BF16 = jnp.bfloat16
HIGHEST = lax.Precision.HIGHEST

D_MODEL = 1024
GRID_W = 64
HEAD_DIM = 64
ROPE_THETA = 10000.0
NORM_EPS = 1e-6
NEG_INF = -1e30
ADA_CHUNKS = 6
LOG2E = 1.4426950408889634

NA_HEADS = 8
NA_WIN_H = 8
NA_WIN_W = 16
DIFF_HEADS = 4
DIFF_V_DIM = 2 * HEAD_DIM
GQA_Q_HEADS = 8
GQA_KV_HEADS = 2
MLA_HEADS = 8
MLA_NOPE = 64
MLA_ROPE = 32
MLA_QK = MLA_NOPE + MLA_ROPE
MLA_V = 64
MLA_Q_RANK = 256
MLA_KV_RANK = 128
D_FF = 2816
N_EXPERTS = 8
D_FF_EXPERT = 3584

LANES = 128
MXU_DIM = 256
VMEM_LIMIT = 56 * 1024 * 1024
MOD_ROWS = 16
MOE_TF = 512
SUM_ROWS = 16
ROW_TM = 1024
FLASH_TQ = 1024
FLASH_TK = 512
EXP2_SAFE_RANGE = 64.0


def _cparams(sem):
    return pltpu.CompilerParams(dimension_semantics=sem, vmem_limit_bytes=VMEM_LIMIT)


def _dot(a, b):
    return jnp.dot(a, b, preferred_element_type=F32)


def _dot_nt(a, b):
    return lax.dot_general(a, b, (((1,), (1,)), ((), ())), preferred_element_type=F32)


def _silu(x):
    return x * (1.0 / (1.0 + jnp.exp(-x)))


def _modvec_kernel(c_ref, w_ref, b_ref, o_ref):
    s = _silu(c_ref[...])
    o_ref[...] = jnp.dot(s, w_ref[...], preferred_element_type=F32, precision=HIGHEST) + b_ref[...]


def _modvec(cc, ada_w, ada_b):
    n_layers, d, n = ada_w.shape
    tn = 768
    return pl.pallas_call(
        _modvec_kernel,
        out_shape=jax.ShapeDtypeStruct((n_layers, MOD_ROWS, n), F32),
        grid=(n_layers, n // tn),
        in_specs=[
            pl.BlockSpec((MOD_ROWS, d), lambda l, j: (0, 0)),
            pl.BlockSpec((None, d, tn), lambda l, j: (l, 0, j)),
            pl.BlockSpec((None, 1, tn), lambda l, j: (l, 0, j)),
        ],
        out_specs=pl.BlockSpec((None, MOD_ROWS, tn), lambda l, j: (l, 0, j)),
        compiler_params=_cparams(("arbitrary", "arbitrary")),
        name="modvec",
    )(cc, ada_w, ada_b)


def _norm_mod(x, g, shift, scale):
    ms = jnp.mean(x * x, axis=-1, keepdims=True)
    return (x * lax.rsqrt(ms + NORM_EPS) * g) * (1.0 + scale) + shift


def _row_blocks(n, size=256):
    size = min(size, n)
    return [slice(r, r + size) for r in range(0, n, size)]


def _group_norm(t, e, g, inv_n):
    outs = []
    for c in range(t.shape[1] // MXU_DIM):
        cols = slice(c * MXU_DIM, (c + 1) * MXU_DIM)
        tc = t[:, cols]
        ss = _dot((tc * tc).astype(BF16), e)
        outs.append(tc * lax.rsqrt(ss * inv_n + NORM_EPS) * g[:, cols])
    return outs[0] if len(outs) == 1 else jnp.concatenate(outs, axis=1)


def _rope_chunks(t, cos, sin, lane_hi, shift):
    outs = []
    for c in range(t.shape[1] // LANES):
        tc = t[:, c * LANES:(c + 1) * LANES]
        up = pltpu.roll(tc, LANES - shift, 1)
        dn = pltpu.roll(tc, shift, 1)
        outs.append(tc * cos + jnp.where(lane_hi, dn, up) * sin)
    return outs[0] if len(outs) == 1 else jnp.concatenate(outs, axis=1)


def _rope_tables(seq, unit, pair_shift):
    pos = jnp.arange(seq)
    rows, cols = pos // GRID_W, pos % GRID_W
    quarter = unit // 4
    assert quarter == pair_shift
    freqs = ROPE_THETA ** (-jnp.arange(quarter, dtype=F32) / quarter)
    lane = jnp.arange(LANES)
    u = lane % unit
    use_col = (u // (unit // 2)) == 1
    fi = u % quarter
    p = jnp.where(use_col[None, :], cols[:, None], rows[:, None]).astype(F32)
    ang = p * freqs[fi][None, :]
    second = ((u % (unit // 2)) // quarter) == 1
    cos = jnp.cos(ang)
    sin = jnp.where(second[None, :], jnp.sin(ang), -jnp.sin(ang))
    return cos, sin


def _even_proj_kernel(x_ref, sh_ref, sc_ref, ng_ref, w_ref, e_ref, hg_ref, cos_ref, sin_ref,
                      qa_ref, ka_ref, va_ref, qb_ref, kb_ref, vb_ref, *, use_rope):
    hg = hg_ref[...]
    e = e_ref[...]
    inv_n = 1.0 / HEAD_DIM
    for rows in _row_blocks(x_ref.shape[1]):
        h = _norm_mod(x_ref[0, rows, :], ng_ref[...], sh_ref[0], sc_ref[0])
        p = _dot(h.astype(BF16), w_ref[...])
        qa = _group_norm(p[:, 0:512], e, hg[0:1], inv_n)
        ka = _group_norm(p[:, 512:1024], e, hg[1:2], inv_n)
        qb = _group_norm(p[:, 1536:2048], e, hg[2:3], inv_n)
        kb = _group_norm(p[:, 2048:2560], e, hg[3:4], inv_n)
        if use_rope:
            lane = lax.broadcasted_iota(jnp.int32, (1, LANES), 1)
            lane_hi = (lane & 16) != 0
            cos, sin = cos_ref[rows, :], sin_ref[rows, :]
            qb = _rope_chunks(qb, cos, sin, lane_hi, 16)
            kb = _rope_chunks(kb, cos, sin, lane_hi, 16)
        qa_ref[0, rows, :] = qa.astype(BF16)
        ka_ref[0, rows, :] = ka.astype(BF16)
        va_ref[0, rows, :] = p[:, 1024:1536].astype(BF16)
        qb_ref[0, rows, :] = qb.astype(BF16)
        kb_ref[0, rows, :] = kb.astype(BF16)
        vb_ref[0, :, rows] = p[:, 2560:3072].T.astype(BF16)


def _even_proj(x, shift, scale, ng, w, e64, hg, cos, sin, *, use_rope, tm):
    b, s, d = x.shape
    n = w.shape[1]
    row = lambda bi, i: (bi, i, 0)
    vec = lambda bi, i: (bi, 0, 0)
    const = lambda bi, i: (0, 0)
    out = jax.ShapeDtypeStruct((b, s, 512), BF16)
    return pl.pallas_call(
        functools.partial(_even_proj_kernel, use_rope=use_rope),
        out_shape=(out,) * 5 + (jax.ShapeDtypeStruct((b, 512, s), BF16),),
        grid=(b, s // tm),
        in_specs=[
            pl.BlockSpec((1, tm, d), row),
            pl.BlockSpec((1, 1, d), vec),
            pl.BlockSpec((1, 1, d), vec),
            pl.BlockSpec((1, d), const),
            pl.BlockSpec((d, n), const),
            pl.BlockSpec(e64.shape, const),
            pl.BlockSpec((4, 512), const),
            pl.BlockSpec((tm, LANES), lambda bi, i: (i, 0)),
            pl.BlockSpec((tm, LANES), lambda bi, i: (i, 0)),
        ],
        out_specs=(pl.BlockSpec((1, tm, 512), row),) * 5 + (pl.BlockSpec((1, 512, tm), lambda bi, i: (bi, 0, i)),),
        compiler_params=_cparams(("parallel", "parallel")),
        name="even_proj",
    )(x, shift, scale, ng, w, e64, hg, cos, sin)


def _pair_stack(q):
    lane = lax.broadcasted_iota(jnp.int32, q.shape, 1)
    zero = jnp.zeros_like(q)
    return jnp.concatenate([jnp.where(lane < HEAD_DIM, q, zero), jnp.where(lane >= HEAD_DIM, q, zero)], axis=0)


def _pair_merge(o, tq):
    lane = lax.broadcasted_iota(jnp.int32, (tq, LANES), 1)
    return jnp.where(lane < HEAD_DIM, o[:tq], o[tq:])


def _with_ones(v):
    return jnp.concatenate([v, jnp.ones(v.shape, BF16)], axis=1)


def _na_kernel(q_ref, k_ref, v_ref, kc_ref, vc_ref, bias_ref, o_ref, *, rows_per_step, n_rows, direct):
    rb = pl.program_id(2)
    kc = kc_ref[0]
    vc = vc_ref[0]
    n_lat = NA_WIN_H * GRID_W
    if direct:
        nq = rows_per_step * GRID_W
        qs_all = _pair_stack(q_ref[0])
        o_ctx = _dot(jnp.exp2(_dot_nt(qs_all, kc)).astype(BF16), _with_ones(vc))
        for i in range(rows_per_step):
            r = rb * rows_per_step + i
            r0 = jnp.clip(r - NA_WIN_H // 2, 0, n_rows - NA_WIN_H)
            start = pl.multiple_of(r0 * GRID_W, GRID_W)
            kw = k_ref[0, pl.ds(start, n_lat), :]
            vw = v_ref[0, pl.ds(start, n_lat), :]
            lo = slice(i * GRID_W, (i + 1) * GRID_W)
            hi = slice(nq + i * GRID_W, nq + (i + 1) * GRID_W)
            qs = jnp.concatenate([qs_all[lo], qs_all[hi]], axis=0)
            p = jnp.exp2(_dot_nt(qs, kw) + bias_ref[r - r0]).astype(BF16)
            tot = _dot(p, _with_ones(vw)) + jnp.concatenate([o_ctx[lo], o_ctx[hi]], axis=0)
            o = tot[:, :LANES] / tot[:, LANES:]
            o_ref[0, lo, :] = _pair_merge(o, GRID_W).astype(BF16)
        return
    for i in range(rows_per_step):
        r = rb * rows_per_step + i
        r0 = jnp.clip(r - NA_WIN_H // 2, 0, n_rows - NA_WIN_H)
        start = pl.multiple_of(r0 * GRID_W, GRID_W)
        kw = k_ref[0, pl.ds(start, n_lat), :]
        vw = v_ref[0, pl.ds(start, n_lat), :]
        qs = _pair_stack(q_ref[0, i * GRID_W:(i + 1) * GRID_W, :])
        s_lat = _dot_nt(qs, kw) + bias_ref[r - r0]
        s_ctx = _dot_nt(qs, kc)
        m = jnp.maximum(jnp.max(s_lat, axis=-1, keepdims=True), jnp.max(s_ctx, axis=-1, keepdims=True))
        p_lat = jnp.exp2(s_lat - m)
        p_ctx = jnp.exp2(s_ctx - m)
        l = jnp.sum(p_lat, axis=-1, keepdims=True) + jnp.sum(p_ctx, axis=-1, keepdims=True)
        o = _dot(p_lat.astype(BF16), vw) + _dot(p_ctx.astype(BF16), vc)
        o = o / l
        o_ref[0, i * GRID_W:(i + 1) * GRID_W, :] = _pair_merge(o, GRID_W).astype(BF16)


def _na_attention(q, k, v, kc, vc, bias, score_bound, *, rows_per_step=16):
    bound = score_bound + jnp.max(jnp.where(bias > 0.5 * NEG_INF, jnp.abs(bias), 0.0))
    return lax.cond(bound <= EXP2_SAFE_RANGE,
                    lambda: _na_call(q, k, v, kc, vc, bias, rows_per_step=rows_per_step, direct=True),
                    lambda: _na_call(q, k, v, kc, vc, bias, rows_per_step=rows_per_step, direct=False))


def _na_call(q, k, v, kc, vc, bias, *, rows_per_step, direct):
    b, s, _ = q.shape
    c = kc.shape[1]
    n_rows = s // GRID_W
    tq = rows_per_step * GRID_W
    return pl.pallas_call(
        functools.partial(_na_kernel, rows_per_step=rows_per_step, n_rows=n_rows, direct=direct),
        out_shape=jax.ShapeDtypeStruct((b, s, 512), BF16),
        grid=(b, 4, n_rows // rows_per_step),
        in_specs=[
            pl.BlockSpec((1, tq, LANES), lambda bi, j, i: (bi, i, j)),
            pl.BlockSpec((1, s, LANES), lambda bi, j, i: (bi, 0, j)),
            pl.BlockSpec((1, s, LANES), lambda bi, j, i: (bi, 0, j)),
            pl.BlockSpec((1, c, LANES), lambda bi, j, i: (bi, 0, j)),
            pl.BlockSpec((1, c, LANES), lambda bi, j, i: (bi, 0, j)),
            pl.BlockSpec((NA_WIN_H, None, LANES, NA_WIN_H * GRID_W), lambda bi, j, i: (0, j, 0, 0)),
        ],
        out_specs=pl.BlockSpec((1, tq, LANES), lambda bi, j, i: (bi, i, j)),
        compiler_params=_cparams(("parallel", "parallel", "parallel")),
        name="na_attention" + ("" if direct else "_online"),
    )(q, k, v, kc, vc, bias)


def _na_bias_table(rpb):
    w = GRID_W
    col = jnp.arange(w)
    c0 = jnp.clip(col - NA_WIN_W // 2, 0, w - NA_WIN_W)
    col_in = (col[None, :] >= c0[:, None]) & (col[None, :] < c0[:, None] + NA_WIN_W)
    left = (w - 1) - (NA_WIN_W - 1)
    ext = jnp.pad(rpb, ((0, 0), (0, 0), (left, 2 * w - left - (2 * NA_WIN_W - 1))), mode="edge")
    h, nr, _ = rpb.shape
    flat = jnp.broadcast_to(ext[:, :, None, :], (h, nr, w, 2 * w)).reshape(h, nr, w * 2 * w)
    toep = flat[:, :, :w * (2 * w - 1)].reshape(h, nr, w, 2 * w - 1)[:, :, :, w - 1:]
    toep = jnp.where(col_in[None, None], toep * LOG2E, NEG_INF)
    variants = []
    for v in range(NA_WIN_H):
        tv = toep[:, NA_WIN_H - 1 - v:2 * NA_WIN_H - 1 - v]
        variants.append(tv.transpose(0, 2, 1, 3).reshape(NA_HEADS // 2, 2 * w, NA_WIN_H * w))
    return jnp.stack(variants, axis=0).astype(F32)


def _flash_kernel(*refs, mode, online, tq, tk, s_len, c_len, lam_init):
    if mode == "diff":
        q_ref, k_ref, v_ref, kc_ref, vc_ref, lam_ref, sg_ref, o_ref = refs
    else:
        q_ref, k_ref, v_ref, kc_ref, vc_ref, o_ref = refs

    q = q_ref[0]
    if mode == "mla":
        q_parts = (q[:, :LANES], q[:, LANES:])
    else:
        qs = _pair_stack(q)

    def scores(kt):
        if mode == "mla":
            return jnp.concatenate([_dot_nt(q_parts[0], kt[:, :LANES]), _dot_nt(q_parts[1], kt[:, LANES:])], axis=0)
        return _dot_nt(qs, kt)

    def step(kt, vt, carry):
        m, l, acc = carry
        s = scores(kt)
        m_new = jnp.maximum(m, jnp.max(s, axis=-1, keepdims=True))
        alpha = jnp.exp2(m - m_new)
        p = jnp.exp2(s - m_new)
        l = alpha * l + jnp.sum(p, axis=-1, keepdims=True)
        acc = alpha * acc + _dot(p.astype(BF16), vt)
        return m_new, l, acc

    def body(t, carry):
        start = pl.multiple_of(t * tk, tk)
        return step(k_ref[0, pl.ds(start, tk), :], v_ref[0, pl.ds(start, tk), :], carry)

    def scores_t(kt):
        if mode == "mla":
            return jnp.concatenate([_dot_nt(kt[:, :LANES], q_parts[0]), _dot_nt(kt[:, LANES:], q_parts[1])], axis=1)
        return _dot_nt(kt, qs)

    def direct(kt, vtt, acc):
        p = jnp.exp2(scores_t(kt)).astype(BF16)
        v1 = jnp.concatenate([vtt, jnp.ones((SUM_ROWS, vtt.shape[1]), BF16)], axis=0)
        return acc + _dot(v1, p)

    if online:
        carry = (jnp.full((2 * tq, 1), -jnp.inf, F32), jnp.zeros((2 * tq, 1), F32),
                 jnp.zeros((2 * tq, LANES), F32))
        if s_len:
            carry = lax.fori_loop(0, s_len // tk, body, carry)
        if c_len:
            carry = step(kc_ref[0], vc_ref[0], carry)
        _, l, acc = carry
        o = acc / l
    else:
        acc = jnp.zeros((LANES + SUM_ROWS, 2 * tq), F32)
        for t in range(s_len // tk):
            acc = direct(k_ref[0, t * tk:(t + 1) * tk, :], v_ref[0, :, t * tk:(t + 1) * tk], acc)
        if c_len:
            acc = direct(kc_ref[0], vc_ref[0], acc)
        o_t = acc[:LANES] / acc[LANES:LANES + 1]
    if mode == "diff":
        lp = lam_ref[...]
        lam = (jnp.exp(jnp.sum(lp[0:1] * lp[1:2], axis=-1, keepdims=True))
               - jnp.exp(jnp.sum(lp[2:3] * lp[3:4], axis=-1, keepdims=True)) + lam_init)
        if online:
            d = o[:tq] - lam * o[tq:]
            dn = d * lax.rsqrt(jnp.mean(d * d, axis=-1, keepdims=True) + NORM_EPS)
        else:
            d_t = o_t[:, :tq] - lam * o_t[:, tq:]
            dn = (d_t * lax.rsqrt(jnp.mean(d_t * d_t, axis=0, keepdims=True) + NORM_EPS)).T
        o_ref[0] = (dn * sg_ref[...] * (1.0 - lam_init)).astype(BF16)
    elif online:
        o_ref[0] = _pair_merge(o, tq).astype(BF16)
    else:
        row = lax.broadcasted_iota(jnp.int32, (LANES, tq), 0)
        o_ref[0] = jnp.where(row < HEAD_DIM, o_t[:, :tq], o_t[:, tq:]).T.astype(BF16)


def _flash(q, k, vt, kc, vct, *, score_bound=None, **kw):
    rows = lambda a: None if a is None else jnp.swapaxes(a, 1, 2)
    if score_bound is None:
        return _flash_call(q, k, rows(vt), kc, rows(vct), online=True, **kw)
    return lax.cond(score_bound <= EXP2_SAFE_RANGE,
                    lambda: _flash_call(q, k, vt, kc, vct, online=False, **kw),
                    lambda: _flash_call(q, k, rows(vt), kc, rows(vct), online=True, **kw))


def _flash_call(q, k, v, kc, vc, *, mode, online, kv_map, n_chunks, tq, tk, extra=(), lam_init=0.0):
    b, s, _ = q.shape
    qw = 2 * LANES if mode == "mla" else LANES
    c_len = kc.shape[1]
    if k is None:
        k, v, s_len = kc, vc, 0
    else:
        s_len = k.shape[1]
    kk = k.shape[1]
    if online:
        v_spec = lambda n: pl.BlockSpec((1, n, LANES), lambda bi, j, i: (bi, 0, kv_map(j)))
    else:
        v_spec = lambda n: pl.BlockSpec((1, LANES, n), lambda bi, j, i: (bi, kv_map(j), 0))
    in_specs = [
        pl.BlockSpec((1, tq, qw), lambda bi, j, i: (bi, i, j)),
        pl.BlockSpec((1, kk, qw), lambda bi, j, i: (bi, 0, kv_map(j) if mode != "mla" else j)),
        v_spec(kk),
        pl.BlockSpec((1, c_len, qw), lambda bi, j, i: (bi, 0, kv_map(j) if mode != "mla" else j)),
        v_spec(c_len),
    ]
    for e in extra:
        in_specs.append(pl.BlockSpec(e.shape, lambda bi, j, i: (0, 0)))
    return pl.pallas_call(
        functools.partial(_flash_kernel, mode=mode, online=online, tq=tq, tk=tk, s_len=s_len, c_len=c_len,
                          lam_init=lam_init),
        out_shape=jax.ShapeDtypeStruct((b, s, n_chunks * LANES), BF16),
        grid=(b, n_chunks, s // tq),
        in_specs=in_specs,
        out_specs=pl.BlockSpec((1, tq, LANES), lambda bi, j, i: (bi, i, j)),
        compiler_params=_cparams(("parallel", "parallel", "parallel")),
        name="flash_" + mode + ("_online" if online else ""),
    )(q, k, v, kc, vc, *extra)


def _post_ffn_kernel(oa_ref, ob_ref, wa_ref, wb_ref, x_ref, g1_ref, ng_ref, sh_ref, sc_ref, g2_ref,
                     wg_ref, wu_ref, wd_ref, o_ref, *, tf):
    y = _dot(oa_ref[0], wa_ref[...]) + _dot(ob_ref[0], wb_ref[...])
    x1 = x_ref[0] + g1_ref[0] * y
    h = _norm_mod(x1, ng_ref[...], sh_ref[0], sc_ref[0]).astype(BF16)
    acc = jnp.zeros(o_ref.shape[1:], F32)
    for f in range(wg_ref.shape[1] // tf):
        sl = slice(f * tf, (f + 1) * tf)
        a = _dot(h, wg_ref[:, sl])
        u = _dot(h, wu_ref[:, sl])
        acc = acc + _dot((_silu(a) * u).astype(BF16), wd_ref[sl, :])
    o_ref[0] = x1 + g2_ref[0] * acc


def _post_ffn(oa, ob, wa, wb, x, gate1, ng, shift, scale, gate2, wg, wu, wd, *, tm, tf=256):
    b, s, d = x.shape
    f = wg.shape[1]
    row = lambda bi, i: (bi, i, 0)
    vec = lambda bi, i: (bi, 0, 0)
    const = lambda bi, i: (0, 0)
    return pl.pallas_call(
        functools.partial(_post_ffn_kernel, tf=tf),
        out_shape=jax.ShapeDtypeStruct((b, s, d), F32),
        grid=(b, s // tm),
        in_specs=[
            pl.BlockSpec((1, tm, 512), row),
            pl.BlockSpec((1, tm, 512), row),
            pl.BlockSpec((512, d), const),
            pl.BlockSpec((512, d), const),
            pl.BlockSpec((1, tm, d), row),
            pl.BlockSpec((1, 1, d), vec),
            pl.BlockSpec((1, d), const),
            pl.BlockSpec((1, 1, d), vec),
            pl.BlockSpec((1, 1, d), vec),
            pl.BlockSpec((1, 1, d), vec),
            pl.BlockSpec((d, f), const),
            pl.BlockSpec((d, f), const),
            pl.BlockSpec((f, d), const),
        ],
        out_specs=pl.BlockSpec((1, tm, d), row),
        compiler_params=_cparams(("parallel", "parallel")),
        name="post_ffn",
    )(oa, ob, wa, wb, x, gate1, ng, shift, scale, gate2, wg, wu, wd)


def _rms(t, g):
    return t * lax.rsqrt(jnp.mean(t * t, axis=-1, keepdims=True) + NORM_EPS) * g


def _odd_proj_kernel(x_ref, sh_ref, sc_ref, ng_ref, w_ref, wuq_ref, wukv_ref, e_ref, e2_ref, g_ref,
                     cos_ref, sin_ref, cosm_ref, sinm_ref,
                     qc_ref, kc_ref, vc_ref, qd_ref, kd_ref, vd_ref, *, use_rope):
    g = g_ref[...]
    e = e_ref[...]
    e2 = e2_ref[...]
    inv_n = 1.0 / HEAD_DIM
    for rows in _row_blocks(x_ref.shape[1]):
        h = _norm_mod(x_ref[0, rows, :], ng_ref[...], sh_ref[0], sc_ref[0])
        p = _dot(h.astype(BF16), w_ref[...])
        qc = _group_norm(p[:, 0:512], e, g[0:1, 0:512], inv_n)
        kc = _group_norm(p[:, 512:768], e, g[1:2, 0:256], inv_n)
        cq = _rms(p[:, 1024:1280], g[2:3, 0:256])
        ckv = _rms(p[:, 1280:1408], g[3:4, 0:128])
        krc = p[:, 1408:1536]
        qd = _dot(cq.astype(BF16), wuq_ref[...])
        kvv = _dot(ckv.astype(BF16), wukv_ref[...])
        kd = kvv[:, 0:1024] + jnp.concatenate([krc] * MLA_HEADS, axis=1)
        qd = _group_norm(qd, e2, g[4:5], 1.0 / MLA_QK)
        kd = _group_norm(kd, e2, g[5:6], 1.0 / MLA_QK)
        if use_rope:
            lane = lax.broadcasted_iota(jnp.int32, (1, LANES), 1)
            cos, sin = cos_ref[rows, :], sin_ref[rows, :]
            qc = _rope_chunks(qc, cos, sin, (lane & 16) != 0, 16)
            kc = _rope_chunks(kc, cos, sin, (lane & 16) != 0, 16)
            cosm, sinm = cosm_ref[rows, :], sinm_ref[rows, :]
            qd = _rope_chunks(qd, cosm, sinm, (lane & 8) != 0, 8)
            kd = _rope_chunks(kd, cosm, sinm, (lane & 8) != 0, 8)
        qc_ref[0, rows, :] = qc.astype(BF16)
        kc_ref[0, rows, :] = kc.astype(BF16)
        vc_ref[0, :, rows] = p[:, 768:1024].T.astype(BF16)
        qd_ref[0, rows, :] = qd.astype(BF16)
        kd_ref[0, rows, :] = kd.astype(BF16)
        vd_ref[0, :, rows] = kvv[:, 1024:1536].T.astype(BF16)


def _odd_proj(x, shift, scale, ng, w, wuq, wukv, e64, e128, gains, cos, sin, cosm, sinm, *, use_rope, tm):
    b, s, d = x.shape
    row = lambda bi, i: (bi, i, 0)
    vec = lambda bi, i: (bi, 0, 0)
    const = lambda bi, i: (0, 0)
    tab = lambda bi, i: (i, 0)
    widths = (512, 256, 256, 1024, 1024, 512)
    v_outs = (2, 5)
    return pl.pallas_call(
        functools.partial(_odd_proj_kernel, use_rope=use_rope),
        out_shape=tuple(jax.ShapeDtypeStruct((b, n, s) if i in v_outs else (b, s, n), BF16)
                        for i, n in enumerate(widths)),
        grid=(b, s // tm),
        in_specs=[
            pl.BlockSpec((1, tm, d), row),
            pl.BlockSpec((1, 1, d), vec),
            pl.BlockSpec((1, 1, d), vec),
            pl.BlockSpec((1, d), const),
            pl.BlockSpec(w.shape, const),
            pl.BlockSpec(wuq.shape, const),
            pl.BlockSpec(wukv.shape, const),
            pl.BlockSpec(e64.shape, const),
            pl.BlockSpec(e128.shape, const),
            pl.BlockSpec(gains.shape, const),
            pl.BlockSpec((tm, LANES), tab),
            pl.BlockSpec((tm, LANES), tab),
            pl.BlockSpec((tm, LANES), tab),
            pl.BlockSpec((tm, LANES), tab),
        ],
        out_specs=tuple(pl.BlockSpec((1, n, tm), lambda bi, i: (bi, 0, i)) if j in v_outs
                        else pl.BlockSpec((1, tm, n), row) for j, n in enumerate(widths)),
        compiler_params=_cparams(("parallel", "parallel")),
        name="odd_proj",
    )(x, shift, scale, ng, w, wuq, wukv, e64, e128, gains, cos, sin, cosm, sinm)


def _mla_rope_tables(seq):
    cos, sin = _rope_tables(seq, MLA_ROPE, MLA_ROPE // 4)
    lane = jnp.arange(LANES)
    on = (lane >= MLA_NOPE) & (lane < MLA_QK)
    return jnp.where(on[None, :], cos, 1.0), jnp.where(on[None, :], sin, 0.0)


def _router_logits(h, rt_ref):
    h_hi = h.astype(BF16)
    h_lo = (h - h_hi.astype(F32)).astype(BF16)
    both = _dot(h_hi, rt_ref[...]) + _dot(h_lo, rt_ref[...])
    return both[:, :LANES] + both[:, LANES:]


def _route_tile(logits_rows, blk, u_ref, ei_ref, gt_ref, rk_ref, before0):
    logits = logits_rows.T[:N_EXPERTS]
    eidx = lax.broadcasted_iota(jnp.int32, logits.shape, 0)
    m1 = jnp.max(logits, axis=0, keepdims=True)
    i1 = jnp.min(jnp.where(logits == m1, eidx, N_EXPERTS), axis=0, keepdims=True)
    rest = jnp.where(eidx == i1, -jnp.inf, logits)
    m2 = jnp.max(rest, axis=0, keepdims=True)
    i2 = jnp.min(jnp.where(rest == m2, eidx, N_EXPERTS), axis=0, keepdims=True)
    e2 = jnp.exp(m2 - m1)
    g1 = 1.0 / (1.0 + e2)
    sel1 = eidx == i1
    sel2 = eidx == i2
    onehot = jnp.where(sel1 | sel2, 1.0, 0.0)
    pieces = []
    for c in range(logits.shape[1] // blk):
        oh = onehot[:, c * blk:(c + 1) * blk]
        pieces.append(_dot(oh.astype(BF16), u_ref[...]) + before0)
        before0 = before0 + jnp.sum(oh, axis=1, keepdims=True)
    before = jnp.concatenate(pieces, axis=1)
    r1 = jnp.sum(jnp.where(sel1, before, 0.0), axis=0, keepdims=True)
    r2 = jnp.sum(jnp.where(sel2, before, 0.0), axis=0, keepdims=True)
    ei_ref[...] = jnp.concatenate([i1, i2], axis=0)
    gt_ref[...] = jnp.concatenate([g1, e2 * g1], axis=0)
    rk_ref[...] = jnp.concatenate([r1, r2], axis=0).astype(jnp.int32)
    return before0


def _post_attn_router_kernel(oa_ref, ob_ref, wa_ref, wb_ref, x_ref, g_ref, ng_ref, sh_ref, sc_ref, rt_ref, u_ref,
                             x1_ref, h2_ref, ei_ref, gt_ref, rk_ref, cnt_ref, carry_ref):
    @pl.when((pl.program_id(0) == 0) & (pl.program_id(1) == 0))
    def _():
        carry_ref[...] = jnp.zeros_like(carry_ref)

    logits = []
    for rows in _row_blocks(x_ref.shape[1]):
        y = _dot(oa_ref[0, rows, :], wa_ref[...]) + _dot(ob_ref[0, rows, :], wb_ref[...])
        x1 = x_ref[0, rows, :] + g_ref[0] * y
        x1_ref[0, rows, :] = x1
        h = _norm_mod(x1, ng_ref[...], sh_ref[0], sc_ref[0])
        h2_ref[0, rows, :] = h.astype(BF16)
        logits.append(_router_logits(h, rt_ref))
    routed = _route_tile(jnp.concatenate(logits, axis=0), u_ref.shape[0], u_ref, ei_ref, gt_ref, rk_ref,
                         carry_ref[:, 0:1])
    carry_ref[...] = jnp.broadcast_to(routed, carry_ref.shape)
    cnt_ref[...] = jnp.broadcast_to(routed, cnt_ref.shape).astype(jnp.int32)


def _post_attn_router(oa, ob, wa, wb, x, gate, ng, shift, scale, router_t, *, tm):
    b, s, d = x.shape
    n = b * s
    nt = s // tm
    blk = _row_blocks(tm)[0].stop
    tri = (jnp.arange(blk)[:, None] < jnp.arange(blk)[None, :]).astype(BF16)
    row = lambda bi, i: (bi, i, 0)
    vec = lambda bi, i: (bi, 0, 0)
    const = lambda bi, i: (0, 0)
    flat = lambda bi, i: (0, bi * nt + i)
    return pl.pallas_call(
        _post_attn_router_kernel,
        out_shape=(jax.ShapeDtypeStruct((b, s, d), F32), jax.ShapeDtypeStruct((b, s, d), BF16),
                   jax.ShapeDtypeStruct((2, n), jnp.int32), jax.ShapeDtypeStruct((2, n), F32),
                   jax.ShapeDtypeStruct((2, n), jnp.int32), jax.ShapeDtypeStruct((N_EXPERTS, LANES), jnp.int32)),
        grid=(b, nt),
        in_specs=[
            pl.BlockSpec((1, tm, 512), row),
            pl.BlockSpec((1, tm, 512), row),
            pl.BlockSpec((512, d), const),
            pl.BlockSpec((512, d), const),
            pl.BlockSpec((1, tm, d), row),
            pl.BlockSpec((1, 1, d), vec),
            pl.BlockSpec((1, d), const),
            pl.BlockSpec((1, 1, d), vec),
            pl.BlockSpec((1, 1, d), vec),
            pl.BlockSpec((d, 2 * LANES), const),
            pl.BlockSpec((blk, blk), const),
        ],
        out_specs=(pl.BlockSpec((1, tm, d), row), pl.BlockSpec((1, tm, d), row),
                   pl.BlockSpec((2, tm), flat), pl.BlockSpec((2, tm), flat), pl.BlockSpec((2, tm), flat),
                   pl.BlockSpec((N_EXPERTS, LANES), const)),
        scratch_shapes=[pltpu.VMEM((N_EXPERTS, LANES), F32)],
        compiler_params=_cparams(("arbitrary", "arbitrary")),
        name="post_attn_router",
    )(oa, ob, wa, wb, x, gate, ng, shift, scale, router_t, tri)


MOE_SUB = 256
MOE_BLK = 1024
MOE_TC = 256
MOE_GROUP = 4
MOE_NBUF = 16
MOE_AHEAD = MOE_NBUF - MOE_GROUP


def _lookup(table, idx):
    hit = idx[..., None] == jnp.arange(table.shape[0])
    return jnp.sum(jnp.where(hit, table, 0), axis=-1)


def _lookup_cols(rows, idx):
    hit = idx[:, None] == jnp.arange(rows.shape[1])[None, :]
    return jnp.sum(jnp.where(hit, rows, 0), axis=1)


def _moe_plan(ei, rk, counts, n_tok):
    n_chunks = n_tok // MOE_TC
    cap = 2 * n_tok + N_EXPERTS * MOE_BLK
    nb_sub = cap // MOE_SUB
    nb_blk = cap // MOE_BLK
    padded = ((counts + MOE_BLK - 1) // MOE_BLK) * MOE_BLK
    pad_end = jnp.cumsum(padded)
    pad_start = pad_end - padded
    dest = _lookup(pad_start, ei) + rk

    onehot = ei.reshape(2, n_chunks, MOE_TC)[..., None] == jnp.arange(N_EXPERTS)
    cnt = jnp.sum(onehot, axis=(0, 2)).astype(jnp.int32)
    cum = jnp.concatenate([jnp.zeros((1, N_EXPERTS), jnp.int32), jnp.cumsum(cnt, axis=0)], axis=0)

    lo = pad_start[None, :] + cum[:-1]
    has = cnt > 0
    blk0 = lo // MOE_SUB
    two = has & ((lo + cnt - 1) // MOE_SUB > blk0)
    blk1 = jnp.minimum(blk0 + 1, nb_sub - 1)
    win_ok = jnp.stack([has, two], axis=-1).reshape(n_chunks, 2 * N_EXPERTS)
    win_blk = jnp.stack([blk0, blk1], axis=-1).reshape(n_chunks, 2 * N_EXPERTS)
    c_count = jnp.sum(win_ok, axis=1).astype(jnp.int32)
    c_start = (jnp.cumsum(c_count) - c_count).astype(jnp.int32)
    win_pos = jnp.cumsum(win_ok, axis=1) - win_ok
    front = win_ok[:, :, None] & (win_pos[:, :, None] == jnp.arange(2 * N_EXPERTS)[None, None, :])
    packed = jnp.sum(jnp.where(front, win_blk[:, :, None], 0), axis=1)
    q = jnp.arange(n_chunks * 2 * N_EXPERTS)
    c_of_q = jnp.minimum(jnp.sum((c_start + c_count)[None, :] <= q[:, None], axis=1), n_chunks - 1)
    row_q = jnp.dot((c_of_q[:, None] == jnp.arange(n_chunks)[None, :]).astype(F32), packed.astype(F32),
                    precision=HIGHEST)
    p_of_q = q - _lookup(c_start, c_of_q)
    c_blocks = _lookup_cols(row_q, p_of_q).astype(jnp.int32)

    sb = jnp.arange(nb_sub)
    e_sb = jnp.minimum(jnp.sum(sb[:, None] * MOE_SUB >= pad_end[None, :], axis=1), N_EXPERTS - 1)
    counts_sb = _lookup(counts, e_sb)
    r0 = sb * MOE_SUB - _lookup(pad_start, e_sb)
    valid_sb = (sb * MOE_SUB < pad_end[-1]) & (r0 < counts_sb)
    r1 = jnp.minimum(r0 + MOE_SUB, counts_sb) - 1
    hit_sb = e_sb[:, None] == jnp.arange(N_EXPERTS)[None, :]
    cum_sb = jnp.sum(jnp.where(hit_sb[:, None, :], cum[None, 1:, :], 0), axis=-1)
    cmin = jnp.sum(cum_sb <= r0[:, None], axis=1)
    cmax = jnp.sum(cum_sb <= r1[:, None], axis=1)
    items = jnp.where(valid_sb, cmax - cmin + 1, 0)
    off_end = jnp.cumsum(items)
    off = off_end - items
    total = off_end[-1]
    w_max = (2 * n_tok) // MOE_SUB + N_EXPERTS + N_EXPERTS * (n_chunks - 1)
    w = jnp.arange(w_max)
    wv = w < total
    wq = jnp.minimum(w, total - 1)
    w_sb = jnp.minimum(jnp.sum(off_end[None, :] <= wq[:, None], axis=1), nb_sub - 1)
    off_w = _lookup(off, w_sb)
    w_chunk = _lookup(cmin, w_sb) + (wq - off_w)
    w_flag = wv.astype(jnp.int32) + 2 * (wq == off_w).astype(jnp.int32)

    bi = jnp.arange(nb_blk)
    e_blk = jnp.minimum(jnp.sum(bi[:, None] * MOE_BLK >= pad_end[None, :], axis=1), N_EXPERTS - 1)
    rows = jnp.where(bi * MOE_BLK < pad_end[-1],
                     _lookup(counts, e_blk) - (bi * MOE_BLK - _lookup(pad_start, e_blk)), 0)
    n_sub = jnp.clip((rows + MOE_SUB - 1) // MOE_SUB, 0, MOE_BLK // MOE_SUB)
    g_count = jnp.sum(items.reshape(nb_blk, MOE_BLK // MOE_SUB), axis=1)
    g_start = jnp.cumsum(g_count) - g_count
    return dict(dest=dest.astype(jnp.int32), cap=cap,
                w_sb=w_sb.astype(jnp.int32), w_chunk=w_chunk.astype(jnp.int32), w_flag=w_flag,
                g_start=g_start.astype(jnp.int32), g_count=g_count.astype(jnp.int32),
                e_blk=e_blk.astype(jnp.int32), n_sub=n_sub.astype(jnp.int32),
                c_blocks=c_blocks, c_count=c_count, c_start=c_start)


def _moe_gather_kernel(st_ref, cn_ref, wsb_ref, wch_ref, t_hbm, d_ref, g_ref, xg_ref, gs_ref, tbuf, sem):
    i = pl.program_id(0)
    n = cn_ref[i]
    s0 = st_ref[i]
    total = st_ref[pl.num_programs(0) - 1] + cn_ref[pl.num_programs(0) - 1]
    subs = MOE_BLK // MOE_SUB

    def chunk_copy(item):
        slot = lax.rem(item, MOE_NBUF)
        start = pl.multiple_of(wch_ref[item] * MOE_TC, MOE_TC)
        return pltpu.make_async_copy(t_hbm.at[pl.ds(start, MOE_TC), :], tbuf.at[slot], sem.at[slot])

    @pl.when(i == 0)
    def _():
        for a in range(MOE_AHEAD):
            @pl.when(a < total)
            def _():
                chunk_copy(a).start()

    xg_ref[...] = jnp.zeros(xg_ref.shape, BF16)
    gs_ref[...] = jnp.zeros(gs_ref.shape, F32)

    def arrive(item):
        chunk_copy(item).wait()

        @pl.when(item + MOE_AHEAD < total)
        def _():
            chunk_copy(item + MOE_AHEAD).start()

    def contribute(item):
        sb = wsb_ref[item]
        chunk = wch_ref[item]
        d = d_ref[chunk]
        g = g_ref[chunk]
        srow = lax.broadcasted_iota(jnp.int32, (MOE_SUB, MOE_TC), 0) + sb * MOE_SUB
        hit0 = srow == d[0:1]
        hit1 = srow == d[1:2]
        p = jnp.where(hit0 | hit1, 1.0, 0.0).astype(BF16)
        rows = _dot(p, tbuf[lax.rem(item, MOE_NBUF)]).astype(BF16)
        gate = jnp.sum(jnp.where(hit0, g[0:1], 0.0) + jnp.where(hit1, g[1:2], 0.0), axis=-1, keepdims=True)
        r0 = pl.multiple_of((sb - i * subs) * MOE_SUB, MOE_SUB)
        xg_ref[pl.ds(r0, MOE_SUB), :] += rows
        gs_ref[pl.ds(r0, MOE_SUB), :] += gate

    def group(k, carry):
        item = s0 + MOE_GROUP * k
        for a in range(MOE_GROUP):
            arrive(item + a)
        for a in range(MOE_GROUP):
            contribute(item + a)
        return carry

    def single(k, carry):
        arrive(s0 + grouped + k)
        contribute(s0 + grouped + k)
        return carry

    grouped = (n // MOE_GROUP) * MOE_GROUP
    lax.fori_loop(0, n // MOE_GROUP, group, 0)
    lax.fori_loop(0, n - grouped, single, 0)


def _moe_gather(t, dest, gates, plan):
    n_tok, d = t.shape
    cap = plan["cap"]
    n_chunks = n_tok // MOE_TC
    by_chunk = lambda a: a.reshape(2, n_chunks, MOE_TC).transpose(1, 0, 2)
    grid_spec = pltpu.PrefetchScalarGridSpec(
        num_scalar_prefetch=4,
        grid=(cap // MOE_BLK,),
        in_specs=[
            pl.BlockSpec(memory_space=pl.ANY),
            pl.BlockSpec((n_chunks, 2, MOE_TC), lambda i, *_: (0, 0, 0)),
            pl.BlockSpec((n_chunks, 2, MOE_TC), lambda i, *_: (0, 0, 0)),
        ],
        out_specs=(pl.BlockSpec((MOE_BLK, d), lambda i, *_: (i, 0)),
                   pl.BlockSpec((MOE_BLK, 1), lambda i, *_: (i, 0))),
        scratch_shapes=[pltpu.VMEM((MOE_NBUF, MOE_TC, d), BF16), pltpu.SemaphoreType.DMA((MOE_NBUF,))],
    )
    return pl.pallas_call(
        _moe_gather_kernel,
        out_shape=(jax.ShapeDtypeStruct((cap, d), BF16), jax.ShapeDtypeStruct((cap, 1), F32)),
        grid_spec=grid_spec,
        compiler_params=_cparams(("arbitrary",)),
        name="moe_gather",
    )(plan["g_start"], plan["g_count"], plan["w_sb"], plan["w_chunk"], t, by_chunk(dest), by_chunk(gates))


def _moe_expert_kernel(eb_ref, ns_ref, x_ref, gs_ref, w1_ref, w3_ref, w2_ref, y_ref):
    n_sub = ns_ref[pl.program_id(0)]
    subs = MOE_BLK // MOE_SUB

    def mlp(rows):
        xs = x_ref[rows, :]
        acc = jnp.zeros((rows.stop - rows.start, y_ref.shape[1]), F32)
        for f in range(w1_ref.shape[1] // MOE_TF):
            cols = slice(f * MOE_TF, (f + 1) * MOE_TF)
            a = _dot(xs, w1_ref[:, cols])
            u = _dot(xs, w3_ref[:, cols])
            acc = acc + _dot((_silu(a) * u).astype(BF16), w2_ref[cols, :])
        y_ref[rows, :] = (acc * gs_ref[rows, :]).astype(BF16)

    @pl.when(n_sub == subs)
    def _():
        mlp(slice(0, MOE_BLK))

    for sub in range(subs):
        rows = slice(sub * MOE_SUB, (sub + 1) * MOE_SUB)

        @pl.when((n_sub < subs) & (sub < n_sub))
        def _():
            mlp(rows)

        @pl.when(sub >= n_sub)
        def _():
            y_ref[rows, :] = jnp.zeros((MOE_SUB, y_ref.shape[1]), BF16)


def _moe_experts(xg, gs, w1, w3, w2, plan):
    cap, d = xg.shape
    f = w1.shape[2]
    once = pl.Buffered(1)
    grid_spec = pltpu.PrefetchScalarGridSpec(
        num_scalar_prefetch=2,
        grid=(cap // MOE_BLK,),
        in_specs=[
            pl.BlockSpec((MOE_BLK, d), lambda i, eb, ns: (i, 0)),
            pl.BlockSpec((MOE_BLK, 1), lambda i, eb, ns: (i, 0)),
            pl.BlockSpec((None, d, f), lambda i, eb, ns: (eb[i], 0, 0), pipeline_mode=once),
            pl.BlockSpec((None, d, f), lambda i, eb, ns: (eb[i], 0, 0), pipeline_mode=once),
            pl.BlockSpec((None, f, d), lambda i, eb, ns: (eb[i], 0, 0), pipeline_mode=once),
        ],
        out_specs=pl.BlockSpec((MOE_BLK, d), lambda i, eb, ns: (i, 0)),
    )
    return pl.pallas_call(
        _moe_expert_kernel,
        out_shape=jax.ShapeDtypeStruct((cap, d), BF16),
        grid_spec=grid_spec,
        compiler_params=_cparams(("arbitrary",)),
        name="moe_experts",
    )(plan["e_blk"], plan["n_sub"], xg, gs, w1, w3, w2)


MOE_WIN = 2 * N_EXPERTS


def _moe_combine_kernel(st_ref, cn_ref, bl_ref, dt_ref, y_hbm, x1_ref, g_ref, o_ref, ybuf, sem):
    c = pl.program_id(0)
    n = cn_ref[c]
    s0 = st_ref[c]
    total = st_ref[pl.num_programs(0) - 1] + cn_ref[pl.num_programs(0) - 1]
    d = dt_ref[...]

    def block_copy(item):
        slot = lax.rem(item, MOE_NBUF)
        start = pl.multiple_of(bl_ref[item] * MOE_SUB, MOE_SUB)
        return pltpu.make_async_copy(y_hbm.at[pl.ds(start, MOE_SUB), :], ybuf.at[slot], sem.at[slot])

    @pl.when(c == 0)
    def _():
        for a in range(MOE_AHEAD):
            @pl.when(a < total)
            def _():
                block_copy(a).start()

    def arrive(item):
        block_copy(item).wait()

        @pl.when(item + MOE_AHEAD < total)
        def _():
            block_copy(item + MOE_AHEAD).start()

    def rows_of(item):
        scol = lax.broadcasted_iota(jnp.int32, (MOE_TC, MOE_SUB), 1) + bl_ref[item] * MOE_SUB
        p = jnp.where((scol == d[:, 0:1]) | (scol == d[:, 1:2]), 1.0, 0.0).astype(BF16)
        return _dot(p, ybuf[lax.rem(item, MOE_NBUF)])

    def group(k, acc):
        item = s0 + MOE_GROUP * k
        for a in range(MOE_GROUP):
            arrive(item + a)
        for a in range(MOE_GROUP):
            acc = acc + rows_of(item + a)
        return acc

    def single(k, acc):
        arrive(s0 + grouped + k)
        return acc + rows_of(s0 + grouped + k)

    grouped = (n // MOE_GROUP) * MOE_GROUP
    acc = lax.fori_loop(0, n // MOE_GROUP, group, jnp.zeros(o_ref.shape, F32))
    acc = lax.fori_loop(0, n - grouped, single, acc)
    o_ref[...] = x1_ref[...] + g_ref[0] * acc


def _moe_combine(y, dest_t, x1, gate, plan, seq):
    n_tok, d = x1.shape
    n_chunks = n_tok // MOE_TC
    per_b = seq // MOE_TC
    grid_spec = pltpu.PrefetchScalarGridSpec(
        num_scalar_prefetch=3,
        grid=(n_chunks,),
        in_specs=[
            pl.BlockSpec((MOE_TC, 2), lambda c, *_: (c, 0)),
            pl.BlockSpec(memory_space=pl.ANY),
            pl.BlockSpec((MOE_TC, d), lambda c, *_: (c, 0)),
            pl.BlockSpec((1, 1, d), lambda c, *_: (c // per_b, 0, 0)),
        ],
        out_specs=pl.BlockSpec((MOE_TC, d), lambda c, *_: (c, 0)),
        scratch_shapes=[pltpu.VMEM((MOE_NBUF, MOE_SUB, d), BF16), pltpu.SemaphoreType.DMA((MOE_NBUF,))],
    )
    return pl.pallas_call(
        _moe_combine_kernel,
        out_shape=jax.ShapeDtypeStruct((n_tok, d), F32),
        grid_spec=grid_spec,
        compiler_params=_cparams(("arbitrary",)),
        name="moe_combine",
    )(plan["c_start"], plan["c_count"], plan["c_blocks"], dest_t, y, x1, gate)


def _split_mod(mod_l, batch):
    d = D_MODEL
    lat = tuple(mod_l[:batch, k * d:(k + 1) * d][:, None, :] for k in range(ADA_CHUNKS))
    ctx = tuple(jnp.broadcast_to(mod_l[batch:batch + 1, k * d:(k + 1) * d][:, None, :], (batch, 1, d))
                for k in range(ADA_CHUNKS))
    return lat, ctx


def _score_bound(q_gain, k_gain, n):
    return 1.02 * n * jnp.max(jnp.abs(q_gain)) * jnp.max(jnp.abs(k_gain))


def _block_ones(n, block):
    idx = jnp.arange(n) // block
    return (idx[:, None] == idx[None, :]).astype(BF16)


def _even_layer(x, xc, mod_l, norm1_g, norm2_g, w_in, w_out, na_q_g, na_k_g, na_rpb, diff_q_g, diff_k_g,
                lq1, lk1, lq2, lk2, subln_g, wg, wu, wd, layer_idx, need_ctx):
    b, s, d = x.shape
    (sh1, sc1, g1, sh2, sc2, g2), (csh1, csc1, cg1, csh2, csc2, cg2) = _split_mod(mod_l, b)
    qscale = HEAD_DIM ** -0.5 * LOG2E
    lam_init = 0.8 - 0.6 * math.exp(-0.3 * layer_idx)
    hg = jnp.stack([jnp.tile(na_q_g, 8) * qscale, jnp.tile(na_k_g, 8),
                    jnp.tile(diff_q_g, 8) * qscale, jnp.tile(diff_k_g, 8)]).astype(F32)
    e64 = _block_ones(MXU_DIM, HEAD_DIM)
    cos, sin = _rope_tables(s, HEAD_DIM, 16)
    ng1 = norm1_g[None, :]
    ng2 = norm2_g[None, :]
    w_in_b = w_in.astype(BF16)
    wo_a = w_out[:512].astype(BF16)
    wo_b = w_out[512:].astype(BF16)
    wg_b, wu_b, wd_b = wg.astype(BF16), wu.astype(BF16), wd.astype(BF16)
    lam_p = jnp.stack([lq1, lk1, lq2, lk2]).astype(F32)
    sg = subln_g[None, :].astype(F32)
    ident = lambda j: j

    tm = min(ROW_TM, s)
    tc = xc.shape[1]
    qa, ka, va, qb, kb, vb = _even_proj(x, sh1, sc1, ng1, w_in_b, e64, hg, cos, sin, use_rope=True, tm=tm)
    qca, kca, vca, qcb, kcb, vcb = _even_proj(xc, csh1, csc1, ng1, w_in_b, e64, hg, cos, sin, use_rope=False, tm=tc)
    out_a = _na_attention(qa, ka, va, kca, vca, _na_bias_table(na_rpb), _score_bound(hg[0], hg[1], HEAD_DIM))
    out_b = _flash(qb, kb, vb, kcb, vcb, mode="diff", kv_map=ident, n_chunks=4, tq=FLASH_TQ, tk=FLASH_TK,
                   extra=(lam_p, sg), lam_init=lam_init, score_bound=_score_bound(hg[2], hg[3], HEAD_DIM))
    x2 = _post_ffn(out_a, out_b, wo_a, wo_b, x, g1, ng2, sh2, sc2, g2, wg_b, wu_b, wd_b, tm=tm)
    if not need_ctx:
        return x2, None
    oca = _flash_call(qca, None, None, kca, vca, mode="pair", online=True, kv_map=ident, n_chunks=4, tq=tc, tk=tc)
    ocb = _flash(qcb, None, None, kcb, vcb, mode="diff", kv_map=ident, n_chunks=4, tq=tc, tk=tc,
                 extra=(lam_p, sg), lam_init=lam_init)
    xc2 = _post_ffn(oca, ocb, wo_a, wo_b, xc, cg1, ng2, csh2, csc2, cg2, wg_b, wu_b, wd_b, tm=tc)
    return x2, xc2


def _odd_layer(x, xc, mod_l, norm1_g, norm2_g, w_in, w_out, gqa_q_g, gqa_k_g, cq_g, w_uq, ckv_g, w_ukv,
               mla_q_g, mla_k_g, router, w1, w3, w2):
    b, s, d = x.shape
    n_tok = b * s
    (sh1, sc1, g1, sh2, sc2, g2), (csh1, csc1, _, _, _, _) = _split_mod(mod_l, b)
    ng1 = norm1_g[None, :]
    ng2 = norm2_g[None, :]

    z = lambda n: jnp.zeros((d, n), w_in.dtype)
    k0, k1, v0, v1 = (w_in[:, 512 + 64 * i:576 + 64 * i] for i in range(4))
    w_p = jnp.concatenate([w_in[:, 0:512], k0, k0, k1, k1, v0, v0, v1, v1, w_in[:, 768:1024], w_in[:, 1024:1152],
                           z(MLA_NOPE), w_in[:, 1152:1184], z(LANES - MLA_QK)], axis=1).astype(BF16)
    wuq_p = jnp.pad(w_uq.reshape(MLA_Q_RANK, MLA_HEADS, MLA_QK), ((0, 0), (0, 0), (0, LANES - MLA_QK)))
    wuq_p = wuq_p.reshape(MLA_Q_RANK, MLA_HEADS * LANES).astype(BF16)
    ukv = w_ukv.reshape(MLA_KV_RANK, MLA_HEADS, MLA_NOPE + MLA_V)
    uk = jnp.pad(ukv[:, :, :MLA_NOPE], ((0, 0), (0, 0), (0, LANES - MLA_NOPE))).reshape(MLA_KV_RANK, -1)
    uv = ukv[:, :, MLA_NOPE:].reshape(MLA_KV_RANK, -1)
    wukv_p = jnp.concatenate([uk, uv], axis=1).astype(BF16)
    e64 = _block_ones(MXU_DIM, HEAD_DIM)
    e128 = _block_ones(MXU_DIM, LANES)
    qscale = HEAD_DIM ** -0.5 * LOG2E
    mscale = MLA_QK ** -0.5 * LOG2E
    pad_row = lambda v: jnp.pad(v, (0, MLA_HEADS * LANES - v.shape[0]))
    pad_head = lambda v: jnp.tile(jnp.pad(v, (0, LANES - MLA_QK)), MLA_HEADS)
    gains = jnp.stack([pad_row(jnp.tile(gqa_q_g, 8) * qscale), pad_row(jnp.tile(gqa_k_g, 4)),
                       pad_row(cq_g), pad_row(ckv_g), pad_head(mla_q_g) * mscale, pad_head(mla_k_g),
                       jnp.zeros((MLA_HEADS * LANES,), F32), jnp.zeros((MLA_HEADS * LANES,), F32)]).astype(F32)
    cos, sin = _rope_tables(s, HEAD_DIM, 16)
    cosm, sinm = _mla_rope_tables(s)
    wo_a = w_out[:512].astype(BF16)
    wo_b = w_out[512:].astype(BF16)

    tm = min(ROW_TM, s)
    tc = xc.shape[1]
    proj = functools.partial(_odd_proj, ng=ng1, w=w_p, wuq=wuq_p, wukv=wukv_p, e64=e64, e128=e128, gains=gains,
                             cos=cos, sin=sin, cosm=cosm, sinm=sinm)
    qc, kc, vc, qd, kd, vd = proj(x, sh1, sc1, use_rope=True, tm=tm)
    _, kcc, vcc, _, kcd, vcd = proj(xc, csh1, csc1, use_rope=False, tm=tc)
    out_c = _flash(qc, kc, vc, kcc, vcc, mode="pair", kv_map=lambda j: j // 2, n_chunks=4, tq=FLASH_TQ, tk=FLASH_TK,
                   score_bound=_score_bound(gains[0], gains[1], HEAD_DIM))
    out_d = _flash(qd, kd, vd, kcd, vcd, mode="mla", kv_map=lambda j: j, n_chunks=4, tq=FLASH_TQ, tk=FLASH_TK,
                   score_bound=_score_bound(gains[4], gains[5], MLA_QK))
    r_hi = router.astype(BF16)
    r_lo = (router - r_hi.astype(F32)).astype(BF16)
    pad_e = lambda m: jnp.pad(m, ((0, 0), (0, LANES - N_EXPERTS)))
    x1, h2, ei, gt, rk, cnt = _post_attn_router(out_c, out_d, wo_a, wo_b, x, g1, ng2, sh2, sc2,
                                                jnp.concatenate([pad_e(r_hi), pad_e(r_lo)], axis=1), tm=tm)
    plan = _moe_plan(ei, rk, cnt[:, 0], n_tok)
    xg, gs = _moe_gather(h2.reshape(n_tok, d), plan["dest"], gt, plan)
    y = _moe_experts(xg, gs, w1.astype(BF16), w3.astype(BF16), w2.astype(BF16), plan)
    out = _moe_combine(y, plan["dest"].T, x1.reshape(n_tok, d), g2, plan, s)
    return out.reshape(b, s, d)


def _mod_vectors(c, c_ctx, ada_w, ada_b):
    b = c.shape[0]
    cc = jnp.zeros((MOD_ROWS, D_MODEL), F32).at[:b].set(c).at[b].set(c_ctx)
    return _modvec(cc, ada_w, ada_b[:, None, :])


def kernel(x, c, ctx, c_ctx, ada_w, ada_b, norm1_g, norm2_g, ev_w_in, ev_w_out, na_q_g, na_k_g, na_rpb,
           diff_q_g, diff_k_g, diff_lq1, diff_lk1, diff_lq2, diff_lk2, diff_subln_g,
           ffn_w_gate, ffn_w_up, ffn_w_down, od_w_in, od_w_out, gqa_q_g, gqa_k_g, mla_cq_g, mla_w_uq,
           mla_ckv_g, mla_w_ukv, mla_q_g, mla_k_g, moe_router, moe_w1, moe_w3, moe_w2):
    mod = _mod_vectors(c, c_ctx, ada_w, ada_b)
    x, xc = _even_layer(x, ctx, mod[0], norm1_g[0], norm2_g[0], ev_w_in[0], ev_w_out[0], na_q_g[0], na_k_g[0],
                        na_rpb[0], diff_q_g[0], diff_k_g[0], diff_lq1[0], diff_lk1[0], diff_lq2[0], diff_lk2[0],
                        diff_subln_g[0], ffn_w_gate[0], ffn_w_up[0], ffn_w_down[0], 0, True)
    return _odd_layer(x, xc, mod[1], norm1_g[1], norm2_g[1], od_w_in[0], od_w_out[0], gqa_q_g[0], gqa_k_g[0],
                      mla_cq_g[0], mla_w_uq[0], mla_ckv_g[0], mla_w_ukv[0], mla_q_g[0], mla_k_g[0],
                      moe_router[0], moe_w1[0], moe_w3[0], moe_w2[0])
```

```python
import functools
import math

import jax
import jax.numpy as jnp
from jax import lax
from jax.experimental import pallas as pl
from jax.experimental.pallas import tpu as pltpu

F32 = jnp.float32
BF16 = jnp.bfloat16
HIGHEST = lax.Precision.HIGHEST

D_MODEL = 1024
GRID_W = 64
HEAD_DIM = 64
ROPE_THETA = 10000.0
NORM_EPS = 1e-6
NEG_INF = -1e30
ADA_CHUNKS = 6
LOG2E = 1.4426950408889634

NA_HEADS = 8
NA_WIN_H = 8
NA_WIN_W = 16
DIFF_HEADS = 4
DIFF_V_DIM = 2 * HEAD_DIM
GQA_Q_HEADS = 8
GQA_KV_HEADS = 2
MLA_HEADS = 8
MLA_NOPE = 64
MLA_ROPE = 32
MLA_QK = MLA_NOPE + MLA_ROPE
MLA_V = 64
MLA_Q_RANK = 256
MLA_KV_RANK = 128
D_FF = 2816
N_EXPERTS = 8
D_FF_EXPERT = 3584

LANES = 128
MXU_DIM = 256
VMEM_LIMIT = 56 * 1024 * 1024
MOD_ROWS = 16
MOE_TF = 512
SUM_ROWS = 16
ROW_TM = 1024
FLASH_TQ = 1024
FLASH_TK = 512
EXP2_SAFE_RANGE = 64.0


def _cparams(sem):
    return pltpu.CompilerParams(dimension_semantics=sem, vmem_limit_bytes=VMEM_LIMIT)


def _dot(a, b):
    return jnp.dot(a, b, preferred_element_type=F32)


def _dot_nt(a, b):
    return lax.dot_general(a, b, (((1,), (1,)), ((), ())), preferred_element_type=F32)


def _silu(x):
    return x * (1.0 / (1.0 + jnp.exp(-x)))


def _modvec_kernel(c_ref, w_ref, b_ref, o_ref):
    s = _silu(c_ref[...])
    o_ref[...] = jnp.dot(s, w_ref[...], preferred_element_type=F32, precision=HIGHEST) + b_ref[...]


def _modvec(cc, ada_w, ada_b):
    n_layers, d, n = ada_w.shape
    tn = 768
    return pl.pallas_call(
        _modvec_kernel,
        out_shape=jax.ShapeDtypeStruct((n_layers, MOD_ROWS, n), F32),
        grid=(n_layers, n // tn),
        in_specs=[
            pl.BlockSpec((MOD_ROWS, d), lambda l, j: (0, 0)),
            pl.BlockSpec((None, d, tn), lambda l, j: (l, 0, j)),
            pl.BlockSpec((None, 1, tn), lambda l, j: (l, 0, j)),
        ],
        out_specs=pl.BlockSpec((None, MOD_ROWS, tn), lambda l, j: (l, 0, j)),
        compiler_params=_cparams(("arbitrary", "arbitrary")),
        name="modvec",
    )(cc, ada_w, ada_b)


def _norm_mod(x, g, shift, scale):
    ms = jnp.mean(x * x, axis=-1, keepdims=True)
    return (x * lax.rsqrt(ms + NORM_EPS) * g) * (1.0 + scale) + shift


def _row_blocks(n, size=256):
    size = min(size, n)
    return [slice(r, r + size) for r in range(0, n, size)]


def _group_norm(t, e, g, inv_n):
    outs = []
    for c in range(t.shape[1] // MXU_DIM):
        cols = slice(c * MXU_DIM, (c + 1) * MXU_DIM)
        tc = t[:, cols]
        ss = _dot((tc * tc).astype(BF16), e)
        outs.append(tc * lax.rsqrt(ss * inv_n + NORM_EPS) * g[:, cols])
    return outs[0] if len(outs) == 1 else jnp.concatenate(outs, axis=1)


def _rope_chunks(t, cos, sin, lane_hi, shift):
    outs = []
    for c in range(t.shape[1] // LANES):
        tc = t[:, c * LANES:(c + 1) * LANES]
        up = pltpu.roll(tc, LANES - shift, 1)
        dn = pltpu.roll(tc, shift, 1)
        outs.append(tc * cos + jnp.where(lane_hi, dn, up) * sin)
    return outs[0] if len(outs) == 1 else jnp.concatenate(outs, axis=1)


def _rope_tables(seq, unit, pair_shift):
    pos = jnp.arange(seq)
    rows, cols = pos // GRID_W, pos % GRID_W
    quarter = unit // 4
    assert quarter == pair_shift
    freqs = ROPE_THETA ** (-jnp.arange(quarter, dtype=F32) / quarter)
    lane = jnp.arange(LANES)
    u = lane % unit
    use_col = (u // (unit // 2)) == 1
    fi = u % quarter
    p = jnp.where(use_col[None, :], cols[:, None], rows[:, None]).astype(F32)
    ang = p * freqs[fi][None, :]
    second = ((u % (unit // 2)) // quarter) == 1
    cos = jnp.cos(ang)
    sin = jnp.where(second[None, :], jnp.sin(ang), -jnp.sin(ang))
    return cos, sin


def _even_proj_kernel(x_ref, sh_ref, sc_ref, ng_ref, w_ref, e_ref, hg_ref, cos_ref, sin_ref,
                      qa_ref, ka_ref, va_ref, qb_ref, kb_ref, vb_ref, *, use_rope):
    hg = hg_ref[...]
    e = e_ref[...]
    inv_n = 1.0 / HEAD_DIM
    for rows in _row_blocks(x_ref.shape[1]):
        h = _norm_mod(x_ref[0, rows, :], ng_ref[...], sh_ref[0], sc_ref[0])
        p = _dot(h.astype(BF16), w_ref[...])
        qa = _group_norm(p[:, 0:512], e, hg[0:1], inv_n)
        ka = _group_norm(p[:, 512:1024], e, hg[1:2], inv_n)
        qb = _group_norm(p[:, 1536:2048], e, hg[2:3], inv_n)
        kb = _group_norm(p[:, 2048:2560], e, hg[3:4], inv_n)
        if use_rope:
            lane = lax.broadcasted_iota(jnp.int32, (1, LANES), 1)
            lane_hi = (lane & 16) != 0
            cos, sin = cos_ref[rows, :], sin_ref[rows, :]
            qb = _rope_chunks(qb, cos, sin, lane_hi, 16)
            kb = _rope_chunks(kb, cos, sin, lane_hi, 16)
        qa_ref[0, rows, :] = qa.astype(BF16)
        ka_ref[0, rows, :] = ka.astype(BF16)
        va_ref[0, rows, :] = p[:, 1024:1536].astype(BF16)
        qb_ref[0, rows, :] = qb.astype(BF16)
        kb_ref[0, rows, :] = kb.astype(BF16)
        vb_ref[0, :, rows] = p[:, 2560:3072].T.astype(BF16)


def _even_proj(x, shift, scale, ng, w, e64, hg, cos, sin, *, use_rope, tm):
    b, s, d = x.shape
    n = w.shape[1]
    row = lambda bi, i: (bi, i, 0)
    vec = lambda bi, i: (bi, 0, 0)
    const = lambda bi, i: (0, 0)
    out = jax.ShapeDtypeStruct((b, s, 512), BF16)
    return pl.pallas_call(
        functools.partial(_even_proj_kernel, use_rope=use_rope),
        out_shape=(out,) * 5 + (jax.ShapeDtypeStruct((b, 512, s), BF16),),
        grid=(b, s // tm),
        in_specs=[
            pl.BlockSpec((1, tm, d), row),
            pl.BlockSpec((1, 1, d), vec),
            pl.BlockSpec((1, 1, d), vec),
            pl.BlockSpec((1, d), const),
            pl.BlockSpec((d, n), const),
            pl.BlockSpec(e64.shape, const),
            pl.BlockSpec((4, 512), const),
            pl.BlockSpec((tm, LANES), lambda bi, i: (i, 0)),
            pl.BlockSpec((tm, LANES), lambda bi, i: (i, 0)),
        ],
        out_specs=(pl.BlockSpec((1, tm, 512), row),) * 5 + (pl.BlockSpec((1, 512, tm), lambda bi, i: (bi, 0, i)),),
        compiler_params=_cparams(("parallel", "parallel")),
        name="even_proj",
    )(x, shift, scale, ng, w, e64, hg, cos, sin)


def _pair_stack(q):
    lane = lax.broadcasted_iota(jnp.int32, q.shape, 1)
    zero = jnp.zeros_like(q)
    return jnp.concatenate([jnp.where(lane < HEAD_DIM, q, zero), jnp.where(lane >= HEAD_DIM, q, zero)], axis=0)


def _pair_merge(o, tq):
    lane = lax.broadcasted_iota(jnp.int32, (tq, LANES), 1)
    return jnp.where(lane < HEAD_DIM, o[:tq], o[tq:])


def _with_ones(v):
    return jnp.concatenate([v, jnp.ones(v.shape, BF16)], axis=1)


def _na_kernel(q_ref, k_ref, v_ref, kc_ref, vc_ref, bias_ref, o_ref, *, rows_per_step, n_rows, direct):
    rb = pl.program_id(2)
    kc = kc_ref[0]
    vc = vc_ref[0]
    n_lat = NA_WIN_H * GRID_W
    if direct:
        nq = rows_per_step * GRID_W
        qs_all = _pair_stack(q_ref[0])
        o_ctx = _dot(jnp.exp2(_dot_nt(qs_all, kc)).astype(BF16), _with_ones(vc))
        for i in range(rows_per_step):
            r = rb * rows_per_step + i
            r0 = jnp.clip(r - NA_WIN_H // 2, 0, n_rows - NA_WIN_H)
            start = pl.multiple_of(r0 * GRID_W, GRID_W)
            kw = k_ref[0, pl.ds(start, n_lat), :]
            vw = v_ref[0, pl.ds(start, n_lat), :]
            lo = slice(i * GRID_W, (i + 1) * GRID_W)
            hi = slice(nq + i * GRID_W, nq + (i + 1) * GRID_W)
            qs = jnp.concatenate([qs_all[lo], qs_all[hi]], axis=0)
            p = jnp.exp2(_dot_nt(qs, kw) + bias_ref[r - r0]).astype(BF16)
            tot = _dot(p, _with_ones(vw)) + jnp.concatenate([o_ctx[lo], o_ctx[hi]], axis=0)
            o = tot[:, :LANES] / tot[:, LANES:]
            o_ref[0, lo, :] = _pair_merge(o, GRID_W).astype(BF16)
        return
    for i in range(rows_per_step):
        r = rb * rows_per_step + i
        r0 = jnp.clip(r - NA_WIN_H // 2, 0, n_rows - NA_WIN_H)
        start = pl.multiple_of(r0 * GRID_W, GRID_W)
        kw = k_ref[0, pl.ds(start, n_lat), :]
        vw = v_ref[0, pl.ds(start, n_lat), :]
        qs = _pair_stack(q_ref[0, i * GRID_W:(i + 1) * GRID_W, :])
        s_lat = _dot_nt(qs, kw) + bias_ref[r - r0]
        s_ctx = _dot_nt(qs, kc)
        m = jnp.maximum(jnp.max(s_lat, axis=-1, keepdims=True), jnp.max(s_ctx, axis=-1, keepdims=True))
        p_lat = jnp.exp2(s_lat - m)
        p_ctx = jnp.exp2(s_ctx - m)
        l = jnp.sum(p_lat, axis=-1, keepdims=True) + jnp.sum(p_ctx, axis=-1, keepdims=True)
        o = _dot(p_lat.astype(BF16), vw) + _dot(p_ctx.astype(BF16), vc)
        o = o / l
        o_ref[0, i * GRID_W:(i + 1) * GRID_W, :] = _pair_merge(o, GRID_W).astype(BF16)


def _na_attention(q, k, v, kc, vc, bias, score_bound, *, rows_per_step=32):
    bound = score_bound + jnp.max(jnp.where(bias > 0.5 * NEG_INF, jnp.abs(bias), 0.0))
    rows_per_step = min(rows_per_step, q.shape[1] // GRID_W)
    return lax.cond(bound <= EXP2_SAFE_RANGE,
                    lambda: _na_call(q, k, v, kc, vc, bias, rows_per_step=rows_per_step, direct=True),
                    lambda: _na_call(q, k, v, kc, vc, bias, rows_per_step=rows_per_step, direct=False))


def _na_call(q, k, v, kc, vc, bias, *, rows_per_step, direct):
    b, s, _ = q.shape
    c = kc.shape[1]
    n_rows = s // GRID_W
    tq = rows_per_step * GRID_W
    return pl.pallas_call(
        functools.partial(_na_kernel, rows_per_step=rows_per_step, n_rows=n_rows, direct=direct),
        out_shape=jax.ShapeDtypeStruct((b, s, 512), BF16),
        grid=(b, 4, n_rows // rows_per_step),
        in_specs=[
            pl.BlockSpec((1, tq, LANES), lambda bi, j, i: (bi, i, j)),
            pl.BlockSpec((1, s, LANES), lambda bi, j, i: (bi, 0, j)),
            pl.BlockSpec((1, s, LANES), lambda bi, j, i: (bi, 0, j)),
            pl.BlockSpec((1, c, LANES), lambda bi, j, i: (bi, 0, j)),
            pl.BlockSpec((1, c, LANES), lambda bi, j, i: (bi, 0, j)),
            pl.BlockSpec((NA_WIN_H, None, LANES, NA_WIN_H * GRID_W), lambda bi, j, i: (0, j, 0, 0)),
        ],
        out_specs=pl.BlockSpec((1, tq, LANES), lambda bi, j, i: (bi, i, j)),
        compiler_params=_cparams(("parallel", "parallel", "parallel")),
        name="na_attention" + ("" if direct else "_online"),
    )(q, k, v, kc, vc, bias)


def _na_bias_table(rpb):
    w = GRID_W
    col = jnp.arange(w)
    c0 = jnp.clip(col - NA_WIN_W // 2, 0, w - NA_WIN_W)
    col_in = (col[None, :] >= c0[:, None]) & (col[None, :] < c0[:, None] + NA_WIN_W)
    left = (w - 1) - (NA_WIN_W - 1)
    ext = jnp.pad(rpb, ((0, 0), (0, 0), (left, 2 * w - left - (2 * NA_WIN_W - 1))), mode="edge")
    h, nr, _ = rpb.shape
    flat = jnp.broadcast_to(ext[:, :, None, :], (h, nr, w, 2 * w)).reshape(h, nr, w * 2 * w)
    toep = flat[:, :, :w * (2 * w - 1)].reshape(h, nr, w, 2 * w - 1)[:, :, :, w - 1:]
    toep = jnp.where(col_in[None, None], toep * LOG2E, NEG_INF)
    variants = []
    for v in range(NA_WIN_H):
        tv = toep[:, NA_WIN_H - 1 - v:2 * NA_WIN_H - 1 - v]
        variants.append(tv.transpose(0, 2, 1, 3).reshape(NA_HEADS // 2, 2 * w, NA_WIN_H * w))
    return jnp.stack(variants, axis=0).astype(F32)


def _flash_kernel(*refs, mode, online, tq, tk, s_len, c_len, lam_init):
    if mode == "diff":
        q_ref, k_ref, v_ref, kc_ref, vc_ref, lam_ref, sg_ref, o_ref = refs
    else:
        q_ref, k_ref, v_ref, kc_ref, vc_ref, o_ref = refs

    q = q_ref[0]
    if mode == "mla":
        q_parts = (q[:, :LANES], q[:, LANES:])
    else:
        qs = _pair_stack(q)

    def scores(kt):
        if mode == "mla":
            return jnp.concatenate([_dot_nt(q_parts[0], kt[:, :LANES]), _dot_nt(q_parts[1], kt[:, LANES:])], axis=0)
        return _dot_nt(qs, kt)

    def step(kt, vt, carry):
        m, l, acc = carry
        s = scores(kt)
        m_new = jnp.maximum(m, jnp.max(s, axis=-1, keepdims=True))
        alpha = jnp.exp2(m - m_new)
        p = jnp.exp2(s - m_new)
        l = alpha * l + jnp.sum(p, axis=-1, keepdims=True)
        acc = alpha * acc + _dot(p.astype(BF16), vt)
        return m_new, l, acc

    def body(t, carry):
        start = pl.multiple_of(t * tk, tk)
        return step(k_ref[0, pl.ds(start, tk), :], v_ref[0, pl.ds(start, tk), :], carry)

    def scores_t(kt):
        if mode == "mla":
            return jnp.concatenate([_dot_nt(kt[:, :LANES], q_parts[0]), _dot_nt(kt[:, LANES:], q_parts[1])], axis=1)
        return _dot_nt(kt, qs)

    def direct(kt, vtt, acc):
        p = jnp.exp2(scores_t(kt)).astype(BF16)
        v1 = jnp.concatenate([vtt, jnp.ones((SUM_ROWS, vtt.shape[1]), BF16)], axis=0)
        return acc + _dot(v1, p)

    if online:
        carry = (jnp.full((2 * tq, 1), -jnp.inf, F32), jnp.zeros((2 * tq, 1), F32),
                 jnp.zeros((2 * tq, LANES), F32))
        if s_len:
            carry = lax.fori_loop(0, s_len // tk, body, carry)
        if c_len:
            carry = step(kc_ref[0], vc_ref[0], carry)
        _, l, acc = carry
        o = acc / l
    else:
        acc = jnp.zeros((LANES + SUM_ROWS, 2 * tq), F32)
        for t in range(s_len // tk):
            acc = direct(k_ref[0, t * tk:(t + 1) * tk, :], v_ref[0, :, t * tk:(t + 1) * tk], acc)
        if c_len:
            acc = direct(kc_ref[0], vc_ref[0], acc)
        o_t = acc[:LANES] / acc[LANES:LANES + 1]
    if mode == "diff":
        lp = lam_ref[...]
        lam = (jnp.exp(jnp.sum(lp[0:1] * lp[1:2], axis=-1, keepdims=True))
               - jnp.exp(jnp.sum(lp[2:3] * lp[3:4], axis=-1, keepdims=True)) + lam_init)
        if online:
            d = o[:tq] - lam * o[tq:]
            dn = d * lax.rsqrt(jnp.mean(d * d, axis=-1, keepdims=True) + NORM_EPS)
        else:
            d_t = o_t[:, :tq] - lam * o_t[:, tq:]
            dn = (d_t * lax.rsqrt(jnp.mean(d_t * d_t, axis=0, keepdims=True) + NORM_EPS)).T
        o_ref[0] = (dn * sg_ref[...] * (1.0 - lam_init)).astype(BF16)
    elif online:
        o_ref[0] = _pair_merge(o, tq).astype(BF16)
    else:
        row = lax.broadcasted_iota(jnp.int32, (LANES, tq), 0)
        o_ref[0] = jnp.where(row < HEAD_DIM, o_t[:, :tq], o_t[:, tq:]).T.astype(BF16)


def _flash(q, k, vt, kc, vct, *, score_bound=None, **kw):
    rows = lambda a: None if a is None else jnp.swapaxes(a, 1, 2)
    if score_bound is None:
        return _flash_call(q, k, rows(vt), kc, rows(vct), online=True, **kw)
    return lax.cond(score_bound <= EXP2_SAFE_RANGE,
                    lambda: _flash_call(q, k, vt, kc, vct, online=False, **kw),
                    lambda: _flash_call(q, k, rows(vt), kc, rows(vct), online=True, **kw))


def _flash_call(q, k, v, kc, vc, *, mode, online, kv_map, n_chunks, tq, tk, extra=(), lam_init=0.0):
    b, s, _ = q.shape
    qw = 2 * LANES if mode == "mla" else LANES
    c_len = kc.shape[1]
    if k is None:
        k, v, s_len = kc, vc, 0
    else:
        s_len = k.shape[1]
    kk = k.shape[1]
    if online:
        v_spec = lambda n: pl.BlockSpec((1, n, LANES), lambda bi, j, i: (bi, 0, kv_map(j)))
    else:
        v_spec = lambda n: pl.BlockSpec((1, LANES, n), lambda bi, j, i: (bi, kv_map(j), 0))
    in_specs = [
        pl.BlockSpec((1, tq, qw), lambda bi, j, i: (bi, i, j)),
        pl.BlockSpec((1, kk, qw), lambda bi, j, i: (bi, 0, kv_map(j) if mode != "mla" else j)),
        v_spec(kk),
        pl.BlockSpec((1, c_len, qw), lambda bi, j, i: (bi, 0, kv_map(j) if mode != "mla" else j)),
        v_spec(c_len),
    ]
    for e in extra:
        in_specs.append(pl.BlockSpec(e.shape, lambda bi, j, i: (0, 0)))
    return pl.pallas_call(
        functools.partial(_flash_kernel, mode=mode, online=online, tq=tq, tk=tk, s_len=s_len, c_len=c_len,
                          lam_init=lam_init),
        out_shape=jax.ShapeDtypeStruct((b, s, n_chunks * LANES), BF16),
        grid=(b, n_chunks, s // tq),
        in_specs=in_specs,
        out_specs=pl.BlockSpec((1, tq, LANES), lambda bi, j, i: (bi, i, j)),
        compiler_params=_cparams(("parallel", "parallel", "parallel")),
        name="flash_" + mode + ("_online" if online else ""),
    )(q, k, v, kc, vc, *extra)


def _post_ffn_kernel(oa_ref, ob_ref, wa_ref, wb_ref, x_ref, g1_ref, ng_ref, sh_ref, sc_ref, g2_ref,
                     wg_ref, wu_ref, wd_ref, o_ref, *, tf):
    y = _dot(oa_ref[0], wa_ref[...]) + _dot(ob_ref[0], wb_ref[...])
    x1 = x_ref[0] + g1_ref[0] * y
    h = _norm_mod(x1, ng_ref[...], sh_ref[0], sc_ref[0]).astype(BF16)
    acc = jnp.zeros(o_ref.shape[1:], F32)
    for f in range(wg_ref.shape[1] // tf):
        sl = slice(f * tf, (f + 1) * tf)
        a = _dot(h, wg_ref[:, sl])
        u = _dot(h, wu_ref[:, sl])
        acc = acc + _dot((_silu(a) * u).astype(BF16), wd_ref[sl, :])
    o_ref[0] = x1 + g2_ref[0] * acc


def _post_ffn(oa, ob, wa, wb, x, gate1, ng, shift, scale, gate2, wg, wu, wd, *, tm, tf=256):
    b, s, d = x.shape
    f = wg.shape[1]
    row = lambda bi, i: (bi, i, 0)
    vec = lambda bi, i: (bi, 0, 0)
    const = lambda bi, i: (0, 0)
    return pl.pallas_call(
        functools.partial(_post_ffn_kernel, tf=tf),
        out_shape=jax.ShapeDtypeStruct((b, s, d), F32),
        grid=(b, s // tm),
        in_specs=[
            pl.BlockSpec((1, tm, 512), row),
            pl.BlockSpec((1, tm, 512), row),
            pl.BlockSpec((512, d), const),
            pl.BlockSpec((512, d), const),
            pl.BlockSpec((1, tm, d), row),
            pl.BlockSpec((1, 1, d), vec),
            pl.BlockSpec((1, d), const),
            pl.BlockSpec((1, 1, d), vec),
            pl.BlockSpec((1, 1, d), vec),
            pl.BlockSpec((1, 1, d), vec),
            pl.BlockSpec((d, f), const),
            pl.BlockSpec((d, f), const),
            pl.BlockSpec((f, d), const),
        ],
        out_specs=pl.BlockSpec((1, tm, d), row),
        compiler_params=_cparams(("parallel", "parallel")),
        name="post_ffn",
    )(oa, ob, wa, wb, x, gate1, ng, shift, scale, gate2, wg, wu, wd)


def _rms(t, g):
    return t * lax.rsqrt(jnp.mean(t * t, axis=-1, keepdims=True) + NORM_EPS) * g


def _odd_proj_kernel(x_ref, sh_ref, sc_ref, ng_ref, w_ref, wuq_ref, wukv_ref, e_ref, e2_ref, g_ref,
                     cos_ref, sin_ref, cosm_ref, sinm_ref,
                     qc_ref, kc_ref, vc_ref, qd_ref, kd_ref, vd_ref, *, use_rope):
    g = g_ref[...]
    e = e_ref[...]
    e2 = e2_ref[...]
    inv_n = 1.0 / HEAD_DIM
    for rows in _row_blocks(x_ref.shape[1]):
        h = _norm_mod(x_ref[0, rows, :], ng_ref[...], sh_ref[0], sc_ref[0])
        p = _dot(h.astype(BF16), w_ref[...])
        qc = _group_norm(p[:, 0:512], e, g[0:1, 0:512], inv_n)
        kc = _group_norm(p[:, 512:768], e, g[1:2, 0:256], inv_n)
        cq = _rms(p[:, 1024:1280], g[2:3, 0:256])
        ckv = _rms(p[:, 1280:1408], g[3:4, 0:128])
        krc = p[:, 1408:1536]
        qd = _dot(cq.astype(BF16), wuq_ref[...])
        kvv = _dot(ckv.astype(BF16), wukv_ref[...])
        kd = kvv[:, 0:1024] + jnp.concatenate([krc] * MLA_HEADS, axis=1)
        qd = _group_norm(qd, e2, g[4:5], 1.0 / MLA_QK)
        kd = _group_norm(kd, e2, g[5:6], 1.0 / MLA_QK)
        if use_rope:
            lane = lax.broadcasted_iota(jnp.int32, (1, LANES), 1)
            cos, sin = cos_ref[rows, :], sin_ref[rows, :]
            qc = _rope_chunks(qc, cos, sin, (lane & 16) != 0, 16)
            kc = _rope_chunks(kc, cos, sin, (lane & 16) != 0, 16)
            cosm, sinm = cosm_ref[rows, :], sinm_ref[rows, :]
            qd = _rope_chunks(qd, cosm, sinm, (lane & 8) != 0, 8)
            kd = _rope_chunks(kd, cosm, sinm, (lane & 8) != 0, 8)
        qc_ref[0, rows, :] = qc.astype(BF16)
        kc_ref[0, rows, :] = kc.astype(BF16)
        vc_ref[0, :, rows] = p[:, 768:1024].T.astype(BF16)
        qd_ref[0, rows, :] = qd.astype(BF16)
        kd_ref[0, rows, :] = kd.astype(BF16)
        vd_ref[0, :, rows] = kvv[:, 1024:1536].T.astype(BF16)


def _odd_proj(x, shift, scale, ng, w, wuq, wukv, e64, e128, gains, cos, sin, cosm, sinm, *, use_rope, tm):
    b, s, d = x.shape
    row = lambda bi, i: (bi, i, 0)
    vec = lambda bi, i: (bi, 0, 0)
    const = lambda bi, i: (0, 0)
    tab = lambda bi, i: (i, 0)
    widths = (512, 256, 256, 1024, 1024, 512)
    v_outs = (2, 5)
    return pl.pallas_call(
        functools.partial(_odd_proj_kernel, use_rope=use_rope),
        out_shape=tuple(jax.ShapeDtypeStruct((b, n, s) if i in v_outs else (b, s, n), BF16)
                        for i, n in enumerate(widths)),
        grid=(b, s // tm),
        in_specs=[
            pl.BlockSpec((1, tm, d), row),
            pl.BlockSpec((1, 1, d), vec),
            pl.BlockSpec((1, 1, d), vec),
            pl.BlockSpec((1, d), const),
            pl.BlockSpec(w.shape, const),
            pl.BlockSpec(wuq.shape, const),
            pl.BlockSpec(wukv.shape, const),
            pl.BlockSpec(e64.shape, const),
            pl.BlockSpec(e128.shape, const),
            pl.BlockSpec(gains.shape, const),
            pl.BlockSpec((tm, LANES), tab),
            pl.BlockSpec((tm, LANES), tab),
            pl.BlockSpec((tm, LANES), tab),
            pl.BlockSpec((tm, LANES), tab),
        ],
        out_specs=tuple(pl.BlockSpec((1, n, tm), lambda bi, i: (bi, 0, i)) if j in v_outs
                        else pl.BlockSpec((1, tm, n), row) for j, n in enumerate(widths)),
        compiler_params=_cparams(("parallel", "parallel")),
        name="odd_proj",
    )(x, shift, scale, ng, w, wuq, wukv, e64, e128, gains, cos, sin, cosm, sinm)


def _mla_rope_tables(seq):
    cos, sin = _rope_tables(seq, MLA_ROPE, MLA_ROPE // 4)
    lane = jnp.arange(LANES)
    on = (lane >= MLA_NOPE) & (lane < MLA_QK)
    return jnp.where(on[None, :], cos, 1.0), jnp.where(on[None, :], sin, 0.0)


def _router_logits(h, rt_ref):
    h_hi = h.astype(BF16)
    h_lo = (h - h_hi.astype(F32)).astype(BF16)
    both = _dot(h_hi, rt_ref[...]) + _dot(h_lo, rt_ref[...])
    return both[:, :LANES] + both[:, LANES:]


def _route_tile(logits_rows, blk, u_ref, ei_ref, gt_ref, rk_ref, before0):
    logits = logits_rows.T[:N_EXPERTS]
    eidx = lax.broadcasted_iota(jnp.int32, logits.shape, 0)
    m1 = jnp.max(logits, axis=0, keepdims=True)
    i1 = jnp.min(jnp.where(logits == m1, eidx, N_EXPERTS), axis=0, keepdims=True)
    rest = jnp.where(eidx == i1, -jnp.inf, logits)
    m2 = jnp.max(rest, axis=0, keepdims=True)
    i2 = jnp.min(jnp.where(rest == m2, eidx, N_EXPERTS), axis=0, keepdims=True)
    e2 = jnp.exp(m2 - m1)
    g1 = 1.0 / (1.0 + e2)
    sel1 = eidx == i1
    sel2 = eidx == i2
    onehot = jnp.where(sel1 | sel2, 1.0, 0.0)
    pieces = []
    for c in range(logits.shape[1] // blk):
        oh = onehot[:, c * blk:(c + 1) * blk]
        pieces.append(_dot(oh.astype(BF16), u_ref[...]) + before0)
        before0 = before0 + jnp.sum(oh, axis=1, keepdims=True)
    before = jnp.concatenate(pieces, axis=1)
    r1 = jnp.sum(jnp.where(sel1, before, 0.0), axis=0, keepdims=True)
    r2 = jnp.sum(jnp.where(sel2, before, 0.0), axis=0, keepdims=True)
    ei_ref[...] = jnp.concatenate([i1, i2], axis=0)
    gt_ref[...] = jnp.concatenate([g1, e2 * g1], axis=0)
    rk_ref[...] = jnp.concatenate([r1, r2], axis=0).astype(jnp.int32)
    return before0


def _post_attn_router_kernel(oa_ref, ob_ref, wa_ref, wb_ref, x_ref, g_ref, ng_ref, sh_ref, sc_ref, rt_ref, u_ref,
                             x1_ref, h2_ref, ei_ref, gt_ref, rk_ref, cnt_ref, carry_ref):
    @pl.when((pl.program_id(0) == 0) & (pl.program_id(1) == 0))
    def _():
        carry_ref[...] = jnp.zeros_like(carry_ref)

    logits = []
    for rows in _row_blocks(x_ref.shape[1]):
        y = _dot(oa_ref[0, rows, :], wa_ref[...]) + _dot(ob_ref[0, rows, :], wb_ref[...])
        x1 = x_ref[0, rows, :] + g_ref[0] * y
        x1_ref[0, rows, :] = x1
        h = _norm_mod(x1, ng_ref[...], sh_ref[0], sc_ref[0])
        h2_ref[0, rows, :] = h.astype(BF16)
        logits.append(_router_logits(h, rt_ref))
    routed = _route_tile(jnp.concatenate(logits, axis=0), u_ref.shape[0], u_ref, ei_ref, gt_ref, rk_ref,
                         carry_ref[:, 0:1])
    carry_ref[...] = jnp.broadcast_to(routed, carry_ref.shape)
    cnt_ref[...] = jnp.broadcast_to(routed, cnt_ref.shape).astype(jnp.int32)


def _post_attn_router(oa, ob, wa, wb, x, gate, ng, shift, scale, router_t, *, tm):
    b, s, d = x.shape
    n = b * s
    nt = s // tm
    blk = _row_blocks(tm)[0].stop
    tri = (jnp.arange(blk)[:, None] < jnp.arange(blk)[None, :]).astype(BF16)
    row = lambda bi, i: (bi, i, 0)
    vec = lambda bi, i: (bi, 0, 0)
    const = lambda bi, i: (0, 0)
    flat = lambda bi, i: (0, bi * nt + i)
    return pl.pallas_call(
        _post_attn_router_kernel,
        out_shape=(jax.ShapeDtypeStruct((b, s, d), F32), jax.ShapeDtypeStruct((b, s, d), BF16),
                   jax.ShapeDtypeStruct((2, n), jnp.int32), jax.ShapeDtypeStruct((2, n), F32),
                   jax.ShapeDtypeStruct((2, n), jnp.int32), jax.ShapeDtypeStruct((N_EXPERTS, LANES), jnp.int32)),
        grid=(b, nt),
        in_specs=[
            pl.BlockSpec((1, tm, 512), row),
            pl.BlockSpec((1, tm, 512), row),
            pl.BlockSpec((512, d), const),
            pl.BlockSpec((512, d), const),
            pl.BlockSpec((1, tm, d), row),
            pl.BlockSpec((1, 1, d), vec),
            pl.BlockSpec((1, d), const),
            pl.BlockSpec((1, 1, d), vec),
            pl.BlockSpec((1, 1, d), vec),
            pl.BlockSpec((d, 2 * LANES), const),
            pl.BlockSpec((blk, blk), const),
        ],
        out_specs=(pl.BlockSpec((1, tm, d), row), pl.BlockSpec((1, tm, d), row),
                   pl.BlockSpec((2, tm), flat), pl.BlockSpec((2, tm), flat), pl.BlockSpec((2, tm), flat),
                   pl.BlockSpec((N_EXPERTS, LANES), const)),
        scratch_shapes=[pltpu.VMEM((N_EXPERTS, LANES), F32)],
        compiler_params=_cparams(("arbitrary", "arbitrary")),
        name="post_attn_router",
    )(oa, ob, wa, wb, x, gate, ng, shift, scale, router_t, tri)


MOE_SUB = 256
MOE_BLK = 1024
MOE_TC = 256
MOE_GROUP = 4
MOE_NBUF = 16
MOE_AHEAD = MOE_NBUF - MOE_GROUP


def _lookup(table, idx):
    hit = idx[..., None] == jnp.arange(table.shape[0])
    return jnp.sum(jnp.where(hit, table, 0), axis=-1)


def _lookup_cols(rows, idx):
    hit = idx[:, None] == jnp.arange(rows.shape[1])[None, :]
    return jnp.sum(jnp.where(hit, rows, 0), axis=1)


def _moe_plan(ei, rk, counts, n_tok):
    n_chunks = n_tok // MOE_TC
    cap = 2 * n_tok + N_EXPERTS * MOE_BLK
    nb_sub = cap // MOE_SUB
    nb_blk = cap // MOE_BLK
    padded = ((counts + MOE_BLK - 1) // MOE_BLK) * MOE_BLK
    pad_end = jnp.cumsum(padded)
    pad_start = pad_end - padded
    dest = _lookup(pad_start, ei) + rk

    onehot = ei.reshape(2, n_chunks, MOE_TC)[..., None] == jnp.arange(N_EXPERTS)
    cnt = jnp.sum(onehot, axis=(0, 2)).astype(jnp.int32)
    cum = jnp.concatenate([jnp.zeros((1, N_EXPERTS), jnp.int32), jnp.cumsum(cnt, axis=0)], axis=0)

    lo = pad_start[None, :] + cum[:-1]
    has = cnt > 0
    blk0 = lo // MOE_SUB
    two = has & ((lo + cnt - 1) // MOE_SUB > blk0)
    blk1 = jnp.minimum(blk0 + 1, nb_sub - 1)
    win_ok = jnp.stack([has, two], axis=-1).reshape(n_chunks, 2 * N_EXPERTS)
    win_blk = jnp.stack([blk0, blk1], axis=-1).reshape(n_chunks, 2 * N_EXPERTS)
    c_count = jnp.sum(win_ok, axis=1).astype(jnp.int32)
    c_start = (jnp.cumsum(c_count) - c_count).astype(jnp.int32)
    win_pos = jnp.cumsum(win_ok, axis=1) - win_ok
    front = win_ok[:, :, None] & (win_pos[:, :, None] == jnp.arange(2 * N_EXPERTS)[None, None, :])
    packed = jnp.sum(jnp.where(front, win_blk[:, :, None], 0), axis=1)
    q = jnp.arange(n_chunks * 2 * N_EXPERTS)
    c_of_q = jnp.minimum(jnp.sum((c_start + c_count)[None, :] <= q[:, None], axis=1), n_chunks - 1)
    row_q = jnp.dot((c_of_q[:, None] == jnp.arange(n_chunks)[None, :]).astype(F32), packed.astype(F32),
                    precision=HIGHEST)
    p_of_q = q - _lookup(c_start, c_of_q)
    c_blocks = _lookup_cols(row_q, p_of_q).astype(jnp.int32)

    sb = jnp.arange(nb_sub)
    e_sb = jnp.minimum(jnp.sum(sb[:, None] * MOE_SUB >= pad_end[None, :], axis=1), N_EXPERTS - 1)
    counts_sb = _lookup(counts, e_sb)
    r0 = sb * MOE_SUB - _lookup(pad_start, e_sb)
    valid_sb = (sb * MOE_SUB < pad_end[-1]) & (r0 < counts_sb)
    r1 = jnp.minimum(r0 + MOE_SUB, counts_sb) - 1
    hit_sb = e_sb[:, None] == jnp.arange(N_EXPERTS)[None, :]
    cum_sb = jnp.sum(jnp.where(hit_sb[:, None, :], cum[None, 1:, :], 0), axis=-1)
    cmin = jnp.sum(cum_sb <= r0[:, None], axis=1)
    cmax = jnp.sum(cum_sb <= r1[:, None], axis=1)
    items = jnp.where(valid_sb, cmax - cmin + 1, 0)
    off_end = jnp.cumsum(items)
    off = off_end - items
    total = off_end[-1]
    w_max = (2 * n_tok) // MOE_SUB + N_EXPERTS + N_EXPERTS * (n_chunks - 1)
    w = jnp.arange(w_max)
    wv = w < total
    wq = jnp.minimum(w, total - 1)
    w_sb = jnp.minimum(jnp.sum(off_end[None, :] <= wq[:, None], axis=1), nb_sub - 1)
    off_w = _lookup(off, w_sb)
    w_chunk = _lookup(cmin, w_sb) + (wq - off_w)
    w_flag = wv.astype(jnp.int32) + 2 * (wq == off_w).astype(jnp.int32)

    bi = jnp.arange(nb_blk)
    e_blk = jnp.minimum(jnp.sum(bi[:, None] * MOE_BLK >= pad_end[None, :], axis=1), N_EXPERTS - 1)
    rows = jnp.where(bi * MOE_BLK < pad_end[-1],
                     _lookup(counts, e_blk) - (bi * MOE_BLK - _lookup(pad_start, e_blk)), 0)
    n_sub = jnp.clip((rows + MOE_SUB - 1) // MOE_SUB, 0, MOE_BLK // MOE_SUB)
    g_count = jnp.sum(items.reshape(nb_blk, MOE_BLK // MOE_SUB), axis=1)
    g_start = jnp.cumsum(g_count) - g_count
    return dict(dest=dest.astype(jnp.int32), cap=cap,
                w_sb=w_sb.astype(jnp.int32), w_chunk=w_chunk.astype(jnp.int32), w_flag=w_flag,
                g_start=g_start.astype(jnp.int32), g_count=g_count.astype(jnp.int32),
                e_blk=e_blk.astype(jnp.int32), n_sub=n_sub.astype(jnp.int32),
                c_blocks=c_blocks, c_count=c_count, c_start=c_start)


def _moe_gather_kernel(st_ref, cn_ref, wsb_ref, wch_ref, t_hbm, d_ref, g_ref, xg_ref, gs_ref, tbuf, sem):
    i = pl.program_id(0)
    n = cn_ref[i]
    s0 = st_ref[i]
    total = st_ref[pl.num_programs(0) - 1] + cn_ref[pl.num_programs(0) - 1]
    subs = MOE_BLK // MOE_SUB

    def chunk_copy(item):
        slot = lax.rem(item, MOE_NBUF)
        start = pl.multiple_of(wch_ref[item] * MOE_TC, MOE_TC)
        return pltpu.make_async_copy(t_hbm.at[pl.ds(start, MOE_TC), :], tbuf.at[slot], sem.at[slot])

    @pl.when(i == 0)
    def _():
        for a in range(MOE_AHEAD):
            @pl.when(a < total)
            def _():
                chunk_copy(a).start()

    xg_ref[...] = jnp.zeros(xg_ref.shape, BF16)
    gs_ref[...] = jnp.zeros(gs_ref.shape, F32)

    def arrive(item):
        chunk_copy(item).wait()

        @pl.when(item + MOE_AHEAD < total)
        def _():
            chunk_copy(item + MOE_AHEAD).start()

    def contribute(item):
        sb = wsb_ref[item]
        chunk = wch_ref[item]
        d = d_ref[chunk]
        g = g_ref[chunk]
        srow = lax.broadcasted_iota(jnp.int32, (MOE_SUB, MOE_TC), 0) + sb * MOE_SUB
        hit0 = srow == d[0:1]
        hit1 = srow == d[1:2]
        p = jnp.where(hit0 | hit1, 1.0, 0.0).astype(BF16)
        rows = _dot(p, tbuf[lax.rem(item, MOE_NBUF)]).astype(BF16)
        gate = jnp.sum(jnp.where(hit0, g[0:1], 0.0) + jnp.where(hit1, g[1:2], 0.0), axis=-1, keepdims=True)
        r0 = pl.multiple_of((sb - i * subs) * MOE_SUB, MOE_SUB)
        xg_ref[pl.ds(r0, MOE_SUB), :] += rows
        gs_ref[pl.ds(r0, MOE_SUB), :] += gate

    def group(k, carry):
        item = s0 + MOE_GROUP * k
        for a in range(MOE_GROUP):
            arrive(item + a)
        for a in range(MOE_GROUP):
            contribute(item + a)
        return carry

    def single(k, carry):
        arrive(s0 + grouped + k)
        contribute(s0 + grouped + k)
        return carry

    grouped = (n // MOE_GROUP) * MOE_GROUP
    lax.fori_loop(0, n // MOE_GROUP, group, 0)
    lax.fori_loop(0, n - grouped, single, 0)


def _moe_gather(t, dest, gates, plan):
    n_tok, d = t.shape
    cap = plan["cap"]
    n_chunks = n_tok // MOE_TC
    by_chunk = lambda a: a.reshape(2, n_chunks, MOE_TC).transpose(1, 0, 2)
    grid_spec = pltpu.PrefetchScalarGridSpec(
        num_scalar_prefetch=4,
        grid=(cap // MOE_BLK,),
        in_specs=[
            pl.BlockSpec(memory_space=pl.ANY),
            pl.BlockSpec((n_chunks, 2, MOE_TC), lambda i, *_: (0, 0, 0)),
            pl.BlockSpec((n_chunks, 2, MOE_TC), lambda i, *_: (0, 0, 0)),
        ],
        out_specs=(pl.BlockSpec((MOE_BLK, d), lambda i, *_: (i, 0)),
                   pl.BlockSpec((MOE_BLK, 1), lambda i, *_: (i, 0))),
        scratch_shapes=[pltpu.VMEM((MOE_NBUF, MOE_TC, d), BF16), pltpu.SemaphoreType.DMA((MOE_NBUF,))],
    )
    return pl.pallas_call(
        _moe_gather_kernel,
        out_shape=(jax.ShapeDtypeStruct((cap, d), BF16), jax.ShapeDtypeStruct((cap, 1), F32)),
        grid_spec=grid_spec,
        compiler_params=_cparams(("arbitrary",)),
        name="moe_gather",
    )(plan["g_start"], plan["g_count"], plan["w_sb"], plan["w_chunk"], t, by_chunk(dest), by_chunk(gates))


def _moe_expert_kernel(eb_ref, ns_ref, x_ref, gs_ref, w1_ref, w3_ref, w2_ref, y_ref):
    n_sub = ns_ref[pl.program_id(0)]
    subs = MOE_BLK // MOE_SUB

    def mlp(rows):
        xs = x_ref[rows, :]
        acc = jnp.zeros((rows.stop - rows.start, y_ref.shape[1]), F32)
        for f in range(w1_ref.shape[1] // MOE_TF):
            cols = slice(f * MOE_TF, (f + 1) * MOE_TF)
            a = _dot(xs, w1_ref[:, cols])
            u = _dot(xs, w3_ref[:, cols])
            acc = acc + _dot((_silu(a) * u).astype(BF16), w2_ref[cols, :])
        y_ref[rows, :] = (acc * gs_ref[rows, :]).astype(BF16)

    @pl.when(n_sub == subs)
    def _():
        mlp(slice(0, MOE_BLK))

    for sub in range(subs):
        rows = slice(sub * MOE_SUB, (sub + 1) * MOE_SUB)

        @pl.when((n_sub < subs) & (sub < n_sub))
        def _():
            mlp(rows)

        @pl.when(sub >= n_sub)
        def _():
            y_ref[rows, :] = jnp.zeros((MOE_SUB, y_ref.shape[1]), BF16)


def _moe_experts(xg, gs, w1, w3, w2, plan):
    cap, d = xg.shape
    f = w1.shape[2]
    once = pl.Buffered(1)
    grid_spec = pltpu.PrefetchScalarGridSpec(
        num_scalar_prefetch=2,
        grid=(cap // MOE_BLK,),
        in_specs=[
            pl.BlockSpec((MOE_BLK, d), lambda i, eb, ns: (i, 0)),
            pl.BlockSpec((MOE_BLK, 1), lambda i, eb, ns: (i, 0)),
            pl.BlockSpec((None, d, f), lambda i, eb, ns: (eb[i], 0, 0), pipeline_mode=once),
            pl.BlockSpec((None, d, f), lambda i, eb, ns: (eb[i], 0, 0), pipeline_mode=once),
            pl.BlockSpec((None, f, d), lambda i, eb, ns: (eb[i], 0, 0), pipeline_mode=once),
        ],
        out_specs=pl.BlockSpec((MOE_BLK, d), lambda i, eb, ns: (i, 0)),
    )
    return pl.pallas_call(
        _moe_expert_kernel,
        out_shape=jax.ShapeDtypeStruct((cap, d), BF16),
        grid_spec=grid_spec,
        compiler_params=_cparams(("arbitrary",)),
        name="moe_experts",
    )(plan["e_blk"], plan["n_sub"], xg, gs, w1, w3, w2)


MOE_WIN = 2 * N_EXPERTS


def _moe_combine_kernel(st_ref, cn_ref, bl_ref, dt_ref, y_hbm, x1_ref, g_ref, o_ref, ybuf, sem):
    c = pl.program_id(0)
    n = cn_ref[c]
    s0 = st_ref[c]
    total = st_ref[pl.num_programs(0) - 1] + cn_ref[pl.num_programs(0) - 1]
    d = dt_ref[...]

    def block_copy(item):
        slot = lax.rem(item, MOE_NBUF)
        start = pl.multiple_of(bl_ref[item] * MOE_SUB, MOE_SUB)
        return pltpu.make_async_copy(y_hbm.at[pl.ds(start, MOE_SUB), :], ybuf.at[slot], sem.at[slot])

    @pl.when(c == 0)
    def _():
        for a in range(MOE_AHEAD):
            @pl.when(a < total)
            def _():
                block_copy(a).start()

    def arrive(item):
        block_copy(item).wait()

        @pl.when(item + MOE_AHEAD < total)
        def _():
            block_copy(item + MOE_AHEAD).start()

    def rows_of(item):
        scol = lax.broadcasted_iota(jnp.int32, (MOE_TC, MOE_SUB), 1) + bl_ref[item] * MOE_SUB
        p = jnp.where((scol == d[:, 0:1]) | (scol == d[:, 1:2]), 1.0, 0.0).astype(BF16)
        return _dot(p, ybuf[lax.rem(item, MOE_NBUF)])

    def group(k, acc):
        item = s0 + MOE_GROUP * k
        for a in range(MOE_GROUP):
            arrive(item + a)
        for a in range(MOE_GROUP):
            acc = acc + rows_of(item + a)
        return acc

    def single(k, acc):
        arrive(s0 + grouped + k)
        return acc + rows_of(s0 + grouped + k)

    grouped = (n // MOE_GROUP) * MOE_GROUP
    acc = lax.fori_loop(0, n // MOE_GROUP, group, jnp.zeros(o_ref.shape, F32))
    acc = lax.fori_loop(0, n - grouped, single, acc)
    o_ref[...] = x1_ref[...] + g_ref[0] * acc


def _moe_combine(y, dest_t, x1, gate, plan, seq):
    n_tok, d = x1.shape
    n_chunks = n_tok // MOE_TC
    per_b = seq // MOE_TC
    grid_spec = pltpu.PrefetchScalarGridSpec(
        num_scalar_prefetch=3,
        grid=(n_chunks,),
        in_specs=[
            pl.BlockSpec((MOE_TC, 2), lambda c, *_: (c, 0)),
            pl.BlockSpec(memory_space=pl.ANY),
            pl.BlockSpec((MOE_TC, d), lambda c, *_: (c, 0)),
            pl.BlockSpec((1, 1, d), lambda c, *_: (c // per_b, 0, 0)),
        ],
        out_specs=pl.BlockSpec((MOE_TC, d), lambda c, *_: (c, 0)),
        scratch_shapes=[pltpu.VMEM((MOE_NBUF, MOE_SUB, d), BF16), pltpu.SemaphoreType.DMA((MOE_NBUF,))],
    )
    return pl.pallas_call(
        _moe_combine_kernel,
        out_shape=jax.ShapeDtypeStruct((n_tok, d), F32),
        grid_spec=grid_spec,
        compiler_params=_cparams(("arbitrary",)),
        name="moe_combine",
    )(plan["c_start"], plan["c_count"], plan["c_blocks"], dest_t, y, x1, gate)


def _split_mod(mod_l, batch):
    d = D_MODEL
    lat = tuple(mod_l[:batch, k * d:(k + 1) * d][:, None, :] for k in range(ADA_CHUNKS))
    ctx = tuple(jnp.broadcast_to(mod_l[batch:batch + 1, k * d:(k + 1) * d][:, None, :], (batch, 1, d))
                for k in range(ADA_CHUNKS))
    return lat, ctx


def _score_bound(q_gain, k_gain, n):
    return 1.02 * n * jnp.max(jnp.abs(q_gain)) * jnp.max(jnp.abs(k_gain))


def _block_ones(n, block):
    idx = jnp.arange(n) // block
    return (idx[:, None] == idx[None, :]).astype(BF16)


def _even_layer(x, xc, mod_l, norm1_g, norm2_g, w_in, w_out, na_q_g, na_k_g, na_rpb, diff_q_g, diff_k_g,
                lq1, lk1, lq2, lk2, subln_g, wg, wu, wd, layer_idx, need_ctx):
    b, s, d = x.shape
    (sh1, sc1, g1, sh2, sc2, g2), (csh1, csc1, cg1, csh2, csc2, cg2) = _split_mod(mod_l, b)
    qscale = HEAD_DIM ** -0.5 * LOG2E
    lam_init = 0.8 - 0.6 * math.exp(-0.3 * layer_idx)
    hg = jnp.stack([jnp.tile(na_q_g, 8) * qscale, jnp.tile(na_k_g, 8),
                    jnp.tile(diff_q_g, 8) * qscale, jnp.tile(diff_k_g, 8)]).astype(F32)
    e64 = _block_ones(MXU_DIM, HEAD_DIM)
    cos, sin = _rope_tables(s, HEAD_DIM, 16)
    ng1 = norm1_g[None, :]
    ng2 = norm2_g[None, :]
    w_in_b = w_in.astype(BF16)
    wo_a = w_out[:512].astype(BF16)
    wo_b = w_out[512:].astype(BF16)
    wg_b, wu_b, wd_b = wg.astype(BF16), wu.astype(BF16), wd.astype(BF16)
    lam_p = jnp.stack([lq1, lk1, lq2, lk2]).astype(F32)
    sg = subln_g[None, :].astype(F32)
    ident = lambda j: j

    tm = min(ROW_TM, s)
    tc = xc.shape[1]
    qa, ka, va, qb, kb, vb = _even_proj(x, sh1, sc1, ng1, w_in_b, e64, hg, cos, sin, use_rope=True, tm=tm)
    qca, kca, vca, qcb, kcb, vcb = _even_proj(xc, csh1, csc1, ng1, w_in_b, e64, hg, cos, sin, use_rope=False, tm=tc)
    out_a = _na_attention(qa, ka, va, kca, vca, _na_bias_table(na_rpb), _score_bound(hg[0], hg[1], HEAD_DIM))
    out_b = _flash(qb, kb, vb, kcb, vcb, mode="diff", kv_map=ident, n_chunks=4, tq=FLASH_TQ, tk=FLASH_TK,
                   extra=(lam_p, sg), lam_init=lam_init, score_bound=_score_bound(hg[2], hg[3], HEAD_DIM))
    x2 = _post_ffn(out_a, out_b, wo_a, wo_b, x, g1, ng2, sh2, sc2, g2, wg_b, wu_b, wd_b, tm=tm)
    if not need_ctx:
        return x2, None
    oca = _flash_call(qca, None, None, kca, vca, mode="pair", online=True, kv_map=ident, n_chunks=4, tq=tc, tk=tc)
    ocb = _flash(qcb, None, None, kcb, vcb, mode="diff", kv_map=ident, n_chunks=4, tq=tc, tk=tc,
                 extra=(lam_p, sg), lam_init=lam_init)
    xc2 = _post_ffn(oca, ocb, wo_a, wo_b, xc, cg1, ng2, csh2, csc2, cg2, wg_b, wu_b, wd_b, tm=tc)
    return x2, xc2


def _odd_layer(x, xc, mod_l, norm1_g, norm2_g, w_in, w_out, gqa_q_g, gqa_k_g, cq_g, w_uq, ckv_g, w_ukv,
               mla_q_g, mla_k_g, router, w1, w3, w2):
    b, s, d = x.shape
    n_tok = b * s
    (sh1, sc1, g1, sh2, sc2, g2), (csh1, csc1, _, _, _, _) = _split_mod(mod_l, b)
    ng1 = norm1_g[None, :]
    ng2 = norm2_g[None, :]

    z = lambda n: jnp.zeros((d, n), w_in.dtype)
    k0, k1, v0, v1 = (w_in[:, 512 + 64 * i:576 + 64 * i] for i in range(4))
    w_p = jnp.concatenate([w_in[:, 0:512], k0, k0, k1, k1, v0, v0, v1, v1, w_in[:, 768:1024], w_in[:, 1024:1152],
                           z(MLA_NOPE), w_in[:, 1152:1184], z(LANES - MLA_QK)], axis=1).astype(BF16)
    wuq_p = jnp.pad(w_uq.reshape(MLA_Q_RANK, MLA_HEADS, MLA_QK), ((0, 0), (0, 0), (0, LANES - MLA_QK)))
    wuq_p = wuq_p.reshape(MLA_Q_RANK, MLA_HEADS * LANES).astype(BF16)
    ukv = w_ukv.reshape(MLA_KV_RANK, MLA_HEADS, MLA_NOPE + MLA_V)
    uk = jnp.pad(ukv[:, :, :MLA_NOPE], ((0, 0), (0, 0), (0, LANES - MLA_NOPE))).reshape(MLA_KV_RANK, -1)
    uv = ukv[:, :, MLA_NOPE:].reshape(MLA_KV_RANK, -1)
    wukv_p = jnp.concatenate([uk, uv], axis=1).astype(BF16)
    e64 = _block_ones(MXU_DIM, HEAD_DIM)
    e128 = _block_ones(MXU_DIM, LANES)
    qscale = HEAD_DIM ** -0.5 * LOG2E
    mscale = MLA_QK ** -0.5 * LOG2E
    pad_row = lambda v: jnp.pad(v, (0, MLA_HEADS * LANES - v.shape[0]))
    pad_head = lambda v: jnp.tile(jnp.pad(v, (0, LANES - MLA_QK)), MLA_HEADS)
    gains = jnp.stack([pad_row(jnp.tile(gqa_q_g, 8) * qscale), pad_row(jnp.tile(gqa_k_g, 4)),
                       pad_row(cq_g), pad_row(ckv_g), pad_head(mla_q_g) * mscale, pad_head(mla_k_g),
                       jnp.zeros((MLA_HEADS * LANES,), F32), jnp.zeros((MLA_HEADS * LANES,), F32)]).astype(F32)
    cos, sin = _rope_tables(s, HEAD_DIM, 16)
    cosm, sinm = _mla_rope_tables(s)
    wo_a = w_out[:512].astype(BF16)
    wo_b = w_out[512:].astype(BF16)

    tm = min(ROW_TM, s)
    tc = xc.shape[1]
    proj = functools.partial(_odd_proj, ng=ng1, w=w_p, wuq=wuq_p, wukv=wukv_p, e64=e64, e128=e128, gains=gains,
                             cos=cos, sin=sin, cosm=cosm, sinm=sinm)
    qc, kc, vc, qd, kd, vd = proj(x, sh1, sc1, use_rope=True, tm=tm)
    _, kcc, vcc, _, kcd, vcd = proj(xc, csh1, csc1, use_rope=False, tm=tc)
    out_c = _flash(qc, kc, vc, kcc, vcc, mode="pair", kv_map=lambda j: j // 2, n_chunks=4, tq=FLASH_TQ, tk=FLASH_TK,
                   score_bound=_score_bound(gains[0], gains[1], HEAD_DIM))
    out_d = _flash(qd, kd, vd, kcd, vcd, mode="mla", kv_map=lambda j: j, n_chunks=4, tq=FLASH_TQ, tk=FLASH_TK,
                   score_bound=_score_bound(gains[4], gains[5], MLA_QK))
    r_hi = router.astype(BF16)
    r_lo = (router - r_hi.astype(F32)).astype(BF16)
    pad_e = lambda m: jnp.pad(m, ((0, 0), (0, LANES - N_EXPERTS)))
    x1, h2, ei, gt, rk, cnt = _post_attn_router(out_c, out_d, wo_a, wo_b, x, g1, ng2, sh2, sc2,
                                                jnp.concatenate([pad_e(r_hi), pad_e(r_lo)], axis=1), tm=tm)
    plan = _moe_plan(ei, rk, cnt[:, 0], n_tok)
    xg, gs = _moe_gather(h2.reshape(n_tok, d), plan["dest"], gt, plan)
    y = _moe_experts(xg, gs, w1.astype(BF16), w3.astype(BF16), w2.astype(BF16), plan)
    out = _moe_combine(y, plan["dest"].T, x1.reshape(n_tok, d), g2, plan, s)
    return out.reshape(b, s, d)


def _mod_vectors(c, c_ctx, ada_w, ada_b):
    b = c.shape[0]
    cc = jnp.zeros((MOD_ROWS, D_MODEL), F32).at[:b].set(c).at[b].set(c_ctx)
    return _modvec(cc, ada_w, ada_b[:, None, :])


def kernel(x, c, ctx, c_ctx, ada_w, ada_b, norm1_g, norm2_g, ev_w_in, ev_w_out, na_q_g, na_k_g, na_rpb,
           diff_q_g, diff_k_g, diff_lq1, diff_lk1, diff_lq2, diff_lk2, diff_subln_g,
           ffn_w_gate, ffn_w_up, ffn_w_down, od_w_in, od_w_out, gqa_q_g, gqa_k_g, mla_cq_g, mla_w_uq,
           mla_ckv_g, mla_w_ukv, mla_q_g, mla_k_g, moe_router, moe_w1, moe_w3, moe_w2):
    mod = _mod_vectors(c, c_ctx, ada_w, ada_b)
    x, xc = _even_layer(x, ctx, mod[0], norm1_g[0], norm2_g[0], ev_w_in[0], ev_w_out[0], na_q_g[0], na_k_g[0],
                        na_rpb[0], diff_q_g[0], diff_k_g[0], diff_lq1[0], diff_lk1[0], diff_lq2[0], diff_lk2[0],
                        diff_subln_g[0], ffn_w_gate[0], ffn_w_up[0], ffn_w_down[0], 0, True)
    return _odd_layer(x, xc, mod[1], norm1_g[1], norm2_g[1], od_w_in[0], od_w_out[0], gqa_q_g[0], gqa_k_g[0],
                      mla_cq_g[0], mla_w_uq[0], mla_ckv_g[0], mla_w_ukv[0], mla_q_g[0], mla_k_g[0],
                      moe_router[0], moe_w1[0], moe_w3[0], moe_w2[0])
```

```python
import functools
import math

import jax
import jax.numpy as jnp
from jax import lax
from jax.experimental import pallas as pl
from jax.experimental.pallas import tpu as pltpu

F32 = jnp.float32
BF16 = jnp.bfloat16
HIGHEST = lax.Precision.HIGHEST

D_MODEL = 1024
GRID_W = 64
HEAD_DIM = 64
ROPE_THETA = 10000.0
NORM_EPS = 1e-6
NEG_INF = -1e30
ADA_CHUNKS = 6
LOG2E = 1.4426950408889634

NA_HEADS = 8
NA_WIN_H = 8
NA_WIN_W = 16
DIFF_HEADS = 4
DIFF_V_DIM = 2 * HEAD_DIM
GQA_Q_HEADS = 8
GQA_KV_HEADS = 2
MLA_HEADS = 8
MLA_NOPE = 64
MLA_ROPE = 32
MLA_QK = MLA_NOPE + MLA_ROPE
MLA_V = 64
MLA_Q_RANK = 256
MLA_KV_RANK = 128
D_FF = 2816
N_EXPERTS = 8
D_FF_EXPERT = 3584

LANES = 128
MXU_DIM = 256
VMEM_LIMIT = 56 * 1024 * 1024
MOD_ROWS = 16
MOE_TF = 512
SUM_ROWS = 16
ROW_TM = 1024
FLASH_TQ = 1024
FLASH_TK = 512
EXP2_SAFE_RANGE = 64.0


def _cparams(sem):
    return pltpu.CompilerParams(dimension_semantics=sem, vmem_limit_bytes=VMEM_LIMIT)


def _dot(a, b):
    return jnp.dot(a, b, preferred_element_type=F32)


def _dot_nt(a, b):
    return lax.dot_general(a, b, (((1,), (1,)), ((), ())), preferred_element_type=F32)


def _silu(x):
    return x * (1.0 / (1.0 + jnp.exp(-x)))


def _modvec_kernel(c_ref, w_ref, b_ref, o_ref):
    s = _silu(c_ref[...])
    o_ref[...] = jnp.dot(s, w_ref[...], preferred_element_type=F32, precision=HIGHEST) + b_ref[...]


def _modvec(cc, ada_w, ada_b):
    n_layers, d, n = ada_w.shape
    tn = 768
    return pl.pallas_call(
        _modvec_kernel,
        out_shape=jax.ShapeDtypeStruct((n_layers, MOD_ROWS, n), F32),
        grid=(n_layers, n // tn),
        in_specs=[
            pl.BlockSpec((MOD_ROWS, d), lambda l, j: (0, 0)),
            pl.BlockSpec((None, d, tn), lambda l, j: (l, 0, j)),
            pl.BlockSpec((None, 1, tn), lambda l, j: (l, 0, j)),
        ],
        out_specs=pl.BlockSpec((None, MOD_ROWS, tn), lambda l, j: (l, 0, j)),
        compiler_params=_cparams(("arbitrary", "arbitrary")),
        name="modvec",
    )(cc, ada_w, ada_b)


def _norm_mod(x, g, shift, scale):
    ms = jnp.mean(x * x, axis=-1, keepdims=True)
    return (x * lax.rsqrt(ms + NORM_EPS) * g) * (1.0 + scale) + shift


def _row_blocks(n, size=256):
    size = min(size, n)
    return [slice(r, r + size) for r in range(0, n, size)]


def _group_norm(t, e, g, inv_n):
    outs = []
    for c in range(t.shape[1] // MXU_DIM):
        cols = slice(c * MXU_DIM, (c + 1) * MXU_DIM)
        tc = t[:, cols]
        ss = _dot((tc * tc).astype(BF16), e)
        outs.append(tc * lax.rsqrt(ss * inv_n + NORM_EPS) * g[:, cols])
    return outs[0] if len(outs) == 1 else jnp.concatenate(outs, axis=1)


def _rope_chunks(t, cos, sin, lane_hi, shift):
    outs = []
    for c in range(t.shape[1] // LANES):
        tc = t[:, c * LANES:(c + 1) * LANES]
        up = pltpu.roll(tc, LANES - shift, 1)
        dn = pltpu.roll(tc, shift, 1)
        outs.append(tc * cos + jnp.where(lane_hi, dn, up) * sin)
    return outs[0] if len(outs) == 1 else jnp.concatenate(outs, axis=1)


def _rope_tables(seq, unit, pair_shift):
    pos = jnp.arange(seq)
    rows, cols = pos // GRID_W, pos % GRID_W
    quarter = unit // 4
    assert quarter == pair_shift
    freqs = ROPE_THETA ** (-jnp.arange(quarter, dtype=F32) / quarter)
    lane = jnp.arange(LANES)
    u = lane % unit
    use_col = (u // (unit // 2)) == 1
    fi = u % quarter
    p = jnp.where(use_col[None, :], cols[:, None], rows[:, None]).astype(F32)
    ang = p * freqs[fi][None, :]
    second = ((u % (unit // 2)) // quarter) == 1
    cos = jnp.cos(ang)
    sin = jnp.where(second[None, :], jnp.sin(ang), -jnp.sin(ang))
    return cos, sin


def _even_proj_kernel(x_ref, sh_ref, sc_ref, ng_ref, w_ref, e_ref, hg_ref, cos_ref, sin_ref,
                      qa_ref, ka_ref, va_ref, qb_ref, kb_ref, vb_ref, *, use_rope):
    hg = hg_ref[...]
    e = e_ref[...]
    inv_n = 1.0 / HEAD_DIM
    for rows in _row_blocks(x_ref.shape[1]):
        h = _norm_mod(x_ref[0, rows, :], ng_ref[...], sh_ref[0], sc_ref[0])
        p = _dot(h.astype(BF16), w_ref[...])
        qa = _group_norm(p[:, 0:512], e, hg[0:1], inv_n)
        ka = _group_norm(p[:, 512:1024], e, hg[1:2], inv_n)
        qb = _group_norm(p[:, 1536:2048], e, hg[2:3], inv_n)
        kb = _group_norm(p[:, 2048:2560], e, hg[3:4], inv_n)
        if use_rope:
            lane = lax.broadcasted_iota(jnp.int32, (1, LANES), 1)
            lane_hi = (lane & 16) != 0
            cos, sin = cos_ref[rows, :], sin_ref[rows, :]
            qb = _rope_chunks(qb, cos, sin, lane_hi, 16)
            kb = _rope_chunks(kb, cos, sin, lane_hi, 16)
        qa_ref[0, rows, :] = qa.astype(BF16)
        ka_ref[0, rows, :] = ka.astype(BF16)
        va_ref[0, rows, :] = p[:, 1024:1536].astype(BF16)
        qb_ref[0, rows, :] = qb.astype(BF16)
        kb_ref[0, rows, :] = kb.astype(BF16)
        vb_ref[0, :, rows] = p[:, 2560:3072].T.astype(BF16)


def _even_proj(x, shift, scale, ng, w, e64, hg, cos, sin, *, use_rope, tm):
    b, s, d = x.shape
    n = w.shape[1]
    row = lambda bi, i: (bi, i, 0)
    vec = lambda bi, i: (bi, 0, 0)
    const = lambda bi, i: (0, 0)
    out = jax.ShapeDtypeStruct((b, s, 512), BF16)
    return pl.pallas_call(
        functools.partial(_even_proj_kernel, use_rope=use_rope),
        out_shape=(out,) * 5 + (jax.ShapeDtypeStruct((b, 512, s), BF16),),
        grid=(b, s // tm),
        in_specs=[
            pl.BlockSpec((1, tm, d), row),
            pl.BlockSpec((1, 1, d), vec),
            pl.BlockSpec((1, 1, d), vec),
            pl.BlockSpec((1, d), const),
            pl.BlockSpec((d, n), const),
            pl.BlockSpec(e64.shape, const),
            pl.BlockSpec((4, 512), const),
            pl.BlockSpec((tm, LANES), lambda bi, i: (i, 0)),
            pl.BlockSpec((tm, LANES), lambda bi, i: (i, 0)),
        ],
        out_specs=(pl.BlockSpec((1, tm, 512), row),) * 5 + (pl.BlockSpec((1, 512, tm), lambda bi, i: (bi, 0, i)),),
        compiler_params=_cparams(("parallel", "parallel")),
        name="even_proj",
    )(x, shift, scale, ng, w, e64, hg, cos, sin)


def _pair_stack(q):
    lane = lax.broadcasted_iota(jnp.int32, q.shape, 1)
    zero = jnp.zeros_like(q)
    return jnp.concatenate([jnp.where(lane < HEAD_DIM, q, zero), jnp.where(lane >= HEAD_DIM, q, zero)], axis=0)


def _pair_merge(o, tq):
    lane = lax.broadcasted_iota(jnp.int32, (tq, LANES), 1)
    return jnp.where(lane < HEAD_DIM, o[:tq], o[tq:])


def _with_ones(v):
    return jnp.concatenate([v, jnp.ones(v.shape, BF16)], axis=1)


def _na_kernel(q_ref, k_ref, v_ref, kc_ref, vc_ref, bias_ref, o_ref, *, rows_per_step, n_rows, direct):
    rb = pl.program_id(2)
    kc = kc_ref[0]
    vc = vc_ref[0]
    n_lat = NA_WIN_H * GRID_W
    if direct:
        nq = rows_per_step * GRID_W
        qs_all = _pair_stack(q_ref[0])
        o_ctx = _dot(jnp.exp2(_dot_nt(qs_all, kc)).astype(BF16), _with_ones(vc))
        for i in range(rows_per_step):
            r = rb * rows_per_step + i
            r0 = jnp.clip(r - NA_WIN_H // 2, 0, n_rows - NA_WIN_H)
            start = pl.multiple_of(r0 * GRID_W, GRID_W)
            kw = k_ref[0, pl.ds(start, n_lat), :]
            vw = v_ref[0, pl.ds(start, n_lat), :]
            lo = slice(i * GRID_W, (i + 1) * GRID_W)
            hi = slice(nq + i * GRID_W, nq + (i + 1) * GRID_W)
            qs = jnp.concatenate([qs_all[lo], qs_all[hi]], axis=0)
            p = jnp.exp2(_dot_nt(qs, kw) + bias_ref[r - r0]).astype(BF16)
            tot = _dot(p, _with_ones(vw)) + jnp.concatenate([o_ctx[lo], o_ctx[hi]], axis=0)
            o = tot[:, :LANES] / tot[:, LANES:]
            o_ref[0, lo, :] = _pair_merge(o, GRID_W).astype(BF16)
        return
    for i in range(rows_per_step):
        r = rb * rows_per_step + i
        r0 = jnp.clip(r - NA_WIN_H // 2, 0, n_rows - NA_WIN_H)
        start = pl.multiple_of(r0 * GRID_W, GRID_W)
        kw = k_ref[0, pl.ds(start, n_lat), :]
        vw = v_ref[0, pl.ds(start, n_lat), :]
        qs = _pair_stack(q_ref[0, i * GRID_W:(i + 1) * GRID_W, :])
        s_lat = _dot_nt(qs, kw) + bias_ref[r - r0]
        s_ctx = _dot_nt(qs, kc)
        m = jnp.maximum(jnp.max(s_lat, axis=-1, keepdims=True), jnp.max(s_ctx, axis=-1, keepdims=True))
        p_lat = jnp.exp2(s_lat - m)
        p_ctx = jnp.exp2(s_ctx - m)
        l = jnp.sum(p_lat, axis=-1, keepdims=True) + jnp.sum(p_ctx, axis=-1, keepdims=True)
        o = _dot(p_lat.astype(BF16), vw) + _dot(p_ctx.astype(BF16), vc)
        o = o / l
        o_ref[0, i * GRID_W:(i + 1) * GRID_W, :] = _pair_merge(o, GRID_W).astype(BF16)


def _na_attention(q, k, v, kc, vc, bias, score_bound, *, rows_per_step=32):
    bound = score_bound + jnp.max(jnp.where(bias > 0.5 * NEG_INF, jnp.abs(bias), 0.0))
    rows_per_step = min(rows_per_step, q.shape[1] // GRID_W)
    return lax.cond(bound <= EXP2_SAFE_RANGE,
                    lambda: _na_call(q, k, v, kc, vc, bias, rows_per_step=rows_per_step, direct=True),
                    lambda: _na_call(q, k, v, kc, vc, bias, rows_per_step=rows_per_step, direct=False))


def _na_call(q, k, v, kc, vc, bias, *, rows_per_step, direct):
    b, s, _ = q.shape
    c = kc.shape[1]
    n_rows = s // GRID_W
    tq = rows_per_step * GRID_W
    return pl.pallas_call(
        functools.partial(_na_kernel, rows_per_step=rows_per_step, n_rows=n_rows, direct=direct),
        out_shape=jax.ShapeDtypeStruct((b, s, 512), BF16),
        grid=(b, 4, n_rows // rows_per_step),
        in_specs=[
            pl.BlockSpec((1, tq, LANES), lambda bi, j, i: (bi, i, j)),
            pl.BlockSpec((1, s, LANES), lambda bi, j, i: (bi, 0, j)),
            pl.BlockSpec((1, s, LANES), lambda bi, j, i: (bi, 0, j)),
            pl.BlockSpec((1, c, LANES), lambda bi, j, i: (bi, 0, j)),
            pl.BlockSpec((1, c, LANES), lambda bi, j, i: (bi, 0, j)),
            pl.BlockSpec((NA_WIN_H, None, LANES, NA_WIN_H * GRID_W), lambda bi, j, i: (0, j, 0, 0)),
        ],
        out_specs=pl.BlockSpec((1, tq, LANES), lambda bi, j, i: (bi, i, j)),
        compiler_params=_cparams(("parallel", "parallel", "parallel")),
        name="na_attention" + ("" if direct else "_online"),
    )(q, k, v, kc, vc, bias)


def _na_bias_table(rpb):
    w = GRID_W
    col = jnp.arange(w)
    c0 = jnp.clip(col - NA_WIN_W // 2, 0, w - NA_WIN_W)
    col_in = (col[None, :] >= c0[:, None]) & (col[None, :] < c0[:, None] + NA_WIN_W)
    left = (w - 1) - (NA_WIN_W - 1)
    ext = jnp.pad(rpb, ((0, 0), (0, 0), (left, 2 * w - left - (2 * NA_WIN_W - 1))), mode="edge")
    h, nr, _ = rpb.shape
    flat = jnp.broadcast_to(ext[:, :, None, :], (h, nr, w, 2 * w)).reshape(h, nr, w * 2 * w)
    toep = flat[:, :, :w * (2 * w - 1)].reshape(h, nr, w, 2 * w - 1)[:, :, :, w - 1:]
    toep = jnp.where(col_in[None, None], toep * LOG2E, NEG_INF)
    variants = []
    for v in range(NA_WIN_H):
        tv = toep[:, NA_WIN_H - 1 - v:2 * NA_WIN_H - 1 - v]
        variants.append(tv.transpose(0, 2, 1, 3).reshape(NA_HEADS // 2, 2 * w, NA_WIN_H * w))
    return jnp.stack(variants, axis=0).astype(F32)


def _flash_kernel(*refs, mode, online, tq, tk, s_len, c_len, lam_init):
    if mode == "diff":
        q_ref, k_ref, v_ref, kc_ref, vc_ref, lam_ref, sg_ref, o_ref = refs
    else:
        q_ref, k_ref, v_ref, kc_ref, vc_ref, o_ref = refs

    q = q_ref[0]
    if mode == "mla":
        q_parts = (q[:, :LANES], q[:, LANES:])
    else:
        qs = _pair_stack(q)

    def scores(kt):
        if mode == "mla":
            return jnp.concatenate([_dot_nt(q_parts[0], kt[:, :LANES]), _dot_nt(q_parts[1], kt[:, LANES:])], axis=0)
        return _dot_nt(qs, kt)

    def step(kt, vt, carry):
        m, l, acc = carry
        s = scores(kt)
        m_new = jnp.maximum(m, jnp.max(s, axis=-1, keepdims=True))
        alpha = jnp.exp2(m - m_new)
        p = jnp.exp2(s - m_new)
        l = alpha * l + jnp.sum(p, axis=-1, keepdims=True)
        acc = alpha * acc + _dot(p.astype(BF16), vt)
        return m_new, l, acc

    def body(t, carry):
        start = pl.multiple_of(t * tk, tk)
        return step(k_ref[0, pl.ds(start, tk), :], v_ref[0, pl.ds(start, tk), :], carry)

    def scores_t(kt):
        if mode == "mla":
            return jnp.concatenate([_dot_nt(kt[:, :LANES], q_parts[0]), _dot_nt(kt[:, LANES:], q_parts[1])], axis=1)
        return _dot_nt(kt, qs)

    def direct(kt, vtt, acc):
        p = jnp.exp2(scores_t(kt)).astype(BF16)
        v1 = jnp.concatenate([vtt, jnp.ones((SUM_ROWS, vtt.shape[1]), BF16)], axis=0)
        return acc + _dot(v1, p)

    if online:
        carry = (jnp.full((2 * tq, 1), -jnp.inf, F32), jnp.zeros((2 * tq, 1), F32),
                 jnp.zeros((2 * tq, LANES), F32))
        if s_len:
            carry = lax.fori_loop(0, s_len // tk, body, carry)
        if c_len:
            carry = step(kc_ref[0], vc_ref[0], carry)
        _, l, acc = carry
        o = acc / l
    else:
        acc = jnp.zeros((LANES + SUM_ROWS, 2 * tq), F32)
        for t in range(s_len // tk):
            acc = direct(k_ref[0, t * tk:(t + 1) * tk, :], v_ref[0, :, t * tk:(t + 1) * tk], acc)
        if c_len:
            acc = direct(kc_ref[0], vc_ref[0], acc)
        o_t = acc[:LANES] / acc[LANES:LANES + 1]
    if mode == "diff":
        lp = lam_ref[...]
        lam = (jnp.exp(jnp.sum(lp[0:1] * lp[1:2], axis=-1, keepdims=True))
               - jnp.exp(jnp.sum(lp[2:3] * lp[3:4], axis=-1, keepdims=True)) + lam_init)
        if online:
            d = o[:tq] - lam * o[tq:]
            dn = d * lax.rsqrt(jnp.mean(d * d, axis=-1, keepdims=True) + NORM_EPS)
        else:
            d_t = o_t[:, :tq] - lam * o_t[:, tq:]
            dn = (d_t * lax.rsqrt(jnp.mean(d_t * d_t, axis=0, keepdims=True) + NORM_EPS)).T
        o_ref[0] = (dn * sg_ref[...] * (1.0 - lam_init)).astype(BF16)
    elif online:
        o_ref[0] = _pair_merge(o, tq).astype(BF16)
    else:
        row = lax.broadcasted_iota(jnp.int32, (LANES, tq), 0)
        o_ref[0] = jnp.where(row < HEAD_DIM, o_t[:, :tq], o_t[:, tq:]).T.astype(BF16)


def _flash(q, k, vt, kc, vct, *, score_bound=None, **kw):
    rows = lambda a: None if a is None else jnp.swapaxes(a, 1, 2)
    if score_bound is None:
        return _flash_call(q, k, rows(vt), kc, rows(vct), online=True, **kw)
    return lax.cond(score_bound <= EXP2_SAFE_RANGE,
                    lambda: _flash_call(q, k, vt, kc, vct, online=False, **kw),
                    lambda: _flash_call(q, k, rows(vt), kc, rows(vct), online=True, **kw))


def _flash_call(q, k, v, kc, vc, *, mode, online, kv_map, n_chunks, tq, tk, extra=(), lam_init=0.0):
    b, s, _ = q.shape
    qw = 2 * LANES if mode == "mla" else LANES
    c_len = kc.shape[1]
    if k is None:
        k, v, s_len = kc, vc, 0
    else:
        s_len = k.shape[1]
    kk = k.shape[1]
    if online:
        v_spec = lambda n: pl.BlockSpec((1, n, LANES), lambda bi, j, i: (bi, 0, kv_map(j)))
    else:
        v_spec = lambda n: pl.BlockSpec((1, LANES, n), lambda bi, j, i: (bi, kv_map(j), 0))
    in_specs = [
        pl.BlockSpec((1, tq, qw), lambda bi, j, i: (bi, i, j)),
        pl.BlockSpec((1, kk, qw), lambda bi, j, i: (bi, 0, kv_map(j) if mode != "mla" else j)),
        v_spec(kk),
        pl.BlockSpec((1, c_len, qw), lambda bi, j, i: (bi, 0, kv_map(j) if mode != "mla" else j)),
        v_spec(c_len),
    ]
    for e in extra:
        in_specs.append(pl.BlockSpec(e.shape, lambda bi, j, i: (0, 0)))
    return pl.pallas_call(
        functools.partial(_flash_kernel, mode=mode, online=online, tq=tq, tk=tk, s_len=s_len, c_len=c_len,
                          lam_init=lam_init),
        out_shape=jax.ShapeDtypeStruct((b, s, n_chunks * LANES), BF16),
        grid=(b, n_chunks, s // tq),
        in_specs=in_specs,
        out_specs=pl.BlockSpec((1, tq, LANES), lambda bi, j, i: (bi, i, j)),
        compiler_params=_cparams(("parallel", "parallel", "parallel")),
        name="flash_" + mode + ("_online" if online else ""),
    )(q, k, v, kc, vc, *extra)


def _post_ffn_kernel(oa_ref, ob_ref, wa_ref, wb_ref, x_ref, g1_ref, ng_ref, sh_ref, sc_ref, g2_ref,
                     wg_ref, wu_ref, wd_ref, o_ref, *, tf):
    y = _dot(oa_ref[0], wa_ref[...]) + _dot(ob_ref[0], wb_ref[...])
    x1 = x_ref[0] + g1_ref[0] * y
    h = _norm_mod(x1, ng_ref[...], sh_ref[0], sc_ref[0]).astype(BF16)
    acc = jnp.zeros(o_ref.shape[1:], F32)
    for f in range(wg_ref.shape[1] // tf):
        sl = slice(f * tf, (f + 1) * tf)
        a = _dot(h, wg_ref[:, sl])
        u = _dot(h, wu_ref[:, sl])
        acc = acc + _dot((_silu(a) * u).astype(BF16), wd_ref[sl, :])
    o_ref[0] = x1 + g2_ref[0] * acc


def _post_ffn(oa, ob, wa, wb, x, gate1, ng, shift, scale, gate2, wg, wu, wd, *, tm, tf=256):
    b, s, d = x.shape
    f = wg.shape[1]
    row = lambda bi, i: (bi, i, 0)
    vec = lambda bi, i: (bi, 0, 0)
    const = lambda bi, i: (0, 0)
    return pl.pallas_call(
        functools.partial(_post_ffn_kernel, tf=tf),
        out_shape=jax.ShapeDtypeStruct((b, s, d), F32),
        grid=(b, s // tm),
        in_specs=[
            pl.BlockSpec((1, tm, 512), row),
            pl.BlockSpec((1, tm, 512), row),
            pl.BlockSpec((512, d), const),
            pl.BlockSpec((512, d), const),
            pl.BlockSpec((1, tm, d), row),
            pl.BlockSpec((1, 1, d), vec),
            pl.BlockSpec((1, d), const),
            pl.BlockSpec((1, 1, d), vec),
            pl.BlockSpec((1, 1, d), vec),
            pl.BlockSpec((1, 1, d), vec),
            pl.BlockSpec((d, f), const),
            pl.BlockSpec((d, f), const),
            pl.BlockSpec((f, d), const),
        ],
        out_specs=pl.BlockSpec((1, tm, d), row),
        compiler_params=_cparams(("parallel", "parallel")),
        name="post_ffn",
    )(oa, ob, wa, wb, x, gate1, ng, shift, scale, gate2, wg, wu, wd)


def _rms(t, g):
    return t * lax.rsqrt(jnp.mean(t * t, axis=-1, keepdims=True) + NORM_EPS) * g


def _odd_proj_kernel(x_ref, sh_ref, sc_ref, ng_ref, w_ref, wuq_ref, wukv_ref, e_ref, e2_ref, g_ref,
                     cos_ref, sin_ref, cosm_ref, sinm_ref,
                     qc_ref, kc_ref, vc_ref, qd_ref, kd_ref, vd_ref, *, use_rope):
    g = g_ref[...]
    e = e_ref[...]
    e2 = e2_ref[...]
    inv_n = 1.0 / HEAD_DIM
    for rows in _row_blocks(x_ref.shape[1]):
        h = _norm_mod(x_ref[0, rows, :], ng_ref[...], sh_ref[0], sc_ref[0])
        p = _dot(h.astype(BF16), w_ref[...])
        qc = _group_norm(p[:, 0:512], e, g[0:1, 0:512], inv_n)
        kc = _group_norm(p[:, 512:768], e, g[1:2, 0:256], inv_n)
        cq = _rms(p[:, 1024:1280], g[2:3, 0:256])
        ckv = _rms(p[:, 1280:1408], g[3:4, 0:128])
        krc = p[:, 1408:1536]
        qd = _dot(cq.astype(BF16), wuq_ref[...])
        kvv = _dot(ckv.astype(BF16), wukv_ref[...])
        kd = kvv[:, 0:1024] + jnp.concatenate([krc] * MLA_HEADS, axis=1)
        qd = _group_norm(qd, e2, g[4:5], 1.0 / MLA_QK)
        kd = _group_norm(kd, e2, g[5:6], 1.0 / MLA_QK)
        if use_rope:
            lane = lax.broadcasted_iota(jnp.int32, (1, LANES), 1)
            cos, sin = cos_ref[rows, :], sin_ref[rows, :]
            qc = _rope_chunks(qc, cos, sin, (lane & 16) != 0, 16)
            kc = _rope_chunks(kc, cos, sin, (lane & 16) != 0, 16)
            cosm, sinm = cosm_ref[rows, :], sinm_ref[rows, :]
            qd = _rope_chunks(qd, cosm, sinm, (lane & 8) != 0, 8)
            kd = _rope_chunks(kd, cosm, sinm, (lane & 8) != 0, 8)
        qc_ref[0, rows, :] = qc.astype(BF16)
        kc_ref[0, rows, :] = kc.astype(BF16)
        vc_ref[0, :, rows] = p[:, 768:1024].T.astype(BF16)
        qd_ref[0, rows, :] = qd.astype(BF16)
        kd_ref[0, rows, :] = kd.astype(BF16)
        vd_ref[0, :, rows] = kvv[:, 1024:1536].T.astype(BF16)


def _odd_proj(x, shift, scale, ng, w, wuq, wukv, e64, e128, gains, cos, sin, cosm, sinm, *, use_rope, tm):
    b, s, d = x.shape
    row = lambda bi, i: (bi, i, 0)
    vec = lambda bi, i: (bi, 0, 0)
    const = lambda bi, i: (0, 0)
    tab = lambda bi, i: (i, 0)
    widths = (512, 256, 256, 1024, 1024, 512)
    v_outs = (2, 5)
    return pl.pallas_call(
        functools.partial(_odd_proj_kernel, use_rope=use_rope),
        out_shape=tuple(jax.ShapeDtypeStruct((b, n, s) if i in v_outs else (b, s, n), BF16)
                        for i, n in enumerate(widths)),
        grid=(b, s // tm),
        in_specs=[
            pl.BlockSpec((1, tm, d), row),
            pl.BlockSpec((1, 1, d), vec),
            pl.BlockSpec((1, 1, d), vec),
            pl.BlockSpec((1, d), const),
            pl.BlockSpec(w.shape, const),
            pl.BlockSpec(wuq.shape, const),
            pl.BlockSpec(wukv.shape, const),
            pl.BlockSpec(e64.shape, const),
            pl.BlockSpec(e128.shape, const),
            pl.BlockSpec(gains.shape, const),
            pl.BlockSpec((tm, LANES), tab),
            pl.BlockSpec((tm, LANES), tab),
            pl.BlockSpec((tm, LANES), tab),
            pl.BlockSpec((tm, LANES), tab),
        ],
        out_specs=tuple(pl.BlockSpec((1, n, tm), lambda bi, i: (bi, 0, i)) if j in v_outs
                        else pl.BlockSpec((1, tm, n), row) for j, n in enumerate(widths)),
        compiler_params=_cparams(("parallel", "parallel")),
        name="odd_proj",
    )(x, shift, scale, ng, w, wuq, wukv, e64, e128, gains, cos, sin, cosm, sinm)


def _mla_rope_tables(seq):
    cos, sin = _rope_tables(seq, MLA_ROPE, MLA_ROPE // 4)
    lane = jnp.arange(LANES)
    on = (lane >= MLA_NOPE) & (lane < MLA_QK)
    return jnp.where(on[None, :], cos, 1.0), jnp.where(on[None, :], sin, 0.0)


def _router_logits(h, rt_ref):
    h_hi = h.astype(BF16)
    h_lo = (h - h_hi.astype(F32)).astype(BF16)
    both = _dot(h_hi, rt_ref[...]) + _dot(h_lo, rt_ref[...])
    return both[:, :LANES] + both[:, LANES:]


def _route_tile(logits_rows, blk, u_ref, ei_ref, gt_ref, rk_ref, before0):
    logits = logits_rows.T[:N_EXPERTS]
    eidx = lax.broadcasted_iota(jnp.int32, logits.shape, 0)
    m1 = jnp.max(logits, axis=0, keepdims=True)
    i1 = jnp.min(jnp.where(logits == m1, eidx, N_EXPERTS), axis=0, keepdims=True)
    rest = jnp.where(eidx == i1, -jnp.inf, logits)
    m2 = jnp.max(rest, axis=0, keepdims=True)
    i2 = jnp.min(jnp.where(rest == m2, eidx, N_EXPERTS), axis=0, keepdims=True)
    e2 = jnp.exp(m2 - m1)
    g1 = 1.0 / (1.0 + e2)
    sel1 = eidx == i1
    sel2 = eidx == i2
    onehot = jnp.where(sel1 | sel2, 1.0, 0.0)
    pieces = []
    for c in range(logits.shape[1] // blk):
        oh = onehot[:, c * blk:(c + 1) * blk]
        pieces.append(_dot(oh.astype(BF16), u_ref[...]) + before0)
        before0 = before0 + jnp.sum(oh, axis=1, keepdims=True)
    before = jnp.concatenate(pieces, axis=1)
    r1 = jnp.sum(jnp.where(sel1, before, 0.0), axis=0, keepdims=True)
    r2 = jnp.sum(jnp.where(sel2, before, 0.0), axis=0, keepdims=True)
    ei_ref[...] = jnp.concatenate([i1, i2], axis=0)
    gt_ref[...] = jnp.concatenate([g1, e2 * g1], axis=0)
    rk_ref[...] = jnp.concatenate([r1, r2], axis=0).astype(jnp.int32)
    return before0


def _post_attn_router_kernel(oa_ref, ob_ref, wa_ref, wb_ref, x_ref, g_ref, ng_ref, sh_ref, sc_ref, rt_ref, u_ref,
                             x1_ref, h2_ref, ei_ref, gt_ref, rk_ref, cnt_ref, carry_ref):
    @pl.when((pl.program_id(0) == 0) & (pl.program_id(1) == 0))
    def _():
        carry_ref[...] = jnp.zeros_like(carry_ref)

    logits = []
    for rows in _row_blocks(x_ref.shape[1]):
        y = _dot(oa_ref[0, rows, :], wa_ref[...]) + _dot(ob_ref[0, rows, :], wb_ref[...])
        x1 = x_ref[0, rows, :] + g_ref[0] * y
        x1_ref[0, rows, :] = x1
        h = _norm_mod(x1, ng_ref[...], sh_ref[0], sc_ref[0])
        h2_ref[0, rows, :] = h.astype(BF16)
        logits.append(_router_logits(h, rt_ref))
    routed = _route_tile(jnp.concatenate(logits, axis=0), u_ref.shape[0], u_ref, ei_ref, gt_ref, rk_ref,
                         carry_ref[:, 0:1])
    carry_ref[...] = jnp.broadcast_to(routed, carry_ref.shape)
    cnt_ref[...] = jnp.broadcast_to(routed, cnt_ref.shape).astype(jnp.int32)


def _post_attn_router(oa, ob, wa, wb, x, gate, ng, shift, scale, router_t, *, tm):
    b, s, d = x.shape
    n = b * s
    nt = s // tm
    blk = _row_blocks(tm)[0].stop
    tri = (jnp.arange(blk)[:, None] < jnp.arange(blk)[None, :]).astype(BF16)
    row = lambda bi, i: (bi, i, 0)
    vec = lambda bi, i: (bi, 0, 0)
    const = lambda bi, i: (0, 0)
    flat = lambda bi, i: (0, bi * nt + i)
    return pl.pallas_call(
        _post_attn_router_kernel,
        out_shape=(jax.ShapeDtypeStruct((b, s, d), F32), jax.ShapeDtypeStruct((b, s, d), BF16),
                   jax.ShapeDtypeStruct((2, n), jnp.int32), jax.ShapeDtypeStruct((2, n), F32),
                   jax.ShapeDtypeStruct((2, n), jnp.int32), jax.ShapeDtypeStruct((N_EXPERTS, LANES), jnp.int32)),
        grid=(b, nt),
        in_specs=[
            pl.BlockSpec((1, tm, 512), row),
            pl.BlockSpec((1, tm, 512), row),
            pl.BlockSpec((512, d), const),
            pl.BlockSpec((512, d), const),
            pl.BlockSpec((1, tm, d), row),
            pl.BlockSpec((1, 1, d), vec),
            pl.BlockSpec((1, d), const),
            pl.BlockSpec((1, 1, d), vec),
            pl.BlockSpec((1, 1, d), vec),
            pl.BlockSpec((d, 2 * LANES), const),
            pl.BlockSpec((blk, blk), const),
        ],
        out_specs=(pl.BlockSpec((1, tm, d), row), pl.BlockSpec((1, tm, d), row),
                   pl.BlockSpec((2, tm), flat), pl.BlockSpec((2, tm), flat), pl.BlockSpec((2, tm), flat),
                   pl.BlockSpec((N_EXPERTS, LANES), const)),
        scratch_shapes=[pltpu.VMEM((N_EXPERTS, LANES), F32)],
        compiler_params=_cparams(("arbitrary", "arbitrary")),
        name="post_attn_router",
    )(oa, ob, wa, wb, x, gate, ng, shift, scale, router_t, tri)


MOE_SUB = 256
MOE_BLK = 1024
MOE_TC = 256
MOE_GROUP = 4
MOE_NBUF = 16
MOE_AHEAD = MOE_NBUF - MOE_GROUP


def _lookup(table, idx):
    hit = idx[..., None] == jnp.arange(table.shape[0])
    return jnp.sum(jnp.where(hit, table, 0), axis=-1)


def _lookup_cols(rows, idx):
    hit = idx[:, None] == jnp.arange(rows.shape[1])[None, :]
    return jnp.sum(jnp.where(hit, rows, 0), axis=1)


def _moe_plan(ei, rk, counts, n_tok):
    n_chunks = n_tok // MOE_TC
    cap = 2 * n_tok + N_EXPERTS * MOE_BLK
    nb_sub = cap // MOE_SUB
    nb_blk = cap // MOE_BLK
    padded = ((counts + MOE_BLK - 1) // MOE_BLK) * MOE_BLK
    pad_end = jnp.cumsum(padded)
    pad_start = pad_end - padded
    dest = _lookup(pad_start, ei) + rk

    onehot = ei.reshape(2, n_chunks, MOE_TC)[..., None] == jnp.arange(N_EXPERTS)
    cnt = jnp.sum(onehot, axis=(0, 2)).astype(jnp.int32)
    cum = jnp.concatenate([jnp.zeros((1, N_EXPERTS), jnp.int32), jnp.cumsum(cnt, axis=0)], axis=0)

    lo = pad_start[None, :] + cum[:-1]
    has = cnt > 0
    blk0 = lo // MOE_SUB
    two = has & ((lo + cnt - 1) // MOE_SUB > blk0)
    blk1 = jnp.minimum(blk0 + 1, nb_sub - 1)
    win_ok = jnp.stack([has, two], axis=-1).reshape(n_chunks, 2 * N_EXPERTS)
    win_blk = jnp.stack([blk0, blk1], axis=-1).reshape(n_chunks, 2 * N_EXPERTS)
    c_count = jnp.sum(win_ok, axis=1).astype(jnp.int32)
    c_start = (jnp.cumsum(c_count) - c_count).astype(jnp.int32)
    win_pos = jnp.cumsum(win_ok, axis=1) - win_ok
    front = win_ok[:, :, None] & (win_pos[:, :, None] == jnp.arange(2 * N_EXPERTS)[None, None, :])
    packed = jnp.sum(jnp.where(front, win_blk[:, :, None], 0), axis=1)
    q = jnp.arange(n_chunks * 2 * N_EXPERTS)
    c_of_q = jnp.minimum(jnp.sum((c_start + c_count)[None, :] <= q[:, None], axis=1), n_chunks - 1)
    row_q = jnp.dot((c_of_q[:, None] == jnp.arange(n_chunks)[None, :]).astype(F32), packed.astype(F32),
                    precision=HIGHEST)
    p_of_q = q - _lookup(c_start, c_of_q)
    c_blocks = _lookup_cols(row_q, p_of_q).astype(jnp.int32)

    sb = jnp.arange(nb_sub)
    e_sb = jnp.minimum(jnp.sum(sb[:, None] * MOE_SUB >= pad_end[None, :], axis=1), N_EXPERTS - 1)
    counts_sb = _lookup(counts, e_sb)
    r0 = sb * MOE_SUB - _lookup(pad_start, e_sb)
    valid_sb = (sb * MOE_SUB < pad_end[-1]) & (r0 < counts_sb)
    r1 = jnp.minimum(r0 + MOE_SUB, counts_sb) - 1
    hit_sb = e_sb[:, None] == jnp.arange(N_EXPERTS)[None, :]
    cum_sb = jnp.sum(jnp.where(hit_sb[:, None, :], cum[None, 1:, :], 0), axis=-1)
    cmin = jnp.sum(cum_sb <= r0[:, None], axis=1)
    cmax = jnp.sum(cum_sb <= r1[:, None], axis=1)
    items = jnp.where(valid_sb, cmax - cmin + 1, 0)
    off_end = jnp.cumsum(items)
    off = off_end - items
    total = off_end[-1]
    w_max = (2 * n_tok) // MOE_SUB + N_EXPERTS + N_EXPERTS * (n_chunks - 1)
    w = jnp.arange(w_max)
    wv = w < total
    wq = jnp.minimum(w, total - 1)
    w_sb = jnp.minimum(jnp.sum(off_end[None, :] <= wq[:, None], axis=1), nb_sub - 1)
    off_w = _lookup(off, w_sb)
    w_chunk = _lookup(cmin, w_sb) + (wq - off_w)
    w_flag = wv.astype(jnp.int32) + 2 * (wq == off_w).astype(jnp.int32)

    bi = jnp.arange(nb_blk)
    e_blk = jnp.minimum(jnp.sum(bi[:, None] * MOE_BLK >= pad_end[None, :], axis=1), N_EXPERTS - 1)
    rows = jnp.where(bi * MOE_BLK < pad_end[-1],
                     _lookup(counts, e_blk) - (bi * MOE_BLK - _lookup(pad_start, e_blk)), 0)
    n_sub = jnp.clip((rows + MOE_SUB - 1) // MOE_SUB, 0, MOE_BLK // MOE_SUB)
    g_count = jnp.sum(items.reshape(nb_blk, MOE_BLK // MOE_SUB), axis=1)
    g_start = jnp.cumsum(g_count) - g_count
    return dict(dest=dest.astype(jnp.int32), cap=cap,
                w_sb=w_sb.astype(jnp.int32), w_chunk=w_chunk.astype(jnp.int32), w_flag=w_flag,
                g_start=g_start.astype(jnp.int32), g_count=g_count.astype(jnp.int32),
                e_blk=e_blk.astype(jnp.int32), n_sub=n_sub.astype(jnp.int32),
                c_blocks=c_blocks, c_count=c_count, c_start=c_start)


def _moe_gather_kernel(st_ref, cn_ref, wsb_ref, wch_ref, t_hbm, d_ref, g_ref, xg_ref, gs_ref, tbuf, sem):
    i = pl.program_id(0)
    n = cn_ref[i]
    s0 = st_ref[i]
    total = st_ref[pl.num_programs(0) - 1] + cn_ref[pl.num_programs(0) - 1]
    subs = MOE_BLK // MOE_SUB

    def chunk_copy(item):
        slot = lax.rem(item, MOE_NBUF)
        start = pl.multiple_of(wch_ref[item] * MOE_TC, MOE_TC)
        return pltpu.make_async_copy(t_hbm.at[pl.ds(start, MOE_TC), :], tbuf.at[slot], sem.at[slot])

    @pl.when(i == 0)
    def _():
        for a in range(MOE_AHEAD):
            @pl.when(a < total)
            def _():
                chunk_copy(a).start(priority=a % 2)

    xg_ref[...] = jnp.zeros(xg_ref.shape, BF16)
    gs_ref[...] = jnp.zeros(gs_ref.shape, F32)

    def arrive(item, thread=0):
        chunk_copy(item).wait()

        @pl.when(item + MOE_AHEAD < total)
        def _():
            chunk_copy(item + MOE_AHEAD).start(priority=thread)

    def contribute(item):
        sb = wsb_ref[item]
        chunk = wch_ref[item]
        d = d_ref[chunk]
        g = g_ref[chunk]
        srow = lax.broadcasted_iota(jnp.int32, (MOE_SUB, MOE_TC), 0) + sb * MOE_SUB
        hit0 = srow == d[0:1]
        hit1 = srow == d[1:2]
        p = jnp.where(hit0 | hit1, 1.0, 0.0).astype(BF16)
        rows = _dot(p, tbuf[lax.rem(item, MOE_NBUF)]).astype(BF16)
        gate = jnp.sum(jnp.where(hit0, g[0:1], 0.0) + jnp.where(hit1, g[1:2], 0.0), axis=-1, keepdims=True)
        r0 = pl.multiple_of((sb - i * subs) * MOE_SUB, MOE_SUB)
        xg_ref[pl.ds(r0, MOE_SUB), :] += rows
        gs_ref[pl.ds(r0, MOE_SUB), :] += gate

    def group(k, carry):
        item = s0 + MOE_GROUP * k
        for a in range(MOE_GROUP):
            arrive(item + a, a % 2)
        for a in range(MOE_GROUP):
            contribute(item + a)
        return carry

    def single(k, carry):
        arrive(s0 + grouped + k)
        contribute(s0 + grouped + k)
        return carry

    grouped = (n // MOE_GROUP) * MOE_GROUP
    lax.fori_loop(0, n // MOE_GROUP, group, 0)
    lax.fori_loop(0, n - grouped, single, 0)


def _moe_gather(t, dest, gates, plan):
    n_tok, d = t.shape
    cap = plan["cap"]
    n_chunks = n_tok // MOE_TC
    by_chunk = lambda a: a.reshape(2, n_chunks, MOE_TC).transpose(1, 0, 2)
    grid_spec = pltpu.PrefetchScalarGridSpec(
        num_scalar_prefetch=4,
        grid=(cap // MOE_BLK,),
        in_specs=[
            pl.BlockSpec(memory_space=pl.ANY),
            pl.BlockSpec((n_chunks, 2, MOE_TC), lambda i, *_: (0, 0, 0)),
            pl.BlockSpec((n_chunks, 2, MOE_TC), lambda i, *_: (0, 0, 0)),
        ],
        out_specs=(pl.BlockSpec((MOE_BLK, d), lambda i, *_: (i, 0)),
                   pl.BlockSpec((MOE_BLK, 1), lambda i, *_: (i, 0))),
        scratch_shapes=[pltpu.VMEM((MOE_NBUF, MOE_TC, d), BF16), pltpu.SemaphoreType.DMA((MOE_NBUF,))],
    )
    return pl.pallas_call(
        _moe_gather_kernel,
        out_shape=(jax.ShapeDtypeStruct((cap, d), BF16), jax.ShapeDtypeStruct((cap, 1), F32)),
        grid_spec=grid_spec,
        compiler_params=_cparams(("arbitrary",)),
        name="moe_gather",
    )(plan["g_start"], plan["g_count"], plan["w_sb"], plan["w_chunk"], t, by_chunk(dest), by_chunk(gates))


def _moe_expert_kernel(eb_ref, ns_ref, x_ref, gs_ref, w1_ref, w3_ref, w2_ref, y_ref):
    n_sub = ns_ref[pl.program_id(0)]
    subs = MOE_BLK // MOE_SUB

    def mlp(rows):
        xs = x_ref[rows, :]
        acc = jnp.zeros((rows.stop - rows.start, y_ref.shape[1]), F32)
        for f in range(w1_ref.shape[1] // MOE_TF):
            cols = slice(f * MOE_TF, (f + 1) * MOE_TF)
            a = _dot(xs, w1_ref[:, cols])
            u = _dot(xs, w3_ref[:, cols])
            acc = acc + _dot((_silu(a) * u).astype(BF16), w2_ref[cols, :])
        y_ref[rows, :] = (acc * gs_ref[rows, :]).astype(BF16)

    @pl.when(n_sub == subs)
    def _():
        mlp(slice(0, MOE_BLK))

    for sub in range(subs):
        rows = slice(sub * MOE_SUB, (sub + 1) * MOE_SUB)

        @pl.when((n_sub < subs) & (sub < n_sub))
        def _():
            mlp(rows)

        @pl.when(sub >= n_sub)
        def _():
            y_ref[rows, :] = jnp.zeros((MOE_SUB, y_ref.shape[1]), BF16)


def _moe_experts(xg, gs, w1, w3, w2, plan):
    cap, d = xg.shape
    f = w1.shape[2]
    once = pl.Buffered(1)
    grid_spec = pltpu.PrefetchScalarGridSpec(
        num_scalar_prefetch=2,
        grid=(cap // MOE_BLK,),
        in_specs=[
            pl.BlockSpec((MOE_BLK, d), lambda i, eb, ns: (i, 0)),
            pl.BlockSpec((MOE_BLK, 1), lambda i, eb, ns: (i, 0)),
            pl.BlockSpec((None, d, f), lambda i, eb, ns: (eb[i], 0, 0), pipeline_mode=once),
            pl.BlockSpec((None, d, f), lambda i, eb, ns: (eb[i], 0, 0), pipeline_mode=once),
            pl.BlockSpec((None, f, d), lambda i, eb, ns: (eb[i], 0, 0), pipeline_mode=once),
        ],
        out_specs=pl.BlockSpec((MOE_BLK, d), lambda i, eb, ns: (i, 0)),
    )
    return pl.pallas_call(
        _moe_expert_kernel,
        out_shape=jax.ShapeDtypeStruct((cap, d), BF16),
        grid_spec=grid_spec,
        compiler_params=_cparams(("arbitrary",)),
        name="moe_experts",
    )(plan["e_blk"], plan["n_sub"], xg, gs, w1, w3, w2)


MOE_WIN = 2 * N_EXPERTS


def _moe_combine_kernel(st_ref, cn_ref, bl_ref, dt_ref, y_hbm, x1_ref, g_ref, o_ref, ybuf, sem):
    c = pl.program_id(0)
    n = cn_ref[c]
    s0 = st_ref[c]
    total = st_ref[pl.num_programs(0) - 1] + cn_ref[pl.num_programs(0) - 1]
    d = dt_ref[...]

    def block_copy(item):
        slot = lax.rem(item, MOE_NBUF)
        start = pl.multiple_of(bl_ref[item] * MOE_SUB, MOE_SUB)
        return pltpu.make_async_copy(y_hbm.at[pl.ds(start, MOE_SUB), :], ybuf.at[slot], sem.at[slot])

    @pl.when(c == 0)
    def _():
        for a in range(MOE_AHEAD):
            @pl.when(a < total)
            def _():
                block_copy(a).start(priority=a % 2)

    def arrive(item, thread=0):
        block_copy(item).wait()

        @pl.when(item + MOE_AHEAD < total)
        def _():
            block_copy(item + MOE_AHEAD).start(priority=thread)

    def rows_of(item):
        scol = lax.broadcasted_iota(jnp.int32, (MOE_TC, MOE_SUB), 1) + bl_ref[item] * MOE_SUB
        p = jnp.where((scol == d[:, 0:1]) | (scol == d[:, 1:2]), 1.0, 0.0).astype(BF16)
        return _dot(p, ybuf[lax.rem(item, MOE_NBUF)])

    def group(k, acc):
        item = s0 + MOE_GROUP * k
        for a in range(MOE_GROUP):
            arrive(item + a, a % 2)
        for a in range(MOE_GROUP):
            acc = acc + rows_of(item + a)
        return acc

    def single(k, acc):
        arrive(s0 + grouped + k)
        return acc + rows_of(s0 + grouped + k)

    grouped = (n // MOE_GROUP) * MOE_GROUP
    acc = lax.fori_loop(0, n // MOE_GROUP, group, jnp.zeros(o_ref.shape, F32))
    acc = lax.fori_loop(0, n - grouped, single, acc)
    o_ref[...] = x1_ref[...] + g_ref[0] * acc


def _moe_combine(y, dest_t, x1, gate, plan, seq):
    n_tok, d = x1.shape
    n_chunks = n_tok // MOE_TC
    per_b = seq // MOE_TC
    grid_spec = pltpu.PrefetchScalarGridSpec(
        num_scalar_prefetch=3,
        grid=(n_chunks,),
        in_specs=[
            pl.BlockSpec((MOE_TC, 2), lambda c, *_: (c, 0)),
            pl.BlockSpec(memory_space=pl.ANY),
            pl.BlockSpec((MOE_TC, d), lambda c, *_: (c, 0)),
            pl.BlockSpec((1, 1, d), lambda c, *_: (c // per_b, 0, 0)),
        ],
        out_specs=pl.BlockSpec((MOE_TC, d), lambda c, *_: (c, 0)),
        scratch_shapes=[pltpu.VMEM((MOE_NBUF, MOE_SUB, d), BF16), pltpu.SemaphoreType.DMA((MOE_NBUF,))],
    )
    return pl.pallas_call(
        _moe_combine_kernel,
        out_shape=jax.ShapeDtypeStruct((n_tok, d), F32),
        grid_spec=grid_spec,
        compiler_params=_cparams(("arbitrary",)),
        name="moe_combine",
    )(plan["c_start"], plan["c_count"], plan["c_blocks"], dest_t, y, x1, gate)


def _split_mod(mod_l, batch):
    d = D_MODEL
    lat = tuple(mod_l[:batch, k * d:(k + 1) * d][:, None, :] for k in range(ADA_CHUNKS))
    ctx = tuple(jnp.broadcast_to(mod_l[batch:batch + 1, k * d:(k + 1) * d][:, None, :], (batch, 1, d))
                for k in range(ADA_CHUNKS))
    return lat, ctx


def _score_bound(q_gain, k_gain, n):
    return 1.02 * n * jnp.max(jnp.abs(q_gain)) * jnp.max(jnp.abs(k_gain))


def _block_ones(n, block):
    idx = jnp.arange(n) // block
    return (idx[:, None] == idx[None, :]).astype(BF16)


def _even_layer(x, xc, mod_l, norm1_g, norm2_g, w_in, w_out, na_q_g, na_k_g, na_rpb, diff_q_g, diff_k_g,
                lq1, lk1, lq2, lk2, subln_g, wg, wu, wd, layer_idx, need_ctx):
    b, s, d = x.shape
    (sh1, sc1, g1, sh2, sc2, g2), (csh1, csc1, cg1, csh2, csc2, cg2) = _split_mod(mod_l, b)
    qscale = HEAD_DIM ** -0.5 * LOG2E
    lam_init = 0.8 - 0.6 * math.exp(-0.3 * layer_idx)
    hg = jnp.stack([jnp.tile(na_q_g, 8) * qscale, jnp.tile(na_k_g, 8),
                    jnp.tile(diff_q_g, 8) * qscale, jnp.tile(diff_k_g, 8)]).astype(F32)
    e64 = _block_ones(MXU_DIM, HEAD_DIM)
    cos, sin = _rope_tables(s, HEAD_DIM, 16)
    ng1 = norm1_g[None, :]
    ng2 = norm2_g[None, :]
    w_in_b = w_in.astype(BF16)
    wo_a = w_out[:512].astype(BF16)
    wo_b = w_out[512:].astype(BF16)
    wg_b, wu_b, wd_b = wg.astype(BF16), wu.astype(BF16), wd.astype(BF16)
    lam_p = jnp.stack([lq1, lk1, lq2, lk2]).astype(F32)
    sg = subln_g[None, :].astype(F32)
    ident = lambda j: j

    tm = min(ROW_TM, s)
    tc = xc.shape[1]
    qa, ka, va, qb, kb, vb = _even_proj(x, sh1, sc1, ng1, w_in_b, e64, hg, cos, sin, use_rope=True, tm=tm)
    qca, kca, vca, qcb, kcb, vcb = _even_proj(xc, csh1, csc1, ng1, w_in_b, e64, hg, cos, sin, use_rope=False, tm=tc)
    out_a = _na_attention(qa, ka, va, kca, vca, _na_bias_table(na_rpb), _score_bound(hg[0], hg[1], HEAD_DIM))
    out_b = _flash(qb, kb, vb, kcb, vcb, mode="diff", kv_map=ident, n_chunks=4, tq=FLASH_TQ, tk=FLASH_TK,
                   extra=(lam_p, sg), lam_init=lam_init, score_bound=_score_bound(hg[2], hg[3], HEAD_DIM))
    x2 = _post_ffn(out_a, out_b, wo_a, wo_b, x, g1, ng2, sh2, sc2, g2, wg_b, wu_b, wd_b, tm=tm)
    if not need_ctx:
        return x2, None
    oca = _flash_call(qca, None, None, kca, vca, mode="pair", online=True, kv_map=ident, n_chunks=4, tq=tc, tk=tc)
    ocb = _flash(qcb, None, None, kcb, vcb, mode="diff", kv_map=ident, n_chunks=4, tq=tc, tk=tc,
                 extra=(lam_p, sg), lam_init=lam_init)
    xc2 = _post_ffn(oca, ocb, wo_a, wo_b, xc, cg1, ng2, csh2, csc2, cg2, wg_b, wu_b, wd_b, tm=tc)
    return x2, xc2


def _odd_layer(x, xc, mod_l, norm1_g, norm2_g, w_in, w_out, gqa_q_g, gqa_k_g, cq_g, w_uq, ckv_g, w_ukv,
               mla_q_g, mla_k_g, router, w1, w3, w2):
    b, s, d = x.shape
    n_tok = b * s
    (sh1, sc1, g1, sh2, sc2, g2), (csh1, csc1, _, _, _, _) = _split_mod(mod_l, b)
    ng1 = norm1_g[None, :]
    ng2 = norm2_g[None, :]

    z = lambda n: jnp.zeros((d, n), w_in.dtype)
    k0, k1, v0, v1 = (w_in[:, 512 + 64 * i:576 + 64 * i] for i in range(4))
    w_p = jnp.concatenate([w_in[:, 0:512], k0, k0, k1, k1, v0, v0, v1, v1, w_in[:, 768:1024], w_in[:, 1024:1152],
                           z(MLA_NOPE), w_in[:, 1152:1184], z(LANES - MLA_QK)], axis=1).astype(BF16)
    wuq_p = jnp.pad(w_uq.reshape(MLA_Q_RANK, MLA_HEADS, MLA_QK), ((0, 0), (0, 0), (0, LANES - MLA_QK)))
    wuq_p = wuq_p.reshape(MLA_Q_RANK, MLA_HEADS * LANES).astype(BF16)
    ukv = w_ukv.reshape(MLA_KV_RANK, MLA_HEADS, MLA_NOPE + MLA_V)
    uk = jnp.pad(ukv[:, :, :MLA_NOPE], ((0, 0), (0, 0), (0, LANES - MLA_NOPE))).reshape(MLA_KV_RANK, -1)
    uv = ukv[:, :, MLA_NOPE:].reshape(MLA_KV_RANK, -1)
    wukv_p = jnp.concatenate([uk, uv], axis=1).astype(BF16)
    e64 = _block_ones(MXU_DIM, HEAD_DIM)
    e128 = _block_ones(MXU_DIM, LANES)
    qscale = HEAD_DIM ** -0.5 * LOG2E
    mscale = MLA_QK ** -0.5 * LOG2E
    pad_row = lambda v: jnp.pad(v, (0, MLA_HEADS * LANES - v.shape[0]))
    pad_head = lambda v: jnp.tile(jnp.pad(v, (0, LANES - MLA_QK)), MLA_HEADS)
    gains = jnp.stack([pad_row(jnp.tile(gqa_q_g, 8) * qscale), pad_row(jnp.tile(gqa_k_g, 4)),
                       pad_row(cq_g), pad_row(ckv_g), pad_head(mla_q_g) * mscale, pad_head(mla_k_g),
                       jnp.zeros((MLA_HEADS * LANES,), F32), jnp.zeros((MLA_HEADS * LANES,), F32)]).astype(F32)
    cos, sin = _rope_tables(s, HEAD_DIM, 16)
    cosm, sinm = _mla_rope_tables(s)
    wo_a = w_out[:512].astype(BF16)
    wo_b = w_out[512:].astype(BF16)

    tm = min(ROW_TM, s)
    tc = xc.shape[1]
    proj = functools.partial(_odd_proj, ng=ng1, w=w_p, wuq=wuq_p, wukv=wukv_p, e64=e64, e128=e128, gains=gains,
                             cos=cos, sin=sin, cosm=cosm, sinm=sinm)
    qc, kc, vc, qd, kd, vd = proj(x, sh1, sc1, use_rope=True, tm=tm)
    _, kcc, vcc, _, kcd, vcd = proj(xc, csh1, csc1, use_rope=False, tm=tc)
    out_c = _flash(qc, kc, vc, kcc, vcc, mode="pair", kv_map=lambda j: j // 2, n_chunks=4, tq=FLASH_TQ, tk=FLASH_TK,
                   score_bound=_score_bound(gains[0], gains[1], HEAD_DIM))
    out_d = _flash(qd, kd, vd, kcd, vcd, mode="mla", kv_map=lambda j: j, n_chunks=4, tq=FLASH_TQ, tk=FLASH_TK,
                   score_bound=_score_bound(gains[4], gains[5], MLA_QK))
    r_hi = router.astype(BF16)
    r_lo = (router - r_hi.astype(F32)).astype(BF16)
    pad_e = lambda m: jnp.pad(m, ((0, 0), (0, LANES - N_EXPERTS)))
    x1, h2, ei, gt, rk, cnt = _post_attn_router(out_c, out_d, wo_a, wo_b, x, g1, ng2, sh2, sc2,
                                                jnp.concatenate([pad_e(r_hi), pad_e(r_lo)], axis=1), tm=tm)
    plan = _moe_plan(ei, rk, cnt[:, 0], n_tok)
    xg, gs = _moe_gather(h2.reshape(n_tok, d), plan["dest"], gt, plan)
    y = _moe_experts(xg, gs, w1.astype(BF16), w3.astype(BF16), w2.astype(BF16), plan)
    out = _moe_combine(y, plan["dest"].T, x1.reshape(n_tok, d), g2, plan, s)
    return out.reshape(b, s, d)


def _mod_vectors(c, c_ctx, ada_w, ada_b):
    b = c.shape[0]
    cc = jnp.zeros((MOD_ROWS, D_MODEL), F32).at[:b].set(c).at[b].set(c_ctx)
    return _modvec(cc, ada_w, ada_b[:, None, :])


def kernel(x, c, ctx, c_ctx, ada_w, ada_b, norm1_g, norm2_g, ev_w_in, ev_w_out, na_q_g, na_k_g, na_rpb,
           diff_q_g, diff_k_g, diff_lq1, diff_lk1, diff_lq2, diff_lk2, diff_subln_g,
           ffn_w_gate, ffn_w_up, ffn_w_down, od_w_in, od_w_out, gqa_q_g, gqa_k_g, mla_cq_g, mla_w_uq,
           mla_ckv_g, mla_w_ukv, mla_q_g, mla_k_g, moe_router, moe_w1, moe_w3, moe_w2):
    mod = _mod_vectors(c, c_ctx, ada_w, ada_b)
    x, xc = _even_layer(x, ctx, mod[0], norm1_g[0], norm2_g[0], ev_w_in[0], ev_w_out[0], na_q_g[0], na_k_g[0],
                        na_rpb[0], diff_q_g[0], diff_k_g[0], diff_lq1[0], diff_lk1[0], diff_lq2[0], diff_lk2[0],
                        diff_subln_g[0], ffn_w_gate[0], ffn_w_up[0], ffn_w_down[0], 0, True)
    return _odd_layer(x, xc, mod[1], norm1_g[1], norm2_g[1], od_w_in[0], od_w_out[0], gqa_q_g[0], gqa_k_g[0],
                      mla_cq_g[0], mla_w_uq[0], mla_ckv_g[0], mla_w_ukv[0], mla_q_g[0], mla_k_g[0],
                      moe_router[0], moe_w1[0], moe_w3[0], moe_w2[0])
```
